```python
import jax, jax.numpy as jnp
from jax import lax
import numpy as np

D_MODEL = 1024
BATCH = 8
SEQ = 2048
DEPTH = 1
DEC_BATCH = 128
DEC_SEQ = 1
PAST_LEN = 16384
PAGE_SIZE = 128

NH_M = 4
DHV = D_MODEL // NH_M
DQK = DHV // 2
D_M = NH_M * DHV
D_QK = NH_M * DQK
MLSTM_CHUNK = 64
F_BIAS_INIT = 3.0
D_C = D_MODEL
CONV_W = 3
EPS = 1e-6
NEG_BIG = -1e30

SPLIT_SIZES = (D_QK, D_QK, D_M, NH_M, NH_M, D_M, D_M, D_C, D_C, D_C, D_C, D_MODEL, D_MODEL)
N_IN = sum(SPLIT_SIZES)

kernel_name = "hybrid_mlstm_shortconv_gated_merge_step"


def rmsnorm(x, w):
    xf = x.astype(jnp.float32)
    xf = xf * lax.rsqrt(jnp.mean(xf * xf, axis=-1, keepdims=True) + EPS)
    return (xf * w.astype(jnp.float32)).astype(x.dtype)


def mlstm_chunkwise(q, k, v, ig, lf, C0, n0, m0):
    B, L = q.shape[0], q.shape[1]
    Lc = min(MLSTM_CHUNK, L)
    nc = -(-L // Lc)
    pad = nc * Lc - L
    if pad:
        pw = ((0, 0), (0, pad), (0, 0), (0, 0))
        q = jnp.pad(q, pw)
        k = jnp.pad(k, pw)
        v = jnp.pad(v, pw)
        lf = jnp.pad(lf, ((0, 0), (0, pad), (0, 0)))
        ig = jnp.pad(ig, ((0, 0), (0, pad), (0, 0)), constant_values=NEG_BIG)

    def to_chunks(a):
        a = a.reshape((B, nc, Lc) + a.shape[2:])
        return jnp.moveaxis(jnp.moveaxis(a, 1, 0), 3, 2)

    qs, ks, vs, igs, lfs = to_chunks(q), to_chunks(k), to_chunks(v), to_chunks(ig), to_chunks(lf)
    mask = jnp.tril(jnp.ones((Lc, Lc), dtype=bool))

    def step(carry, inp):
        C, n, m = carry
        qc, kc, vc, igc, lfc = inp
        b = jnp.cumsum(lfc, axis=-1)
        logD = b[..., :, None] - b[..., None, :] + igc[..., None, :]
        logD = jnp.where(mask, logD, NEG_BIG)
        inter = b + m[..., None]
        m_t = jnp.maximum(inter, jnp.max(logD, axis=-1))
        Dm = jnp.exp(logD - m_t[..., None])
        w_int = jnp.exp(inter - m_t)
        s = jnp.einsum('bhtd,bhsd->bhts', qc, kc) * Dm
        num = w_int[..., None] * jnp.einsum('bhvd,bhtd->bhtv', C, qc) + jnp.einsum('bhts,bhsv->bhtv', s, vc)
        den = w_int * jnp.einsum('bhd,bhtd->bht', n, qc) + jnp.sum(s, axis=-1)
        h = num / jnp.maximum(jnp.abs(den), jnp.exp(-m_t))[..., None]
        logw_end = b[..., -1:] - b + igc
        inter_end = b[..., -1] + m
        m_new = jnp.maximum(inter_end, jnp.max(logw_end, axis=-1))
        w_end = jnp.exp(logw_end - m_new[..., None])
        f_end = jnp.exp(inter_end - m_new)
        C_new = f_end[..., None, None] * C + jnp.einsum('bhs,bhsv,bhsd->bhvd', w_end, vc, kc)
        n_new = f_end[..., None] * n + jnp.einsum('bhs,bhsd->bhd', w_end, kc)
        return (C_new, n_new, m_new), h

    (C1, n1, m1), hs = lax.scan(step, (C0, n0, m0), (qs, ks, vs, igs, lfs))
    h = jnp.moveaxis(jnp.moveaxis(hs, 0, 1), 2, 3).reshape(B, nc * Lc, NH_M, DHV)[:, :L]
    return h, C1, n1, m1


def short_conv(u, conv0, w):
    L = u.shape[1]
    up = jnp.concatenate([conv0.astype(u.dtype), u], axis=1)
    y = w[0] * up[:, 0:L]
    for j in range(1, CONV_W):
        y = y + w[j] * up[:, j:j + L]
    return y, up[:, -(CONV_W - 1):]


def layer(x, C0, n0, m0, conv0, norm_w, w_in, b_i, b_f, head_norm_w, conv_w, w_proj_m, w_proj_c, w_out):
    B, L, _ = x.shape
    f32 = jnp.float32
    xn = rmsnorm(x, norm_w)
    proj = xn @ w_in
    offs = np.cumsum(SPLIT_SIZES)[:-1].tolist()
    q, k, v, i_pre, f_pre, o_pre, z_m, b_g, c_g, x_c, z_c, g_m, g_c = jnp.split(proj, offs, axis=-1)
    q = q.reshape(B, L, NH_M, DQK).astype(f32)
    k = k.reshape(B, L, NH_M, DQK).astype(f32) * (DQK ** -0.5)
    v = v.reshape(B, L, NH_M, DHV).astype(f32)
    ig = i_pre.astype(f32) + b_i.astype(f32)
    lf = jax.nn.log_sigmoid(f_pre.astype(f32) + b_f.astype(f32))
    h, C1, n1, m1 = mlstm_chunkwise(q, k, v, ig, lf, C0.astype(f32), n0.astype(f32), m0.astype(f32))
    h = h * lax.rsqrt(jnp.mean(h * h, axis=-1, keepdims=True) + EPS)
    h = (h.reshape(B, L, D_M) * head_norm_w.astype(f32)).astype(x.dtype)
    h = h * jax.nn.sigmoid(o_pre) * jax.nn.silu(z_m)
    y_m = h @ w_proj_m
    cv, conv1 = short_conv(c_g * x_c, conv0, conv_w)
    y_c = (jax.nn.silu(z_c) * b_g * cv) @ w_proj_c
    mix = jax.nn.sigmoid(g_m) * y_m + jax.nn.sigmoid(g_c) * y_c
    out = x + mix @ w_out
    return out, C1, n1, m1, conv1


def setup_inputs(seed: int = 0) -> dict:
    key = jax.random.key(seed)
    ks = jax.random.split(key, 17)
    nrm = jax.random.normal
    f32 = jnp.float32
    return {
        "x_prompt": nrm(ks[0], (BATCH, SEQ, D_MODEL), f32),
        "x_sample": nrm(ks[1], (DEC_BATCH, DEC_SEQ, D_MODEL), f32),
        "state_mlstm_C": 0.3 * nrm(ks[2], (DEPTH, DEC_BATCH, NH_M, DHV, DQK), f32),
        "state_mlstm_n": 0.3 * nrm(ks[3], (DEPTH, DEC_BATCH, NH_M, DQK), f32),
        "state_mlstm_m": 2.0 + nrm(ks[4], (DEPTH, DEC_BATCH, NH_M), f32),
        "state_conv": nrm(ks[5], (DEPTH, DEC_BATCH, CONV_W - 1, D_C), f32),
        "norm_w": 1.0 + 0.05 * nrm(ks[6], (DEPTH, D_MODEL), f32),
        "w_in": nrm(ks[7], (DEPTH, D_MODEL, N_IN), f32) * D_MODEL ** -0.5,
        "b_i": 0.1 * nrm(ks[8], (DEPTH, NH_M), f32),
        "b_f": F_BIAS_INIT + 0.5 * nrm(ks[9], (DEPTH, NH_M), f32),
        "head_norm_w": 1.0 + 0.05 * nrm(ks[10], (DEPTH, D_M), f32),
        "conv_w": nrm(ks[11], (DEPTH, CONV_W, D_C), f32) * CONV_W ** -0.5,
        "w_proj_m": nrm(ks[12], (DEPTH, D_M, D_MODEL), f32) * D_M ** -0.5,
        "w_proj_c": nrm(ks[13], (DEPTH, D_C, D_MODEL), f32) * D_C ** -0.5,
        "w_out": nrm(ks[14], (DEPTH, D_MODEL, D_MODEL), f32) * D_MODEL ** -0.5,
        "final_norm_w": 1.0 + 0.05 * nrm(ks[15], (D_MODEL,), f32),
    }


def reference(x_prompt, x_sample, state_mlstm_C, state_mlstm_n, state_mlstm_m, state_conv,
              norm_w, w_in, b_i, b_f, head_norm_w, conv_w, w_proj_m, w_proj_c, w_out, final_norm_w):
    f32 = jnp.float32
    bp = x_prompt.shape[0]
    xp, xs = x_prompt, x_sample
    Cp_l, np_l, mp_l, cp_l = [], [], [], []
    Cs_l, ns_l, ms_l, cs_l = [], [], [], []
    for l in range(DEPTH):
        w = (norm_w[l], w_in[l], b_i[l], b_f[l], head_norm_w[l], conv_w[l], w_proj_m[l], w_proj_c[l], w_out[l])
        C0 = jnp.zeros((bp, NH_M, DHV, DQK), f32)
        n0 = jnp.zeros((bp, NH_M, DQK), f32)
        m0 = jnp.zeros((bp, NH_M), f32)
        cv0 = jnp.zeros((bp, CONV_W - 1, D_C), xp.dtype)
        xp, Cp, npp, mp, cp = layer(xp, C0, n0, m0, cv0, *w)
        xs, Cs, ns, ms, cs = layer(xs, state_mlstm_C[l], state_mlstm_n[l], state_mlstm_m[l], state_conv[l], *w)
        Cp_l.append(Cp.astype(state_mlstm_C.dtype)); np_l.append(npp.astype(state_mlstm_n.dtype))
        mp_l.append(mp.astype(state_mlstm_m.dtype)); cp_l.append(cp.astype(state_conv.dtype))
        Cs_l.append(Cs.astype(state_mlstm_C.dtype)); ns_l.append(ns.astype(state_mlstm_n.dtype))
        ms_l.append(ms.astype(state_mlstm_m.dtype)); cs_l.append(cs.astype(state_conv.dtype))
    y_prompt = rmsnorm(xp, final_norm_w)
    y_sample = rmsnorm(xs, final_norm_w)
    new_C_prompt = jnp.stack(Cp_l)
    new_n_prompt = jnp.stack(np_l)
    new_m_prompt = jnp.stack(mp_l)
    new_conv_prompt = jnp.stack(cp_l)
    new_C_sample = jnp.stack(Cs_l)
    new_n_sample = jnp.stack(ns_l)
    new_m_sample = jnp.stack(ms_l)
    new_conv_sample = jnp.stack(cs_l)
    return (y_prompt, y_sample, new_C_prompt, new_n_prompt, new_m_prompt, new_conv_prompt,
            new_C_sample, new_n_sample, new_m_sample, new_conv_sample)
```

```python
import functools

import jax
import jax.numpy as jnp
from jax import lax
from jax.experimental import pallas as pl
from jax.experimental.pallas import tpu as pltpu

F32 = jnp.float32
BF16 = jnp.bfloat16

D_MODEL = 1024
NH = 4
DHV = 256
DQK = 128
D_QK = NH * DQK
D_M = NH * DHV
D_C = D_MODEL
CONV_W = 3
EPS = 1e-6
NEG_BIG = -1e30
K_SCALE = DQK ** -0.5

LANES = 128
SUBLANES = 8

OFF_Q = 0
OFF_K = OFF_Q + D_QK
OFF_V = OFF_K + D_QK
OFF_G = OFF_V + D_M
OFF_O = OFF_G + LANES
OFF_ZM = OFF_O + D_M
OFF_BG = OFF_ZM + D_M
OFF_CG = OFF_BG + D_C
OFF_XC = OFF_CG + D_C
OFF_ZC = OFF_XC + D_C
OFF_GM = OFF_ZC + D_C
OFF_GC = OFF_GM + D_MODEL
N_PACK = OFF_GC + D_MODEL
N_GATE_COLS = 2 * NH

PROMPT_BLOCK = 256
SAMPLE_SEQ_BLOCK = 8
SAMPLE_PROJ_COLS = 1152
VMEM_LIMIT_PROMPT = 56 * 1024 * 1024
VMEM_LIMIT_SAMPLE = 48 * 1024 * 1024


def _sigmoid(x):
    return 1.0 / (1.0 + jnp.exp(-x))


def _silu(x):
    return x * _sigmoid(x)


def _log_sigmoid(x):
    return jnp.minimum(x, 0.0) - jnp.log1p(jnp.exp(-jnp.abs(x)))


def _rmsnorm(x, w):
    return x * lax.rsqrt(jnp.mean(x * x, axis=-1, keepdims=True) + EPS) * w


def _dot(a, b):
    return jnp.dot(a, b, preferred_element_type=F32)


def _dot_nt(a, b):
    return lax.dot_general(a, b, (((1,), (1,)), ((), ())), preferred_element_type=F32)


def _dot_tn(a, b):
    return lax.dot_general(a, b, (((0,), (0,)), ((), ())), preferred_element_type=F32)


def _split3(x):
    hi = x.astype(BF16)
    r = x - hi.astype(F32)
    mid = r.astype(BF16)
    lo = (r - mid.astype(F32)).astype(BF16)
    return hi, mid, lo


def _prompt_kernel(x_ref, w_ref, wpm_ref, wpc_ref, wout_ref, nw_ref, gb_ref, hnw_ref, cw_ref, fnw_ref,
                   y_ref, c_ref, n_ref, m_ref, conv_ref,
                   hg_s, ubuf_s):
    tl = x_ref.shape[1]
    l = pl.program_id(1)

    @pl.when(l == 0)
    def _():
        c_ref[...] = jnp.zeros_like(c_ref)
        n_ref[...] = jnp.zeros_like(n_ref)
        m_ref[...] = jnp.zeros_like(m_ref)
        ubuf_s[0:SUBLANES, :] = jnp.zeros((SUBLANES, D_C), F32)

    x = x_ref[0]
    xn = _rmsnorm(x, nw_ref[...]).astype(BF16)

    def proj(off, width):
        return _dot(xn, w_ref[:, off:off + width])

    g = proj(OFF_G, LANES) + gb_ref[...]
    lane = lax.broadcasted_iota(jnp.int32, (tl, LANES), 1)
    gc = jnp.where(lane < NH, g, _log_sigmoid(g))
    row = lax.broadcasted_iota(jnp.int32, (tl, tl), 0)
    col = lax.broadcasted_iota(jnp.int32, (tl, tl), 1)
    causal = row >= col
    tril = jnp.where(causal, 1.0, 0.0).astype(BF16)
    hi, mid, lo = _split3(gc)
    bc = _dot(tril, hi) + _dot(tril, mid) + _dot(tril, lo)
    gt = jnp.where(lane < NH, gc, bc).T

    qkv = proj(OFF_Q, OFF_G)

    for h in range(NH):
        q_f = qkv[:, OFF_Q + h * DQK:OFF_Q + (h + 1) * DQK]
        k_f = qkv[:, OFF_K + h * DQK:OFF_K + (h + 1) * DQK] * K_SCALE
        v_f = qkv[:, OFF_V + h * DHV:OFF_V + (h + 1) * DHV]
        q_b = q_f.astype(BF16)
        k_b = k_f.astype(BF16)
        v_b = v_f.astype(BF16)
        c_old = c_ref[0, 0, h]
        n_old = n_ref[0, 0, h:h + 1, :]
        m_old = m_ref[0, h:h + 1, 0:1]

        b_c = bc[:, NH + h:NH + h + 1]
        ig_c = gc[:, h:h + 1]
        ig_r = gt[h:h + 1, :]
        b_r = gt[NH + h:NH + h + 1, :]
        log_d = jnp.where(causal, b_c - b_r + ig_r, NEG_BIG)
        inter = b_c + m_old
        m_t = jnp.maximum(inter, jnp.max(log_d, axis=-1, keepdims=True))
        d_m = jnp.exp(log_d - m_t)
        w_int = jnp.exp(inter - m_t)
        s = _dot_nt(q_b, k_b) * d_m
        num = w_int * _dot_nt(q_b, c_old.astype(BF16)) + _dot(s.astype(BF16), v_b)
        den = w_int * jnp.sum(q_f * n_old, axis=-1, keepdims=True) + jnp.sum(s, axis=-1, keepdims=True)
        h_t = num / jnp.maximum(jnp.abs(den), jnp.exp(-m_t))

        b_end = b_c[tl - 1:tl, :]
        inter_end = b_end + m_old
        m_new = jnp.maximum(inter_end, jnp.max(b_end - b_r + ig_r, axis=-1, keepdims=True))
        w_end = jnp.exp(b_end - b_c + ig_c - m_new)
        f_end = jnp.exp(inter_end - m_new)
        c_ref[0, 0, h] = f_end * c_old + _dot_tn((w_end * v_f).astype(BF16), k_b)
        n_ref[0, 0, h:h + 1, :] = f_end * n_old + jnp.sum(w_end * k_f, axis=0, keepdims=True)
        m_ref[0, h:h + 1, :] = jnp.broadcast_to(m_new, (1, LANES))

        hn = h_t * lax.rsqrt(jnp.mean(h_t * h_t, axis=-1, keepdims=True) + EPS)
        hn = hn * hnw_ref[:, h * DHV:(h + 1) * DHV]
        o_h = proj(OFF_O + h * DHV, DHV)
        zm_h = proj(OFF_ZM + h * DHV, DHV)
        hg_s[:, h * DHV:(h + 1) * DHV] = (hn * _sigmoid(o_h) * _silu(zm_h)).astype(BF16)

    y_m = _dot(hg_s[...], wpm_ref[...])

    u = proj(OFF_CG, D_C) * proj(OFF_XC, D_C)
    ubuf_s[SUBLANES:SUBLANES + tl, :] = u
    cv = cw_ref[0:1, :] * ubuf_s[SUBLANES - 2:SUBLANES - 2 + tl, :]
    cv = cv + cw_ref[1:2, :] * ubuf_s[SUBLANES - 1:SUBLANES - 1 + tl, :]
    cv = cv + cw_ref[2:3, :] * u
    yc_in = (_silu(proj(OFF_ZC, D_C)) * proj(OFF_BG, D_C) * cv).astype(BF16)
    y_c = _dot(yc_in, wpc_ref[...])
    ubuf_s[0:SUBLANES, :] = ubuf_s[tl:tl + SUBLANES, :]

    @pl.when(l == pl.num_programs(1) - 1)
    def _():
        conv_ref[0, 0] = ubuf_s[SUBLANES - (CONV_W - 1):SUBLANES, :]

    mix = (_sigmoid(proj(OFF_GM, D_MODEL)) * y_m + _sigmoid(proj(OFF_GC, D_MODEL)) * y_c).astype(BF16)
    out = x + _dot(mix, wout_ref[...])
    y_ref[0] = _rmsnorm(out, fnw_ref[...])


def _resident(shape):
    return pl.BlockSpec(shape, lambda *_: (0,) * len(shape), pipeline_mode=pl.Buffered(1))


def _prompt_call(x, w_all, wpm, wpc, wout, nw, gb, hnw, cw, fnw):
    bsz, seq, _ = x.shape
    tl = PROMPT_BLOCK
    grid = (bsz, seq // tl)
    out_shape = (
        jax.ShapeDtypeStruct((bsz, seq, D_MODEL), F32),
        jax.ShapeDtypeStruct((1, bsz, NH, DHV, DQK), F32),
        jax.ShapeDtypeStruct((1, bsz, NH, DQK), F32),
        jax.ShapeDtypeStruct((bsz, SUBLANES, LANES), F32),
        jax.ShapeDtypeStruct((1, bsz, CONV_W - 1, D_C), F32),
    )
    in_specs = [
        pl.BlockSpec((1, tl, D_MODEL), lambda b, l: (b, l, 0)),
        _resident((D_MODEL, N_PACK)),
        _resident((D_M, D_MODEL)),
        _resident((D_C, D_MODEL)),
        _resident((D_MODEL, D_MODEL)),
        _resident((1, D_MODEL)),
        _resident((1, LANES)),
        _resident((1, D_M)),
        _resident((CONV_W, D_C)),
        _resident((1, D_MODEL)),
    ]
    out_specs = (
        pl.BlockSpec((1, tl, D_MODEL), lambda b, l: (b, l, 0)),
        pl.BlockSpec((1, 1, NH, DHV, DQK), lambda b, l: (0, b, 0, 0, 0)),
        pl.BlockSpec((1, 1, NH, DQK), lambda b, l: (0, b, 0, 0)),
        pl.BlockSpec((1, SUBLANES, LANES), lambda b, l: (b, 0, 0)),
        pl.BlockSpec((1, 1, CONV_W - 1, D_C), lambda b, l: (0, b, 0, 0)),
    )
    return pl.pallas_call(
        _prompt_kernel,
        grid=grid,
        in_specs=in_specs,
        out_specs=out_specs,
        out_shape=out_shape,
        scratch_shapes=[
            pltpu.VMEM((tl, D_M), BF16),
            pltpu.VMEM((tl + 2 * SUBLANES, D_C), F32),
        ],
        compiler_params=pltpu.CompilerParams(
            dimension_semantics=("arbitrary", "arbitrary"),
            vmem_limit_bytes=VMEM_LIMIT_PROMPT),
        name="prompt_layer",
    )(x, w_all, wpm, wpc, wout, nw, gb, hnw, cw, fnw)


def _sample_proj_kernel(x_ref, nw_ref, w_ref, o_ref):
    xn = _rmsnorm(x_ref[...], nw_ref[...]).astype(BF16)
    o_ref[...] = _dot(xn, w_ref[...])


def _sample_proj_call(x, nw, w_all):
    nseq = x.shape[0]
    cols = SAMPLE_PROJ_COLS
    return pl.pallas_call(
        _sample_proj_kernel,
        grid=(N_PACK // cols,),
        in_specs=[
            pl.BlockSpec((nseq, D_MODEL), lambda j: (0, 0)),
            pl.BlockSpec((1, D_MODEL), lambda j: (0, 0)),
            pl.BlockSpec((D_MODEL, cols), lambda j: (0, j)),
        ],
        out_specs=pl.BlockSpec((nseq, cols), lambda j: (0, j)),
        out_shape=jax.ShapeDtypeStruct((nseq, N_PACK), F32),
        compiler_params=pltpu.CompilerParams(dimension_semantics=("arbitrary",),
                                             vmem_limit_bytes=VMEM_LIMIT_SAMPLE),
        name="sample_in_proj",
    )(x, nw, w_all)


def _sample_state_kernel(q_ref, k_ref, g_ref, gb_ref, vc_ref, c_ref, n_ref, m_ref,
                         ht_ref, co_ref, no_ref, mo_ref):
    nb = q_ref.shape[0]
    g = g_ref[...] + gb_ref[...]
    ig = g[:, 0:NH]
    lf = _log_sigmoid(g[:, NH:2 * NH])
    inter = lf + m_ref[0]
    m_t = jnp.maximum(inter, ig)
    w_end = jnp.exp(ig - m_t)
    f_end = jnp.exp(inter - m_t)
    floor = jnp.exp(-m_t)
    mo_ref[0] = m_t
    for h in range(NH):
        q_h = q_ref[:, h * DQK:(h + 1) * DQK]
        k_h = k_ref[:, h * DQK:(h + 1) * DQK] * K_SCALE
        n_h = n_ref[:, h * DQK:(h + 1) * DQK]
        w_h = w_end[:, h:h + 1]
        f_h = f_end[:, h:h + 1]
        s = jnp.sum(q_h * k_h, axis=-1, keepdims=True) * w_h
        den = f_h * jnp.sum(n_h * q_h, axis=-1, keepdims=True) + s
        denom = jnp.maximum(jnp.abs(den), floor[:, h:h + 1])
        no_ref[:, h * DQK:(h + 1) * DQK] = f_h * n_h + w_h * k_h
        for j in range(nb):
            c_old = c_ref[0, j, h]
            v_c = vc_ref[0, h, :, j:j + 1]
            f_s = f_h[j:j + 1, :]
            cq = jnp.sum(c_old * q_h[j:j + 1, :], axis=-1, keepdims=True)
            num = f_s * cq + s[j:j + 1, :] * v_c
            ht_ref[0, h, :, j:j + 1] = num / denom[j:j + 1, :]
            co_ref[0, j, h] = f_s * c_old + (w_h[j:j + 1, :] * v_c) * k_h[j:j + 1, :]


def _sample_state_call(proj, gb, v_col, c0, n0, m0):
    nseq = proj.shape[0]
    nb = SAMPLE_SEQ_BLOCK
    nblk = nseq // nb
    out_shape = (
        jax.ShapeDtypeStruct((nblk, NH, DHV, nb), F32),
        jax.ShapeDtypeStruct(c0.shape, F32),
        jax.ShapeDtypeStruct((nseq, D_QK), F32),
        jax.ShapeDtypeStruct(m0.shape, F32),
    )
    in_specs = [
        pl.BlockSpec((nb, D_QK), lambda i: (i, OFF_Q // D_QK)),
        pl.BlockSpec((nb, D_QK), lambda i: (i, OFF_K // D_QK)),
        pl.BlockSpec((nb, LANES), lambda i: (i, OFF_G // LANES)),
        pl.BlockSpec((1, LANES), lambda i: (0, 0)),
        pl.BlockSpec((1, NH, DHV, nb), lambda i: (i, 0, 0, 0)),
        pl.BlockSpec((1, nb, NH, DHV, DQK), lambda i: (0, i, 0, 0, 0)),
        pl.BlockSpec((nb, D_QK), lambda i: (i, 0)),
        pl.BlockSpec((1, nb, NH), lambda i: (0, i, 0)),
    ]
    out_specs = (
        pl.BlockSpec((1, NH, DHV, nb), lambda i: (i, 0, 0, 0)),
        pl.BlockSpec((1, nb, NH, DHV, DQK), lambda i: (0, i, 0, 0, 0)),
        pl.BlockSpec((nb, D_QK), lambda i: (i, 0)),
        pl.BlockSpec((1, nb, NH), lambda i: (0, i, 0)),
    )
    return pl.pallas_call(
        _sample_state_kernel,
        grid=(nblk,),
        in_specs=in_specs,
        out_specs=out_specs,
        out_shape=out_shape,
        compiler_params=pltpu.CompilerParams(dimension_semantics=("arbitrary",),
                                             vmem_limit_bytes=VMEM_LIMIT_SAMPLE),
        name="sample_state",
    )(proj, proj, proj, gb, v_col, c0, n0, m0)


def _sample_tail_kernel(x_ref, p_ref, h_ref, conv_ref, wpm_ref, wpc_ref, wout_ref, hnw_ref, cw_ref, fnw_ref,
                        y_ref, convo_ref):
    def piece(off, width):
        return p_ref[:, off:off + width]

    for h in range(NH):
        h_t = h_ref[:, h * DHV:(h + 1) * DHV]
        hn = h_t * lax.rsqrt(jnp.mean(h_t * h_t, axis=-1, keepdims=True) + EPS)
        hn = hn * hnw_ref[:, h * DHV:(h + 1) * DHV]
        o_h = piece(OFF_O + h * DHV, DHV)
        zm_h = piece(OFF_ZM + h * DHV, DHV)
        hg = (hn * _sigmoid(o_h) * _silu(zm_h)).astype(BF16)
        part = _dot(hg, wpm_ref[h * DHV:(h + 1) * DHV, :])
        y_m = part if h == 0 else y_m + part

    u = piece(OFF_CG, D_C) * piece(OFF_XC, D_C)
    cv = cw_ref[0:1, :] * conv_ref[:, 0:D_C]
    cv = cv + cw_ref[1:2, :] * conv_ref[:, D_C:2 * D_C]
    cv = cv + cw_ref[2:3, :] * u
    convo_ref[:, 0:D_C] = conv_ref[:, D_C:2 * D_C]
    convo_ref[:, D_C:2 * D_C] = u
    yc_in = (_silu(piece(OFF_ZC, D_C)) * piece(OFF_BG, D_C) * cv).astype(BF16)
    y_c = _dot(yc_in, wpc_ref[...])
    mix = (_sigmoid(piece(OFF_GM, D_MODEL)) * y_m + _sigmoid(piece(OFF_GC, D_MODEL)) * y_c).astype(BF16)
    out = x_ref[...] + _dot(mix, wout_ref[...])
    y_ref[...] = _rmsnorm(out, fnw_ref[...])


def _sample_tail_call(x, proj, h_s, conv0, wpm, wpc, wout, hnw, cw, fnw):
    nseq = x.shape[0]
    args = (x, proj, h_s, conv0, wpm, wpc, wout, hnw, cw, fnw)
    full = lambda a: pl.BlockSpec(a.shape, lambda i: (0,) * a.ndim)
    return pl.pallas_call(
        _sample_tail_kernel,
        grid=(1,),
        in_specs=[full(a) for a in args],
        out_specs=(pl.BlockSpec((nseq, D_MODEL), lambda i: (0, 0)),
                   pl.BlockSpec((nseq, (CONV_W - 1) * D_C), lambda i: (0, 0))),
        out_shape=(jax.ShapeDtypeStruct((nseq, D_MODEL), F32),
                   jax.ShapeDtypeStruct((nseq, (CONV_W - 1) * D_C), F32)),
        compiler_params=pltpu.CompilerParams(dimension_semantics=("arbitrary",),
                                             vmem_limit_bytes=VMEM_LIMIT_SAMPLE),
        name="sample_tail",
    )(*args)


def kernel(x_prompt, x_sample, state_mlstm_C, state_mlstm_n, state_mlstm_m, state_conv, norm_w, w_in, b_i, b_f,
           head_norm_w, conv_w, w_proj_m, w_proj_c, w_out, final_norm_w):
    depth = norm_w.shape[0]
    assert depth == 1, "single-layer trunk"
    bsz = x_prompt.shape[0]
    nseq = x_sample.shape[0]

    n_head_cols = OFF_G + N_GATE_COLS
    w_all = jnp.concatenate(
        [w_in[0, :, :n_head_cols],
         jnp.zeros((D_MODEL, LANES - N_GATE_COLS), w_in.dtype),
         w_in[0, :, n_head_cols:]], axis=1).astype(BF16)
    wpm = w_proj_m[0].astype(BF16)
    wpc = w_proj_c[0].astype(BF16)
    wout = w_out[0].astype(BF16)
    gb = jnp.concatenate([b_i[0], b_f[0], jnp.zeros((LANES - N_GATE_COLS,), F32)]).reshape(1, LANES)
    nw = norm_w[0].reshape(1, D_MODEL)
    hnw = head_norm_w[0].reshape(1, D_M)
    cw = conv_w[0]
    fnw = final_norm_w.reshape(1, D_MODEL)

    y_p, c_p, n_p, m_p, conv_p = _prompt_call(x_prompt, w_all, wpm, wpc, wout, nw, gb, hnw, cw, fnw)
    m_p = m_p[:, :NH, 0].reshape(1, bsz, NH)

    xs = x_sample.reshape(nseq, D_MODEL)
    proj_s = _sample_proj_call(xs, nw, w_all)
    nb = SAMPLE_SEQ_BLOCK
    v_col = proj_s[:, OFF_V:OFF_V + D_M].reshape(nseq // nb, nb, NH, DHV).transpose(0, 2, 3, 1)
    h_t, c_s, n_s, m_s = _sample_state_call(
        proj_s, gb, v_col, state_mlstm_C, state_mlstm_n[0].reshape(nseq, D_QK), state_mlstm_m)
    h_s = h_t.transpose(0, 3, 1, 2).reshape(nseq, D_M)
    conv0 = state_conv[0].reshape(nseq, (CONV_W - 1) * D_C)
    y_s, conv_s = _sample_tail_call(xs, proj_s, h_s, conv0, wpm, wpc, wout, hnw, cw, fnw)

    return (y_p, y_s.reshape(nseq, 1, D_MODEL), c_p, n_p, m_p, conv_p,
            c_s, n_s.reshape(1, nseq, NH, DQK), m_s,
            conv_s.reshape(1, nseq, CONV_W - 1, D_C))
```

```python
import jax
import jax.numpy as jnp
from jax import lax
from jax.experimental import pallas as pl
from jax.experimental.pallas import tpu as pltpu

F32 = jnp.float32
BF16 = jnp.bfloat16

D_MODEL = 1024
NH = 4
DHV = 256
DQK = 128
D_QK = NH * DQK
D_M = NH * DHV
D_C = D_MODEL
CONV_W = 3
EPS = 1e-6
NEG_BIG = -1e30
K_SCALE = DQK ** -0.5

LANES = 128
SUBLANES = 8

OFF_Q = 0
OFF_K = OFF_Q + D_QK
OFF_V = OFF_K + D_QK
N_HEAD = OFF_V + D_M
N_GATE_COLS = 2 * NH
R_O = 0
R_ZM = R_O + D_M
R_BG = R_ZM + D_M
R_CG = R_BG + D_C
R_XC = R_CG + D_C
R_ZC = R_XC + D_C
R_GM = R_ZC + D_C
R_GC = R_GM + D_MODEL
N_REST = R_GC + D_MODEL

PROMPT_BLOCK = 256
SAMPLE_SEQ_BLOCK = 8
SAMPLE_PROJ_COLS = 1024
VMEM_LIMIT_PROMPT = 56 * 1024 * 1024
VMEM_LIMIT_SAMPLE = 48 * 1024 * 1024


def _sigmoid(x):
    return 1.0 / (1.0 + jnp.exp(-x))


def _silu(x):
    return x * _sigmoid(x)


def _log_sigmoid(x):
    return jnp.minimum(x, 0.0) - jnp.log1p(jnp.exp(-jnp.abs(x)))


def _rmsnorm(x, w):
    return x * lax.rsqrt(jnp.mean(x * x, axis=-1, keepdims=True) + EPS) * w


def _dot(a, b):
    return jnp.dot(a, b, preferred_element_type=F32)


def _dot_nt(a, b):
    return lax.dot_general(a, b, (((1,), (1,)), ((), ())), preferred_element_type=F32)


def _dot_tn(a, b):
    return lax.dot_general(a, b, (((0,), (0,)), ((), ())), preferred_element_type=F32)


def _split3(x):
    hi = x.astype(BF16)
    r = x - hi.astype(F32)
    mid = r.astype(BF16)
    lo = (r - mid.astype(F32)).astype(BF16)
    return hi, mid, lo


def _prompt_kernel(x_ref, wh_ref, wg_ref, wr_ref, wpm_ref, wpc_ref, wout_ref,
                   nw_ref, gb_ref, hnw_ref, cw_ref, fnw_ref,
                   y_ref, c_ref, n_ref, m_ref, conv_ref,
                   hg_s, ubuf_s):
    tl = x_ref.shape[1]
    l = pl.program_id(1)

    @pl.when(l == 0)
    def _():
        c_ref[...] = jnp.zeros_like(c_ref)
        n_ref[...] = jnp.zeros_like(n_ref)
        m_ref[...] = jnp.zeros_like(m_ref)
        ubuf_s[0:SUBLANES, :] = jnp.zeros((SUBLANES, D_C), F32)

    x = x_ref[0]
    xn = _rmsnorm(x, nw_ref[...]).astype(BF16)

    def rest(off, width):
        return _dot(xn, wr_ref[:, off:off + width])

    g = _dot(xn, wg_ref[...]) + gb_ref[...]
    lane = lax.broadcasted_iota(jnp.int32, (tl, LANES), 1)
    gc = jnp.where(lane < NH, g, _log_sigmoid(g))
    row = lax.broadcasted_iota(jnp.int32, (tl, tl), 0)
    col = lax.broadcasted_iota(jnp.int32, (tl, tl), 1)
    causal = row >= col
    tril = jnp.where(causal, 1.0, 0.0).astype(BF16)
    hi, mid, lo = _split3(gc)
    bc = _dot(tril, hi) + _dot(tril, mid) + _dot(tril, lo)
    gt = jnp.where(lane < NH, gc, bc).T

    qkv = _dot(xn, wh_ref[...])

    for h in range(NH):
        q_f = qkv[:, OFF_Q + h * DQK:OFF_Q + (h + 1) * DQK]
        k_f = qkv[:, OFF_K + h * DQK:OFF_K + (h + 1) * DQK] * K_SCALE
        v_f = qkv[:, OFF_V + h * DHV:OFF_V + (h + 1) * DHV]
        q_b = q_f.astype(BF16)
        k_b = k_f.astype(BF16)
        v_b = v_f.astype(BF16)
        c_old = c_ref[0, 0, h]
        n_old = n_ref[0, 0, h:h + 1, :]
        m_old = m_ref[0, h:h + 1, 0:1]

        b_c = bc[:, NH + h:NH + h + 1]
        ig_c = gc[:, h:h + 1]
        ig_r = gt[h:h + 1, :]
        b_r = gt[NH + h:NH + h + 1, :]
        log_d = jnp.where(causal, b_c - b_r + ig_r, NEG_BIG)
        inter = b_c + m_old
        m_t = jnp.maximum(inter, jnp.max(log_d, axis=-1, keepdims=True))
        d_m = jnp.exp(log_d - m_t)
        w_int = jnp.exp(inter - m_t)
        s = _dot_nt(q_b, k_b) * d_m
        num = w_int * _dot_nt(q_b, c_old.astype(BF16)) + _dot(s.astype(BF16), v_b)
        den = w_int * jnp.sum(q_f * n_old, axis=-1, keepdims=True) + jnp.sum(s, axis=-1, keepdims=True)
        h_t = num / jnp.maximum(jnp.abs(den), jnp.exp(-m_t))

        b_end = b_c[tl - 1:tl, :]
        inter_end = b_end + m_old
        m_new = jnp.maximum(inter_end, jnp.max(b_end - b_r + ig_r, axis=-1, keepdims=True))
        w_end = jnp.exp(b_end - b_c + ig_c - m_new)
        f_end = jnp.exp(inter_end - m_new)
        c_ref[0, 0, h] = f_end * c_old + _dot_tn((w_end * v_f).astype(BF16), k_b)
        n_ref[0, 0, h:h + 1, :] = f_end * n_old + jnp.sum(w_end * k_f, axis=0, keepdims=True)
        m_ref[0, h:h + 1, :] = jnp.broadcast_to(m_new, (1, LANES))

        hn = h_t * lax.rsqrt(jnp.mean(h_t * h_t, axis=-1, keepdims=True) + EPS)
        hn = hn * hnw_ref[:, h * DHV:(h + 1) * DHV]
        o_h = rest(R_O + h * DHV, DHV)
        zm_h = rest(R_ZM + h * DHV, DHV)
        hg_s[:, h * DHV:(h + 1) * DHV] = (hn * _sigmoid(o_h) * _silu(zm_h)).astype(BF16)

    y_m = _dot(hg_s[...], wpm_ref[...])

    u = rest(R_CG, D_C) * rest(R_XC, D_C)
    ubuf_s[SUBLANES:SUBLANES + tl, :] = u
    cv = cw_ref[0:1, :] * ubuf_s[SUBLANES - 2:SUBLANES - 2 + tl, :]
    cv = cv + cw_ref[1:2, :] * ubuf_s[SUBLANES - 1:SUBLANES - 1 + tl, :]
    cv = cv + cw_ref[2:3, :] * u
    yc_in = (_silu(rest(R_ZC, D_C)) * rest(R_BG, D_C) * cv).astype(BF16)
    y_c = _dot(yc_in, wpc_ref[...])
    ubuf_s[0:SUBLANES, :] = ubuf_s[tl:tl + SUBLANES, :]

    @pl.when(l == pl.num_programs(1) - 1)
    def _():
        conv_ref[0, 0] = ubuf_s[SUBLANES - (CONV_W - 1):SUBLANES, :]

    mix = (_sigmoid(rest(R_GM, D_MODEL)) * y_m + _sigmoid(rest(R_GC, D_MODEL)) * y_c).astype(BF16)
    out = x + _dot(mix, wout_ref[...])
    y_ref[0] = _rmsnorm(out, fnw_ref[...])


def _resident(shape):
    return pl.BlockSpec(shape, lambda *_: (0,) * len(shape), pipeline_mode=pl.Buffered(1))


def _prompt_call(x, wh, wg, wr, wpm, wpc, wout, nw, gb, hnw, cw, fnw):
    bsz, seq, _ = x.shape
    tl = PROMPT_BLOCK
    grid = (bsz, seq // tl)
    out_shape = (
        jax.ShapeDtypeStruct((bsz, seq, D_MODEL), F32),
        jax.ShapeDtypeStruct((1, bsz, NH, DHV, DQK), F32),
        jax.ShapeDtypeStruct((1, bsz, NH, DQK), F32),
        jax.ShapeDtypeStruct((bsz, SUBLANES, LANES), F32),
        jax.ShapeDtypeStruct((1, bsz, CONV_W - 1, D_C), F32),
    )
    in_specs = [
        pl.BlockSpec((1, tl, D_MODEL), lambda b, l: (b, l, 0)),
        _resident((D_MODEL, N_HEAD)),
        _resident((D_MODEL, LANES)),
        _resident((D_MODEL, N_REST)),
        _resident((D_M, D_MODEL)),
        _resident((D_C, D_MODEL)),
        _resident((D_MODEL, D_MODEL)),
        _resident((1, D_MODEL)),
        _resident((1, LANES)),
        _resident((1, D_M)),
        _resident((CONV_W, D_C)),
        _resident((1, D_MODEL)),
    ]
    out_specs = (
        pl.BlockSpec((1, tl, D_MODEL), lambda b, l: (b, l, 0)),
        pl.BlockSpec((1, 1, NH, DHV, DQK), lambda b, l: (0, b, 0, 0, 0)),
        pl.BlockSpec((1, 1, NH, DQK), lambda b, l: (0, b, 0, 0)),
        pl.BlockSpec((1, SUBLANES, LANES), lambda b, l: (b, 0, 0)),
        pl.BlockSpec((1, 1, CONV_W - 1, D_C), lambda b, l: (0, b, 0, 0)),
    )
    return pl.pallas_call(
        _prompt_kernel,
        grid=grid,
        in_specs=in_specs,
        out_specs=out_specs,
        out_shape=out_shape,
        scratch_shapes=[
            pltpu.VMEM((tl, D_M), BF16),
            pltpu.VMEM((tl + 2 * SUBLANES, D_C), F32),
        ],
        compiler_params=pltpu.CompilerParams(
            dimension_semantics=("arbitrary", "arbitrary"),
            vmem_limit_bytes=VMEM_LIMIT_PROMPT),
        name="prompt_layer",
    )(x, wh, wg, wr, wpm, wpc, wout, nw, gb, hnw, cw, fnw)


def _sample_proj_kernel(x_ref, nw_ref, wh_ref, wg_ref, wr_ref, qk_ref, vt_ref, g_ref, r_ref):
    xn = _rmsnorm(x_ref[...], nw_ref[...]).astype(BF16)

    @pl.when(pl.program_id(0) == 0)
    def _():
        head = _dot(xn, wh_ref[...])
        qk_ref[...] = head[:, OFF_Q:OFF_V]
        vt_ref[...] = head[:, OFF_V:N_HEAD].T
        g_ref[...] = _dot(xn, wg_ref[...])

    r_ref[...] = _dot(xn, wr_ref[...])


def _sample_proj_call(x, nw, wh, wg, wr):
    nseq = x.shape[0]
    cols = SAMPLE_PROJ_COLS
    const = lambda shape: pl.BlockSpec(shape, lambda j: (0,) * len(shape))
    return pl.pallas_call(
        _sample_proj_kernel,
        grid=(N_REST // cols,),
        in_specs=[
            const((nseq, D_MODEL)),
            const((1, D_MODEL)),
            const((D_MODEL, N_HEAD)),
            const((D_MODEL, LANES)),
            pl.BlockSpec((D_MODEL, cols), lambda j: (0, j)),
        ],
        out_specs=(
            const((nseq, 2 * D_QK)),
            const((D_M, nseq)),
            const((nseq, LANES)),
            pl.BlockSpec((nseq, cols), lambda j: (0, j)),
        ),
        out_shape=(
            jax.ShapeDtypeStruct((nseq, 2 * D_QK), F32),
            jax.ShapeDtypeStruct((D_M, nseq), F32),
            jax.ShapeDtypeStruct((nseq, LANES), F32),
            jax.ShapeDtypeStruct((nseq, N_REST), F32),
        ),
        compiler_params=pltpu.CompilerParams(dimension_semantics=("arbitrary",),
                                             vmem_limit_bytes=VMEM_LIMIT_SAMPLE),
        name="sample_in_proj",
    )(x, nw, wh, wg, wr)


def _sample_state_kernel(qk_ref, vt_ref, g_ref, gb_ref, c_ref, n_ref, m_ref,
                         ht_ref, co_ref, no_ref, mo_ref,
                         q_s, kw_s, f_s, zt_s):
    nseq = qk_ref.shape[0]
    nb = c_ref.shape[1]
    i = pl.program_id(0)
    lane = lax.broadcasted_iota(jnp.int32, (DHV, nseq), 1)

    @pl.when(i == 0)
    def _():
        g = g_ref[...] + gb_ref[...]
        ig = g[:, 0:NH]
        lf = _log_sigmoid(g[:, NH:2 * NH])
        inter = lf + m_ref[0]
        m_t = jnp.maximum(inter, ig)
        w_end = jnp.exp(ig - m_t)
        f_end = jnp.exp(inter - m_t)
        floor = jnp.exp(-m_t)
        mo_ref[0] = m_t
        lane_z = lax.broadcasted_iota(jnp.int32, (nseq, LANES), 1)
        z = jnp.zeros((nseq, LANES), F32)
        for h in range(NH):
            q_h = qk_ref[:, h * DQK:(h + 1) * DQK]
            k_h = qk_ref[:, D_QK + h * DQK:D_QK + (h + 1) * DQK] * K_SCALE
            n_h = n_ref[:, h * DQK:(h + 1) * DQK]
            w_h = w_end[:, h:h + 1]
            f_h = f_end[:, h:h + 1]
            s = jnp.sum(q_h * k_h, axis=-1, keepdims=True) * w_h
            den = f_h * jnp.sum(n_h * q_h, axis=-1, keepdims=True) + s
            denom = jnp.maximum(jnp.abs(den), floor[:, h:h + 1])
            no_ref[:, h * DQK:(h + 1) * DQK] = f_h * n_h + w_h * k_h
            q_s[h] = q_h.astype(BF16)
            kw_s[h] = (w_h * k_h).astype(BF16)
            f_s[h] = jnp.broadcast_to(f_h, (nseq, LANES))
            z = jnp.where(lane_z == h, f_h, z)
            z = jnp.where(lane_z == NH + h, s, z)
            z = jnp.where(lane_z == 2 * NH + h, denom, z)
        zt_s[...] = z.T
        ht_ref[...] = jnp.zeros_like(ht_ref)

    for h in range(NH):
        acc = ht_ref[h * DHV:(h + 1) * DHV, :]
        v_t = vt_ref[h * DHV:(h + 1) * DHV, :]
        for j in range(nb):
            b = i * nb + j
            sel = lane == b
            c_old = c_ref[0, j, h]
            acc = jnp.where(sel, _dot_nt(c_old.astype(BF16), q_s[h]), acc)
            upd = _dot(jnp.where(sel, v_t, 0.0).astype(BF16), kw_s[h])
            co_ref[0, j, h] = f_s[h, pl.ds(b, 1), :] * c_old + upd
        ht_ref[h * DHV:(h + 1) * DHV, :] = acc

    @pl.when(i == pl.num_programs(0) - 1)
    def _():
        for h in range(NH):
            num = (zt_s[h:h + 1, :] * ht_ref[h * DHV:(h + 1) * DHV, :]
                   + zt_s[NH + h:NH + h + 1, :] * vt_ref[h * DHV:(h + 1) * DHV, :])
            ht_ref[h * DHV:(h + 1) * DHV, :] = num / zt_s[2 * NH + h:2 * NH + h + 1, :]


def _sample_state_call(qk, vt, g, gb, c0, n0, m0):
    nseq = qk.shape[0]
    nb = SAMPLE_SEQ_BLOCK
    const = lambda shape: pl.BlockSpec(shape, lambda i: (0,) * len(shape))
    c_spec = pl.BlockSpec((1, nb, NH, DHV, DQK), lambda i: (0, i, 0, 0, 0))
    return pl.pallas_call(
        _sample_state_kernel,
        grid=(nseq // nb,),
        in_specs=[
            const((nseq, 2 * D_QK)),
            const((D_M, nseq)),
            const((nseq, LANES)),
            const((1, LANES)),
            c_spec,
            const((nseq, D_QK)),
            const((1, nseq, NH)),
        ],
        out_specs=(
            const((D_M, nseq)),
            c_spec,
            const((nseq, D_QK)),
            const((1, nseq, NH)),
        ),
        out_shape=(
            jax.ShapeDtypeStruct((D_M, nseq), F32),
            jax.ShapeDtypeStruct(c0.shape, F32),
            jax.ShapeDtypeStruct((nseq, D_QK), F32),
            jax.ShapeDtypeStruct(m0.shape, F32),
        ),
        scratch_shapes=[
            pltpu.VMEM((NH, nseq, DQK), BF16),
            pltpu.VMEM((NH, nseq, DQK), BF16),
            pltpu.VMEM((NH, nseq, LANES), F32),
            pltpu.VMEM((LANES, nseq), F32),
        ],
        compiler_params=pltpu.CompilerParams(dimension_semantics=("arbitrary",),
                                             vmem_limit_bytes=VMEM_LIMIT_SAMPLE),
        name="sample_state",
    )(qk, vt, g, gb, c0, n0, m0)


def _sample_tail_kernel(x_ref, r_ref, ht_ref, conv_ref, wpm_ref, wpc_ref, wout_ref, hnw_ref, cw_ref, fnw_ref,
                        y_ref, convo_ref):
    def piece(off, width):
        return r_ref[:, off:off + width]

    for h in range(NH):
        h_t = ht_ref[h * DHV:(h + 1) * DHV, :].T
        hn = h_t * lax.rsqrt(jnp.mean(h_t * h_t, axis=-1, keepdims=True) + EPS)
        hn = hn * hnw_ref[:, h * DHV:(h + 1) * DHV]
        o_h = piece(R_O + h * DHV, DHV)
        zm_h = piece(R_ZM + h * DHV, DHV)
        hg = (hn * _sigmoid(o_h) * _silu(zm_h)).astype(BF16)
        part = _dot(hg, wpm_ref[h * DHV:(h + 1) * DHV, :])
        y_m = part if h == 0 else y_m + part

    u = piece(R_CG, D_C) * piece(R_XC, D_C)
    cv = cw_ref[0:1, :] * conv_ref[:, 0:D_C]
    cv = cv + cw_ref[1:2, :] * conv_ref[:, D_C:2 * D_C]
    cv = cv + cw_ref[2:3, :] * u
    convo_ref[:, 0:D_C] = conv_ref[:, D_C:2 * D_C]
    convo_ref[:, D_C:2 * D_C] = u
    yc_in = (_silu(piece(R_ZC, D_C)) * piece(R_BG, D_C) * cv).astype(BF16)
    y_c = _dot(yc_in, wpc_ref[...])
    mix = (_sigmoid(piece(R_GM, D_MODEL)) * y_m + _sigmoid(piece(R_GC, D_MODEL)) * y_c).astype(BF16)
    out = x_ref[...] + _dot(mix, wout_ref[...])
    y_ref[...] = _rmsnorm(out, fnw_ref[...])


def _sample_tail_call(x, rest, h_t, conv0, wpm, wpc, wout, hnw, cw, fnw):
    nseq = x.shape[0]
    args = (x, rest, h_t, conv0, wpm, wpc, wout, hnw, cw, fnw)
    full = lambda a: pl.BlockSpec(a.shape, lambda i: (0,) * a.ndim)
    return pl.pallas_call(
        _sample_tail_kernel,
        grid=(1,),
        in_specs=[full(a) for a in args],
        out_specs=(pl.BlockSpec((nseq, D_MODEL), lambda i: (0, 0)),
                   pl.BlockSpec((nseq, (CONV_W - 1) * D_C), lambda i: (0, 0))),
        out_shape=(jax.ShapeDtypeStruct((nseq, D_MODEL), F32),
                   jax.ShapeDtypeStruct((nseq, (CONV_W - 1) * D_C), F32)),
        compiler_params=pltpu.CompilerParams(dimension_semantics=("arbitrary",),
                                             vmem_limit_bytes=VMEM_LIMIT_SAMPLE),
        name="sample_tail",
    )(*args)


def kernel(x_prompt, x_sample, state_mlstm_C, state_mlstm_n, state_mlstm_m, state_conv, norm_w, w_in, b_i, b_f,
           head_norm_w, conv_w, w_proj_m, w_proj_c, w_out, final_norm_w):
    depth = norm_w.shape[0]
    assert depth == 1, "single-layer trunk"
    bsz = x_prompt.shape[0]
    nseq = x_sample.shape[0]

    w0 = w_in[0]
    wh = w0[:, :N_HEAD].astype(BF16)
    wg = jnp.pad(w0[:, N_HEAD:N_HEAD + N_GATE_COLS], ((0, 0), (0, LANES - N_GATE_COLS))).astype(BF16)
    wr = w0[:, N_HEAD + N_GATE_COLS:].astype(BF16)
    wpm = w_proj_m[0].astype(BF16)
    wpc = w_proj_c[0].astype(BF16)
    wout = w_out[0].astype(BF16)
    gb = jnp.pad(jnp.concatenate([b_i[0], b_f[0]]), (0, LANES - N_GATE_COLS)).reshape(1, LANES)
    nw = norm_w[0].reshape(1, D_MODEL)
    hnw = head_norm_w[0].reshape(1, D_M)
    cw = conv_w[0]
    fnw = final_norm_w.reshape(1, D_MODEL)

    y_p, c_p, n_p, m_p, conv_p = _prompt_call(x_prompt, wh, wg, wr, wpm, wpc, wout, nw, gb, hnw, cw, fnw)
    m_p = m_p[:, :NH, 0].reshape(1, bsz, NH)

    xs = x_sample.reshape(nseq, D_MODEL)
    qk_s, vt_s, g_s, rest_s = _sample_proj_call(xs, nw, wh, wg, wr)
    ht_s, c_s, n_s, m_s = _sample_state_call(
        qk_s, vt_s, g_s, gb, state_mlstm_C, state_mlstm_n[0].reshape(nseq, D_QK), state_mlstm_m)
    conv0 = state_conv[0].reshape(nseq, (CONV_W - 1) * D_C)
    y_s, conv_s = _sample_tail_call(xs, rest_s, ht_s, conv0, wpm, wpc, wout, hnw, cw, fnw)

    return (y_p, y_s.reshape(nseq, 1, D_MODEL), c_p, n_p, m_p, conv_p,
            c_s, n_s.reshape(1, nseq, NH, DQK), m_s,
            conv_s.reshape(1, nseq, CONV_W - 1, D_C))
```

```python
import jax
import jax.numpy as jnp
from jax import lax
from jax.experimental import pallas as pl
from jax.experimental.pallas import tpu as pltpu

F32 = jnp.float32
BF16 = jnp.bfloat16

D_MODEL = 1024
NH = 4
DHV = 256
DQK = 128
D_QK = NH * DQK
D_M = NH * DHV
D_C = D_MODEL
CONV_W = 3
EPS = 1e-6
NEG_BIG = -1e30
K_SCALE = DQK ** -0.5

LANES = 128
SUBLANES = 8

OFF_Q = 0
OFF_K = OFF_Q + D_QK
OFF_V = OFF_K + D_QK
N_HEAD = OFF_V + D_M
N_GATE_COLS = 2 * NH
R_O = 0
R_ZM = R_O + D_M
R_BG = R_ZM + D_M
R_CG = R_BG + D_C
R_XC = R_CG + D_C
R_ZC = R_XC + D_C
R_GM = R_ZC + D_C
R_GC = R_GM + D_MODEL
N_REST = R_GC + D_MODEL

PROMPT_BLOCK = 512
MLSTM_CHUNK = 256
SAMPLE_SEQ_BLOCK = 8
SAMPLE_PROJ_COLS = 1024
VMEM_LIMIT_PROMPT = 60 * 1024 * 1024
VMEM_LIMIT_SAMPLE = 48 * 1024 * 1024


def _sigmoid(x):
    return 1.0 / (1.0 + jnp.exp(-x))


def _silu(x):
    return x * _sigmoid(x)


def _log_sigmoid(x):
    return jnp.minimum(x, 0.0) - jnp.log1p(jnp.exp(-jnp.abs(x)))


def _rmsnorm(x, w):
    return x * lax.rsqrt(jnp.mean(x * x, axis=-1, keepdims=True) + EPS) * w


def _dot(a, b):
    return jnp.dot(a, b, preferred_element_type=F32)


def _dot_nt(a, b):
    return lax.dot_general(a, b, (((1,), (1,)), ((), ())), preferred_element_type=F32)


def _dot_tn(a, b):
    return lax.dot_general(a, b, (((0,), (0,)), ((), ())), preferred_element_type=F32)


def _split3(x):
    hi = x.astype(BF16)
    r = x - hi.astype(F32)
    mid = r.astype(BF16)
    lo = (r - mid.astype(F32)).astype(BF16)
    return hi, mid, lo


def _prompt_kernel(x_ref, wh_ref, wg_ref, wr_ref, wpm_ref, wpc_ref, wout_ref,
                   nw_ref, gb_ref, hnw_ref, cw_ref, fnw_ref,
                   y_ref, c_ref, n_ref, m_ref, conv_ref,
                   hg_s, ubuf_s):
    tl = x_ref.shape[1]
    l = pl.program_id(1)

    @pl.when(l == 0)
    def _():
        c_ref[...] = jnp.zeros_like(c_ref)
        n_ref[...] = jnp.zeros_like(n_ref)
        m_ref[...] = jnp.zeros_like(m_ref)
        ubuf_s[0:SUBLANES, :] = jnp.zeros((SUBLANES, D_C), F32)

    x = x_ref[0]
    xn = _rmsnorm(x, nw_ref[...]).astype(BF16)

    def rest(off, width):
        return _dot(xn, wr_ref[:, off:off + width])

    g = _dot(xn, wg_ref[...]) + gb_ref[...]
    gc = jnp.where(lax.broadcasted_iota(jnp.int32, (tl, LANES), 1) < NH, g, _log_sigmoid(g))
    qkv = _dot(xn, wh_ref[...])

    ch = MLSTM_CHUNK
    lane = lax.broadcasted_iota(jnp.int32, (ch, LANES), 1)
    row = lax.broadcasted_iota(jnp.int32, (ch, ch), 0)
    col = lax.broadcasted_iota(jnp.int32, (ch, ch), 1)
    causal = row >= col
    tril = jnp.where(causal, 1.0, 0.0).astype(BF16)
    chunk_gates = []
    for r0 in range(0, tl, ch):
        gc_c = gc[r0:r0 + ch, :]
        hi, mid, lo = _split3(gc_c)
        bc = _dot(tril, hi) + _dot(tril, mid) + _dot(tril, lo)
        gt = jnp.where(lane < NH, gc_c, bc).T
        chunk_gates.append((gc_c, bc, gt))

    def mlstm_unit(ci, h):
        r0 = ci * ch
        gc_c, bc, gt = chunk_gates[ci]
        xn_c = xn[r0:r0 + ch, :]
        sig_o = _sigmoid(_dot(xn_c, wr_ref[:, R_O + h * DHV:R_O + (h + 1) * DHV]))
        silu_z = _silu(_dot(xn_c, wr_ref[:, R_ZM + h * DHV:R_ZM + (h + 1) * DHV]))
        q_f = qkv[r0:r0 + ch, OFF_Q + h * DQK:OFF_Q + (h + 1) * DQK]
        k_f = qkv[r0:r0 + ch, OFF_K + h * DQK:OFF_K + (h + 1) * DQK] * K_SCALE
        v_f = qkv[r0:r0 + ch, OFF_V + h * DHV:OFF_V + (h + 1) * DHV]
        q_b = q_f.astype(BF16)
        k_b = k_f.astype(BF16)
        v_b = v_f.astype(BF16)
        c_old = c_ref[0, 0, h]
        n_old = n_ref[0, 0, h:h + 1, :]
        m_old = m_ref[0, h:h + 1, 0:1]

        b_c = bc[:, NH + h:NH + h + 1]
        ig_c = gc_c[:, h:h + 1]
        ig_r = gt[h:h + 1, :]
        b_r = gt[NH + h:NH + h + 1, :]
        log_d = jnp.where(causal, b_c - b_r + ig_r, NEG_BIG)
        inter = b_c + m_old
        m_t = jnp.maximum(inter, jnp.max(log_d, axis=-1, keepdims=True))
        d_m = jnp.exp(log_d - m_t)
        w_int = jnp.exp(inter - m_t)
        s = _dot_nt(q_b, k_b) * d_m
        num = w_int * _dot_nt(q_b, c_old.astype(BF16)) + _dot(s.astype(BF16), v_b)
        den = w_int * jnp.sum(q_f * n_old, axis=-1, keepdims=True) + jnp.sum(s, axis=-1, keepdims=True)
        h_t = num / jnp.maximum(jnp.abs(den), jnp.exp(-m_t))

        b_end = b_c[ch - 1:ch, :]
        inter_end = b_end + m_old
        m_new = jnp.maximum(inter_end, jnp.max(b_end - b_r + ig_r, axis=-1, keepdims=True))
        w_end = jnp.exp(b_end - b_c + ig_c - m_new)
        f_end = jnp.exp(inter_end - m_new)
        c_ref[0, 0, h] = f_end * c_old + _dot_tn((w_end * v_f).astype(BF16), k_b)
        n_ref[0, 0, h:h + 1, :] = f_end * n_old + jnp.sum(w_end * k_f, axis=0, keepdims=True)
        m_ref[0, h:h + 1, :] = jnp.broadcast_to(m_new, (1, LANES))

        hn = h_t * lax.rsqrt(jnp.mean(h_t * h_t, axis=-1, keepdims=True) + EPS)
        hn = hn * hnw_ref[:, h * DHV:(h + 1) * DHV]
        hg_s[r0:r0 + ch, h * DHV:(h + 1) * DHV] = (hn * sig_o * silu_z).astype(BF16)

    val = {}

    def conv_input():
        u = rest(R_CG, D_C) * rest(R_XC, D_C)
        ubuf_s[SUBLANES:SUBLANES + tl, :] = u
        cv = cw_ref[0:1, :] * ubuf_s[SUBLANES - 2:SUBLANES - 2 + tl, :]
        cv = cv + cw_ref[1:2, :] * ubuf_s[SUBLANES - 1:SUBLANES - 1 + tl, :]
        val["cv"] = cv + cw_ref[2:3, :] * u

    def conv_gate():
        val["yc_in"] = (_silu(rest(R_ZC, D_C)) * rest(R_BG, D_C) * val["cv"]).astype(BF16)

    def conv_proj():
        val["gy_c"] = _sigmoid(rest(R_GC, D_MODEL)) * _dot(val["yc_in"], wpc_ref[...])

    def merge_gate():
        val["sig_gm"] = _sigmoid(rest(R_GM, D_MODEL))

    fillers = [conv_input, conv_gate, conv_proj, merge_gate]
    units = [(ci, h) for ci in range(tl // ch) for h in range(NH)]
    per = len(units) // len(fillers)
    for i, (ci, h) in enumerate(units):
        mlstm_unit(ci, h)
        if (i + 1) % per == 0:
            fillers[(i + 1) // per - 1]()

    ubuf_s[0:SUBLANES, :] = ubuf_s[tl:tl + SUBLANES, :]

    @pl.when(l == pl.num_programs(1) - 1)
    def _():
        conv_ref[0, 0] = ubuf_s[SUBLANES - (CONV_W - 1):SUBLANES, :]

    y_m = _dot(hg_s[...], wpm_ref[...])
    mix = (val["sig_gm"] * y_m + val["gy_c"]).astype(BF16)
    out = x + _dot(mix, wout_ref[...])
    y_ref[0] = _rmsnorm(out, fnw_ref[...])


def _resident(shape):
    return pl.BlockSpec(shape, lambda *_: (0,) * len(shape), pipeline_mode=pl.Buffered(1))


def _prompt_call(x, wh, wg, wr, wpm, wpc, wout, nw, gb, hnw, cw, fnw):
    bsz, seq, _ = x.shape
    tl = PROMPT_BLOCK
    grid = (bsz, seq // tl)
    out_shape = (
        jax.ShapeDtypeStruct((bsz, seq, D_MODEL), F32),
        jax.ShapeDtypeStruct((1, bsz, NH, DHV, DQK), F32),
        jax.ShapeDtypeStruct((1, bsz, NH, DQK), F32),
        jax.ShapeDtypeStruct((bsz, SUBLANES, LANES), F32),
        jax.ShapeDtypeStruct((1, bsz, CONV_W - 1, D_C), F32),
    )
    in_specs = [
        pl.BlockSpec((1, tl, D_MODEL), lambda b, l: (b, l, 0)),
        _resident((D_MODEL, N_HEAD)),
        _resident((D_MODEL, LANES)),
        _resident((D_MODEL, N_REST)),
        _resident((D_M, D_MODEL)),
        _resident((D_C, D_MODEL)),
        _resident((D_MODEL, D_MODEL)),
        _resident((1, D_MODEL)),
        _resident((1, LANES)),
        _resident((1, D_M)),
        _resident((CONV_W, D_C)),
        _resident((1, D_MODEL)),
    ]
    out_specs = (
        pl.BlockSpec((1, tl, D_MODEL), lambda b, l: (b, l, 0)),
        pl.BlockSpec((1, 1, NH, DHV, DQK), lambda b, l: (0, b, 0, 0, 0)),
        pl.BlockSpec((1, 1, NH, DQK), lambda b, l: (0, b, 0, 0)),
        pl.BlockSpec((1, SUBLANES, LANES), lambda b, l: (b, 0, 0)),
        pl.BlockSpec((1, 1, CONV_W - 1, D_C), lambda b, l: (0, b, 0, 0)),
    )
    return pl.pallas_call(
        _prompt_kernel,
        grid=grid,
        in_specs=in_specs,
        out_specs=out_specs,
        out_shape=out_shape,
        scratch_shapes=[
            pltpu.VMEM((tl, D_M), BF16),
            pltpu.VMEM((tl + 2 * SUBLANES, D_C), F32),
        ],
        compiler_params=pltpu.CompilerParams(
            dimension_semantics=("arbitrary", "arbitrary"),
            vmem_limit_bytes=VMEM_LIMIT_PROMPT),
        name="prompt_layer",
    )(x, wh, wg, wr, wpm, wpc, wout, nw, gb, hnw, cw, fnw)


def _sample_proj_kernel(x_ref, nw_ref, wh_ref, wg_ref, wr_ref, qk_ref, vt_ref, g_ref, r_ref):
    xn = _rmsnorm(x_ref[...], nw_ref[...]).astype(BF16)

    @pl.when(pl.program_id(0) == 0)
    def _():
        head = _dot(xn, wh_ref[...])
        qk_ref[...] = head[:, OFF_Q:OFF_V]
        vt_ref[...] = head[:, OFF_V:N_HEAD].T
        g_ref[...] = _dot(xn, wg_ref[...])

    r_ref[...] = _dot(xn, wr_ref[...])


def _sample_proj_call(x, nw, wh, wg, wr):
    nseq = x.shape[0]
    cols = SAMPLE_PROJ_COLS
    const = lambda shape: pl.BlockSpec(shape, lambda j: (0,) * len(shape))
    return pl.pallas_call(
        _sample_proj_kernel,
        grid=(N_REST // cols,),
        in_specs=[
            const((nseq, D_MODEL)),
            const((1, D_MODEL)),
            const((D_MODEL, N_HEAD)),
            const((D_MODEL, LANES)),
            pl.BlockSpec((D_MODEL, cols), lambda j: (0, j)),
        ],
        out_specs=(
            const((nseq, 2 * D_QK)),
            const((D_M, nseq)),
            const((nseq, LANES)),
            pl.BlockSpec((nseq, cols), lambda j: (0, j)),
        ),
        out_shape=(
            jax.ShapeDtypeStruct((nseq, 2 * D_QK), F32),
            jax.ShapeDtypeStruct((D_M, nseq), F32),
            jax.ShapeDtypeStruct((nseq, LANES), F32),
            jax.ShapeDtypeStruct((nseq, N_REST), F32),
        ),
        compiler_params=pltpu.CompilerParams(dimension_semantics=("arbitrary",),
                                             vmem_limit_bytes=VMEM_LIMIT_SAMPLE),
        name="sample_in_proj",
    )(x, nw, wh, wg, wr)


def _sample_state_kernel(qk_ref, vt_ref, g_ref, gb_ref, c_ref, n_ref, m_ref,
                         ht_ref, co_ref, no_ref, mo_ref,
                         q_s, kw_s, f_s, zt_s):
    nseq = qk_ref.shape[0]
    nb = c_ref.shape[1]
    i = pl.program_id(0)
    lane = lax.broadcasted_iota(jnp.int32, (DHV, nseq), 1)

    @pl.when(i == 0)
    def _():
        g = g_ref[...] + gb_ref[...]
        ig = g[:, 0:NH]
        lf = _log_sigmoid(g[:, NH:2 * NH])
        inter = lf + m_ref[0]
        m_t = jnp.maximum(inter, ig)
        w_end = jnp.exp(ig - m_t)
        f_end = jnp.exp(inter - m_t)
        floor = jnp.exp(-m_t)
        mo_ref[0] = m_t
        lane_z = lax.broadcasted_iota(jnp.int32, (nseq, LANES), 1)
        z = jnp.zeros((nseq, LANES), F32)
        for h in range(NH):
            q_h = qk_ref[:, h * DQK:(h + 1) * DQK]
            k_h = qk_ref[:, D_QK + h * DQK:D_QK + (h + 1) * DQK] * K_SCALE
            n_h = n_ref[:, h * DQK:(h + 1) * DQK]
            w_h = w_end[:, h:h + 1]
            f_h = f_end[:, h:h + 1]
            s = jnp.sum(q_h * k_h, axis=-1, keepdims=True) * w_h
            den = f_h * jnp.sum(n_h * q_h, axis=-1, keepdims=True) + s
            denom = jnp.maximum(jnp.abs(den), floor[:, h:h + 1])
            no_ref[:, h * DQK:(h + 1) * DQK] = f_h * n_h + w_h * k_h
            q_s[h] = q_h.astype(BF16)
            kw_s[h] = (w_h * k_h).astype(BF16)
            f_s[h] = jnp.broadcast_to(f_h, (nseq, LANES))
            z = jnp.where(lane_z == h, f_h, z)
            z = jnp.where(lane_z == NH + h, s, z)
            z = jnp.where(lane_z == 2 * NH + h, denom, z)
        zt_s[...] = z.T
        ht_ref[...] = jnp.zeros_like(ht_ref)

    for h in range(NH):
        acc = ht_ref[h * DHV:(h + 1) * DHV, :]
        v_t = vt_ref[h * DHV:(h + 1) * DHV, :]
        for j in range(nb):
            b = i * nb + j
            sel = lane == b
            c_old = c_ref[0, j, h]
            acc = jnp.where(sel, _dot_nt(c_old.astype(BF16), q_s[h]), acc)
            upd = _dot(jnp.where(sel, v_t, 0.0).astype(BF16), kw_s[h])
            co_ref[0, j, h] = f_s[h, pl.ds(b, 1), :] * c_old + upd
        ht_ref[h * DHV:(h + 1) * DHV, :] = acc

    @pl.when(i == pl.num_programs(0) - 1)
    def _():
        for h in range(NH):
            num = (zt_s[h:h + 1, :] * ht_ref[h * DHV:(h + 1) * DHV, :]
                   + zt_s[NH + h:NH + h + 1, :] * vt_ref[h * DHV:(h + 1) * DHV, :])
            ht_ref[h * DHV:(h + 1) * DHV, :] = num / zt_s[2 * NH + h:2 * NH + h + 1, :]


def _sample_state_call(qk, vt, g, gb, c0, n0, m0):
    nseq = qk.shape[0]
    nb = SAMPLE_SEQ_BLOCK
    const = lambda shape: pl.BlockSpec(shape, lambda i: (0,) * len(shape))
    c_spec = pl.BlockSpec((1, nb, NH, DHV, DQK), lambda i: (0, i, 0, 0, 0))
    return pl.pallas_call(
        _sample_state_kernel,
        grid=(nseq // nb,),
        in_specs=[
            const((nseq, 2 * D_QK)),
            const((D_M, nseq)),
            const((nseq, LANES)),
            const((1, LANES)),
            c_spec,
            const((nseq, D_QK)),
            const((1, nseq, NH)),
        ],
        out_specs=(
            const((D_M, nseq)),
            c_spec,
            const((nseq, D_QK)),
            const((1, nseq, NH)),
        ),
        out_shape=(
            jax.ShapeDtypeStruct((D_M, nseq), F32),
            jax.ShapeDtypeStruct(c0.shape, F32),
            jax.ShapeDtypeStruct((nseq, D_QK), F32),
            jax.ShapeDtypeStruct(m0.shape, F32),
        ),
        scratch_shapes=[
            pltpu.VMEM((NH, nseq, DQK), BF16),
            pltpu.VMEM((NH, nseq, DQK), BF16),
            pltpu.VMEM((NH, nseq, LANES), F32),
            pltpu.VMEM((LANES, nseq), F32),
        ],
        compiler_params=pltpu.CompilerParams(dimension_semantics=("arbitrary",),
                                             vmem_limit_bytes=VMEM_LIMIT_SAMPLE),
        name="sample_state",
    )(qk, vt, g, gb, c0, n0, m0)


def _sample_tail_kernel(x_ref, r_ref, ht_ref, conv_ref, wpm_ref, wpc_ref, wout_ref, hnw_ref, cw_ref, fnw_ref,
                        y_ref, convo_ref):
    def piece(off, width):
        return r_ref[:, off:off + width]

    for h in range(NH):
        h_t = ht_ref[h * DHV:(h + 1) * DHV, :].T
        hn = h_t * lax.rsqrt(jnp.mean(h_t * h_t, axis=-1, keepdims=True) + EPS)
        hn = hn * hnw_ref[:, h * DHV:(h + 1) * DHV]
        o_h = piece(R_O + h * DHV, DHV)
        zm_h = piece(R_ZM + h * DHV, DHV)
        hg = (hn * _sigmoid(o_h) * _silu(zm_h)).astype(BF16)
        part = _dot(hg, wpm_ref[h * DHV:(h + 1) * DHV, :])
        y_m = part if h == 0 else y_m + part

    u = piece(R_CG, D_C) * piece(R_XC, D_C)
    cv = cw_ref[0:1, :] * conv_ref[:, 0:D_C]
    cv = cv + cw_ref[1:2, :] * conv_ref[:, D_C:2 * D_C]
    cv = cv + cw_ref[2:3, :] * u
    convo_ref[:, 0:D_C] = conv_ref[:, D_C:2 * D_C]
    convo_ref[:, D_C:2 * D_C] = u
    yc_in = (_silu(piece(R_ZC, D_C)) * piece(R_BG, D_C) * cv).astype(BF16)
    y_c = _dot(yc_in, wpc_ref[...])
    mix = (_sigmoid(piece(R_GM, D_MODEL)) * y_m + _sigmoid(piece(R_GC, D_MODEL)) * y_c).astype(BF16)
    out = x_ref[...] + _dot(mix, wout_ref[...])
    y_ref[...] = _rmsnorm(out, fnw_ref[...])


def _sample_tail_call(x, rest, h_t, conv0, wpm, wpc, wout, hnw, cw, fnw):
    nseq = x.shape[0]
    args = (x, rest, h_t, conv0, wpm, wpc, wout, hnw, cw, fnw)
    full = lambda a: pl.BlockSpec(a.shape, lambda i: (0,) * a.ndim)
    return pl.pallas_call(
        _sample_tail_kernel,
        grid=(1,),
        in_specs=[full(a) for a in args],
        out_specs=(pl.BlockSpec((nseq, D_MODEL), lambda i: (0, 0)),
                   pl.BlockSpec((nseq, (CONV_W - 1) * D_C), lambda i: (0, 0))),
        out_shape=(jax.ShapeDtypeStruct((nseq, D_MODEL), F32),
                   jax.ShapeDtypeStruct((nseq, (CONV_W - 1) * D_C), F32)),
        compiler_params=pltpu.CompilerParams(dimension_semantics=("arbitrary",),
                                             vmem_limit_bytes=VMEM_LIMIT_SAMPLE),
        name="sample_tail",
    )(*args)


def kernel(x_prompt, x_sample, state_mlstm_C, state_mlstm_n, state_mlstm_m, state_conv, norm_w, w_in, b_i, b_f,
           head_norm_w, conv_w, w_proj_m, w_proj_c, w_out, final_norm_w):
    depth = norm_w.shape[0]
    assert depth == 1, "single-layer trunk"
    bsz = x_prompt.shape[0]
    nseq = x_sample.shape[0]

    w0 = w_in[0]
    wh = w0[:, :N_HEAD].astype(BF16)
    wg = jnp.pad(w0[:, N_HEAD:N_HEAD + N_GATE_COLS], ((0, 0), (0, LANES - N_GATE_COLS))).astype(BF16)
    wr = w0[:, N_HEAD + N_GATE_COLS:].astype(BF16)
    wpm = w_proj_m[0].astype(BF16)
    wpc = w_proj_c[0].astype(BF16)
    wout = w_out[0].astype(BF16)
    gb = jnp.pad(jnp.concatenate([b_i[0], b_f[0]]), (0, LANES - N_GATE_COLS)).reshape(1, LANES)
    nw = norm_w[0].reshape(1, D_MODEL)
    hnw = head_norm_w[0].reshape(1, D_M)
    cw = conv_w[0]
    fnw = final_norm_w.reshape(1, D_MODEL)

    y_p, c_p, n_p, m_p, conv_p = _prompt_call(x_prompt, wh, wg, wr, wpm, wpc, wout, nw, gb, hnw, cw, fnw)
    m_p = m_p[:, :NH, 0].reshape(1, bsz, NH)

    xs = x_sample.reshape(nseq, D_MODEL)
    qk_s, vt_s, g_s, rest_s = _sample_proj_call(xs, nw, wh, wg, wr)
    ht_s, c_s, n_s, m_s = _sample_state_call(
        qk_s, vt_s, g_s, gb, state_mlstm_C, state_mlstm_n[0].reshape(nseq, D_QK), state_mlstm_m)
    conv0 = state_conv[0].reshape(nseq, (CONV_W - 1) * D_C)
    y_s, conv_s = _sample_tail_call(xs, rest_s, ht_s, conv0, wpm, wpc, wout, hnw, cw, fnw)

    return (y_p, y_s.reshape(nseq, 1, D_MODEL), c_p, n_p, m_p, conv_p,
            c_s, n_s.reshape(1, nseq, NH, DQK), m_s,
            conv_s.reshape(1, nseq, CONV_W - 1, D_C))
```

```python
import jax
import jax.numpy as jnp
from jax import lax
from jax.experimental import pallas as pl
from jax.experimental.pallas import tpu as pltpu

F32 = jnp.float32
BF16 = jnp.bfloat16

D_MODEL = 1024
NH = 4
DHV = 256
DQK = 128
D_QK = NH * DQK
D_M = NH * DHV
D_C = D_MODEL
CONV_W = 3
EPS = 1e-6
NEG_BIG = -1e30
K_SCALE = DQK ** -0.5

LANES = 128
SUBLANES = 8

OFF_Q = 0
OFF_K = OFF_Q + D_QK
OFF_V = OFF_K + D_QK
N_HEAD = OFF_V + D_M
N_GATE_COLS = 2 * NH
R_O = 0
R_ZM = R_O + D_M
R_BG = R_ZM + D_M
R_CG = R_BG + D_C
R_XC = R_CG + D_C
R_ZC = R_XC + D_C
R_GM = R_ZC + D_C
R_GC = R_GM + D_MODEL
N_REST = R_GC + D_MODEL

PROMPT_BLOCK = 512
MLSTM_CHUNK = 256
SAMPLE_SEQ_BLOCK = 8
SAMPLE_PROJ_COLS = 1024
VMEM_LIMIT_PROMPT = 60 * 1024 * 1024
VMEM_LIMIT_SAMPLE = 48 * 1024 * 1024


def _sigmoid(x):
    return 0.5 * jnp.tanh(0.5 * x) + 0.5


def _silu(x):
    return x * _sigmoid(x)


def _log_sigmoid(x):
    return jnp.minimum(x, 0.0) - jnp.log1p(jnp.exp(-jnp.abs(x)))


def _rmsnorm(x, w):
    return x * lax.rsqrt(jnp.mean(x * x, axis=-1, keepdims=True) + EPS) * w


def _dot(a, b):
    return jnp.dot(a, b, preferred_element_type=F32)


def _dot_nt(a, b):
    return lax.dot_general(a, b, (((1,), (1,)), ((), ())), preferred_element_type=F32)


def _dot_tn(a, b):
    return lax.dot_general(a, b, (((0,), (0,)), ((), ())), preferred_element_type=F32)


def _split3(x):
    hi = x.astype(BF16)
    r = x - hi.astype(F32)
    mid = r.astype(BF16)
    lo = (r - mid.astype(F32)).astype(BF16)
    return hi, mid, lo


PREP_COLS = 1024
PREP_ROWS = 128


def _weight_prep_kernel(a_ref, b_ref, c_ref, wr_ref, wh_ref, wg_ref):
    j = pl.program_id(0)
    for r0 in range(0, D_MODEL, PREP_ROWS):
        both = jnp.concatenate([a_ref[0, r0:r0 + PREP_ROWS, :], b_ref[0, r0:r0 + PREP_ROWS, :]], axis=1)
        wr_ref[r0:r0 + PREP_ROWS, :] = both[:, N_GATE_COLS:N_GATE_COLS + PREP_COLS].astype(BF16)
    wh_ref[...] = c_ref[0].astype(BF16)

    @pl.when(j == 0)
    def _():
        lane = lax.broadcasted_iota(jnp.int32, (D_MODEL, LANES), 1)
        wg_ref[...] = jnp.where(lane < N_GATE_COLS, a_ref[0, :, 0:LANES], 0.0).astype(BF16)


def _weight_prep_call(w_in):
    nsteps = N_REST // PREP_COLS
    head_cols = N_HEAD // nsteps
    lanes_per_step = PREP_COLS // LANES
    return pl.pallas_call(
        _weight_prep_kernel,
        grid=(nsteps,),
        in_specs=[
            pl.BlockSpec((1, D_MODEL, PREP_COLS), lambda j: (0, 0, N_HEAD // PREP_COLS + j)),
            pl.BlockSpec((1, D_MODEL, LANES), lambda j: (0, 0, N_HEAD // LANES + (j + 1) * lanes_per_step)),
            pl.BlockSpec((1, D_MODEL, head_cols), lambda j: (0, 0, j)),
        ],
        out_specs=(
            pl.BlockSpec((D_MODEL, PREP_COLS), lambda j: (0, j)),
            pl.BlockSpec((D_MODEL, head_cols), lambda j: (0, j)),
            pl.BlockSpec((D_MODEL, LANES), lambda j: (0, 0)),
        ),
        out_shape=(
            jax.ShapeDtypeStruct((D_MODEL, N_REST), BF16),
            jax.ShapeDtypeStruct((D_MODEL, N_HEAD), BF16),
            jax.ShapeDtypeStruct((D_MODEL, LANES), BF16),
        ),
        compiler_params=pltpu.CompilerParams(dimension_semantics=("arbitrary",),
                                             vmem_limit_bytes=VMEM_LIMIT_SAMPLE),
        name="weight_prep",
    )(w_in, w_in, w_in)


def _prompt_kernel(x_ref, wh_ref, wg_ref, wr_ref, wpm_ref, wpc_ref, wout_ref,
                   nw_ref, gb_ref, hnw_ref, cw_ref, fnw_ref,
                   y_ref, c_ref, n_ref, m_ref, conv_ref,
                   hg_s, ubuf_s):
    tl = x_ref.shape[1]
    l = pl.program_id(1)

    @pl.when(l == 0)
    def _():
        c_ref[...] = jnp.zeros_like(c_ref)
        n_ref[...] = jnp.zeros_like(n_ref)
        m_ref[...] = jnp.zeros_like(m_ref)
        ubuf_s[0:SUBLANES, :] = jnp.zeros((SUBLANES, D_C), F32)

    x = x_ref[0]
    xn = _rmsnorm(x, nw_ref[...]).astype(BF16)

    def rest(off, width):
        return _dot(xn, wr_ref[:, off:off + width])

    g = _dot(xn, wg_ref[...]) + gb_ref[...]
    gc = jnp.where(lax.broadcasted_iota(jnp.int32, (tl, LANES), 1) < NH, g, _log_sigmoid(g))
    qkv = _dot(xn, wh_ref[...])

    ch = MLSTM_CHUNK
    lane = lax.broadcasted_iota(jnp.int32, (ch, LANES), 1)
    row = lax.broadcasted_iota(jnp.int32, (ch, ch), 0)
    col = lax.broadcasted_iota(jnp.int32, (ch, ch), 1)
    causal = row >= col
    tril = jnp.where(causal, 1.0, 0.0).astype(BF16)
    chunk_gates = []
    for r0 in range(0, tl, ch):
        gc_c = gc[r0:r0 + ch, :]
        hi, mid, lo = _split3(gc_c)
        bc = _dot(tril, hi) + _dot(tril, mid) + _dot(tril, lo)
        gt = jnp.where(lane < NH, gc_c, bc).T
        chunk_gates.append((gc_c, bc, gt))

    def mlstm_unit(ci, h):
        r0 = ci * ch
        gc_c, bc, gt = chunk_gates[ci]
        xn_c = xn[r0:r0 + ch, :]
        sig_o = _sigmoid(_dot(xn_c, wr_ref[:, R_O + h * DHV:R_O + (h + 1) * DHV]))
        silu_z = _silu(_dot(xn_c, wr_ref[:, R_ZM + h * DHV:R_ZM + (h + 1) * DHV]))
        q_f = qkv[r0:r0 + ch, OFF_Q + h * DQK:OFF_Q + (h + 1) * DQK]
        k_f = qkv[r0:r0 + ch, OFF_K + h * DQK:OFF_K + (h + 1) * DQK] * K_SCALE
        v_f = qkv[r0:r0 + ch, OFF_V + h * DHV:OFF_V + (h + 1) * DHV]
        q_b = q_f.astype(BF16)
        k_b = k_f.astype(BF16)
        v_b = v_f.astype(BF16)
        c_old = c_ref[0, 0, h]
        n_old = n_ref[0, 0, h:h + 1, :]
        m_old = m_ref[0, h:h + 1, 0:1]

        b_c = bc[:, NH + h:NH + h + 1]
        ig_c = gc_c[:, h:h + 1]
        ig_r = gt[h:h + 1, :]
        b_r = gt[NH + h:NH + h + 1, :]
        log_d = jnp.where(causal, b_c - b_r + ig_r, NEG_BIG)
        inter = b_c + m_old
        m_t = jnp.maximum(inter, jnp.max(log_d, axis=-1, keepdims=True))
        d_m = jnp.exp(log_d - m_t)
        w_int = jnp.exp(inter - m_t)
        s = _dot_nt(q_b, k_b) * d_m
        num = w_int * _dot_nt(q_b, c_old.astype(BF16)) + _dot(s.astype(BF16), v_b)
        den = w_int * jnp.sum(q_f * n_old, axis=-1, keepdims=True) + jnp.sum(s, axis=-1, keepdims=True)
        h_t = num / jnp.maximum(jnp.abs(den), jnp.exp(-m_t))

        b_end = b_c[ch - 1:ch, :]
        inter_end = b_end + m_old
        m_new = jnp.maximum(inter_end, jnp.max(b_end - b_r + ig_r, axis=-1, keepdims=True))
        w_end = jnp.exp(b_end - b_c + ig_c - m_new)
        f_end = jnp.exp(inter_end - m_new)
        c_ref[0, 0, h] = f_end * c_old + _dot_tn((w_end * v_f).astype(BF16), k_b)
        n_ref[0, 0, h:h + 1, :] = f_end * n_old + jnp.sum(w_end * k_f, axis=0, keepdims=True)
        m_ref[0, h:h + 1, :] = jnp.broadcast_to(m_new, (1, LANES))

        hn = h_t * lax.rsqrt(jnp.mean(h_t * h_t, axis=-1, keepdims=True) + EPS)
        hn = hn * hnw_ref[:, h * DHV:(h + 1) * DHV]
        hg_s[r0:r0 + ch, h * DHV:(h + 1) * DHV] = (hn * sig_o * silu_z).astype(BF16)

    val = {}

    def conv_input():
        u = rest(R_CG, D_C) * rest(R_XC, D_C)
        ubuf_s[SUBLANES:SUBLANES + tl, :] = u
        cv = cw_ref[0:1, :] * ubuf_s[SUBLANES - 2:SUBLANES - 2 + tl, :]
        cv = cv + cw_ref[1:2, :] * ubuf_s[SUBLANES - 1:SUBLANES - 1 + tl, :]
        val["cv"] = cv + cw_ref[2:3, :] * u

    def conv_gate():
        val["yc_in"] = (_silu(rest(R_ZC, D_C)) * rest(R_BG, D_C) * val["cv"]).astype(BF16)

    def conv_proj():
        val["gy_c"] = _sigmoid(rest(R_GC, D_MODEL)) * _dot(val["yc_in"], wpc_ref[...])

    def merge_gate():
        val["sig_gm"] = _sigmoid(rest(R_GM, D_MODEL))

    fillers = [conv_input, conv_gate, conv_proj, merge_gate]
    units = [(ci, h) for ci in range(tl // ch) for h in range(NH)]
    per = len(units) // len(fillers)
    for i, (ci, h) in enumerate(units):
        mlstm_unit(ci, h)
        if (i + 1) % per == 0:
            fillers[(i + 1) // per - 1]()

    ubuf_s[0:SUBLANES, :] = ubuf_s[tl:tl + SUBLANES, :]

    @pl.when(l == pl.num_programs(1) - 1)
    def _():
        conv_ref[0, 0] = ubuf_s[SUBLANES - (CONV_W - 1):SUBLANES, :]

    y_m = _dot(hg_s[...], wpm_ref[...])
    mix = (val["sig_gm"] * y_m + val["gy_c"]).astype(BF16)
    out = x + _dot(mix, wout_ref[...])
    y_ref[0] = _rmsnorm(out, fnw_ref[...])


def _resident(shape):
    return pl.BlockSpec(shape, lambda *_: (0,) * len(shape), pipeline_mode=pl.Buffered(1))


def _prompt_call(x, wh, wg, wr, wpm, wpc, wout, nw, gb, hnw, cw, fnw):
    bsz, seq, _ = x.shape
    tl = PROMPT_BLOCK
    grid = (bsz, seq // tl)
    out_shape = (
        jax.ShapeDtypeStruct((bsz, seq, D_MODEL), F32),
        jax.ShapeDtypeStruct((1, bsz, NH, DHV, DQK), F32),
        jax.ShapeDtypeStruct((1, bsz, NH, DQK), F32),
        jax.ShapeDtypeStruct((bsz, SUBLANES, LANES), F32),
        jax.ShapeDtypeStruct((1, bsz, CONV_W - 1, D_C), F32),
    )
    in_specs = [
        pl.BlockSpec((1, tl, D_MODEL), lambda b, l: (b, l, 0)),
        _resident((D_MODEL, N_HEAD)),
        _resident((D_MODEL, LANES)),
        _resident((D_MODEL, N_REST)),
        _resident((D_M, D_MODEL)),
        _resident((D_C, D_MODEL)),
        _resident((D_MODEL, D_MODEL)),
        _resident((1, D_MODEL)),
        _resident((1, LANES)),
        _resident((1, D_M)),
        _resident((CONV_W, D_C)),
        _resident((1, D_MODEL)),
    ]
    out_specs = (
        pl.BlockSpec((1, tl, D_MODEL), lambda b, l: (b, l, 0)),
        pl.BlockSpec((1, 1, NH, DHV, DQK), lambda b, l: (0, b, 0, 0, 0)),
        pl.BlockSpec((1, 1, NH, DQK), lambda b, l: (0, b, 0, 0)),
        pl.BlockSpec((1, SUBLANES, LANES), lambda b, l: (b, 0, 0)),
        pl.BlockSpec((1, 1, CONV_W - 1, D_C), lambda b, l: (0, b, 0, 0)),
    )
    return pl.pallas_call(
        _prompt_kernel,
        grid=grid,
        in_specs=in_specs,
        out_specs=out_specs,
        out_shape=out_shape,
        scratch_shapes=[
            pltpu.VMEM((tl, D_M), BF16),
            pltpu.VMEM((tl + 2 * SUBLANES, D_C), F32),
        ],
        compiler_params=pltpu.CompilerParams(
            dimension_semantics=("arbitrary", "arbitrary"),
            vmem_limit_bytes=VMEM_LIMIT_PROMPT),
        name="prompt_layer",
    )(x, wh, wg, wr, wpm, wpc, wout, nw, gb, hnw, cw, fnw)


def _sample_proj_kernel(x_ref, nw_ref, wh_ref, wg_ref, wr_ref, qk_ref, vt_ref, g_ref, r_ref):
    xn = _rmsnorm(x_ref[...], nw_ref[...]).astype(BF16)

    @pl.when(pl.program_id(0) == 0)
    def _():
        head = _dot(xn, wh_ref[...])
        qk_ref[...] = head[:, OFF_Q:OFF_V]
        vt_ref[...] = head[:, OFF_V:N_HEAD].T
        g_ref[...] = _dot(xn, wg_ref[...])

    r_ref[...] = _dot(xn, wr_ref[...])


def _sample_proj_call(x, nw, wh, wg, wr):
    nseq = x.shape[0]
    cols = SAMPLE_PROJ_COLS
    const = lambda shape: pl.BlockSpec(shape, lambda j: (0,) * len(shape))
    return pl.pallas_call(
        _sample_proj_kernel,
        grid=(N_REST // cols,),
        in_specs=[
            const((nseq, D_MODEL)),
            const((1, D_MODEL)),
            const((D_MODEL, N_HEAD)),
            const((D_MODEL, LANES)),
            pl.BlockSpec((D_MODEL, cols), lambda j: (0, j)),
        ],
        out_specs=(
            const((nseq, 2 * D_QK)),
            const((D_M, nseq)),
            const((nseq, LANES)),
            pl.BlockSpec((nseq, cols), lambda j: (0, j)),
        ),
        out_shape=(
            jax.ShapeDtypeStruct((nseq, 2 * D_QK), F32),
            jax.ShapeDtypeStruct((D_M, nseq), F32),
            jax.ShapeDtypeStruct((nseq, LANES), F32),
            jax.ShapeDtypeStruct((nseq, N_REST), F32),
        ),
        compiler_params=pltpu.CompilerParams(dimension_semantics=("arbitrary",),
                                             vmem_limit_bytes=VMEM_LIMIT_SAMPLE),
        name="sample_in_proj",
    )(x, nw, wh, wg, wr)


def _sample_state_kernel(qk_ref, vt_ref, g_ref, gb_ref, c_ref, n_ref, m_ref,
                         ht_ref, co_ref, no_ref, mo_ref,
                         q_s, kw_s, f_s, zt_s):
    nseq = qk_ref.shape[0]
    nb = c_ref.shape[1]
    i = pl.program_id(0)
    lane = lax.broadcasted_iota(jnp.int32, (DHV, nseq), 1)

    @pl.when(i == 0)
    def _():
        g = g_ref[...] + gb_ref[...]
        ig = g[:, 0:NH]
        lf = _log_sigmoid(g[:, NH:2 * NH])
        inter = lf + m_ref[0]
        m_t = jnp.maximum(inter, ig)
        w_end = jnp.exp(ig - m_t)
        f_end = jnp.exp(inter - m_t)
        floor = jnp.exp(-m_t)
        mo_ref[0] = m_t
        lane_z = lax.broadcasted_iota(jnp.int32, (nseq, LANES), 1)
        z = jnp.zeros((nseq, LANES), F32)
        for h in range(NH):
            q_h = qk_ref[:, h * DQK:(h + 1) * DQK]
            k_h = qk_ref[:, D_QK + h * DQK:D_QK + (h + 1) * DQK] * K_SCALE
            n_h = n_ref[:, h * DQK:(h + 1) * DQK]
            w_h = w_end[:, h:h + 1]
            f_h = f_end[:, h:h + 1]
            s = jnp.sum(q_h * k_h, axis=-1, keepdims=True) * w_h
            den = f_h * jnp.sum(n_h * q_h, axis=-1, keepdims=True) + s
            denom = jnp.maximum(jnp.abs(den), floor[:, h:h + 1])
            no_ref[:, h * DQK:(h + 1) * DQK] = f_h * n_h + w_h * k_h
            q_s[h] = q_h.astype(BF16)
            kw_s[h] = (w_h * k_h).astype(BF16)
            f_s[h] = jnp.broadcast_to(f_h, (nseq, LANES))
            z = jnp.where(lane_z == h, f_h, z)
            z = jnp.where(lane_z == NH + h, s, z)
            z = jnp.where(lane_z == 2 * NH + h, denom, z)
        zt_s[...] = z.T
        ht_ref[...] = jnp.zeros_like(ht_ref)

    for h in range(NH):
        acc = ht_ref[h * DHV:(h + 1) * DHV, :]
        v_t = vt_ref[h * DHV:(h + 1) * DHV, :]
        for j in range(nb):
            b = i * nb + j
            sel = lane == b
            c_old = c_ref[0, j, h]
            acc = jnp.where(sel, _dot_nt(c_old.astype(BF16), q_s[h]), acc)
            upd = _dot(jnp.where(sel, v_t, 0.0).astype(BF16), kw_s[h])
            co_ref[0, j, h] = f_s[h, pl.ds(b, 1), :] * c_old + upd
        ht_ref[h * DHV:(h + 1) * DHV, :] = acc

    @pl.when(i == pl.num_programs(0) - 1)
    def _():
        for h in range(NH):
            num = (zt_s[h:h + 1, :] * ht_ref[h * DHV:(h + 1) * DHV, :]
                   + zt_s[NH + h:NH + h + 1, :] * vt_ref[h * DHV:(h + 1) * DHV, :])
            ht_ref[h * DHV:(h + 1) * DHV, :] = num / zt_s[2 * NH + h:2 * NH + h + 1, :]


def _sample_state_call(qk, vt, g, gb, c0, n0, m0):
    nseq = qk.shape[0]
    nb = SAMPLE_SEQ_BLOCK
    const = lambda shape: pl.BlockSpec(shape, lambda i: (0,) * len(shape))
    c_spec = pl.BlockSpec((1, nb, NH, DHV, DQK), lambda i: (0, i, 0, 0, 0))
    return pl.pallas_call(
        _sample_state_kernel,
        grid=(nseq // nb,),
        in_specs=[
            const((nseq, 2 * D_QK)),
            const((D_M, nseq)),
            const((nseq, LANES)),
            const((1, LANES)),
            c_spec,
            const((nseq, D_QK)),
            const((1, nseq, NH)),
        ],
        out_specs=(
            const((D_M, nseq)),
            c_spec,
            const((nseq, D_QK)),
            const((1, nseq, NH)),
        ),
        out_shape=(
            jax.ShapeDtypeStruct((D_M, nseq), F32),
            jax.ShapeDtypeStruct(c0.shape, F32),
            jax.ShapeDtypeStruct((nseq, D_QK), F32),
            jax.ShapeDtypeStruct(m0.shape, F32),
        ),
        scratch_shapes=[
            pltpu.VMEM((NH, nseq, DQK), BF16),
            pltpu.VMEM((NH, nseq, DQK), BF16),
            pltpu.VMEM((NH, nseq, LANES), F32),
            pltpu.VMEM((LANES, nseq), F32),
        ],
        compiler_params=pltpu.CompilerParams(dimension_semantics=("arbitrary",),
                                             vmem_limit_bytes=VMEM_LIMIT_SAMPLE),
        name="sample_state",
    )(qk, vt, g, gb, c0, n0, m0)


def _sample_tail_kernel(x_ref, r_ref, ht_ref, conv_ref, wpm_ref, wpc_ref, wout_ref, hnw_ref, cw_ref, fnw_ref,
                        y_ref, convo_ref):
    def piece(off, width):
        return r_ref[:, off:off + width]

    for h in range(NH):
        h_t = ht_ref[h * DHV:(h + 1) * DHV, :].T
        hn = h_t * lax.rsqrt(jnp.mean(h_t * h_t, axis=-1, keepdims=True) + EPS)
        hn = hn * hnw_ref[:, h * DHV:(h + 1) * DHV]
        o_h = piece(R_O + h * DHV, DHV)
        zm_h = piece(R_ZM + h * DHV, DHV)
        hg = (hn * _sigmoid(o_h) * _silu(zm_h)).astype(BF16)
        part = _dot(hg, wpm_ref[h * DHV:(h + 1) * DHV, :])
        y_m = part if h == 0 else y_m + part

    u = piece(R_CG, D_C) * piece(R_XC, D_C)
    cv = cw_ref[0:1, :] * conv_ref[:, 0:D_C]
    cv = cv + cw_ref[1:2, :] * conv_ref[:, D_C:2 * D_C]
    cv = cv + cw_ref[2:3, :] * u
    convo_ref[:, 0:D_C] = conv_ref[:, D_C:2 * D_C]
    convo_ref[:, D_C:2 * D_C] = u
    yc_in = (_silu(piece(R_ZC, D_C)) * piece(R_BG, D_C) * cv).astype(BF16)
    y_c = _dot(yc_in, wpc_ref[...])
    mix = (_sigmoid(piece(R_GM, D_MODEL)) * y_m + _sigmoid(piece(R_GC, D_MODEL)) * y_c).astype(BF16)
    out = x_ref[...] + _dot(mix, wout_ref[...])
    y_ref[...] = _rmsnorm(out, fnw_ref[...])


def _sample_tail_call(x, rest, h_t, conv0, wpm, wpc, wout, hnw, cw, fnw):
    nseq = x.shape[0]
    args = (x, rest, h_t, conv0, wpm, wpc, wout, hnw, cw, fnw)
    full = lambda a: pl.BlockSpec(a.shape, lambda i: (0,) * a.ndim)
    return pl.pallas_call(
        _sample_tail_kernel,
        grid=(1,),
        in_specs=[full(a) for a in args],
        out_specs=(pl.BlockSpec((nseq, D_MODEL), lambda i: (0, 0)),
                   pl.BlockSpec((nseq, (CONV_W - 1) * D_C), lambda i: (0, 0))),
        out_shape=(jax.ShapeDtypeStruct((nseq, D_MODEL), F32),
                   jax.ShapeDtypeStruct((nseq, (CONV_W - 1) * D_C), F32)),
        compiler_params=pltpu.CompilerParams(dimension_semantics=("arbitrary",),
                                             vmem_limit_bytes=VMEM_LIMIT_SAMPLE),
        name="sample_tail",
    )(*args)


def kernel(x_prompt, x_sample, state_mlstm_C, state_mlstm_n, state_mlstm_m, state_conv, norm_w, w_in, b_i, b_f,
           head_norm_w, conv_w, w_proj_m, w_proj_c, w_out, final_norm_w):
    depth = norm_w.shape[0]
    assert depth == 1, "single-layer trunk"
    bsz = x_prompt.shape[0]
    nseq = x_sample.shape[0]

    assert w_in.shape == (1, D_MODEL, N_HEAD + N_GATE_COLS + N_REST)
    wr, wh, wg = _weight_prep_call(w_in)
    wpm = w_proj_m[0].astype(BF16)
    wpc = w_proj_c[0].astype(BF16)
    wout = w_out[0].astype(BF16)
    gb = jnp.pad(jnp.concatenate([b_i[0], b_f[0]]), (0, LANES - N_GATE_COLS)).reshape(1, LANES)
    nw = norm_w[0].reshape(1, D_MODEL)
    hnw = head_norm_w[0].reshape(1, D_M)
    cw = conv_w[0]
    fnw = final_norm_w.reshape(1, D_MODEL)

    y_p, c_p, n_p, m_p, conv_p = _prompt_call(x_prompt, wh, wg, wr, wpm, wpc, wout, nw, gb, hnw, cw, fnw)
    m_p = m_p[:, :NH, 0].reshape(1, bsz, NH)

    xs = x_sample.reshape(nseq, D_MODEL)
    qk_s, vt_s, g_s, rest_s = _sample_proj_call(xs, nw, wh, wg, wr)
    ht_s, c_s, n_s, m_s = _sample_state_call(
        qk_s, vt_s, g_s, gb, state_mlstm_C, state_mlstm_n[0].reshape(nseq, D_QK), state_mlstm_m)
    conv0 = state_conv[0].reshape(nseq, (CONV_W - 1) * D_C)
    y_s, conv_s = _sample_tail_call(xs, rest_s, ht_s, conv0, wpm, wpc, wout, hnw, cw, fnw)

    return (y_p, y_s.reshape(nseq, 1, D_MODEL), c_p, n_p, m_p, conv_p,
            c_s, n_s.reshape(1, nseq, NH, DQK), m_s,
            conv_s.reshape(1, nseq, CONV_W - 1, D_C))
```

```python
import jax
import jax.numpy as jnp
from jax import lax
from jax.experimental import pallas as pl
from jax.experimental.pallas import tpu as pltpu

F32 = jnp.float32
BF16 = jnp.bfloat16

D_MODEL = 1024
NH = 4
DHV = 256
DQK = 128
D_QK = NH * DQK
D_M = NH * DHV
D_C = D_MODEL
CONV_W = 3
EPS = 1e-6
NEG_BIG = -1e30
K_SCALE = DQK ** -0.5

LANES = 128
SUBLANES = 8

OFF_Q = 0
OFF_K = OFF_Q + D_QK
OFF_V = OFF_K + D_QK
N_HEAD = OFF_V + D_M
N_GATE_COLS = 2 * NH
R_O = 0
R_ZM = R_O + D_M
R_BG = R_ZM + D_M
R_CG = R_BG + D_C
R_XC = R_CG + D_C
R_ZC = R_XC + D_C
R_GM = R_ZC + D_C
R_GC = R_GM + D_MODEL
N_REST = R_GC + D_MODEL

PROMPT_BLOCK = 512
MLSTM_CHUNK = 256
SAMPLE_SEQ_BLOCK = 8
SAMPLE_PROJ_COLS = 1024
VMEM_LIMIT_PROMPT = 60 * 1024 * 1024
VMEM_LIMIT_SAMPLE = 48 * 1024 * 1024


def _sigmoid(x):
    return 0.5 * jnp.tanh(0.5 * x) + 0.5


def _silu(x):
    return x * _sigmoid(x)


def _log_sigmoid(x):
    return jnp.minimum(x, 0.0) - jnp.log1p(jnp.exp(-jnp.abs(x)))


def _rmsnorm(x, w):
    return x * lax.rsqrt(jnp.mean(x * x, axis=-1, keepdims=True) + EPS) * w


def _dot(a, b):
    return jnp.dot(a, b, preferred_element_type=F32)


def _dot_nt(a, b):
    return lax.dot_general(a, b, (((1,), (1,)), ((), ())), preferred_element_type=F32)


def _dot_tn(a, b):
    return lax.dot_general(a, b, (((0,), (0,)), ((), ())), preferred_element_type=F32)


def _split3(x):
    hi = x.astype(BF16)
    r = x - hi.astype(F32)
    mid = r.astype(BF16)
    lo = (r - mid.astype(F32)).astype(BF16)
    return hi, mid, lo


PREP_COLS = 1024
PREP_CHUNK = 256


def _weight_prep_kernel(a_ref, c_ref, g_ref, wr_ref, wh_ref, wg_ref):
    for r0 in range(0, PREP_COLS, PREP_CHUNK):
        wr_ref[:, r0:r0 + PREP_CHUNK] = a_ref[0, r0:r0 + PREP_CHUNK, :].T.astype(BF16)
    wh_ref[...] = c_ref[0].T.astype(BF16)

    @pl.when(pl.program_id(0) == 0)
    def _():
        lane = lax.broadcasted_iota(jnp.int32, (D_MODEL, LANES), 1)
        wg_ref[...] = jnp.where(lane < N_GATE_COLS, g_ref[0].T, 0.0).astype(BF16)


def _weight_prep_call(w_in_t):
    nsteps = N_REST // PREP_COLS
    head_cols = N_HEAD // nsteps
    rest_row0 = N_HEAD + N_GATE_COLS
    return pl.pallas_call(
        _weight_prep_kernel,
        grid=(nsteps,),
        in_specs=[
            pl.BlockSpec((pl.Element(1), pl.Element(PREP_COLS), pl.Element(D_MODEL)),
                         lambda j: (0, pl.multiple_of(rest_row0 + j * PREP_COLS, SUBLANES), 0)),
            pl.BlockSpec((1, head_cols, D_MODEL), lambda j: (0, j, 0)),
            pl.BlockSpec((1, LANES, D_MODEL), lambda j: (0, N_HEAD // LANES, 0)),
        ],
        out_specs=(
            pl.BlockSpec((D_MODEL, PREP_COLS), lambda j: (0, j)),
            pl.BlockSpec((D_MODEL, head_cols), lambda j: (0, j)),
            pl.BlockSpec((D_MODEL, LANES), lambda j: (0, 0)),
        ),
        out_shape=(
            jax.ShapeDtypeStruct((D_MODEL, N_REST), BF16),
            jax.ShapeDtypeStruct((D_MODEL, N_HEAD), BF16),
            jax.ShapeDtypeStruct((D_MODEL, LANES), BF16),
        ),
        compiler_params=pltpu.CompilerParams(dimension_semantics=("arbitrary",),
                                             vmem_limit_bytes=VMEM_LIMIT_SAMPLE),
        name="weight_prep",
    )(w_in_t, w_in_t, w_in_t)


def _prompt_kernel(x_ref, wh_ref, wg_ref, wr_ref, wpm_ref, wpc_ref, wout_ref,
                   nw_ref, gb_ref, hnw_ref, cw_ref, fnw_ref,
                   y_ref, c_ref, n_ref, m_ref, conv_ref,
                   hg_s, ubuf_s):
    tl = x_ref.shape[1]
    l = pl.program_id(1)

    @pl.when(l == 0)
    def _():
        c_ref[...] = jnp.zeros_like(c_ref)
        n_ref[...] = jnp.zeros_like(n_ref)
        m_ref[...] = jnp.zeros_like(m_ref)
        ubuf_s[0:SUBLANES, :] = jnp.zeros((SUBLANES, D_C), F32)

    x = x_ref[0]
    xn = _rmsnorm(x, nw_ref[...]).astype(BF16)

    def rest(off, width):
        return _dot(xn, wr_ref[:, off:off + width])

    g = _dot(xn, wg_ref[...]) + gb_ref[...]
    gc = jnp.where(lax.broadcasted_iota(jnp.int32, (tl, LANES), 1) < NH, g, _log_sigmoid(g))
    qkv = _dot(xn, wh_ref[...])

    ch = MLSTM_CHUNK
    lane = lax.broadcasted_iota(jnp.int32, (ch, LANES), 1)
    row = lax.broadcasted_iota(jnp.int32, (ch, ch), 0)
    col = lax.broadcasted_iota(jnp.int32, (ch, ch), 1)
    causal = row >= col
    tril = jnp.where(causal, 1.0, 0.0).astype(BF16)
    chunk_gates = []
    for r0 in range(0, tl, ch):
        gc_c = gc[r0:r0 + ch, :]
        hi, mid, lo = _split3(gc_c)
        bc = _dot(tril, hi) + _dot(tril, mid) + _dot(tril, lo)
        gt = jnp.where(lane < NH, gc_c, bc).T
        chunk_gates.append((gc_c, bc, gt))

    def mlstm_unit(ci, h):
        r0 = ci * ch
        gc_c, bc, gt = chunk_gates[ci]
        xn_c = xn[r0:r0 + ch, :]
        sig_o = _sigmoid(_dot(xn_c, wr_ref[:, R_O + h * DHV:R_O + (h + 1) * DHV]))
        silu_z = _silu(_dot(xn_c, wr_ref[:, R_ZM + h * DHV:R_ZM + (h + 1) * DHV]))
        q_f = qkv[r0:r0 + ch, OFF_Q + h * DQK:OFF_Q + (h + 1) * DQK]
        k_f = qkv[r0:r0 + ch, OFF_K + h * DQK:OFF_K + (h + 1) * DQK] * K_SCALE
        v_f = qkv[r0:r0 + ch, OFF_V + h * DHV:OFF_V + (h + 1) * DHV]
        q_b = q_f.astype(BF16)
        k_b = k_f.astype(BF16)
        v_b = v_f.astype(BF16)
        c_old = c_ref[0, 0, h]
        n_old = n_ref[0, 0, h:h + 1, :]
        m_old = m_ref[0, h:h + 1, 0:1]

        b_c = bc[:, NH + h:NH + h + 1]
        ig_c = gc_c[:, h:h + 1]
        ig_r = gt[h:h + 1, :]
        b_r = gt[NH + h:NH + h + 1, :]
        log_d = jnp.where(causal, b_c - b_r + ig_r, NEG_BIG)
        inter = b_c + m_old
        m_t = jnp.maximum(inter, jnp.max(log_d, axis=-1, keepdims=True))
        d_m = jnp.exp(log_d - m_t)
        w_int = jnp.exp(inter - m_t)
        s = _dot_nt(q_b, k_b) * d_m
        num = w_int * _dot_nt(q_b, c_old.astype(BF16)) + _dot(s.astype(BF16), v_b)
        den = w_int * jnp.sum(q_f * n_old, axis=-1, keepdims=True) + jnp.sum(s, axis=-1, keepdims=True)
        h_t = num / jnp.maximum(jnp.abs(den), jnp.exp(-m_t))

        b_end = b_c[ch - 1:ch, :]
        inter_end = b_end + m_old
        m_new = jnp.maximum(inter_end, jnp.max(b_end - b_r + ig_r, axis=-1, keepdims=True))
        w_end = jnp.exp(b_end - b_c + ig_c - m_new)
        f_end = jnp.exp(inter_end - m_new)
        c_ref[0, 0, h] = f_end * c_old + _dot_tn((w_end * v_f).astype(BF16), k_b)
        n_ref[0, 0, h:h + 1, :] = f_end * n_old + jnp.sum(w_end * k_f, axis=0, keepdims=True)
        m_ref[0, h:h + 1, :] = jnp.broadcast_to(m_new, (1, LANES))

        hn = h_t * lax.rsqrt(jnp.mean(h_t * h_t, axis=-1, keepdims=True) + EPS)
        hn = hn * hnw_ref[:, h * DHV:(h + 1) * DHV]
        hg_s[r0:r0 + ch, h * DHV:(h + 1) * DHV] = (hn * sig_o * silu_z).astype(BF16)

    val = {}

    def conv_input():
        u = rest(R_CG, D_C) * rest(R_XC, D_C)
        ubuf_s[SUBLANES:SUBLANES + tl, :] = u
        cv = cw_ref[0:1, :] * ubuf_s[SUBLANES - 2:SUBLANES - 2 + tl, :]
        cv = cv + cw_ref[1:2, :] * ubuf_s[SUBLANES - 1:SUBLANES - 1 + tl, :]
        val["cv"] = cv + cw_ref[2:3, :] * u

    def conv_gate():
        val["yc_in"] = (_silu(rest(R_ZC, D_C)) * rest(R_BG, D_C) * val["cv"]).astype(BF16)

    def conv_proj():
        val["gy_c"] = _sigmoid(rest(R_GC, D_MODEL)) * _dot(val["yc_in"], wpc_ref[...])

    def merge_gate():
        val["sig_gm"] = _sigmoid(rest(R_GM, D_MODEL))

    fillers = [conv_input, conv_gate, conv_proj, merge_gate]
    units = [(ci, h) for ci in range(tl // ch) for h in range(NH)]
    per = len(units) // len(fillers)
    for i, (ci, h) in enumerate(units):
        mlstm_unit(ci, h)
        if (i + 1) % per == 0:
            fillers[(i + 1) // per - 1]()

    ubuf_s[0:SUBLANES, :] = ubuf_s[tl:tl + SUBLANES, :]

    @pl.when(l == pl.num_programs(1) - 1)
    def _():
        conv_ref[0, 0] = ubuf_s[SUBLANES - (CONV_W - 1):SUBLANES, :]

    y_m = _dot(hg_s[...], wpm_ref[...])
    mix = (val["sig_gm"] * y_m + val["gy_c"]).astype(BF16)
    out = x + _dot(mix, wout_ref[...])
    y_ref[0] = _rmsnorm(out, fnw_ref[...])


def _resident(shape):
    return pl.BlockSpec(shape, lambda *_: (0,) * len(shape), pipeline_mode=pl.Buffered(1))


def _prompt_call(x, wh, wg, wr, wpm, wpc, wout, nw, gb, hnw, cw, fnw):
    bsz, seq, _ = x.shape
    tl = PROMPT_BLOCK
    grid = (bsz, seq // tl)
    out_shape = (
        jax.ShapeDtypeStruct((bsz, seq, D_MODEL), F32),
        jax.ShapeDtypeStruct((1, bsz, NH, DHV, DQK), F32),
        jax.ShapeDtypeStruct((1, bsz, NH, DQK), F32),
        jax.ShapeDtypeStruct((bsz, SUBLANES, LANES), F32),
        jax.ShapeDtypeStruct((1, bsz, CONV_W - 1, D_C), F32),
    )
    in_specs = [
        pl.BlockSpec((1, tl, D_MODEL), lambda b, l: (b, l, 0)),
        _resident((D_MODEL, N_HEAD)),
        _resident((D_MODEL, LANES)),
        _resident((D_MODEL, N_REST)),
        _resident((D_M, D_MODEL)),
        _resident((D_C, D_MODEL)),
        _resident((D_MODEL, D_MODEL)),
        _resident((1, D_MODEL)),
        _resident((1, LANES)),
        _resident((1, D_M)),
        _resident((CONV_W, D_C)),
        _resident((1, D_MODEL)),
    ]
    out_specs = (
        pl.BlockSpec((1, tl, D_MODEL), lambda b, l: (b, l, 0)),
        pl.BlockSpec((1, 1, NH, DHV, DQK), lambda b, l: (0, b, 0, 0, 0)),
        pl.BlockSpec((1, 1, NH, DQK), lambda b, l: (0, b, 0, 0)),
        pl.BlockSpec((1, SUBLANES, LANES), lambda b, l: (b, 0, 0)),
        pl.BlockSpec((1, 1, CONV_W - 1, D_C), lambda b, l: (0, b, 0, 0)),
    )
    return pl.pallas_call(
        _prompt_kernel,
        grid=grid,
        in_specs=in_specs,
        out_specs=out_specs,
        out_shape=out_shape,
        scratch_shapes=[
            pltpu.VMEM((tl, D_M), BF16),
            pltpu.VMEM((tl + 2 * SUBLANES, D_C), F32),
        ],
        compiler_params=pltpu.CompilerParams(
            dimension_semantics=("arbitrary", "arbitrary"),
            vmem_limit_bytes=VMEM_LIMIT_PROMPT),
        name="prompt_layer",
    )(x, wh, wg, wr, wpm, wpc, wout, nw, gb, hnw, cw, fnw)


def _sample_proj_kernel(x_ref, nw_ref, wh_ref, wg_ref, wr_ref, qk_ref, vt_ref, g_ref, r_ref):
    xn = _rmsnorm(x_ref[...], nw_ref[...]).astype(BF16)

    @pl.when(pl.program_id(0) == 0)
    def _():
        head = _dot(xn, wh_ref[...])
        qk_ref[...] = head[:, OFF_Q:OFF_V]
        vt_ref[...] = head[:, OFF_V:N_HEAD].T
        g_ref[...] = _dot(xn, wg_ref[...])

    r_ref[...] = _dot(xn, wr_ref[...])


def _sample_proj_call(x, nw, wh, wg, wr):
    nseq = x.shape[0]
    cols = SAMPLE_PROJ_COLS
    const = lambda shape: pl.BlockSpec(shape, lambda j: (0,) * len(shape))
    return pl.pallas_call(
        _sample_proj_kernel,
        grid=(N_REST // cols,),
        in_specs=[
            const((nseq, D_MODEL)),
            const((1, D_MODEL)),
            const((D_MODEL, N_HEAD)),
            const((D_MODEL, LANES)),
            pl.BlockSpec((D_MODEL, cols), lambda j: (0, j)),
        ],
        out_specs=(
            const((nseq, 2 * D_QK)),
            const((D_M, nseq)),
            const((nseq, LANES)),
            pl.BlockSpec((nseq, cols), lambda j: (0, j)),
        ),
        out_shape=(
            jax.ShapeDtypeStruct((nseq, 2 * D_QK), F32),
            jax.ShapeDtypeStruct((D_M, nseq), F32),
            jax.ShapeDtypeStruct((nseq, LANES), F32),
            jax.ShapeDtypeStruct((nseq, N_REST), F32),
        ),
        compiler_params=pltpu.CompilerParams(dimension_semantics=("arbitrary",),
                                             vmem_limit_bytes=VMEM_LIMIT_SAMPLE),
        name="sample_in_proj",
    )(x, nw, wh, wg, wr)


def _sample_state_kernel(qk_ref, vt_ref, g_ref, gb_ref, c_ref, n_ref, m_ref,
                         ht_ref, co_ref, no_ref, mo_ref,
                         q_s, kw_s, f_s, zt_s):
    nseq = qk_ref.shape[0]
    nb = c_ref.shape[1]
    i = pl.program_id(0)
    lane = lax.broadcasted_iota(jnp.int32, (DHV, nseq), 1)

    @pl.when(i == 0)
    def _():
        g = g_ref[...] + gb_ref[...]
        ig = g[:, 0:NH]
        lf = _log_sigmoid(g[:, NH:2 * NH])
        inter = lf + m_ref[0]
        m_t = jnp.maximum(inter, ig)
        w_end = jnp.exp(ig - m_t)
        f_end = jnp.exp(inter - m_t)
        floor = jnp.exp(-m_t)
        mo_ref[0] = m_t
        lane_z = lax.broadcasted_iota(jnp.int32, (nseq, LANES), 1)
        z = jnp.zeros((nseq, LANES), F32)
        for h in range(NH):
            q_h = qk_ref[:, h * DQK:(h + 1) * DQK]
            k_h = qk_ref[:, D_QK + h * DQK:D_QK + (h + 1) * DQK] * K_SCALE
            n_h = n_ref[:, h * DQK:(h + 1) * DQK]
            w_h = w_end[:, h:h + 1]
            f_h = f_end[:, h:h + 1]
            s = jnp.sum(q_h * k_h, axis=-1, keepdims=True) * w_h
            den = f_h * jnp.sum(n_h * q_h, axis=-1, keepdims=True) + s
            denom = jnp.maximum(jnp.abs(den), floor[:, h:h + 1])
            no_ref[:, h * DQK:(h + 1) * DQK] = f_h * n_h + w_h * k_h
            q_s[h] = q_h.astype(BF16)
            kw_s[h] = (w_h * k_h).astype(BF16)
            f_s[h] = jnp.broadcast_to(f_h, (nseq, LANES))
            z = jnp.where(lane_z == h, f_h, z)
            z = jnp.where(lane_z == NH + h, s, z)
            z = jnp.where(lane_z == 2 * NH + h, denom, z)
        zt_s[...] = z.T
        ht_ref[...] = jnp.zeros_like(ht_ref)

    for h in range(NH):
        acc = ht_ref[h * DHV:(h + 1) * DHV, :]
        v_t = vt_ref[h * DHV:(h + 1) * DHV, :]
        for j in range(nb):
            b = i * nb + j
            sel = lane == b
            c_old = c_ref[0, j, h]
            acc = jnp.where(sel, _dot_nt(c_old.astype(BF16), q_s[h]), acc)
            upd = _dot(jnp.where(sel, v_t, 0.0).astype(BF16), kw_s[h])
            co_ref[0, j, h] = f_s[h, pl.ds(b, 1), :] * c_old + upd
        ht_ref[h * DHV:(h + 1) * DHV, :] = acc

    @pl.when(i == pl.num_programs(0) - 1)
    def _():
        for h in range(NH):
            num = (zt_s[h:h + 1, :] * ht_ref[h * DHV:(h + 1) * DHV, :]
                   + zt_s[NH + h:NH + h + 1, :] * vt_ref[h * DHV:(h + 1) * DHV, :])
            ht_ref[h * DHV:(h + 1) * DHV, :] = num / zt_s[2 * NH + h:2 * NH + h + 1, :]


def _sample_state_call(qk, vt, g, gb, c0, n0, m0):
    nseq = qk.shape[0]
    nb = SAMPLE_SEQ_BLOCK
    const = lambda shape: pl.BlockSpec(shape, lambda i: (0,) * len(shape))
    c_spec = pl.BlockSpec((1, nb, NH, DHV, DQK), lambda i: (0, i, 0, 0, 0))
    return pl.pallas_call(
        _sample_state_kernel,
        grid=(nseq // nb,),
        in_specs=[
            const((nseq, 2 * D_QK)),
            const((D_M, nseq)),
            const((nseq, LANES)),
            const((1, LANES)),
            c_spec,
            const((nseq, D_QK)),
            const((1, nseq, NH)),
        ],
        out_specs=(
            const((D_M, nseq)),
            c_spec,
            const((nseq, D_QK)),
            const((1, nseq, NH)),
        ),
        out_shape=(
            jax.ShapeDtypeStruct((D_M, nseq), F32),
            jax.ShapeDtypeStruct(c0.shape, F32),
            jax.ShapeDtypeStruct((nseq, D_QK), F32),
            jax.ShapeDtypeStruct(m0.shape, F32),
        ),
        scratch_shapes=[
            pltpu.VMEM((NH, nseq, DQK), BF16),
            pltpu.VMEM((NH, nseq, DQK), BF16),
            pltpu.VMEM((NH, nseq, LANES), F32),
            pltpu.VMEM((LANES, nseq), F32),
        ],
        compiler_params=pltpu.CompilerParams(dimension_semantics=("arbitrary",),
                                             vmem_limit_bytes=VMEM_LIMIT_SAMPLE),
        name="sample_state",
    )(qk, vt, g, gb, c0, n0, m0)


def _sample_tail_kernel(x_ref, r_ref, ht_ref, conv_ref, wpm_ref, wpc_ref, wout_ref, hnw_ref, cw_ref, fnw_ref,
                        y_ref, convo_ref):
    def piece(off, width):
        return r_ref[:, off:off + width]

    for h in range(NH):
        h_t = ht_ref[h * DHV:(h + 1) * DHV, :].T
        hn = h_t * lax.rsqrt(jnp.mean(h_t * h_t, axis=-1, keepdims=True) + EPS)
        hn = hn * hnw_ref[:, h * DHV:(h + 1) * DHV]
        o_h = piece(R_O + h * DHV, DHV)
        zm_h = piece(R_ZM + h * DHV, DHV)
        hg = (hn * _sigmoid(o_h) * _silu(zm_h)).astype(BF16)
        part = _dot(hg, wpm_ref[h * DHV:(h + 1) * DHV, :])
        y_m = part if h == 0 else y_m + part

    u = piece(R_CG, D_C) * piece(R_XC, D_C)
    cv = cw_ref[0:1, :] * conv_ref[:, 0:D_C]
    cv = cv + cw_ref[1:2, :] * conv_ref[:, D_C:2 * D_C]
    cv = cv + cw_ref[2:3, :] * u
    convo_ref[:, 0:D_C] = conv_ref[:, D_C:2 * D_C]
    convo_ref[:, D_C:2 * D_C] = u
    yc_in = (_silu(piece(R_ZC, D_C)) * piece(R_BG, D_C) * cv).astype(BF16)
    y_c = _dot(yc_in, wpc_ref[...])
    mix = (_sigmoid(piece(R_GM, D_MODEL)) * y_m + _sigmoid(piece(R_GC, D_MODEL)) * y_c).astype(BF16)
    out = x_ref[...] + _dot(mix, wout_ref[...])
    y_ref[...] = _rmsnorm(out, fnw_ref[...])


def _sample_tail_call(x, rest, h_t, conv0, wpm, wpc, wout, hnw, cw, fnw):
    nseq = x.shape[0]
    args = (x, rest, h_t, conv0, wpm, wpc, wout, hnw, cw, fnw)
    full = lambda a: pl.BlockSpec(a.shape, lambda i: (0,) * a.ndim)
    return pl.pallas_call(
        _sample_tail_kernel,
        grid=(1,),
        in_specs=[full(a) for a in args],
        out_specs=(pl.BlockSpec((nseq, D_MODEL), lambda i: (0, 0)),
                   pl.BlockSpec((nseq, (CONV_W - 1) * D_C), lambda i: (0, 0))),
        out_shape=(jax.ShapeDtypeStruct((nseq, D_MODEL), F32),
                   jax.ShapeDtypeStruct((nseq, (CONV_W - 1) * D_C), F32)),
        compiler_params=pltpu.CompilerParams(dimension_semantics=("arbitrary",),
                                             vmem_limit_bytes=VMEM_LIMIT_SAMPLE),
        name="sample_tail",
    )(*args)


def kernel(x_prompt, x_sample, state_mlstm_C, state_mlstm_n, state_mlstm_m, state_conv, norm_w, w_in, b_i, b_f,
           head_norm_w, conv_w, w_proj_m, w_proj_c, w_out, final_norm_w):
    depth = norm_w.shape[0]
    assert depth == 1, "single-layer trunk"
    bsz = x_prompt.shape[0]
    nseq = x_sample.shape[0]

    assert w_in.shape == (1, D_MODEL, N_HEAD + N_GATE_COLS + N_REST)
    wr, wh, wg = _weight_prep_call(jnp.swapaxes(w_in, 1, 2))
    wpm = w_proj_m[0].astype(BF16)
    wpc = w_proj_c[0].astype(BF16)
    wout = w_out[0].astype(BF16)
    gb = jnp.pad(jnp.concatenate([b_i[0], b_f[0]]), (0, LANES - N_GATE_COLS)).reshape(1, LANES)
    nw = norm_w[0].reshape(1, D_MODEL)
    hnw = head_norm_w[0].reshape(1, D_M)
    cw = conv_w[0]
    fnw = final_norm_w.reshape(1, D_MODEL)

    y_p, c_p, n_p, m_p, conv_p = _prompt_call(x_prompt, wh, wg, wr, wpm, wpc, wout, nw, gb, hnw, cw, fnw)
    m_p = m_p[:, :NH, 0].reshape(1, bsz, NH)

    xs = x_sample.reshape(nseq, D_MODEL)
    qk_s, vt_s, g_s, rest_s = _sample_proj_call(xs, nw, wh, wg, wr)
    ht_s, c_s, n_s, m_s = _sample_state_call(
        qk_s, vt_s, g_s, gb, state_mlstm_C, state_mlstm_n[0].reshape(nseq, D_QK), state_mlstm_m)
    conv0 = state_conv[0].reshape(nseq, (CONV_W - 1) * D_C)
    y_s, conv_s = _sample_tail_call(xs, rest_s, ht_s, conv0, wpm, wpc, wout, hnw, cw, fnw)

    return (y_p, y_s.reshape(nseq, 1, D_MODEL), c_p, n_p, m_p, conv_p,
            c_s, n_s.reshape(1, nseq, NH, DQK), m_s,
            conv_s.reshape(1, nseq, CONV_W - 1, D_C))
```

```python
import jax
import jax.numpy as jnp
from jax import lax
from jax.experimental import pallas as pl
from jax.experimental.pallas import tpu as pltpu

F32 = jnp.float32
BF16 = jnp.bfloat16

D_MODEL = 1024
NH = 4
DHV = 256
DQK = 128
D_QK = NH * DQK
D_M = NH * DHV
D_C = D_MODEL
CONV_W = 3
EPS = 1e-6
NEG_BIG = -1e30
K_SCALE = DQK ** -0.5

LANES = 128
SUBLANES = 8

OFF_Q = 0
OFF_K = OFF_Q + D_QK
OFF_V = OFF_K + D_QK
N_HEAD = OFF_V + D_M
N_GATE_COLS = 2 * NH
R_O = 0
R_ZM = R_O + D_M
R_BG = R_ZM + D_M
R_CG = R_BG + D_C
R_XC = R_CG + D_C
R_ZC = R_XC + D_C
R_GM = R_ZC + D_C
R_GC = R_GM + D_MODEL
N_REST = R_GC + D_MODEL

PROMPT_BLOCK = 512
MLSTM_CHUNK = 256
SAMPLE_SEQ_BLOCK = 8
SAMPLE_PROJ_COLS = 1024
VMEM_LIMIT_PROMPT = 60 * 1024 * 1024
VMEM_LIMIT_SAMPLE = 48 * 1024 * 1024


def _sigmoid(x):
    return 0.5 * jnp.tanh(0.5 * x) + 0.5


def _silu(x):
    return x * _sigmoid(x)


def _log_sigmoid(x):
    return jnp.minimum(x, 0.0) - jnp.log1p(jnp.exp(-jnp.abs(x)))


def _rmsnorm(x, w):
    return x * lax.rsqrt(jnp.mean(x * x, axis=-1, keepdims=True) + EPS) * w


def _dot(a, b):
    return jnp.dot(a, b, preferred_element_type=F32)


def _dot_nt(a, b):
    return lax.dot_general(a, b, (((1,), (1,)), ((), ())), preferred_element_type=F32)


def _dot_tn(a, b):
    return lax.dot_general(a, b, (((0,), (0,)), ((), ())), preferred_element_type=F32)


def _split3(x):
    hi = x.astype(BF16)
    r = x - hi.astype(F32)
    mid = r.astype(BF16)
    lo = (r - mid.astype(F32)).astype(BF16)
    return hi, mid, lo


PREP_COLS = 1024
PREP_CHUNK = 256


PREP_STEPS = N_REST // PREP_COLS
PREP_HEAD_COLS = N_HEAD // PREP_STEPS
PREP_QK_STEPS = (2 * D_QK) // PREP_HEAD_COLS
PREP_SQ_ROWS = D_MODEL // PREP_STEPS


def _weight_prep_kernel(a_ref, c_ref, g_ref, pm_ref, pc_ref, po_ref, xs_ref, nw_ref,
                        wr_ref, wh_ref, wg_ref, wpm_ref, wpc_ref, wout_ref,
                        hs_ref, vt_ref, gs_ref, rs_ref):
    j = pl.program_id(0)
    xn = _rmsnorm(xs_ref[...], nw_ref[...]).astype(BF16)
    for r0 in range(0, PREP_COLS, PREP_CHUNK):
        w_t = a_ref[0, r0:r0 + PREP_CHUNK, :].T.astype(BF16)
        wr_ref[:, r0:r0 + PREP_CHUNK] = w_t
        rs_ref[:, r0:r0 + PREP_CHUNK] = _dot(xn, w_t)
    w_t = c_ref[0].T.astype(BF16)
    wh_ref[...] = w_t
    head = _dot(xn, w_t)
    hs_ref[...] = head

    @pl.when(j >= PREP_QK_STEPS)
    def _():
        vt_ref[...] = head.T

    @pl.when(j == 0)
    def _():
        lane = lax.broadcasted_iota(jnp.int32, (D_MODEL, LANES), 1)
        w_g = jnp.where(lane < N_GATE_COLS, g_ref[0].T, 0.0).astype(BF16)
        wg_ref[...] = w_g
        gs_ref[...] = _dot(xn, w_g)

    wpm_ref[...] = pm_ref[0].astype(BF16)
    wpc_ref[...] = pc_ref[0].astype(BF16)
    wout_ref[...] = po_ref[0].astype(BF16)


def _weight_prep_call(w_in_t, w_proj_m, w_proj_c, w_out, xs, nw):
    nseq = xs.shape[0]
    assert PREP_HEAD_COLS == DHV, "one v head per step"
    rest_row0 = N_HEAD + N_GATE_COLS
    const = lambda shape: pl.BlockSpec(shape, lambda j: (0,) * len(shape))
    sq_in = pl.BlockSpec((1, PREP_SQ_ROWS, D_MODEL), lambda j: (0, j, 0))
    sq_out = pl.BlockSpec((PREP_SQ_ROWS, D_MODEL), lambda j: (j, 0))
    sq_shape = jax.ShapeDtypeStruct((D_MODEL, D_MODEL), BF16)
    return pl.pallas_call(
        _weight_prep_kernel,
        grid=(PREP_STEPS,),
        in_specs=[
            pl.BlockSpec((pl.Element(1), pl.Element(PREP_COLS), pl.Element(D_MODEL)),
                         lambda j: (0, pl.multiple_of(rest_row0 + j * PREP_COLS, SUBLANES), 0)),
            pl.BlockSpec((1, PREP_HEAD_COLS, D_MODEL), lambda j: (0, j, 0)),
            pl.BlockSpec((1, LANES, D_MODEL), lambda j: (0, N_HEAD // LANES, 0)),
            sq_in, sq_in, sq_in,
            const((nseq, D_MODEL)),
            const((1, D_MODEL)),
        ],
        out_specs=(
            pl.BlockSpec((D_MODEL, PREP_COLS), lambda j: (0, j)),
            pl.BlockSpec((D_MODEL, PREP_HEAD_COLS), lambda j: (0, j)),
            const((D_MODEL, LANES)),
            sq_out, sq_out, sq_out,
            pl.BlockSpec((nseq, PREP_HEAD_COLS), lambda j: (0, j)),
            pl.BlockSpec((DHV, nseq), lambda j: (jnp.maximum(j - PREP_QK_STEPS, 0), 0)),
            const((nseq, LANES)),
            pl.BlockSpec((nseq, PREP_COLS), lambda j: (0, j)),
        ),
        out_shape=(
            jax.ShapeDtypeStruct((D_MODEL, N_REST), BF16),
            jax.ShapeDtypeStruct((D_MODEL, N_HEAD), BF16),
            jax.ShapeDtypeStruct((D_MODEL, LANES), BF16),
            sq_shape, sq_shape, sq_shape,
            jax.ShapeDtypeStruct((nseq, N_HEAD), F32),
            jax.ShapeDtypeStruct((D_M, nseq), F32),
            jax.ShapeDtypeStruct((nseq, LANES), F32),
            jax.ShapeDtypeStruct((nseq, N_REST), F32),
        ),
        compiler_params=pltpu.CompilerParams(dimension_semantics=("arbitrary",),
                                             vmem_limit_bytes=VMEM_LIMIT_SAMPLE),
        name="weight_prep",
    )(w_in_t, w_in_t, w_in_t, w_proj_m, w_proj_c, w_out, xs, nw)


def _prompt_kernel(x_ref, wh_ref, wg_ref, wr_ref, wpm_ref, wpc_ref, wout_ref,
                   nw_ref, gb_ref, hnw_ref, cw_ref, fnw_ref,
                   y_ref, c_ref, n_ref, m_ref, conv_ref,
                   hg_s, ubuf_s):
    tl = x_ref.shape[1]
    l = pl.program_id(1)

    @pl.when(l == 0)
    def _():
        c_ref[...] = jnp.zeros_like(c_ref)
        n_ref[...] = jnp.zeros_like(n_ref)
        m_ref[...] = jnp.zeros_like(m_ref)
        ubuf_s[0:SUBLANES, :] = jnp.zeros((SUBLANES, D_C), F32)

    x = x_ref[0]
    xn = _rmsnorm(x, nw_ref[...]).astype(BF16)

    def rest(off, width):
        return _dot(xn, wr_ref[:, off:off + width])

    g = _dot(xn, wg_ref[...]) + gb_ref[...]
    gc = jnp.where(lax.broadcasted_iota(jnp.int32, (tl, LANES), 1) < NH, g, _log_sigmoid(g))
    qkv = _dot(xn, wh_ref[...])

    ch = MLSTM_CHUNK
    lane = lax.broadcasted_iota(jnp.int32, (ch, LANES), 1)
    row = lax.broadcasted_iota(jnp.int32, (ch, ch), 0)
    col = lax.broadcasted_iota(jnp.int32, (ch, ch), 1)
    causal = row >= col
    tril = jnp.where(causal, 1.0, 0.0).astype(BF16)
    chunk_gates = []
    for r0 in range(0, tl, ch):
        gc_c = gc[r0:r0 + ch, :]
        hi, mid, lo = _split3(gc_c)
        bc = _dot(tril, hi) + _dot(tril, mid) + _dot(tril, lo)
        gt = jnp.where(lane < NH, gc_c, bc).T
        chunk_gates.append((gc_c, bc, gt))

    def mlstm_unit(ci, h):
        r0 = ci * ch
        gc_c, bc, gt = chunk_gates[ci]
        xn_c = xn[r0:r0 + ch, :]
        sig_o = _sigmoid(_dot(xn_c, wr_ref[:, R_O + h * DHV:R_O + (h + 1) * DHV]))
        silu_z = _silu(_dot(xn_c, wr_ref[:, R_ZM + h * DHV:R_ZM + (h + 1) * DHV]))
        q_f = qkv[r0:r0 + ch, OFF_Q + h * DQK:OFF_Q + (h + 1) * DQK]
        k_f = qkv[r0:r0 + ch, OFF_K + h * DQK:OFF_K + (h + 1) * DQK] * K_SCALE
        v_f = qkv[r0:r0 + ch, OFF_V + h * DHV:OFF_V + (h + 1) * DHV]
        q_b = q_f.astype(BF16)
        k_b = k_f.astype(BF16)
        v_b = v_f.astype(BF16)
        c_old = c_ref[0, 0, h]
        n_old = n_ref[0, 0, h:h + 1, :]
        m_old = m_ref[0, h:h + 1, 0:1]

        b_c = bc[:, NH + h:NH + h + 1]
        ig_c = gc_c[:, h:h + 1]
        ig_r = gt[h:h + 1, :]
        b_r = gt[NH + h:NH + h + 1, :]
        log_d = jnp.where(causal, b_c - b_r + ig_r, NEG_BIG)
        inter = b_c + m_old
        m_t = jnp.maximum(inter, jnp.max(log_d, axis=-1, keepdims=True))
        d_m = jnp.exp(log_d - m_t)
        w_int = jnp.exp(inter - m_t)
        s = _dot_nt(q_b, k_b) * d_m
        num = w_int * _dot_nt(q_b, c_old.astype(BF16)) + _dot(s.astype(BF16), v_b)
        den = w_int * jnp.sum(q_f * n_old, axis=-1, keepdims=True) + jnp.sum(s, axis=-1, keepdims=True)
        h_t = num / jnp.maximum(jnp.abs(den), jnp.exp(-m_t))

        b_end = b_c[ch - 1:ch, :]
        inter_end = b_end + m_old
        m_new = jnp.maximum(inter_end, jnp.max(b_end - b_r + ig_r, axis=-1, keepdims=True))
        w_end = jnp.exp(b_end - b_c + ig_c - m_new)
        f_end = jnp.exp(inter_end - m_new)
        c_ref[0, 0, h] = f_end * c_old + _dot_tn((w_end * v_f).astype(BF16), k_b)
        n_ref[0, 0, h:h + 1, :] = f_end * n_old + jnp.sum(w_end * k_f, axis=0, keepdims=True)
        m_ref[0, h:h + 1, :] = jnp.broadcast_to(m_new, (1, LANES))

        hn = h_t * lax.rsqrt(jnp.mean(h_t * h_t, axis=-1, keepdims=True) + EPS)
        hn = hn * hnw_ref[:, h * DHV:(h + 1) * DHV]
        hg_s[r0:r0 + ch, h * DHV:(h + 1) * DHV] = (hn * sig_o * silu_z).astype(BF16)

    val = {}

    def conv_input():
        u = rest(R_CG, D_C) * rest(R_XC, D_C)
        ubuf_s[SUBLANES:SUBLANES + tl, :] = u
        cv = cw_ref[0:1, :] * ubuf_s[SUBLANES - 2:SUBLANES - 2 + tl, :]
        cv = cv + cw_ref[1:2, :] * ubuf_s[SUBLANES - 1:SUBLANES - 1 + tl, :]
        val["cv"] = cv + cw_ref[2:3, :] * u

    def conv_gate():
        val["yc_in"] = (_silu(rest(R_ZC, D_C)) * rest(R_BG, D_C) * val["cv"]).astype(BF16)

    def conv_proj():
        val["gy_c"] = _sigmoid(rest(R_GC, D_MODEL)) * _dot(val["yc_in"], wpc_ref[...])

    def merge_gate():
        val["sig_gm"] = _sigmoid(rest(R_GM, D_MODEL))

    fillers = [conv_input, conv_gate, conv_proj, merge_gate]
    units = [(ci, h) for ci in range(tl // ch) for h in range(NH)]
    per = len(units) // len(fillers)
    for i, (ci, h) in enumerate(units):
        mlstm_unit(ci, h)
        if (i + 1) % per == 0:
            fillers[(i + 1) // per - 1]()

    ubuf_s[0:SUBLANES, :] = ubuf_s[tl:tl + SUBLANES, :]

    @pl.when(l == pl.num_programs(1) - 1)
    def _():
        conv_ref[0, 0] = ubuf_s[SUBLANES - (CONV_W - 1):SUBLANES, :]

    y_m = _dot(hg_s[...], wpm_ref[...])
    mix = (val["sig_gm"] * y_m + val["gy_c"]).astype(BF16)
    out = x + _dot(mix, wout_ref[...])
    y_ref[0] = _rmsnorm(out, fnw_ref[...])


def _resident(shape):
    return pl.BlockSpec(shape, lambda *_: (0,) * len(shape), pipeline_mode=pl.Buffered(1))


def _prompt_call(x, wh, wg, wr, wpm, wpc, wout, nw, gb, hnw, cw, fnw):
    bsz, seq, _ = x.shape
    tl = PROMPT_BLOCK
    grid = (bsz, seq // tl)
    out_shape = (
        jax.ShapeDtypeStruct((bsz, seq, D_MODEL), F32),
        jax.ShapeDtypeStruct((1, bsz, NH, DHV, DQK), F32),
        jax.ShapeDtypeStruct((1, bsz, NH, DQK), F32),
        jax.ShapeDtypeStruct((bsz, SUBLANES, LANES), F32),
        jax.ShapeDtypeStruct((1, bsz, CONV_W - 1, D_C), F32),
    )
    in_specs = [
        pl.BlockSpec((1, tl, D_MODEL), lambda b, l: (b, l, 0)),
        _resident((D_MODEL, N_HEAD)),
        _resident((D_MODEL, LANES)),
        _resident((D_MODEL, N_REST)),
        _resident((D_M, D_MODEL)),
        _resident((D_C, D_MODEL)),
        _resident((D_MODEL, D_MODEL)),
        _resident((1, D_MODEL)),
        _resident((1, LANES)),
        _resident((1, D_M)),
        _resident((CONV_W, D_C)),
        _resident((1, D_MODEL)),
    ]
    out_specs = (
        pl.BlockSpec((1, tl, D_MODEL), lambda b, l: (b, l, 0)),
        pl.BlockSpec((1, 1, NH, DHV, DQK), lambda b, l: (0, b, 0, 0, 0)),
        pl.BlockSpec((1, 1, NH, DQK), lambda b, l: (0, b, 0, 0)),
        pl.BlockSpec((1, SUBLANES, LANES), lambda b, l: (b, 0, 0)),
        pl.BlockSpec((1, 1, CONV_W - 1, D_C), lambda b, l: (0, b, 0, 0)),
    )
    return pl.pallas_call(
        _prompt_kernel,
        grid=grid,
        in_specs=in_specs,
        out_specs=out_specs,
        out_shape=out_shape,
        scratch_shapes=[
            pltpu.VMEM((tl, D_M), BF16),
            pltpu.VMEM((tl + 2 * SUBLANES, D_C), F32),
        ],
        compiler_params=pltpu.CompilerParams(
            dimension_semantics=("arbitrary", "arbitrary"),
            vmem_limit_bytes=VMEM_LIMIT_PROMPT),
        name="prompt_layer",
    )(x, wh, wg, wr, wpm, wpc, wout, nw, gb, hnw, cw, fnw)


def _sample_state_kernel(qk_ref, vt_ref, g_ref, gb_ref, c_ref, n_ref, m_ref,
                         x_ref, r_ref, conv_ref, wpm_ref, wpc_ref, wout_ref, hnw_ref, cw_ref, fnw_ref,
                         co_ref, no_ref, mo_ref, y_ref, convo_ref,
                         q_s, kw_s, f_s, zt_s, ht_ref):
    nseq = qk_ref.shape[0]
    nb = c_ref.shape[1]
    i = pl.program_id(0)
    lane = lax.broadcasted_iota(jnp.int32, (DHV, nseq), 1)

    @pl.when(i == 0)
    def _():
        g = g_ref[...] + gb_ref[...]
        ig = g[:, 0:NH]
        lf = _log_sigmoid(g[:, NH:2 * NH])
        inter = lf + m_ref[0]
        m_t = jnp.maximum(inter, ig)
        w_end = jnp.exp(ig - m_t)
        f_end = jnp.exp(inter - m_t)
        floor = jnp.exp(-m_t)
        mo_ref[0] = m_t
        lane_z = lax.broadcasted_iota(jnp.int32, (nseq, LANES), 1)
        z = jnp.zeros((nseq, LANES), F32)
        for h in range(NH):
            q_h = qk_ref[:, h * DQK:(h + 1) * DQK]
            k_h = qk_ref[:, D_QK + h * DQK:D_QK + (h + 1) * DQK] * K_SCALE
            n_h = n_ref[:, h * DQK:(h + 1) * DQK]
            w_h = w_end[:, h:h + 1]
            f_h = f_end[:, h:h + 1]
            s = jnp.sum(q_h * k_h, axis=-1, keepdims=True) * w_h
            den = f_h * jnp.sum(n_h * q_h, axis=-1, keepdims=True) + s
            denom = jnp.maximum(jnp.abs(den), floor[:, h:h + 1])
            no_ref[:, h * DQK:(h + 1) * DQK] = f_h * n_h + w_h * k_h
            q_s[h] = q_h.astype(BF16)
            kw_s[h] = (w_h * k_h).astype(BF16)
            f_s[h] = jnp.broadcast_to(f_h, (nseq, LANES))
            z = jnp.where(lane_z == h, f_h, z)
            z = jnp.where(lane_z == NH + h, s, z)
            z = jnp.where(lane_z == 2 * NH + h, denom, z)
        zt_s[...] = z.T
        ht_ref[...] = jnp.zeros_like(ht_ref)

    for h in range(NH):
        acc = ht_ref[h * DHV:(h + 1) * DHV, :]
        v_t = vt_ref[h * DHV:(h + 1) * DHV, :]
        for j in range(nb):
            b = i * nb + j
            sel = lane == b
            c_old = c_ref[0, j, h]
            acc = jnp.where(sel, _dot_nt(c_old.astype(BF16), q_s[h]), acc)
            upd = _dot(jnp.where(sel, v_t, 0.0).astype(BF16), kw_s[h])
            co_ref[0, j, h] = f_s[h, pl.ds(b, 1), :] * c_old + upd
        ht_ref[h * DHV:(h + 1) * DHV, :] = acc

    @pl.when(i == pl.num_programs(0) - 1)
    def _():
        def piece(off, width):
            return r_ref[:, off:off + width]

        for h in range(NH):
            num = (zt_s[h:h + 1, :] * ht_ref[h * DHV:(h + 1) * DHV, :]
                   + zt_s[NH + h:NH + h + 1, :] * vt_ref[h * DHV:(h + 1) * DHV, :])
            h_t = (num / zt_s[2 * NH + h:2 * NH + h + 1, :]).T
            hn = h_t * lax.rsqrt(jnp.mean(h_t * h_t, axis=-1, keepdims=True) + EPS)
            hn = hn * hnw_ref[:, h * DHV:(h + 1) * DHV]
            o_h = piece(R_O + h * DHV, DHV)
            zm_h = piece(R_ZM + h * DHV, DHV)
            hg = (hn * _sigmoid(o_h) * _silu(zm_h)).astype(BF16)
            part = _dot(hg, wpm_ref[h * DHV:(h + 1) * DHV, :])
            y_m = part if h == 0 else y_m + part

        u = piece(R_CG, D_C) * piece(R_XC, D_C)
        cv = cw_ref[0:1, :] * conv_ref[:, 0:D_C]
        cv = cv + cw_ref[1:2, :] * conv_ref[:, D_C:2 * D_C]
        cv = cv + cw_ref[2:3, :] * u
        convo_ref[:, 0:D_C] = conv_ref[:, D_C:2 * D_C]
        convo_ref[:, D_C:2 * D_C] = u
        yc_in = (_silu(piece(R_ZC, D_C)) * piece(R_BG, D_C) * cv).astype(BF16)
        y_c = _dot(yc_in, wpc_ref[...])
        mix = (_sigmoid(piece(R_GM, D_MODEL)) * y_m + _sigmoid(piece(R_GC, D_MODEL)) * y_c).astype(BF16)
        out = x_ref[...] + _dot(mix, wout_ref[...])
        y_ref[...] = _rmsnorm(out, fnw_ref[...])


def _sample_state_call(head_s, vt, g, gb, c0, n0, m0, x, rest, conv0, wpm, wpc, wout, hnw, cw, fnw):
    nseq = x.shape[0]
    nb = SAMPLE_SEQ_BLOCK
    c_spec = pl.BlockSpec((1, nb, NH, DHV, DQK), lambda i: (0, i, 0, 0, 0))
    const = lambda shape: pl.BlockSpec(shape, lambda i: (0,) * len(shape))
    return pl.pallas_call(
        _sample_state_kernel,
        grid=(nseq // nb,),
        in_specs=[
            _resident((nseq, 2 * D_QK)),
            _resident((D_M, nseq)),
            _resident((nseq, LANES)),
            _resident((1, LANES)),
            c_spec,
            _resident((nseq, D_QK)),
            _resident((1, nseq, NH)),
            _resident((nseq, D_MODEL)),
            _resident((nseq, N_REST)),
            _resident((nseq, (CONV_W - 1) * D_C)),
            _resident((D_M, D_MODEL)),
            _resident((D_C, D_MODEL)),
            _resident((D_MODEL, D_MODEL)),
            _resident((1, D_M)),
            _resident((CONV_W, D_C)),
            _resident((1, D_MODEL)),
        ],
        out_specs=(
            c_spec,
            const((nseq, D_QK)),
            const((1, nseq, NH)),
            const((nseq, D_MODEL)),
            const((nseq, (CONV_W - 1) * D_C)),
        ),
        out_shape=(
            jax.ShapeDtypeStruct(c0.shape, F32),
            jax.ShapeDtypeStruct((nseq, D_QK), F32),
            jax.ShapeDtypeStruct(m0.shape, F32),
            jax.ShapeDtypeStruct((nseq, D_MODEL), F32),
            jax.ShapeDtypeStruct((nseq, (CONV_W - 1) * D_C), F32),
        ),
        scratch_shapes=[
            pltpu.VMEM((NH, nseq, DQK), BF16),
            pltpu.VMEM((NH, nseq, DQK), BF16),
            pltpu.VMEM((NH, nseq, LANES), F32),
            pltpu.VMEM((LANES, nseq), F32),
            pltpu.VMEM((D_M, nseq), F32),
        ],
        compiler_params=pltpu.CompilerParams(dimension_semantics=("arbitrary",),
                                             vmem_limit_bytes=VMEM_LIMIT_SAMPLE),
        name="sample_state",
    )(head_s, vt, g, gb, c0, n0, m0, x, rest, conv0, wpm, wpc, wout, hnw, cw, fnw)


def kernel(x_prompt, x_sample, state_mlstm_C, state_mlstm_n, state_mlstm_m, state_conv, norm_w, w_in, b_i, b_f,
           head_norm_w, conv_w, w_proj_m, w_proj_c, w_out, final_norm_w):
    depth = norm_w.shape[0]
    assert depth == 1, "single-layer trunk"
    bsz = x_prompt.shape[0]
    nseq = x_sample.shape[0]

    assert w_in.shape == (1, D_MODEL, N_HEAD + N_GATE_COLS + N_REST)
    gb = jnp.pad(jnp.concatenate([b_i[0], b_f[0]]), (0, LANES - N_GATE_COLS)).reshape(1, LANES)
    nw = norm_w[0].reshape(1, D_MODEL)
    hnw = head_norm_w[0].reshape(1, D_M)
    cw = conv_w[0]
    fnw = final_norm_w.reshape(1, D_MODEL)
    xs = x_sample.reshape(nseq, D_MODEL)
    (wr, wh, wg, wpm, wpc, wout, head_s, vt_s, g_s, rest_s) = _weight_prep_call(
        jnp.swapaxes(w_in, 1, 2), w_proj_m, w_proj_c, w_out, xs, nw)

    y_p, c_p, n_p, m_p, conv_p = _prompt_call(x_prompt, wh, wg, wr, wpm, wpc, wout, nw, gb, hnw, cw, fnw)
    m_p = m_p[:, :NH, 0].reshape(1, bsz, NH)

    conv0 = state_conv[0].reshape(nseq, (CONV_W - 1) * D_C)
    c_s, n_s, m_s, y_s, conv_s = _sample_state_call(
        head_s, vt_s, g_s, gb, state_mlstm_C, state_mlstm_n[0].reshape(nseq, D_QK), state_mlstm_m,
        xs, rest_s, conv0, wpm, wpc, wout, hnw, cw, fnw)

    return (y_p, y_s.reshape(nseq, 1, D_MODEL), c_p, n_p, m_p, conv_p,
            c_s, n_s.reshape(1, nseq, NH, DQK), m_s,
            conv_s.reshape(1, nseq, CONV_W - 1, D_C))
```

```python
import jax
import jax.numpy as jnp
from jax import lax
from jax.experimental import pallas as pl
from jax.experimental.pallas import tpu as pltpu

F32 = jnp.float32
BF16 = jnp.bfloat16

D_MODEL = 1024
NH = 4
DHV = 256
DQK = 128
D_QK = NH * DQK
D_M = NH * DHV
D_C = D_MODEL
CONV_W = 3
EPS = 1e-6
NEG_BIG = -1e30
K_SCALE = DQK ** -0.5

LANES = 128
SUBLANES = 8

OFF_Q = 0
OFF_K = OFF_Q + D_QK
OFF_V = OFF_K + D_QK
N_HEAD = OFF_V + D_M
N_GATE_COLS = 2 * NH
R_O = 0
R_ZM = R_O + D_M
R_BG = R_ZM + D_M
R_CG = R_BG + D_C
R_XC = R_CG + D_C
R_ZC = R_XC + D_C
R_GM = R_ZC + D_C
R_GC = R_GM + D_MODEL
N_REST = R_GC + D_MODEL

PROMPT_BLOCK = 512
MLSTM_CHUNK = 256
SAMPLE_SEQ_BLOCK = 8
SAMPLE_PROJ_COLS = 1024
VMEM_LIMIT_PROMPT = 60 * 1024 * 1024
VMEM_LIMIT_SAMPLE = 48 * 1024 * 1024


def _sigmoid(x):
    return 0.5 * jnp.tanh(0.5 * x) + 0.5


def _silu(x):
    return x * _sigmoid(x)


def _log_sigmoid(x):
    return jnp.minimum(x, 0.0) - jnp.log1p(jnp.exp(-jnp.abs(x)))


def _rmsnorm(x, w):
    return x * lax.rsqrt(jnp.mean(x * x, axis=-1, keepdims=True) + EPS) * w


def _dot(a, b):
    return jnp.dot(a, b, preferred_element_type=F32)


def _dot_nt(a, b):
    return lax.dot_general(a, b, (((1,), (1,)), ((), ())), preferred_element_type=F32)


def _dot_tn(a, b):
    return lax.dot_general(a, b, (((0,), (0,)), ((), ())), preferred_element_type=F32)


def _split3(x):
    hi = x.astype(BF16)
    r = x - hi.astype(F32)
    mid = r.astype(BF16)
    lo = (r - mid.astype(F32)).astype(BF16)
    return hi, mid, lo


PREP_COLS = 1024
PREP_CHUNK = 256


PREP_STEPS = N_REST // PREP_COLS
PREP_HEAD_COLS = N_HEAD // PREP_STEPS
PREP_QK_STEPS = (2 * D_QK) // PREP_HEAD_COLS
PREP_SQ_ROWS = D_MODEL // PREP_STEPS


def _weight_prep_kernel(a_ref, c_ref, g_ref, pm_ref, pc_ref, po_ref, xs_ref, nw_ref,
                        wr_ref, wh_ref, wg_ref, wpm_ref, wpc_ref, wout_ref,
                        hs_ref, vt_ref, gs_ref, rs_ref):
    j = pl.program_id(0)
    xn = _rmsnorm(xs_ref[...], nw_ref[...]).astype(BF16)
    for r0 in range(0, PREP_COLS, PREP_CHUNK):
        w_t = a_ref[0, r0:r0 + PREP_CHUNK, :].T.astype(BF16)
        wr_ref[:, r0:r0 + PREP_CHUNK] = w_t
        rs_ref[:, r0:r0 + PREP_CHUNK] = _dot(xn, w_t)
    w_t = c_ref[0].T.astype(BF16)
    wh_ref[...] = w_t
    head = _dot(xn, w_t)
    hs_ref[...] = head

    @pl.when(j >= PREP_QK_STEPS)
    def _():
        vt_ref[...] = head.T

    @pl.when(j == 0)
    def _():
        lane = lax.broadcasted_iota(jnp.int32, (D_MODEL, LANES), 1)
        w_g = jnp.where(lane < N_GATE_COLS, g_ref[0].T, 0.0).astype(BF16)
        wg_ref[...] = w_g
        gs_ref[...] = _dot(xn, w_g)

    wpm_ref[...] = pm_ref[0].astype(BF16)
    wpc_ref[...] = pc_ref[0].astype(BF16)
    wout_ref[...] = po_ref[0].astype(BF16)


def _weight_prep_call(w_in_t, w_proj_m, w_proj_c, w_out, xs, nw):
    nseq = xs.shape[0]
    assert PREP_HEAD_COLS == DHV, "one v head per step"
    rest_row0 = N_HEAD + N_GATE_COLS
    const = lambda shape: pl.BlockSpec(shape, lambda j: (0,) * len(shape))
    sq_in = pl.BlockSpec((1, PREP_SQ_ROWS, D_MODEL), lambda j: (0, j, 0))
    sq_out = pl.BlockSpec((PREP_SQ_ROWS, D_MODEL), lambda j: (j, 0))
    sq_shape = jax.ShapeDtypeStruct((D_MODEL, D_MODEL), BF16)
    return pl.pallas_call(
        _weight_prep_kernel,
        grid=(PREP_STEPS,),
        in_specs=[
            pl.BlockSpec((pl.Element(1), pl.Element(PREP_COLS), pl.Element(D_MODEL)),
                         lambda j: (0, pl.multiple_of(rest_row0 + j * PREP_COLS, SUBLANES), 0)),
            pl.BlockSpec((1, PREP_HEAD_COLS, D_MODEL), lambda j: (0, j, 0)),
            pl.BlockSpec((1, LANES, D_MODEL), lambda j: (0, N_HEAD // LANES, 0)),
            sq_in, sq_in, sq_in,
            const((nseq, D_MODEL)),
            const((1, D_MODEL)),
        ],
        out_specs=(
            pl.BlockSpec((D_MODEL, PREP_COLS), lambda j: (0, j)),
            pl.BlockSpec((D_MODEL, PREP_HEAD_COLS), lambda j: (0, j)),
            const((D_MODEL, LANES)),
            sq_out, sq_out, sq_out,
            pl.BlockSpec((nseq, PREP_HEAD_COLS), lambda j: (0, j)),
            pl.BlockSpec((DHV, nseq), lambda j: (jnp.maximum(j - PREP_QK_STEPS, 0), 0)),
            const((nseq, LANES)),
            pl.BlockSpec((nseq, PREP_COLS), lambda j: (0, j)),
        ),
        out_shape=(
            jax.ShapeDtypeStruct((D_MODEL, N_REST), BF16),
            jax.ShapeDtypeStruct((D_MODEL, N_HEAD), BF16),
            jax.ShapeDtypeStruct((D_MODEL, LANES), BF16),
            sq_shape, sq_shape, sq_shape,
            jax.ShapeDtypeStruct((nseq, N_HEAD), F32),
            jax.ShapeDtypeStruct((D_M, nseq), F32),
            jax.ShapeDtypeStruct((nseq, LANES), F32),
            jax.ShapeDtypeStruct((nseq, N_REST), F32),
        ),
        compiler_params=pltpu.CompilerParams(dimension_semantics=("arbitrary",),
                                             vmem_limit_bytes=VMEM_LIMIT_SAMPLE),
        name="weight_prep",
    )(w_in_t, w_in_t, w_in_t, w_proj_m, w_proj_c, w_out, xs, nw)


def _prompt_kernel(x_ref, wh_ref, wg_ref, wr_ref, wpm_ref, wpc_ref, wout_ref,
                   nw_ref, gb_ref, hnw_ref, cw_ref, fnw_ref,
                   y_ref, c_ref, n_ref, m_ref, conv_ref,
                   hg_s, ubuf_s, ct_s):
    tl = x_ref.shape[1]
    l = pl.program_id(1)

    @pl.when(l == 0)
    def _():
        ct_s[...] = jnp.zeros_like(ct_s)
        n_ref[...] = jnp.zeros_like(n_ref)
        m_ref[...] = jnp.zeros_like(m_ref)
        ubuf_s[0:SUBLANES, :] = jnp.zeros((SUBLANES, D_C), F32)

    x = x_ref[0]
    xn = _rmsnorm(x, nw_ref[...]).astype(BF16)

    def rest(off, width):
        return _dot(xn, wr_ref[:, off:off + width])

    g = _dot(xn, wg_ref[...]) + gb_ref[...]
    gc = jnp.where(lax.broadcasted_iota(jnp.int32, (tl, LANES), 1) < NH, g, _log_sigmoid(g))
    qkv = _dot(xn, wh_ref[...])

    ch = MLSTM_CHUNK
    lane = lax.broadcasted_iota(jnp.int32, (ch, LANES), 1)
    row = lax.broadcasted_iota(jnp.int32, (ch, ch), 0)
    col = lax.broadcasted_iota(jnp.int32, (ch, ch), 1)
    causal = row >= col
    row_g = lax.broadcasted_iota(jnp.int32, (ch, LANES), 0)
    chunk_gates = []
    for r0 in range(0, tl, ch):
        gc_c = gc[r0:r0 + ch, :]
        bc = gc_c
        shift = 1
        while shift < ch:
            bc = bc + jnp.where(row_g >= shift, pltpu.roll(bc, shift, axis=0), 0.0)
            shift *= 2
        gt = jnp.where(lane < NH, gc_c, bc).T
        chunk_gates.append((gc_c, bc, gt))

    def mlstm_unit(ci, h):
        r0 = ci * ch
        gc_c, bc, gt = chunk_gates[ci]
        xn_c = xn[r0:r0 + ch, :]
        sig_o = _sigmoid(_dot(xn_c, wr_ref[:, R_O + h * DHV:R_O + (h + 1) * DHV]))
        silu_z = _silu(_dot(xn_c, wr_ref[:, R_ZM + h * DHV:R_ZM + (h + 1) * DHV]))
        q_f = qkv[r0:r0 + ch, OFF_Q + h * DQK:OFF_Q + (h + 1) * DQK]
        k_f = qkv[r0:r0 + ch, OFF_K + h * DQK:OFF_K + (h + 1) * DQK] * K_SCALE
        v_f = qkv[r0:r0 + ch, OFF_V + h * DHV:OFF_V + (h + 1) * DHV]
        q_b = q_f.astype(BF16)
        k_b = k_f.astype(BF16)
        v_b = v_f.astype(BF16)
        ct_old = ct_s[h]
        n_old = n_ref[0, 0, h:h + 1, :]
        m_old = m_ref[0, h:h + 1, 0:1]

        b_c = bc[:, NH + h:NH + h + 1]
        ig_c = gc_c[:, h:h + 1]
        ig_r = gt[h:h + 1, :]
        b_r = gt[NH + h:NH + h + 1, :]
        log_d = jnp.where(causal, b_c - b_r + ig_r, NEG_BIG)
        inter = b_c + m_old
        m_t = jnp.maximum(inter, jnp.max(log_d, axis=-1, keepdims=True))
        d_m = jnp.exp(log_d - m_t)
        w_int = jnp.exp(inter - m_t)
        s = _dot_nt(q_b, k_b) * d_m
        num = w_int * _dot(q_b, ct_old.astype(BF16)) + _dot(s.astype(BF16), v_b)
        den = w_int * jnp.sum(q_f * n_old, axis=-1, keepdims=True) + jnp.sum(s, axis=-1, keepdims=True)
        h_t = num / jnp.maximum(jnp.abs(den), jnp.exp(-m_t))

        b_end = b_c[ch - 1:ch, :]
        inter_end = b_end + m_old
        m_new = jnp.maximum(inter_end, jnp.max(b_end - b_r + ig_r, axis=-1, keepdims=True))
        w_end = jnp.exp(b_end - b_c + ig_c - m_new)
        f_end = jnp.exp(inter_end - m_new)
        ct_s[h] = f_end * ct_old + _dot_tn(k_b, (w_end * v_f).astype(BF16))
        n_ref[0, 0, h:h + 1, :] = f_end * n_old + jnp.sum(w_end * k_f, axis=0, keepdims=True)
        m_ref[0, h:h + 1, :] = jnp.broadcast_to(m_new, (1, LANES))

        hn = h_t * lax.rsqrt(jnp.mean(h_t * h_t, axis=-1, keepdims=True) + EPS)
        hn = hn * hnw_ref[:, h * DHV:(h + 1) * DHV]
        hg_s[r0:r0 + ch, h * DHV:(h + 1) * DHV] = (hn * sig_o * silu_z).astype(BF16)

    val = {}

    def conv_input():
        u = rest(R_CG, D_C) * rest(R_XC, D_C)
        ubuf_s[SUBLANES:SUBLANES + tl, :] = u
        cv = cw_ref[0:1, :] * ubuf_s[SUBLANES - 2:SUBLANES - 2 + tl, :]
        cv = cv + cw_ref[1:2, :] * ubuf_s[SUBLANES - 1:SUBLANES - 1 + tl, :]
        val["cv"] = cv + cw_ref[2:3, :] * u

    def conv_gate():
        val["yc_in"] = (_silu(rest(R_ZC, D_C)) * rest(R_BG, D_C) * val["cv"]).astype(BF16)

    def conv_proj():
        val["gy_c"] = _sigmoid(rest(R_GC, D_MODEL)) * _dot(val["yc_in"], wpc_ref[...])

    def merge_gate():
        val["sig_gm"] = _sigmoid(rest(R_GM, D_MODEL))

    fillers = [conv_input, conv_gate, conv_proj, merge_gate]
    units = [(ci, h) for ci in range(tl // ch) for h in range(NH)]
    per = len(units) // len(fillers)
    for i, (ci, h) in enumerate(units):
        mlstm_unit(ci, h)
        if (i + 1) % per == 0:
            fillers[(i + 1) // per - 1]()

    ubuf_s[0:SUBLANES, :] = ubuf_s[tl:tl + SUBLANES, :]

    conv_ref[0, 0] = ubuf_s[SUBLANES - (CONV_W - 1):SUBLANES, :]
    for h in range(NH):
        c_ref[0, 0, h] = ct_s[h].T

    y_m = _dot(hg_s[...], wpm_ref[...])
    mix = (val["sig_gm"] * y_m + val["gy_c"]).astype(BF16)
    out = x + _dot(mix, wout_ref[...])
    y_ref[0] = _rmsnorm(out, fnw_ref[...])


def _resident(shape):
    return pl.BlockSpec(shape, lambda *_: (0,) * len(shape), pipeline_mode=pl.Buffered(1))


def _prompt_call(x, wh, wg, wr, wpm, wpc, wout, nw, gb, hnw, cw, fnw):
    bsz, seq, _ = x.shape
    tl = PROMPT_BLOCK
    grid = (bsz, seq // tl)
    out_shape = (
        jax.ShapeDtypeStruct((bsz, seq, D_MODEL), F32),
        jax.ShapeDtypeStruct((1, bsz, NH, DHV, DQK), F32),
        jax.ShapeDtypeStruct((1, bsz, NH, DQK), F32),
        jax.ShapeDtypeStruct((bsz, SUBLANES, LANES), F32),
        jax.ShapeDtypeStruct((1, bsz, CONV_W - 1, D_C), F32),
    )
    in_specs = [
        pl.BlockSpec((1, tl, D_MODEL), lambda b, l: (b, l, 0)),
        _resident((D_MODEL, N_HEAD)),
        _resident((D_MODEL, LANES)),
        _resident((D_MODEL, N_REST)),
        _resident((D_M, D_MODEL)),
        _resident((D_C, D_MODEL)),
        _resident((D_MODEL, D_MODEL)),
        _resident((1, D_MODEL)),
        _resident((1, LANES)),
        _resident((1, D_M)),
        _resident((CONV_W, D_C)),
        _resident((1, D_MODEL)),
    ]
    out_specs = (
        pl.BlockSpec((1, tl, D_MODEL), lambda b, l: (b, l, 0)),
        pl.BlockSpec((1, 1, NH, DHV, DQK), lambda b, l: (0, b, 0, 0, 0)),
        pl.BlockSpec((1, 1, NH, DQK), lambda b, l: (0, b, 0, 0)),
        pl.BlockSpec((1, SUBLANES, LANES), lambda b, l: (b, 0, 0)),
        pl.BlockSpec((1, 1, CONV_W - 1, D_C), lambda b, l: (0, b, 0, 0)),
    )
    return pl.pallas_call(
        _prompt_kernel,
        grid=grid,
        in_specs=in_specs,
        out_specs=out_specs,
        out_shape=out_shape,
        scratch_shapes=[
            pltpu.VMEM((tl, D_M), BF16),
            pltpu.VMEM((tl + 2 * SUBLANES, D_C), F32),
            pltpu.VMEM((NH, DQK, DHV), F32),
        ],
        compiler_params=pltpu.CompilerParams(
            dimension_semantics=("arbitrary", "arbitrary"),
            vmem_limit_bytes=VMEM_LIMIT_PROMPT),
        name="prompt_layer",
    )(x, wh, wg, wr, wpm, wpc, wout, nw, gb, hnw, cw, fnw)


def _sample_state_kernel(qk_ref, vt_ref, g_ref, gb_ref, c_ref, n_ref, m_ref,
                         x_ref, r_ref, conv_ref, wpm_ref, wpc_ref, wout_ref, hnw_ref, cw_ref, fnw_ref,
                         co_ref, no_ref, mo_ref, y_ref, convo_ref,
                         q_s, kw_s, f_s, zt_s, ht_ref):
    nseq = qk_ref.shape[0]
    nb = c_ref.shape[1]
    i = pl.program_id(0)
    lane = lax.broadcasted_iota(jnp.int32, (DHV, nseq), 1)

    @pl.when(i == 0)
    def _():
        g = g_ref[...] + gb_ref[...]
        ig = g[:, 0:NH]
        lf = _log_sigmoid(g[:, NH:2 * NH])
        inter = lf + m_ref[0]
        m_t = jnp.maximum(inter, ig)
        w_end = jnp.exp(ig - m_t)
        f_end = jnp.exp(inter - m_t)
        floor = jnp.exp(-m_t)
        mo_ref[0] = m_t
        lane_z = lax.broadcasted_iota(jnp.int32, (nseq, LANES), 1)
        z = jnp.zeros((nseq, LANES), F32)
        for h in range(NH):
            q_h = qk_ref[:, h * DQK:(h + 1) * DQK]
            k_h = qk_ref[:, D_QK + h * DQK:D_QK + (h + 1) * DQK] * K_SCALE
            n_h = n_ref[:, h * DQK:(h + 1) * DQK]
            w_h = w_end[:, h:h + 1]
            f_h = f_end[:, h:h + 1]
            s = jnp.sum(q_h * k_h, axis=-1, keepdims=True) * w_h
            den = f_h * jnp.sum(n_h * q_h, axis=-1, keepdims=True) + s
            denom = jnp.maximum(jnp.abs(den), floor[:, h:h + 1])
            no_ref[:, h * DQK:(h + 1) * DQK] = f_h * n_h + w_h * k_h
            q_s[h] = q_h.astype(BF16)
            kw_s[h] = (w_h * k_h).astype(BF16)
            f_s[h] = jnp.broadcast_to(f_h, (nseq, LANES))
            z = jnp.where(lane_z == h, f_h, z)
            z = jnp.where(lane_z == NH + h, s, z)
            z = jnp.where(lane_z == 2 * NH + h, denom, z)
        zt_s[...] = z.T
        ht_ref[...] = jnp.zeros_like(ht_ref)

    for h in range(NH):
        acc = ht_ref[h * DHV:(h + 1) * DHV, :]
        v_t = vt_ref[h * DHV:(h + 1) * DHV, :]
        for j in range(nb):
            b = i * nb + j
            sel = lane == b
            c_old = c_ref[0, j, h]
            acc = jnp.where(sel, _dot_nt(c_old.astype(BF16), q_s[h]), acc)
            upd = _dot(jnp.where(sel, v_t, 0.0).astype(BF16), kw_s[h])
            co_ref[0, j, h] = f_s[h, pl.ds(b, 1), :] * c_old + upd
        ht_ref[h * DHV:(h + 1) * DHV, :] = acc

    @pl.when(i == pl.num_programs(0) - 1)
    def _():
        def piece(off, width):
            return r_ref[:, off:off + width]

        for h in range(NH):
            num = (zt_s[h:h + 1, :] * ht_ref[h * DHV:(h + 1) * DHV, :]
                   + zt_s[NH + h:NH + h + 1, :] * vt_ref[h * DHV:(h + 1) * DHV, :])
            h_t = (num / zt_s[2 * NH + h:2 * NH + h + 1, :]).T
            hn = h_t * lax.rsqrt(jnp.mean(h_t * h_t, axis=-1, keepdims=True) + EPS)
            hn = hn * hnw_ref[:, h * DHV:(h + 1) * DHV]
            o_h = piece(R_O + h * DHV, DHV)
            zm_h = piece(R_ZM + h * DHV, DHV)
            hg = (hn * _sigmoid(o_h) * _silu(zm_h)).astype(BF16)
            part = _dot(hg, wpm_ref[h * DHV:(h + 1) * DHV, :])
            y_m = part if h == 0 else y_m + part

        u = piece(R_CG, D_C) * piece(R_XC, D_C)
        cv = cw_ref[0:1, :] * conv_ref[:, 0:D_C]
        cv = cv + cw_ref[1:2, :] * conv_ref[:, D_C:2 * D_C]
        cv = cv + cw_ref[2:3, :] * u
        convo_ref[:, 0:D_C] = conv_ref[:, D_C:2 * D_C]
        convo_ref[:, D_C:2 * D_C] = u
        yc_in = (_silu(piece(R_ZC, D_C)) * piece(R_BG, D_C) * cv).astype(BF16)
        y_c = _dot(yc_in, wpc_ref[...])
        mix = (_sigmoid(piece(R_GM, D_MODEL)) * y_m + _sigmoid(piece(R_GC, D_MODEL)) * y_c).astype(BF16)
        out = x_ref[...] + _dot(mix, wout_ref[...])
        y_ref[...] = _rmsnorm(out, fnw_ref[...])


def _sample_state_call(head_s, vt, g, gb, c0, n0, m0, x, rest, conv0, wpm, wpc, wout, hnw, cw, fnw):
    nseq = x.shape[0]
    nb = SAMPLE_SEQ_BLOCK
    c_spec = pl.BlockSpec((1, nb, NH, DHV, DQK), lambda i: (0, i, 0, 0, 0))
    const = lambda shape: pl.BlockSpec(shape, lambda i: (0,) * len(shape))
    return pl.pallas_call(
        _sample_state_kernel,
        grid=(nseq // nb,),
        in_specs=[
            _resident((nseq, 2 * D_QK)),
            _resident((D_M, nseq)),
            _resident((nseq, LANES)),
            _resident((1, LANES)),
            c_spec,
            _resident((nseq, D_QK)),
            _resident((1, nseq, NH)),
            _resident((nseq, D_MODEL)),
            _resident((nseq, N_REST)),
            _resident((nseq, (CONV_W - 1) * D_C)),
            _resident((D_M, D_MODEL)),
            _resident((D_C, D_MODEL)),
            _resident((D_MODEL, D_MODEL)),
            _resident((1, D_M)),
            _resident((CONV_W, D_C)),
            _resident((1, D_MODEL)),
        ],
        out_specs=(
            c_spec,
            const((nseq, D_QK)),
            const((1, nseq, NH)),
            const((nseq, D_MODEL)),
            const((nseq, (CONV_W - 1) * D_C)),
        ),
        out_shape=(
            jax.ShapeDtypeStruct(c0.shape, F32),
            jax.ShapeDtypeStruct((nseq, D_QK), F32),
            jax.ShapeDtypeStruct(m0.shape, F32),
            jax.ShapeDtypeStruct((nseq, D_MODEL), F32),
            jax.ShapeDtypeStruct((nseq, (CONV_W - 1) * D_C), F32),
        ),
        scratch_shapes=[
            pltpu.VMEM((NH, nseq, DQK), BF16),
            pltpu.VMEM((NH, nseq, DQK), BF16),
            pltpu.VMEM((NH, nseq, LANES), F32),
            pltpu.VMEM((LANES, nseq), F32),
            pltpu.VMEM((D_M, nseq), F32),
        ],
        compiler_params=pltpu.CompilerParams(dimension_semantics=("arbitrary",),
                                             vmem_limit_bytes=VMEM_LIMIT_SAMPLE),
        name="sample_state",
    )(head_s, vt, g, gb, c0, n0, m0, x, rest, conv0, wpm, wpc, wout, hnw, cw, fnw)


def kernel(x_prompt, x_sample, state_mlstm_C, state_mlstm_n, state_mlstm_m, state_conv, norm_w, w_in, b_i, b_f,
           head_norm_w, conv_w, w_proj_m, w_proj_c, w_out, final_norm_w):
    depth = norm_w.shape[0]
    assert depth == 1, "single-layer trunk"
    bsz = x_prompt.shape[0]
    nseq = x_sample.shape[0]

    assert w_in.shape == (1, D_MODEL, N_HEAD + N_GATE_COLS + N_REST)
    gb = jnp.pad(jnp.concatenate([b_i[0], b_f[0]]), (0, LANES - N_GATE_COLS)).reshape(1, LANES)
    nw = norm_w[0].reshape(1, D_MODEL)
    hnw = head_norm_w[0].reshape(1, D_M)
    cw = conv_w[0]
    fnw = final_norm_w.reshape(1, D_MODEL)
    xs = x_sample.reshape(nseq, D_MODEL)
    (wr, wh, wg, wpm, wpc, wout, head_s, vt_s, g_s, rest_s) = _weight_prep_call(
        jnp.swapaxes(w_in, 1, 2), w_proj_m, w_proj_c, w_out, xs, nw)

    y_p, c_p, n_p, m_p, conv_p = _prompt_call(x_prompt, wh, wg, wr, wpm, wpc, wout, nw, gb, hnw, cw, fnw)
    m_p = m_p[:, :NH, 0].reshape(1, bsz, NH)

    conv0 = state_conv[0].reshape(nseq, (CONV_W - 1) * D_C)
    c_s, n_s, m_s, y_s, conv_s = _sample_state_call(
        head_s, vt_s, g_s, gb, state_mlstm_C, state_mlstm_n[0].reshape(nseq, D_QK), state_mlstm_m,
        xs, rest_s, conv0, wpm, wpc, wout, hnw, cw, fnw)

    return (y_p, y_s.reshape(nseq, 1, D_MODEL), c_p, n_p, m_p, conv_p,
            c_s, n_s.reshape(1, nseq, NH, DQK), m_s,
            conv_s.reshape(1, nseq, CONV_W - 1, D_C))
```

```python
import jax
import jax.numpy as jnp
from jax import lax
from jax.experimental import pallas as pl
from jax.experimental.pallas import tpu as pltpu
from jax.experimental.pallas import tpu_sc as plsc

F32 = jnp.float32
BF16 = jnp.bfloat16

D_MODEL = 1024
NH = 4
DHV = 256
DQK = 128
D_QK = NH * DQK
D_M = NH * DHV
D_C = D_MODEL
CONV_W = 3
EPS = 1e-6
NEG_BIG = -1e30
K_SCALE = DQK ** -0.5

LANES = 128
SUBLANES = 8

OFF_Q = 0
OFF_K = OFF_Q + D_QK
OFF_V = OFF_K + D_QK
N_HEAD = OFF_V + D_M
N_GATE_COLS = 2 * NH
R_O = 0
R_ZM = R_O + D_M
R_BG = R_ZM + D_M
R_CG = R_BG + D_C
R_XC = R_CG + D_C
R_ZC = R_XC + D_C
R_GM = R_ZC + D_C
R_GC = R_GM + D_MODEL
N_REST = R_GC + D_MODEL

PROMPT_BLOCK = 512
MLSTM_CHUNK = 256
SAMPLE_SEQ_BLOCK = 8
SAMPLE_PROJ_COLS = 1024
VMEM_LIMIT_PROMPT = 60 * 1024 * 1024
VMEM_LIMIT_SAMPLE = 48 * 1024 * 1024


def _sigmoid(x):
    return 0.5 * jnp.tanh(0.5 * x) + 0.5


def _silu(x):
    return x * _sigmoid(x)


def _log_sigmoid(x):
    return jnp.minimum(x, 0.0) - jnp.log1p(jnp.exp(-jnp.abs(x)))


def _rmsnorm(x, w):
    return x * lax.rsqrt(jnp.mean(x * x, axis=-1, keepdims=True) + EPS) * w


def _dot(a, b):
    return jnp.dot(a, b, preferred_element_type=F32)


def _dot_nt(a, b):
    return lax.dot_general(a, b, (((1,), (1,)), ((), ())), preferred_element_type=F32)


def _dot_tn(a, b):
    return lax.dot_general(a, b, (((0,), (0,)), ((), ())), preferred_element_type=F32)


def _split3(x):
    hi = x.astype(BF16)
    r = x - hi.astype(F32)
    mid = r.astype(BF16)
    lo = (r - mid.astype(F32)).astype(BF16)
    return hi, mid, lo


PREP_COLS = 1024
PREP_CHUNK = 256


PREP_STEPS = N_REST // PREP_COLS
PREP_HEAD_COLS = N_HEAD // PREP_STEPS
PREP_QK_STEPS = (2 * D_QK) // PREP_HEAD_COLS
PREP_SQ_ROWS = D_MODEL // PREP_STEPS


def _weight_prep_kernel(a_ref, c_ref, g_ref, pm_ref, pc_ref, po_ref, xs_ref, nw_ref,
                        wr_ref, wh_ref, wg_ref, wpm_ref, wpc_ref, wout_ref,
                        hs_ref, vt_ref, gs_ref, rs_ref):
    j = pl.program_id(0)
    xn = _rmsnorm(xs_ref[...], nw_ref[...]).astype(BF16)
    for r0 in range(0, PREP_COLS, PREP_CHUNK):
        w_t = a_ref[0, r0:r0 + PREP_CHUNK, :].T.astype(BF16)
        wr_ref[:, r0:r0 + PREP_CHUNK] = w_t
        rs_ref[:, r0:r0 + PREP_CHUNK] = _dot(xn, w_t)
    w_t = c_ref[0].T.astype(BF16)
    wh_ref[...] = w_t
    head = _dot(xn, w_t)
    hs_ref[...] = head

    @pl.when(j >= PREP_QK_STEPS)
    def _():
        vt_ref[...] = head.T

    @pl.when(j == 0)
    def _():
        lane = lax.broadcasted_iota(jnp.int32, (D_MODEL, LANES), 1)
        w_g = jnp.where(lane < N_GATE_COLS, g_ref[0].T, 0.0).astype(BF16)
        wg_ref[...] = w_g
        gs_ref[...] = _dot(xn, w_g)

    wpm_ref[...] = pm_ref[0].astype(BF16)
    wpc_ref[...] = pc_ref[0].astype(BF16)
    wout_ref[...] = po_ref[0].astype(BF16)


def _weight_prep_call(w_in_t, w_proj_m, w_proj_c, w_out, xs, nw):
    nseq = xs.shape[0]
    assert PREP_HEAD_COLS == DHV, "one v head per step"
    rest_row0 = N_HEAD + N_GATE_COLS
    const = lambda shape: pl.BlockSpec(shape, lambda j: (0,) * len(shape))
    sq_in = pl.BlockSpec((1, PREP_SQ_ROWS, D_MODEL), lambda j: (0, j, 0))
    sq_out = pl.BlockSpec((PREP_SQ_ROWS, D_MODEL), lambda j: (j, 0))
    sq_shape = jax.ShapeDtypeStruct((D_MODEL, D_MODEL), BF16)
    return pl.pallas_call(
        _weight_prep_kernel,
        grid=(PREP_STEPS,),
        in_specs=[
            pl.BlockSpec((pl.Element(1), pl.Element(PREP_COLS), pl.Element(D_MODEL)),
                         lambda j: (0, pl.multiple_of(rest_row0 + j * PREP_COLS, SUBLANES), 0)),
            pl.BlockSpec((1, PREP_HEAD_COLS, D_MODEL), lambda j: (0, j, 0)),
            pl.BlockSpec((1, LANES, D_MODEL), lambda j: (0, N_HEAD // LANES, 0)),
            sq_in, sq_in, sq_in,
            const((nseq, D_MODEL)),
            const((1, D_MODEL)),
        ],
        out_specs=(
            pl.BlockSpec((D_MODEL, PREP_COLS), lambda j: (0, j)),
            pl.BlockSpec((D_MODEL, PREP_HEAD_COLS), lambda j: (0, j)),
            const((D_MODEL, LANES)),
            sq_out, sq_out, sq_out,
            pl.BlockSpec((nseq, PREP_HEAD_COLS), lambda j: (0, j)),
            pl.BlockSpec((DHV, nseq), lambda j: (jnp.maximum(j - PREP_QK_STEPS, 0), 0)),
            const((nseq, LANES)),
            pl.BlockSpec((nseq, PREP_COLS), lambda j: (0, j)),
        ),
        out_shape=(
            jax.ShapeDtypeStruct((D_MODEL, N_REST), BF16),
            jax.ShapeDtypeStruct((D_MODEL, N_HEAD), BF16),
            jax.ShapeDtypeStruct((D_MODEL, LANES), BF16),
            sq_shape, sq_shape, sq_shape,
            jax.ShapeDtypeStruct((nseq, N_HEAD), F32),
            jax.ShapeDtypeStruct((D_M, nseq), F32),
            jax.ShapeDtypeStruct((nseq, LANES), F32),
            jax.ShapeDtypeStruct((nseq, N_REST), F32),
        ),
        compiler_params=pltpu.CompilerParams(dimension_semantics=("arbitrary",),
                                             vmem_limit_bytes=VMEM_LIMIT_SAMPLE),
        name="weight_prep",
    )(w_in_t, w_in_t, w_in_t, w_proj_m, w_proj_c, w_out, xs, nw)


def _prompt_kernel(x_ref, wh_ref, wg_ref, wr_ref, wpm_ref, wpc_ref, wout_ref,
                   nw_ref, gb_ref, hnw_ref, cw_ref, fnw_ref,
                   y_ref, c_ref, n_ref, m_ref, conv_ref,
                   hg_s, ubuf_s, ct_s):
    tl = x_ref.shape[1]
    l = pl.program_id(1)

    @pl.when(l == 0)
    def _():
        ct_s[...] = jnp.zeros_like(ct_s)
        n_ref[...] = jnp.zeros_like(n_ref)
        m_ref[...] = jnp.zeros_like(m_ref)
        ubuf_s[0:SUBLANES, :] = jnp.zeros((SUBLANES, D_C), F32)

    x = x_ref[0]
    xn = _rmsnorm(x, nw_ref[...]).astype(BF16)

    def rest(off, width):
        return _dot(xn, wr_ref[:, off:off + width])

    g = _dot(xn, wg_ref[...]) + gb_ref[...]
    gc = jnp.where(lax.broadcasted_iota(jnp.int32, (tl, LANES), 1) < NH, g, _log_sigmoid(g))
    qkv = _dot(xn, wh_ref[...])

    ch = MLSTM_CHUNK
    lane = lax.broadcasted_iota(jnp.int32, (ch, LANES), 1)
    row = lax.broadcasted_iota(jnp.int32, (ch, ch), 0)
    col = lax.broadcasted_iota(jnp.int32, (ch, ch), 1)
    causal = row >= col
    row_g = lax.broadcasted_iota(jnp.int32, (ch, LANES), 0)
    chunk_gates = []
    for r0 in range(0, tl, ch):
        gc_c = gc[r0:r0 + ch, :]
        bc = gc_c
        shift = 1
        while shift < ch:
            bc = bc + jnp.where(row_g >= shift, pltpu.roll(bc, shift, axis=0), 0.0)
            shift *= 2
        gt = jnp.where(lane < NH, gc_c, bc).T
        chunk_gates.append((gc_c, bc, gt))

    def mlstm_unit(ci, h):
        r0 = ci * ch
        gc_c, bc, gt = chunk_gates[ci]
        xn_c = xn[r0:r0 + ch, :]
        sig_o = _sigmoid(_dot(xn_c, wr_ref[:, R_O + h * DHV:R_O + (h + 1) * DHV]))
        silu_z = _silu(_dot(xn_c, wr_ref[:, R_ZM + h * DHV:R_ZM + (h + 1) * DHV]))
        q_f = qkv[r0:r0 + ch, OFF_Q + h * DQK:OFF_Q + (h + 1) * DQK]
        k_f = qkv[r0:r0 + ch, OFF_K + h * DQK:OFF_K + (h + 1) * DQK] * K_SCALE
        v_f = qkv[r0:r0 + ch, OFF_V + h * DHV:OFF_V + (h + 1) * DHV]
        q_b = q_f.astype(BF16)
        k_b = k_f.astype(BF16)
        v_b = v_f.astype(BF16)
        ct_old = ct_s[h]
        n_old = n_ref[0, 0, h:h + 1, :]
        m_old = m_ref[0, h:h + 1, 0:1]

        b_c = bc[:, NH + h:NH + h + 1]
        ig_c = gc_c[:, h:h + 1]
        ig_r = gt[h:h + 1, :]
        b_r = gt[NH + h:NH + h + 1, :]
        log_d = jnp.where(causal, b_c - b_r + ig_r, NEG_BIG)
        inter = b_c + m_old
        m_t = jnp.maximum(inter, jnp.max(log_d, axis=-1, keepdims=True))
        d_m = jnp.exp(log_d - m_t)
        w_int = jnp.exp(inter - m_t)
        s = _dot_nt(q_b, k_b) * d_m
        num = w_int * _dot(q_b, ct_old.astype(BF16)) + _dot(s.astype(BF16), v_b)
        den = w_int * jnp.sum(q_f * n_old, axis=-1, keepdims=True) + jnp.sum(s, axis=-1, keepdims=True)
        h_t = num / jnp.maximum(jnp.abs(den), jnp.exp(-m_t))

        b_end = b_c[ch - 1:ch, :]
        inter_end = b_end + m_old
        m_new = jnp.maximum(inter_end, jnp.max(b_end - b_r + ig_r, axis=-1, keepdims=True))
        w_end = jnp.exp(b_end - b_c + ig_c - m_new)
        f_end = jnp.exp(inter_end - m_new)
        ct_s[h] = f_end * ct_old + _dot_tn(k_b, (w_end * v_f).astype(BF16))
        n_ref[0, 0, h:h + 1, :] = f_end * n_old + jnp.sum(w_end * k_f, axis=0, keepdims=True)
        m_ref[0, h:h + 1, :] = jnp.broadcast_to(m_new, (1, LANES))

        hn = h_t * lax.rsqrt(jnp.mean(h_t * h_t, axis=-1, keepdims=True) + EPS)
        hn = hn * hnw_ref[:, h * DHV:(h + 1) * DHV]
        hg_s[r0:r0 + ch, h * DHV:(h + 1) * DHV] = (hn * sig_o * silu_z).astype(BF16)

    val = {}

    def conv_input():
        u = rest(R_CG, D_C) * rest(R_XC, D_C)
        ubuf_s[SUBLANES:SUBLANES + tl, :] = u
        cv = cw_ref[0:1, :] * ubuf_s[SUBLANES - 2:SUBLANES - 2 + tl, :]
        cv = cv + cw_ref[1:2, :] * ubuf_s[SUBLANES - 1:SUBLANES - 1 + tl, :]
        val["cv"] = cv + cw_ref[2:3, :] * u

    def conv_gate():
        val["yc_in"] = (_silu(rest(R_ZC, D_C)) * rest(R_BG, D_C) * val["cv"]).astype(BF16)

    def conv_proj():
        val["gy_c"] = _sigmoid(rest(R_GC, D_MODEL)) * _dot(val["yc_in"], wpc_ref[...])

    def merge_gate():
        val["sig_gm"] = _sigmoid(rest(R_GM, D_MODEL))

    fillers = [conv_input, conv_gate, conv_proj, merge_gate]
    units = [(ci, h) for ci in range(tl // ch) for h in range(NH)]
    per = len(units) // len(fillers)
    for i, (ci, h) in enumerate(units):
        mlstm_unit(ci, h)
        if (i + 1) % per == 0:
            fillers[(i + 1) // per - 1]()

    ubuf_s[0:SUBLANES, :] = ubuf_s[tl:tl + SUBLANES, :]

    conv_ref[0, 0] = ubuf_s[SUBLANES - (CONV_W - 1):SUBLANES, :]
    for h in range(NH):
        c_ref[0, 0, h] = ct_s[h].T

    y_m = _dot(hg_s[...], wpm_ref[...])
    mix = (val["sig_gm"] * y_m + val["gy_c"]).astype(BF16)
    out = x + _dot(mix, wout_ref[...])
    y_ref[0] = _rmsnorm(out, fnw_ref[...])


def _resident(shape):
    return pl.BlockSpec(shape, lambda *_: (0,) * len(shape), pipeline_mode=pl.Buffered(1))


def _prompt_call(x, wh, wg, wr, wpm, wpc, wout, nw, gb, hnw, cw, fnw):
    bsz, seq, _ = x.shape
    tl = PROMPT_BLOCK
    grid = (bsz, seq // tl)
    out_shape = (
        jax.ShapeDtypeStruct((bsz, seq, D_MODEL), F32),
        jax.ShapeDtypeStruct((1, bsz, NH, DHV, DQK), F32),
        jax.ShapeDtypeStruct((1, bsz, NH, DQK), F32),
        jax.ShapeDtypeStruct((bsz, SUBLANES, LANES), F32),
        jax.ShapeDtypeStruct((1, bsz, CONV_W - 1, D_C), F32),
    )
    in_specs = [
        pl.BlockSpec((1, tl, D_MODEL), lambda b, l: (b, l, 0)),
        _resident((D_MODEL, N_HEAD)),
        _resident((D_MODEL, LANES)),
        _resident((D_MODEL, N_REST)),
        _resident((D_M, D_MODEL)),
        _resident((D_C, D_MODEL)),
        _resident((D_MODEL, D_MODEL)),
        _resident((1, D_MODEL)),
        _resident((1, LANES)),
        _resident((1, D_M)),
        _resident((CONV_W, D_C)),
        _resident((1, D_MODEL)),
    ]
    out_specs = (
        pl.BlockSpec((1, tl, D_MODEL), lambda b, l: (b, l, 0)),
        pl.BlockSpec((1, 1, NH, DHV, DQK), lambda b, l: (0, b, 0, 0, 0)),
        pl.BlockSpec((1, 1, NH, DQK), lambda b, l: (0, b, 0, 0)),
        pl.BlockSpec((1, SUBLANES, LANES), lambda b, l: (b, 0, 0)),
        pl.BlockSpec((1, 1, CONV_W - 1, D_C), lambda b, l: (0, b, 0, 0)),
    )
    return pl.pallas_call(
        _prompt_kernel,
        grid=grid,
        in_specs=in_specs,
        out_specs=out_specs,
        out_shape=out_shape,
        scratch_shapes=[
            pltpu.VMEM((tl, D_M), BF16),
            pltpu.VMEM((tl + 2 * SUBLANES, D_C), F32),
            pltpu.VMEM((NH, DQK, DHV), F32),
        ],
        compiler_params=pltpu.CompilerParams(
            dimension_semantics=("arbitrary", "arbitrary"),
            vmem_limit_bytes=VMEM_LIMIT_PROMPT),
        name="prompt_layer",
    )(x, wh, wg, wr, wpm, wpc, wout, nw, gb, hnw, cw, fnw)


def _sample_state_kernel(qk_ref, vt_ref, g_ref, gb_ref, c_ref, n_ref, m_ref,
                         x_ref, r_ref, conv_ref, wpm_ref, wpc_ref, wout_ref, hnw_ref, cw_ref, fnw_ref,
                         co_ref, no_ref, mo_ref, y_ref, convo_ref,
                         q_s, kw_s, f_s, zt_s, ht_ref):
    nseq = qk_ref.shape[0]
    nb = c_ref.shape[1]
    i = pl.program_id(0)
    lane = lax.broadcasted_iota(jnp.int32, (DHV, nseq), 1)

    @pl.when(i == 0)
    def _():
        g = g_ref[...] + gb_ref[...]
        ig = g[:, 0:NH]
        lf = _log_sigmoid(g[:, NH:2 * NH])
        inter = lf + m_ref[0]
        m_t = jnp.maximum(inter, ig)
        w_end = jnp.exp(ig - m_t)
        f_end = jnp.exp(inter - m_t)
        floor = jnp.exp(-m_t)
        mo_ref[0] = m_t
        lane_z = lax.broadcasted_iota(jnp.int32, (nseq, LANES), 1)
        z = jnp.zeros((nseq, LANES), F32)
        for h in range(NH):
            q_h = qk_ref[:, h * DQK:(h + 1) * DQK]
            k_h = qk_ref[:, D_QK + h * DQK:D_QK + (h + 1) * DQK] * K_SCALE
            n_h = n_ref[:, h * DQK:(h + 1) * DQK]
            w_h = w_end[:, h:h + 1]
            f_h = f_end[:, h:h + 1]
            s = jnp.sum(q_h * k_h, axis=-1, keepdims=True) * w_h
            den = f_h * jnp.sum(n_h * q_h, axis=-1, keepdims=True) + s
            denom = jnp.maximum(jnp.abs(den), floor[:, h:h + 1])
            no_ref[:, h * DQK:(h + 1) * DQK] = f_h * n_h + w_h * k_h
            q_s[h] = q_h.astype(BF16)
            kw_s[h] = (w_h * k_h).astype(BF16)
            f_s[h] = jnp.broadcast_to(f_h, (nseq, LANES))
            z = jnp.where(lane_z == h, f_h, z)
            z = jnp.where(lane_z == NH + h, s, z)
            z = jnp.where(lane_z == 2 * NH + h, denom, z)
        zt_s[...] = z.T
        ht_ref[...] = jnp.zeros_like(ht_ref)

    for h in range(NH):
        acc = ht_ref[h * DHV:(h + 1) * DHV, :]
        v_t = vt_ref[h * DHV:(h + 1) * DHV, :]
        for j in range(nb):
            b = i * nb + j
            sel = lane == b
            c_old = c_ref[0, j, h]
            acc = jnp.where(sel, _dot_nt(c_old.astype(BF16), q_s[h]), acc)
            upd = _dot(jnp.where(sel, v_t, 0.0).astype(BF16), kw_s[h])
            co_ref[0, j, h] = f_s[h, pl.ds(b, 1), :] * c_old + upd
        ht_ref[h * DHV:(h + 1) * DHV, :] = acc

    @pl.when(i == pl.num_programs(0) - 1)
    def _():
        def piece(off, width):
            return r_ref[:, off:off + width]

        for h in range(NH):
            num = (zt_s[h:h + 1, :] * ht_ref[h * DHV:(h + 1) * DHV, :]
                   + zt_s[NH + h:NH + h + 1, :] * vt_ref[h * DHV:(h + 1) * DHV, :])
            h_t = (num / zt_s[2 * NH + h:2 * NH + h + 1, :]).T
            hn = h_t * lax.rsqrt(jnp.mean(h_t * h_t, axis=-1, keepdims=True) + EPS)
            hn = hn * hnw_ref[:, h * DHV:(h + 1) * DHV]
            o_h = piece(R_O + h * DHV, DHV)
            zm_h = piece(R_ZM + h * DHV, DHV)
            hg = (hn * _sigmoid(o_h) * _silu(zm_h)).astype(BF16)
            part = _dot(hg, wpm_ref[h * DHV:(h + 1) * DHV, :])
            y_m = part if h == 0 else y_m + part

        u = piece(R_CG, D_C) * piece(R_XC, D_C)
        cv = cw_ref[0:1, :] * conv_ref[:, 0:D_C]
        cv = cv + cw_ref[1:2, :] * conv_ref[:, D_C:2 * D_C]
        cv = cv + cw_ref[2:3, :] * u
        convo_ref[:, 0:D_C] = conv_ref[:, D_C:2 * D_C]
        convo_ref[:, D_C:2 * D_C] = u
        yc_in = (_silu(piece(R_ZC, D_C)) * piece(R_BG, D_C) * cv).astype(BF16)
        y_c = _dot(yc_in, wpc_ref[...])
        mix = (_sigmoid(piece(R_GM, D_MODEL)) * y_m + _sigmoid(piece(R_GC, D_MODEL)) * y_c).astype(BF16)
        out = x_ref[...] + _dot(mix, wout_ref[...])
        y_ref[...] = _rmsnorm(out, fnw_ref[...])


def _sample_state_call(head_s, vt, g, gb, c0, n0, m0, x, rest, conv0, wpm, wpc, wout, hnw, cw, fnw):
    nseq = x.shape[0]
    nb = SAMPLE_SEQ_BLOCK
    c_spec = pl.BlockSpec((1, nb, NH, DHV, DQK), lambda i: (0, i, 0, 0, 0))
    const = lambda shape: pl.BlockSpec(shape, lambda i: (0,) * len(shape))
    return pl.pallas_call(
        _sample_state_kernel,
        grid=(nseq // nb,),
        in_specs=[
            _resident((nseq, 2 * D_QK)),
            _resident((D_M, nseq)),
            _resident((nseq, LANES)),
            _resident((1, LANES)),
            c_spec,
            _resident((nseq, D_QK)),
            _resident((1, nseq, NH)),
            _resident((nseq, D_MODEL)),
            _resident((nseq, N_REST)),
            _resident((nseq, (CONV_W - 1) * D_C)),
            _resident((D_M, D_MODEL)),
            _resident((D_C, D_MODEL)),
            _resident((D_MODEL, D_MODEL)),
            _resident((1, D_M)),
            _resident((CONV_W, D_C)),
            _resident((1, D_MODEL)),
        ],
        out_specs=(
            c_spec,
            const((nseq, D_QK)),
            const((1, nseq, NH)),
            const((nseq, D_MODEL)),
            const((nseq, (CONV_W - 1) * D_C)),
        ),
        out_shape=(
            jax.ShapeDtypeStruct(c0.shape, F32),
            jax.ShapeDtypeStruct((nseq, D_QK), F32),
            jax.ShapeDtypeStruct(m0.shape, F32),
            jax.ShapeDtypeStruct((nseq, D_MODEL), F32),
            jax.ShapeDtypeStruct((nseq, (CONV_W - 1) * D_C), F32),
        ),
        scratch_shapes=[
            pltpu.VMEM((NH, nseq, DQK), BF16),
            pltpu.VMEM((NH, nseq, DQK), BF16),
            pltpu.VMEM((NH, nseq, LANES), F32),
            pltpu.VMEM((LANES, nseq), F32),
            pltpu.VMEM((D_M, nseq), F32),
        ],
        compiler_params=pltpu.CompilerParams(dimension_semantics=("arbitrary",),
                                             vmem_limit_bytes=VMEM_LIMIT_SAMPLE),
        name="sample_state",
    )(head_s, vt, g, gb, c0, n0, m0, x, rest, conv0, wpm, wpc, wout, hnw, cw, fnw)


SC_LANES = 16
SC_CORES = 2
SC_SUBCORES = 16
SC_PARAM_ROWS = 8


def _sample_gate_scalars(g_ref, gb_ref, m_ref):
    g = g_ref[...] + gb_ref[...]
    ig = g[:, 0:NH]
    lf = _log_sigmoid(g[:, NH:2 * NH])
    inter = lf + m_ref[0]
    m_t = jnp.maximum(inter, ig)
    return m_t, jnp.exp(ig - m_t), jnp.exp(inter - m_t), jnp.exp(-m_t)


def _sample_gates_kernel(hs_ref, g_ref, gb_ref, n_ref, m_ref, kw_ref, fb_ref, no_ref, mo_ref):
    nseq = hs_ref.shape[0]
    m_t, w_end, f_end, _ = _sample_gate_scalars(g_ref, gb_ref, m_ref)
    mo_ref[0] = m_t
    for h in range(NH):
        k_h = hs_ref[:, OFF_K + h * DQK:OFF_K + (h + 1) * DQK] * K_SCALE
        n_h = n_ref[:, h * DQK:(h + 1) * DQK]
        w_h = w_end[:, h:h + 1]
        f_h = f_end[:, h:h + 1]
        kw_ref[:, h * DQK:(h + 1) * DQK] = w_h * k_h
        fb_ref[:, h * DQK:(h + 1) * DQK] = jnp.broadcast_to(f_h, (nseq, DQK))
        no_ref[:, h * DQK:(h + 1) * DQK] = f_h * n_h + w_h * k_h


def _sample_gates_call(head_s, g, gb, n0, m0):
    nseq = head_s.shape[0]
    args = (head_s, g, gb, n0, m0)
    full = lambda a: pl.BlockSpec(a.shape, lambda i: (0,) * a.ndim)
    row = jax.ShapeDtypeStruct((nseq, D_QK), F32)
    return pl.pallas_call(
        _sample_gates_kernel,
        grid=(1,),
        in_specs=[full(a) for a in args],
        out_specs=(full(row), full(row), full(row), full(m0)),
        out_shape=(row, row, row, jax.ShapeDtypeStruct(m0.shape, F32)),
        name="sample_gates",
    )(*args)


def _sample_state_sc_call(c0, params):
    npairs = c0.shape[0]
    nworkers = SC_CORES * SC_SUBCORES
    per_worker = npairs // nworkers
    nvec = DQK // SC_LANES
    cq_rows = DHV * SC_LANES // DQK
    mesh = plsc.VectorSubcoreMesh(core_axis_name="c", subcore_axis_name="s")

    def body(c_hbm, p_hbm, co_hbm, cq_hbm, c_v, p_v, cq_v):
        wid = lax.axis_index("c") * SC_SUBCORES + lax.axis_index("s")

        @pl.loop(0, per_worker)
        def _(t):
            pair = wid * per_worker + t
            pltpu.sync_copy(c_hbm.at[pair], c_v)
            pltpu.sync_copy(p_hbm.at[pair], p_v)
            q = [p_v[0, pl.ds(SC_LANES * j, SC_LANES)] for j in range(nvec)]
            kw = [p_v[1, pl.ds(SC_LANES * j, SC_LANES)] for j in range(nvec)]
            f_vec = p_v[4, pl.ds(0, SC_LANES)]
            zero_i = jnp.zeros((SC_LANES,), jnp.int32)

            @pl.loop(0, DHV)
            def _(r):
                v_r = plsc.load_gather(p_v, [zero_i + (2 + r // DQK), zero_i + r % DQK])
                acc = jnp.zeros((SC_LANES,), F32)
                for j in range(nvec):
                    c = c_v[r, pl.ds(SC_LANES * j, SC_LANES)]
                    acc = acc + c * q[j]
                    c_v[r, pl.ds(SC_LANES * j, SC_LANES)] = f_vec * c + v_r * kw[j]
                cq_v[r // nvec, pl.ds((r % nvec) * SC_LANES, SC_LANES)] = acc

            pltpu.sync_copy(c_v, co_hbm.at[pair])
            pltpu.sync_copy(cq_v, cq_hbm.at[pair])

    return pl.kernel(
        body,
        out_type=(jax.ShapeDtypeStruct(c0.shape, F32),
                  jax.ShapeDtypeStruct((npairs, cq_rows, DQK), F32)),
        mesh=mesh,
        scratch_types=[pltpu.VMEM((DHV, DQK), F32),
                       pltpu.VMEM((SC_PARAM_ROWS, DQK), F32),
                       pltpu.VMEM((cq_rows, DQK), F32)],
        compiler_params=pltpu.CompilerParams(use_tc_tiling_on_sc=True, needs_layout_passes=False),
        name="sample_state_sc",
    )(c0, params)


def _sample_tail_kernel(cq_ref, hs_ref, g_ref, gb_ref, n_ref, m_ref,
                        x_ref, r_ref, conv_ref, wpm_ref, wpc_ref, wout_ref, hnw_ref, cw_ref, fnw_ref,
                        y_ref, convo_ref):
    def piece(off, width):
        return r_ref[:, off:off + width]

    m_t, w_end, f_end, floor = _sample_gate_scalars(g_ref, gb_ref, m_ref)
    for h in range(NH):
        q_h = hs_ref[:, OFF_Q + h * DQK:OFF_Q + (h + 1) * DQK]
        k_h = hs_ref[:, OFF_K + h * DQK:OFF_K + (h + 1) * DQK] * K_SCALE
        v_h = hs_ref[:, OFF_V + h * DHV:OFF_V + (h + 1) * DHV]
        n_h = n_ref[:, h * DQK:(h + 1) * DQK]
        w_h = w_end[:, h:h + 1]
        f_h = f_end[:, h:h + 1]
        s = jnp.sum(q_h * k_h, axis=-1, keepdims=True) * w_h
        den = f_h * jnp.sum(n_h * q_h, axis=-1, keepdims=True) + s
        num = f_h * cq_ref[:, h * DHV:(h + 1) * DHV] + s * v_h
        h_t = num / jnp.maximum(jnp.abs(den), floor[:, h:h + 1])
        hn = h_t * lax.rsqrt(jnp.mean(h_t * h_t, axis=-1, keepdims=True) + EPS)
        hn = hn * hnw_ref[:, h * DHV:(h + 1) * DHV]
        o_h = piece(R_O + h * DHV, DHV)
        zm_h = piece(R_ZM + h * DHV, DHV)
        hg = (hn * _sigmoid(o_h) * _silu(zm_h)).astype(BF16)
        part = _dot(hg, wpm_ref[h * DHV:(h + 1) * DHV, :])
        y_m = part if h == 0 else y_m + part

    u = piece(R_CG, D_C) * piece(R_XC, D_C)
    cv = cw_ref[0:1, :] * conv_ref[:, 0:D_C]
    cv = cv + cw_ref[1:2, :] * conv_ref[:, D_C:2 * D_C]
    cv = cv + cw_ref[2:3, :] * u
    convo_ref[:, 0:D_C] = conv_ref[:, D_C:2 * D_C]
    convo_ref[:, D_C:2 * D_C] = u
    yc_in = (_silu(piece(R_ZC, D_C)) * piece(R_BG, D_C) * cv).astype(BF16)
    y_c = _dot(yc_in, wpc_ref[...])
    mix = (_sigmoid(piece(R_GM, D_MODEL)) * y_m + _sigmoid(piece(R_GC, D_MODEL)) * y_c).astype(BF16)
    out = x_ref[...] + _dot(mix, wout_ref[...])
    y_ref[...] = _rmsnorm(out, fnw_ref[...])


def _sample_tail_call(cq, head_s, g, gb, n0, m0, x, rest, conv0, wpm, wpc, wout, hnw, cw, fnw):
    nseq = x.shape[0]
    args = (cq, head_s, g, gb, n0, m0, x, rest, conv0, wpm, wpc, wout, hnw, cw, fnw)
    full = lambda a: pl.BlockSpec(a.shape, lambda i: (0,) * a.ndim)
    return pl.pallas_call(
        _sample_tail_kernel,
        grid=(1,),
        in_specs=[full(a) for a in args],
        out_specs=(pl.BlockSpec((nseq, D_MODEL), lambda i: (0, 0)),
                   pl.BlockSpec((nseq, (CONV_W - 1) * D_C), lambda i: (0, 0))),
        out_shape=(jax.ShapeDtypeStruct((nseq, D_MODEL), F32),
                   jax.ShapeDtypeStruct((nseq, (CONV_W - 1) * D_C), F32)),
        compiler_params=pltpu.CompilerParams(dimension_semantics=("arbitrary",),
                                             vmem_limit_bytes=VMEM_LIMIT_SAMPLE),
        name="sample_tail",
    )(*args)


def kernel(x_prompt, x_sample, state_mlstm_C, state_mlstm_n, state_mlstm_m, state_conv, norm_w, w_in, b_i, b_f,
           head_norm_w, conv_w, w_proj_m, w_proj_c, w_out, final_norm_w):
    depth = norm_w.shape[0]
    assert depth == 1, "single-layer trunk"
    bsz = x_prompt.shape[0]
    nseq = x_sample.shape[0]

    assert w_in.shape == (1, D_MODEL, N_HEAD + N_GATE_COLS + N_REST)
    gb = jnp.pad(jnp.concatenate([b_i[0], b_f[0]]), (0, LANES - N_GATE_COLS)).reshape(1, LANES)
    nw = norm_w[0].reshape(1, D_MODEL)
    hnw = head_norm_w[0].reshape(1, D_M)
    cw = conv_w[0]
    fnw = final_norm_w.reshape(1, D_MODEL)
    xs = x_sample.reshape(nseq, D_MODEL)
    (wr, wh, wg, wpm, wpc, wout, head_s, vt_s, g_s, rest_s) = _weight_prep_call(
        jnp.swapaxes(w_in, 1, 2), w_proj_m, w_proj_c, w_out, xs, nw)

    n0 = state_mlstm_n[0].reshape(nseq, D_QK)
    kw_s, fb_s, n_s, m_s = _sample_gates_call(head_s, g_s, gb, n0, state_mlstm_m)
    npairs = nseq * NH
    params = jnp.concatenate(
        [head_s[:, OFF_Q:OFF_K].reshape(npairs, 1, DQK),
         kw_s.reshape(npairs, 1, DQK),
         head_s[:, OFF_V:N_HEAD].reshape(npairs, DHV // DQK, DQK),
         fb_s.reshape(npairs, 1, DQK),
         jnp.zeros((npairs, SC_PARAM_ROWS - 3 - DHV // DQK, DQK), F32)], axis=1)
    c_s, cq_part = _sample_state_sc_call(state_mlstm_C.reshape(npairs, DHV, DQK), params)
    c_s = c_s.reshape(state_mlstm_C.shape)

    y_p, c_p, n_p, m_p, conv_p = _prompt_call(x_prompt, wh, wg, wr, wpm, wpc, wout, nw, gb, hnw, cw, fnw)
    m_p = m_p[:, :NH, 0].reshape(1, bsz, NH)

    cq = cq_part.reshape(nseq, D_M, SC_LANES).sum(axis=-1)
    conv0 = state_conv[0].reshape(nseq, (CONV_W - 1) * D_C)
    y_s, conv_s = _sample_tail_call(cq, head_s, g_s, gb, n0, state_mlstm_m,
                                    xs, rest_s, conv0, wpm, wpc, wout, hnw, cw, fnw)

    return (y_p, y_s.reshape(nseq, 1, D_MODEL), c_p, n_p, m_p, conv_p,
            c_s, n_s.reshape(1, nseq, NH, DQK), m_s,
            conv_s.reshape(1, nseq, CONV_W - 1, D_C))
```

```python
import jax
import jax.numpy as jnp
from jax import lax
from jax.experimental import pallas as pl
from jax.experimental.pallas import tpu as pltpu
from jax.experimental.pallas import tpu_sc as plsc

F32 = jnp.float32
BF16 = jnp.bfloat16

D_MODEL = 1024
NH = 4
DHV = 256
DQK = 128
D_QK = NH * DQK
D_M = NH * DHV
D_C = D_MODEL
CONV_W = 3
EPS = 1e-6
NEG_BIG = -1e30
K_SCALE = DQK ** -0.5

LANES = 128
SUBLANES = 8

OFF_Q = 0
OFF_K = OFF_Q + D_QK
OFF_V = OFF_K + D_QK
N_HEAD = OFF_V + D_M
N_GATE_COLS = 2 * NH
R_O = 0
R_ZM = R_O + D_M
R_BG = R_ZM + D_M
R_CG = R_BG + D_C
R_XC = R_CG + D_C
R_ZC = R_XC + D_C
R_GM = R_ZC + D_C
R_GC = R_GM + D_MODEL
N_REST = R_GC + D_MODEL

PROMPT_BLOCK = 512
MLSTM_CHUNK = 256
SAMPLE_SEQ_BLOCK = 8
SAMPLE_PROJ_COLS = 1024
VMEM_LIMIT_PROMPT = 60 * 1024 * 1024
VMEM_LIMIT_SAMPLE = 48 * 1024 * 1024


def _sigmoid(x):
    return 0.5 * jnp.tanh(0.5 * x) + 0.5


def _silu(x):
    return x * _sigmoid(x)


def _log_sigmoid(x):
    return jnp.minimum(x, 0.0) - jnp.log1p(jnp.exp(-jnp.abs(x)))


def _rmsnorm(x, w):
    return x * lax.rsqrt(jnp.mean(x * x, axis=-1, keepdims=True) + EPS) * w


def _dot(a, b):
    return jnp.dot(a, b, preferred_element_type=F32)


def _dot_nt(a, b):
    return lax.dot_general(a, b, (((1,), (1,)), ((), ())), preferred_element_type=F32)


def _dot_tn(a, b):
    return lax.dot_general(a, b, (((0,), (0,)), ((), ())), preferred_element_type=F32)


def _split3(x):
    hi = x.astype(BF16)
    r = x - hi.astype(F32)
    mid = r.astype(BF16)
    lo = (r - mid.astype(F32)).astype(BF16)
    return hi, mid, lo


PREP_COLS = 1024
PREP_CHUNK = 256


PREP_STEPS = N_REST // PREP_COLS
PREP_HEAD_COLS = N_HEAD // PREP_STEPS
PREP_QK_STEPS = (2 * D_QK) // PREP_HEAD_COLS
PREP_SQ_ROWS = D_MODEL // PREP_STEPS


def _weight_prep_kernel(a_ref, c_ref, g_ref, pm_ref, pc_ref, po_ref, xs_ref, nw_ref,
                        wr_ref, wh_ref, wg_ref, wpm_ref, wpc_ref, wout_ref,
                        hs_ref, vt_ref, gs_ref, rs_ref):
    j = pl.program_id(0)
    xn = _rmsnorm(xs_ref[...], nw_ref[...]).astype(BF16)
    for r0 in range(0, PREP_COLS, PREP_CHUNK):
        w_t = a_ref[0, r0:r0 + PREP_CHUNK, :].T.astype(BF16)
        wr_ref[:, r0:r0 + PREP_CHUNK] = w_t
        rs_ref[:, r0:r0 + PREP_CHUNK] = _dot(xn, w_t)
    w_t = c_ref[0].T.astype(BF16)
    wh_ref[...] = w_t
    head = _dot(xn, w_t)
    hs_ref[...] = head

    @pl.when(j >= PREP_QK_STEPS)
    def _():
        vt_ref[...] = head.T

    @pl.when(j == 0)
    def _():
        lane = lax.broadcasted_iota(jnp.int32, (D_MODEL, LANES), 1)
        w_g = jnp.where(lane < N_GATE_COLS, g_ref[0].T, 0.0).astype(BF16)
        wg_ref[...] = w_g
        gs_ref[...] = _dot(xn, w_g)

    wpm_ref[...] = pm_ref[0].astype(BF16)
    wpc_ref[...] = pc_ref[0].astype(BF16)
    wout_ref[...] = po_ref[0].astype(BF16)


def _weight_prep_call(w_in_t, w_proj_m, w_proj_c, w_out, xs, nw):
    nseq = xs.shape[0]
    assert PREP_HEAD_COLS == DHV, "one v head per step"
    rest_row0 = N_HEAD + N_GATE_COLS
    const = lambda shape: pl.BlockSpec(shape, lambda j: (0,) * len(shape))
    sq_in = pl.BlockSpec((1, PREP_SQ_ROWS, D_MODEL), lambda j: (0, j, 0))
    sq_out = pl.BlockSpec((PREP_SQ_ROWS, D_MODEL), lambda j: (j, 0))
    sq_shape = jax.ShapeDtypeStruct((D_MODEL, D_MODEL), BF16)
    return pl.pallas_call(
        _weight_prep_kernel,
        grid=(PREP_STEPS,),
        in_specs=[
            pl.BlockSpec((pl.Element(1), pl.Element(PREP_COLS), pl.Element(D_MODEL)),
                         lambda j: (0, pl.multiple_of(rest_row0 + j * PREP_COLS, SUBLANES), 0)),
            pl.BlockSpec((1, PREP_HEAD_COLS, D_MODEL), lambda j: (0, j, 0)),
            pl.BlockSpec((1, LANES, D_MODEL), lambda j: (0, N_HEAD // LANES, 0)),
            sq_in, sq_in, sq_in,
            const((nseq, D_MODEL)),
            const((1, D_MODEL)),
        ],
        out_specs=(
            pl.BlockSpec((D_MODEL, PREP_COLS), lambda j: (0, j)),
            pl.BlockSpec((D_MODEL, PREP_HEAD_COLS), lambda j: (0, j)),
            const((D_MODEL, LANES)),
            sq_out, sq_out, sq_out,
            pl.BlockSpec((nseq, PREP_HEAD_COLS), lambda j: (0, j)),
            pl.BlockSpec((DHV, nseq), lambda j: (jnp.maximum(j - PREP_QK_STEPS, 0), 0)),
            const((nseq, LANES)),
            pl.BlockSpec((nseq, PREP_COLS), lambda j: (0, j)),
        ),
        out_shape=(
            jax.ShapeDtypeStruct((D_MODEL, N_REST), BF16),
            jax.ShapeDtypeStruct((D_MODEL, N_HEAD), BF16),
            jax.ShapeDtypeStruct((D_MODEL, LANES), BF16),
            sq_shape, sq_shape, sq_shape,
            jax.ShapeDtypeStruct((nseq, N_HEAD), F32),
            jax.ShapeDtypeStruct((D_M, nseq), F32),
            jax.ShapeDtypeStruct((nseq, LANES), F32),
            jax.ShapeDtypeStruct((nseq, N_REST), F32),
        ),
        compiler_params=pltpu.CompilerParams(dimension_semantics=("arbitrary",),
                                             vmem_limit_bytes=VMEM_LIMIT_SAMPLE),
        name="weight_prep",
    )(w_in_t, w_in_t, w_in_t, w_proj_m, w_proj_c, w_out, xs, nw)


def _prompt_kernel(x_ref, wh_ref, wg_ref, wr_ref, wpm_ref, wpc_ref, wout_ref,
                   nw_ref, gb_ref, hnw_ref, cw_ref, fnw_ref,
                   y_ref, c_ref, n_ref, m_ref, conv_ref,
                   hg_s, ubuf_s, ct_s):
    tl = x_ref.shape[1]
    l = pl.program_id(1)

    @pl.when(l == 0)
    def _():
        ct_s[...] = jnp.zeros_like(ct_s)
        n_ref[...] = jnp.zeros_like(n_ref)
        m_ref[...] = jnp.zeros_like(m_ref)
        ubuf_s[0:SUBLANES, :] = jnp.zeros((SUBLANES, D_C), F32)

    x = x_ref[0]
    xn = _rmsnorm(x, nw_ref[...]).astype(BF16)

    def rest(off, width):
        return _dot(xn, wr_ref[:, off:off + width])

    g = _dot(xn, wg_ref[...]) + gb_ref[...]
    gc = jnp.where(lax.broadcasted_iota(jnp.int32, (tl, LANES), 1) < NH, g, _log_sigmoid(g))
    qkv = _dot(xn, wh_ref[...])

    ch = MLSTM_CHUNK
    lane = lax.broadcasted_iota(jnp.int32, (ch, LANES), 1)
    row = lax.broadcasted_iota(jnp.int32, (ch, ch), 0)
    col = lax.broadcasted_iota(jnp.int32, (ch, ch), 1)
    causal = row >= col
    row_g = lax.broadcasted_iota(jnp.int32, (ch, LANES), 0)
    chunk_gates = []
    for r0 in range(0, tl, ch):
        gc_c = gc[r0:r0 + ch, :]
        bc = gc_c
        shift = 1
        while shift < ch:
            bc = bc + jnp.where(row_g >= shift, pltpu.roll(bc, shift, axis=0), 0.0)
            shift *= 2
        gt = jnp.where(lane < NH, gc_c, bc).T
        chunk_gates.append((gc_c, bc, gt))

    def mlstm_unit(ci, h):
        r0 = ci * ch
        gc_c, bc, gt = chunk_gates[ci]
        xn_c = xn[r0:r0 + ch, :]
        sig_o = _sigmoid(_dot(xn_c, wr_ref[:, R_O + h * DHV:R_O + (h + 1) * DHV]))
        silu_z = _silu(_dot(xn_c, wr_ref[:, R_ZM + h * DHV:R_ZM + (h + 1) * DHV]))
        q_f = qkv[r0:r0 + ch, OFF_Q + h * DQK:OFF_Q + (h + 1) * DQK]
        k_f = qkv[r0:r0 + ch, OFF_K + h * DQK:OFF_K + (h + 1) * DQK] * K_SCALE
        v_f = qkv[r0:r0 + ch, OFF_V + h * DHV:OFF_V + (h + 1) * DHV]
        q_b = q_f.astype(BF16)
        k_b = k_f.astype(BF16)
        v_b = v_f.astype(BF16)
        ct_old = ct_s[h]
        n_old = n_ref[0, 0, h:h + 1, :]
        m_old = m_ref[0, h:h + 1, 0:1]

        b_c = bc[:, NH + h:NH + h + 1]
        ig_c = gc_c[:, h:h + 1]
        ig_r = gt[h:h + 1, :]
        b_r = gt[NH + h:NH + h + 1, :]
        log_d = jnp.where(causal, b_c - b_r + ig_r, NEG_BIG)
        inter = b_c + m_old
        m_t = jnp.maximum(inter, jnp.max(log_d, axis=-1, keepdims=True))
        d_m = jnp.exp(log_d - m_t)
        w_int = jnp.exp(inter - m_t)
        s = _dot_nt(q_b, k_b) * d_m
        num = w_int * _dot(q_b, ct_old.astype(BF16)) + _dot(s.astype(BF16), v_b)
        den = w_int * jnp.sum(q_f * n_old, axis=-1, keepdims=True) + jnp.sum(s, axis=-1, keepdims=True)
        h_t = num / jnp.maximum(jnp.abs(den), jnp.exp(-m_t))

        b_end = b_c[ch - 1:ch, :]
        inter_end = b_end + m_old
        m_new = jnp.maximum(inter_end, jnp.max(b_end - b_r + ig_r, axis=-1, keepdims=True))
        w_end = jnp.exp(b_end - b_c + ig_c - m_new)
        f_end = jnp.exp(inter_end - m_new)
        ct_s[h] = f_end * ct_old + _dot_tn(k_b, (w_end * v_f).astype(BF16))
        n_ref[0, 0, h:h + 1, :] = f_end * n_old + jnp.sum(w_end * k_f, axis=0, keepdims=True)
        m_ref[0, h:h + 1, :] = jnp.broadcast_to(m_new, (1, LANES))

        hn = h_t * lax.rsqrt(jnp.mean(h_t * h_t, axis=-1, keepdims=True) + EPS)
        hn = hn * hnw_ref[:, h * DHV:(h + 1) * DHV]
        hg_s[r0:r0 + ch, h * DHV:(h + 1) * DHV] = (hn * sig_o * silu_z).astype(BF16)

    val = {}

    def conv_input():
        u = rest(R_CG, D_C) * rest(R_XC, D_C)
        ubuf_s[SUBLANES:SUBLANES + tl, :] = u
        cv = cw_ref[0:1, :] * ubuf_s[SUBLANES - 2:SUBLANES - 2 + tl, :]
        cv = cv + cw_ref[1:2, :] * ubuf_s[SUBLANES - 1:SUBLANES - 1 + tl, :]
        val["cv"] = cv + cw_ref[2:3, :] * u

    def conv_gate():
        val["yc_in"] = (_silu(rest(R_ZC, D_C)) * rest(R_BG, D_C) * val["cv"]).astype(BF16)

    def conv_proj():
        val["gy_c"] = _sigmoid(rest(R_GC, D_MODEL)) * _dot(val["yc_in"], wpc_ref[...])

    def merge_gate():
        val["sig_gm"] = _sigmoid(rest(R_GM, D_MODEL))

    fillers = [conv_input, conv_gate, conv_proj, merge_gate]
    units = [(ci, h) for ci in range(tl // ch) for h in range(NH)]
    per = len(units) // len(fillers)
    for i, (ci, h) in enumerate(units):
        mlstm_unit(ci, h)
        if (i + 1) % per == 0:
            fillers[(i + 1) // per - 1]()

    ubuf_s[0:SUBLANES, :] = ubuf_s[tl:tl + SUBLANES, :]

    conv_ref[0, 0] = ubuf_s[SUBLANES - (CONV_W - 1):SUBLANES, :]
    for h in range(NH):
        c_ref[0, 0, h] = ct_s[h].T

    y_m = _dot(hg_s[...], wpm_ref[...])
    mix = (val["sig_gm"] * y_m + val["gy_c"]).astype(BF16)
    out = x + _dot(mix, wout_ref[...])
    y_ref[0] = _rmsnorm(out, fnw_ref[...])


def _resident(shape):
    return pl.BlockSpec(shape, lambda *_: (0,) * len(shape), pipeline_mode=pl.Buffered(1))


def _prompt_call(x, wh, wg, wr, wpm, wpc, wout, nw, gb, hnw, cw, fnw):
    bsz, seq, _ = x.shape
    tl = PROMPT_BLOCK
    grid = (bsz, seq // tl)
    out_shape = (
        jax.ShapeDtypeStruct((bsz, seq, D_MODEL), F32),
        jax.ShapeDtypeStruct((1, bsz, NH, DHV, DQK), F32),
        jax.ShapeDtypeStruct((1, bsz, NH, DQK), F32),
        jax.ShapeDtypeStruct((bsz, SUBLANES, LANES), F32),
        jax.ShapeDtypeStruct((1, bsz, CONV_W - 1, D_C), F32),
    )
    in_specs = [
        pl.BlockSpec((1, tl, D_MODEL), lambda b, l: (b, l, 0)),
        _resident((D_MODEL, N_HEAD)),
        _resident((D_MODEL, LANES)),
        _resident((D_MODEL, N_REST)),
        _resident((D_M, D_MODEL)),
        _resident((D_C, D_MODEL)),
        _resident((D_MODEL, D_MODEL)),
        _resident((1, D_MODEL)),
        _resident((1, LANES)),
        _resident((1, D_M)),
        _resident((CONV_W, D_C)),
        _resident((1, D_MODEL)),
    ]
    out_specs = (
        pl.BlockSpec((1, tl, D_MODEL), lambda b, l: (b, l, 0)),
        pl.BlockSpec((1, 1, NH, DHV, DQK), lambda b, l: (0, b, 0, 0, 0)),
        pl.BlockSpec((1, 1, NH, DQK), lambda b, l: (0, b, 0, 0)),
        pl.BlockSpec((1, SUBLANES, LANES), lambda b, l: (b, 0, 0)),
        pl.BlockSpec((1, 1, CONV_W - 1, D_C), lambda b, l: (0, b, 0, 0)),
    )
    return pl.pallas_call(
        _prompt_kernel,
        grid=grid,
        in_specs=in_specs,
        out_specs=out_specs,
        out_shape=out_shape,
        scratch_shapes=[
            pltpu.VMEM((tl, D_M), BF16),
            pltpu.VMEM((tl + 2 * SUBLANES, D_C), F32),
            pltpu.VMEM((NH, DQK, DHV), F32),
        ],
        compiler_params=pltpu.CompilerParams(
            dimension_semantics=("arbitrary", "arbitrary"),
            vmem_limit_bytes=VMEM_LIMIT_PROMPT),
        name="prompt_layer",
    )(x, wh, wg, wr, wpm, wpc, wout, nw, gb, hnw, cw, fnw)


def _sample_state_kernel(qk_ref, vt_ref, g_ref, gb_ref, c_ref, n_ref, m_ref,
                         x_ref, r_ref, conv_ref, wpm_ref, wpc_ref, wout_ref, hnw_ref, cw_ref, fnw_ref,
                         co_ref, no_ref, mo_ref, y_ref, convo_ref,
                         q_s, kw_s, f_s, zt_s, ht_ref):
    nseq = qk_ref.shape[0]
    nb = c_ref.shape[1]
    i = pl.program_id(0)
    lane = lax.broadcasted_iota(jnp.int32, (DHV, nseq), 1)

    @pl.when(i == 0)
    def _():
        g = g_ref[...] + gb_ref[...]
        ig = g[:, 0:NH]
        lf = _log_sigmoid(g[:, NH:2 * NH])
        inter = lf + m_ref[0]
        m_t = jnp.maximum(inter, ig)
        w_end = jnp.exp(ig - m_t)
        f_end = jnp.exp(inter - m_t)
        floor = jnp.exp(-m_t)
        mo_ref[0] = m_t
        lane_z = lax.broadcasted_iota(jnp.int32, (nseq, LANES), 1)
        z = jnp.zeros((nseq, LANES), F32)
        for h in range(NH):
            q_h = qk_ref[:, h * DQK:(h + 1) * DQK]
            k_h = qk_ref[:, D_QK + h * DQK:D_QK + (h + 1) * DQK] * K_SCALE
            n_h = n_ref[:, h * DQK:(h + 1) * DQK]
            w_h = w_end[:, h:h + 1]
            f_h = f_end[:, h:h + 1]
            s = jnp.sum(q_h * k_h, axis=-1, keepdims=True) * w_h
            den = f_h * jnp.sum(n_h * q_h, axis=-1, keepdims=True) + s
            denom = jnp.maximum(jnp.abs(den), floor[:, h:h + 1])
            no_ref[:, h * DQK:(h + 1) * DQK] = f_h * n_h + w_h * k_h
            q_s[h] = q_h.astype(BF16)
            kw_s[h] = (w_h * k_h).astype(BF16)
            f_s[h] = jnp.broadcast_to(f_h, (nseq, LANES))
            z = jnp.where(lane_z == h, f_h, z)
            z = jnp.where(lane_z == NH + h, s, z)
            z = jnp.where(lane_z == 2 * NH + h, denom, z)
        zt_s[...] = z.T
        ht_ref[...] = jnp.zeros_like(ht_ref)

    for h in range(NH):
        acc = ht_ref[h * DHV:(h + 1) * DHV, :]
        v_t = vt_ref[h * DHV:(h + 1) * DHV, :]
        for j in range(nb):
            b = i * nb + j
            sel = lane == b
            c_old = c_ref[0, j, h]
            acc = jnp.where(sel, _dot_nt(c_old.astype(BF16), q_s[h]), acc)
            upd = _dot(jnp.where(sel, v_t, 0.0).astype(BF16), kw_s[h])
            co_ref[0, j, h] = f_s[h, pl.ds(b, 1), :] * c_old + upd
        ht_ref[h * DHV:(h + 1) * DHV, :] = acc

    @pl.when(i == pl.num_programs(0) - 1)
    def _():
        def piece(off, width):
            return r_ref[:, off:off + width]

        for h in range(NH):
            num = (zt_s[h:h + 1, :] * ht_ref[h * DHV:(h + 1) * DHV, :]
                   + zt_s[NH + h:NH + h + 1, :] * vt_ref[h * DHV:(h + 1) * DHV, :])
            h_t = (num / zt_s[2 * NH + h:2 * NH + h + 1, :]).T
            hn = h_t * lax.rsqrt(jnp.mean(h_t * h_t, axis=-1, keepdims=True) + EPS)
            hn = hn * hnw_ref[:, h * DHV:(h + 1) * DHV]
            o_h = piece(R_O + h * DHV, DHV)
            zm_h = piece(R_ZM + h * DHV, DHV)
            hg = (hn * _sigmoid(o_h) * _silu(zm_h)).astype(BF16)
            part = _dot(hg, wpm_ref[h * DHV:(h + 1) * DHV, :])
            y_m = part if h == 0 else y_m + part

        u = piece(R_CG, D_C) * piece(R_XC, D_C)
        cv = cw_ref[0:1, :] * conv_ref[:, 0:D_C]
        cv = cv + cw_ref[1:2, :] * conv_ref[:, D_C:2 * D_C]
        cv = cv + cw_ref[2:3, :] * u
        convo_ref[:, 0:D_C] = conv_ref[:, D_C:2 * D_C]
        convo_ref[:, D_C:2 * D_C] = u
        yc_in = (_silu(piece(R_ZC, D_C)) * piece(R_BG, D_C) * cv).astype(BF16)
        y_c = _dot(yc_in, wpc_ref[...])
        mix = (_sigmoid(piece(R_GM, D_MODEL)) * y_m + _sigmoid(piece(R_GC, D_MODEL)) * y_c).astype(BF16)
        out = x_ref[...] + _dot(mix, wout_ref[...])
        y_ref[...] = _rmsnorm(out, fnw_ref[...])


def _sample_state_call(head_s, vt, g, gb, c0, n0, m0, x, rest, conv0, wpm, wpc, wout, hnw, cw, fnw):
    nseq = x.shape[0]
    nb = SAMPLE_SEQ_BLOCK
    c_spec = pl.BlockSpec((1, nb, NH, DHV, DQK), lambda i: (0, i, 0, 0, 0))
    const = lambda shape: pl.BlockSpec(shape, lambda i: (0,) * len(shape))
    return pl.pallas_call(
        _sample_state_kernel,
        grid=(nseq // nb,),
        in_specs=[
            _resident((nseq, 2 * D_QK)),
            _resident((D_M, nseq)),
            _resident((nseq, LANES)),
            _resident((1, LANES)),
            c_spec,
            _resident((nseq, D_QK)),
            _resident((1, nseq, NH)),
            _resident((nseq, D_MODEL)),
            _resident((nseq, N_REST)),
            _resident((nseq, (CONV_W - 1) * D_C)),
            _resident((D_M, D_MODEL)),
            _resident((D_C, D_MODEL)),
            _resident((D_MODEL, D_MODEL)),
            _resident((1, D_M)),
            _resident((CONV_W, D_C)),
            _resident((1, D_MODEL)),
        ],
        out_specs=(
            c_spec,
            const((nseq, D_QK)),
            const((1, nseq, NH)),
            const((nseq, D_MODEL)),
            const((nseq, (CONV_W - 1) * D_C)),
        ),
        out_shape=(
            jax.ShapeDtypeStruct(c0.shape, F32),
            jax.ShapeDtypeStruct((nseq, D_QK), F32),
            jax.ShapeDtypeStruct(m0.shape, F32),
            jax.ShapeDtypeStruct((nseq, D_MODEL), F32),
            jax.ShapeDtypeStruct((nseq, (CONV_W - 1) * D_C), F32),
        ),
        scratch_shapes=[
            pltpu.VMEM((NH, nseq, DQK), BF16),
            pltpu.VMEM((NH, nseq, DQK), BF16),
            pltpu.VMEM((NH, nseq, LANES), F32),
            pltpu.VMEM((LANES, nseq), F32),
            pltpu.VMEM((D_M, nseq), F32),
        ],
        compiler_params=pltpu.CompilerParams(dimension_semantics=("arbitrary",),
                                             vmem_limit_bytes=VMEM_LIMIT_SAMPLE),
        name="sample_state",
    )(head_s, vt, g, gb, c0, n0, m0, x, rest, conv0, wpm, wpc, wout, hnw, cw, fnw)


SC_LANES = 16
SC_CORES = 2
SC_SUBCORES = 16
SC_PARAM_ROWS = 8


def _sample_gate_scalars(g_ref, gb_ref, m_ref):
    g = g_ref[...] + gb_ref[...]
    ig = g[:, 0:NH]
    lf = _log_sigmoid(g[:, NH:2 * NH])
    inter = lf + m_ref[0]
    m_t = jnp.maximum(inter, ig)
    return m_t, jnp.exp(ig - m_t), jnp.exp(inter - m_t), jnp.exp(-m_t)


def _sample_gates_kernel(hs_ref, g_ref, gb_ref, n_ref, m_ref, kw_ref, fb_ref, no_ref, mo_ref):
    nseq = hs_ref.shape[0]
    m_t, w_end, f_end, _ = _sample_gate_scalars(g_ref, gb_ref, m_ref)
    mo_ref[0] = m_t
    for h in range(NH):
        k_h = hs_ref[:, OFF_K + h * DQK:OFF_K + (h + 1) * DQK] * K_SCALE
        n_h = n_ref[:, h * DQK:(h + 1) * DQK]
        w_h = w_end[:, h:h + 1]
        f_h = f_end[:, h:h + 1]
        kw_ref[:, h * DQK:(h + 1) * DQK] = w_h * k_h
        fb_ref[:, h * DQK:(h + 1) * DQK] = jnp.broadcast_to(f_h, (nseq, DQK))
        no_ref[:, h * DQK:(h + 1) * DQK] = f_h * n_h + w_h * k_h


def _sample_gates_call(head_s, g, gb, n0, m0):
    nseq = head_s.shape[0]
    args = (head_s, g, gb, n0, m0)
    full = lambda a: pl.BlockSpec(a.shape, lambda i: (0,) * a.ndim)
    row = jax.ShapeDtypeStruct((nseq, D_QK), F32)
    return pl.pallas_call(
        _sample_gates_kernel,
        grid=(1,),
        in_specs=[full(a) for a in args],
        out_specs=(full(row), full(row), full(row), full(m0)),
        out_shape=(row, row, row, jax.ShapeDtypeStruct(m0.shape, F32)),
        name="sample_gates",
    )(*args)


def _sample_state_sc_call(c0, params):
    nseq = c0.shape[1]
    nworkers = SC_CORES * SC_SUBCORES
    per_worker = nseq * NH // nworkers
    nvec = DQK // SC_LANES
    mesh = plsc.VectorSubcoreMesh(core_axis_name="c", subcore_axis_name="s")

    def body(c_hbm, p_hbm, co_hbm, cq_hbm, c_v, p_v, cq_v):
        wid = lax.axis_index("c") * SC_SUBCORES + lax.axis_index("s")
        lane = lax.iota(jnp.int32, SC_LANES)
        zero_i = jnp.zeros((SC_LANES,), jnp.int32)

        @pl.loop(0, per_worker)
        def _(t):
            pair = wid * per_worker + t
            b = pair // NH
            h = pair % NH
            pltpu.sync_copy(c_hbm.at[0, b, h], c_v)
            pltpu.sync_copy(p_hbm.at[pair], p_v)
            q = [p_v[0, pl.ds(SC_LANES * j, SC_LANES)] for j in range(nvec)]
            kw = [p_v[1, pl.ds(SC_LANES * j, SC_LANES)] for j in range(nvec)]
            f_vec = p_v[4, pl.ds(0, SC_LANES)]

            @pl.loop(0, DHV // SC_LANES)
            def _(g):
                cq_vec = jnp.zeros((SC_LANES,), F32)
                for i in range(SC_LANES):
                    r = g * SC_LANES + i
                    v_r = plsc.load_gather(p_v, [zero_i + (2 + r // DQK), zero_i + r % DQK])
                    acc = jnp.zeros((SC_LANES,), F32)
                    for j in range(nvec):
                        c = c_v[r, pl.ds(SC_LANES * j, SC_LANES)]
                        acc = acc + c * q[j]
                        c_v[r, pl.ds(SC_LANES * j, SC_LANES)] = f_vec * c + v_r * kw[j]
                    cq_vec = jnp.where(lane == i, jnp.sum(acc), cq_vec)
                cq_v[pl.ds(g * SC_LANES, SC_LANES)] = cq_vec

            pltpu.sync_copy(c_v, co_hbm.at[0, b, h])
            pltpu.sync_copy(cq_v, cq_hbm.at[b, pl.ds(h * DHV, DHV)])

    return pl.kernel(
        body,
        out_type=(jax.ShapeDtypeStruct(c0.shape, F32),
                  jax.ShapeDtypeStruct((nseq, D_M), F32)),
        mesh=mesh,
        scratch_types=[pltpu.VMEM((DHV, DQK), F32),
                       pltpu.VMEM((SC_PARAM_ROWS, DQK), F32),
                       pltpu.VMEM((DHV,), F32)],
        compiler_params=pltpu.CompilerParams(use_tc_tiling_on_sc=True, needs_layout_passes=False),
        name="sample_state_sc",
    )(c0, params)


def _sample_tail_kernel(cq_ref, hs_ref, g_ref, gb_ref, n_ref, m_ref,
                        x_ref, r_ref, conv_ref, wpm_ref, wpc_ref, wout_ref, hnw_ref, cw_ref, fnw_ref,
                        y_ref, convo_ref):
    def piece(off, width):
        return r_ref[:, off:off + width]

    m_t, w_end, f_end, floor = _sample_gate_scalars(g_ref, gb_ref, m_ref)
    for h in range(NH):
        q_h = hs_ref[:, OFF_Q + h * DQK:OFF_Q + (h + 1) * DQK]
        k_h = hs_ref[:, OFF_K + h * DQK:OFF_K + (h + 1) * DQK] * K_SCALE
        v_h = hs_ref[:, OFF_V + h * DHV:OFF_V + (h + 1) * DHV]
        n_h = n_ref[:, h * DQK:(h + 1) * DQK]
        w_h = w_end[:, h:h + 1]
        f_h = f_end[:, h:h + 1]
        s = jnp.sum(q_h * k_h, axis=-1, keepdims=True) * w_h
        den = f_h * jnp.sum(n_h * q_h, axis=-1, keepdims=True) + s
        num = f_h * cq_ref[:, h * DHV:(h + 1) * DHV] + s * v_h
        h_t = num / jnp.maximum(jnp.abs(den), floor[:, h:h + 1])
        hn = h_t * lax.rsqrt(jnp.mean(h_t * h_t, axis=-1, keepdims=True) + EPS)
        hn = hn * hnw_ref[:, h * DHV:(h + 1) * DHV]
        o_h = piece(R_O + h * DHV, DHV)
        zm_h = piece(R_ZM + h * DHV, DHV)
        hg = (hn * _sigmoid(o_h) * _silu(zm_h)).astype(BF16)
        part = _dot(hg, wpm_ref[h * DHV:(h + 1) * DHV, :])
        y_m = part if h == 0 else y_m + part

    u = piece(R_CG, D_C) * piece(R_XC, D_C)
    cv = cw_ref[0:1, :] * conv_ref[:, 0:D_C]
    cv = cv + cw_ref[1:2, :] * conv_ref[:, D_C:2 * D_C]
    cv = cv + cw_ref[2:3, :] * u
    convo_ref[:, 0:D_C] = conv_ref[:, D_C:2 * D_C]
    convo_ref[:, D_C:2 * D_C] = u
    yc_in = (_silu(piece(R_ZC, D_C)) * piece(R_BG, D_C) * cv).astype(BF16)
    y_c = _dot(yc_in, wpc_ref[...])
    mix = (_sigmoid(piece(R_GM, D_MODEL)) * y_m + _sigmoid(piece(R_GC, D_MODEL)) * y_c).astype(BF16)
    out = x_ref[...] + _dot(mix, wout_ref[...])
    y_ref[...] = _rmsnorm(out, fnw_ref[...])


def _sample_tail_call(cq, head_s, g, gb, n0, m0, x, rest, conv0, wpm, wpc, wout, hnw, cw, fnw):
    nseq = x.shape[0]
    args = (cq, head_s, g, gb, n0, m0, x, rest, conv0, wpm, wpc, wout, hnw, cw, fnw)
    full = lambda a: pl.BlockSpec(a.shape, lambda i: (0,) * a.ndim)
    return pl.pallas_call(
        _sample_tail_kernel,
        grid=(1,),
        in_specs=[full(a) for a in args],
        out_specs=(pl.BlockSpec((nseq, D_MODEL), lambda i: (0, 0)),
                   pl.BlockSpec((nseq, (CONV_W - 1) * D_C), lambda i: (0, 0))),
        out_shape=(jax.ShapeDtypeStruct((nseq, D_MODEL), F32),
                   jax.ShapeDtypeStruct((nseq, (CONV_W - 1) * D_C), F32)),
        compiler_params=pltpu.CompilerParams(dimension_semantics=("arbitrary",),
                                             vmem_limit_bytes=VMEM_LIMIT_SAMPLE),
        name="sample_tail",
    )(*args)


def kernel(x_prompt, x_sample, state_mlstm_C, state_mlstm_n, state_mlstm_m, state_conv, norm_w, w_in, b_i, b_f,
           head_norm_w, conv_w, w_proj_m, w_proj_c, w_out, final_norm_w):
    depth = norm_w.shape[0]
    assert depth == 1, "single-layer trunk"
    bsz = x_prompt.shape[0]
    nseq = x_sample.shape[0]

    assert w_in.shape == (1, D_MODEL, N_HEAD + N_GATE_COLS + N_REST)
    gb = jnp.pad(jnp.concatenate([b_i[0], b_f[0]]), (0, LANES - N_GATE_COLS)).reshape(1, LANES)
    nw = norm_w[0].reshape(1, D_MODEL)
    hnw = head_norm_w[0].reshape(1, D_M)
    cw = conv_w[0]
    fnw = final_norm_w.reshape(1, D_MODEL)
    xs = x_sample.reshape(nseq, D_MODEL)
    (wr, wh, wg, wpm, wpc, wout, head_s, vt_s, g_s, rest_s) = _weight_prep_call(
        jnp.swapaxes(w_in, 1, 2), w_proj_m, w_proj_c, w_out, xs, nw)

    n0 = state_mlstm_n[0].reshape(nseq, D_QK)
    kw_s, fb_s, n_s, m_s = _sample_gates_call(head_s, g_s, gb, n0, state_mlstm_m)
    npairs = nseq * NH
    params = jnp.concatenate(
        [head_s[:, OFF_Q:OFF_K].reshape(npairs, 1, DQK),
         kw_s.reshape(npairs, 1, DQK),
         head_s[:, OFF_V:N_HEAD].reshape(npairs, DHV // DQK, DQK),
         fb_s.reshape(npairs, 1, DQK),
         jnp.zeros((npairs, SC_PARAM_ROWS - 3 - DHV // DQK, DQK), F32)], axis=1)
    c_s, cq = _sample_state_sc_call(state_mlstm_C, params)

    y_p, c_p, n_p, m_p, conv_p = _prompt_call(x_prompt, wh, wg, wr, wpm, wpc, wout, nw, gb, hnw, cw, fnw)
    m_p = m_p[:, :NH, 0].reshape(1, bsz, NH)

    conv0 = state_conv[0].reshape(nseq, (CONV_W - 1) * D_C)
    y_s, conv_s = _sample_tail_call(cq, head_s, g_s, gb, n0, state_mlstm_m,
                                    xs, rest_s, conv0, wpm, wpc, wout, hnw, cw, fnw)

    return (y_p, y_s.reshape(nseq, 1, D_MODEL), c_p, n_p, m_p, conv_p,
            c_s, n_s.reshape(1, nseq, NH, DQK), m_s,
            conv_s.reshape(1, nseq, CONV_W - 1, D_C))
```

```python
import jax
import jax.numpy as jnp
from jax import lax
from jax.experimental import pallas as pl
from jax.experimental.pallas import tpu as pltpu
from jax.experimental.pallas import tpu_sc as plsc

F32 = jnp.float32
BF16 = jnp.bfloat16

D_MODEL = 1024
NH = 4
DHV = 256
DQK = 128
D_QK = NH * DQK
D_M = NH * DHV
D_C = D_MODEL
CONV_W = 3
EPS = 1e-6
NEG_BIG = -1e30
K_SCALE = DQK ** -0.5

LANES = 128
SUBLANES = 8

OFF_Q = 0
OFF_K = OFF_Q + D_QK
OFF_V = OFF_K + D_QK
N_HEAD = OFF_V + D_M
N_GATE_COLS = 2 * NH
R_O = 0
R_ZM = R_O + D_M
R_BG = R_ZM + D_M
R_CG = R_BG + D_C
R_XC = R_CG + D_C
R_ZC = R_XC + D_C
R_GM = R_ZC + D_C
R_GC = R_GM + D_MODEL
N_REST = R_GC + D_MODEL

PROMPT_BLOCK = 512
MLSTM_CHUNK = 256
VMEM_LIMIT_PROMPT = 60 * 1024 * 1024
VMEM_LIMIT_SAMPLE = 48 * 1024 * 1024


def _sigmoid(x):
    return 0.5 * jnp.tanh(0.5 * x) + 0.5


def _silu(x):
    return x * _sigmoid(x)


def _log_sigmoid(x):
    return jnp.minimum(x, 0.0) - jnp.log1p(jnp.exp(-jnp.abs(x)))


def _rmsnorm(x, w):
    return x * lax.rsqrt(jnp.mean(x * x, axis=-1, keepdims=True) + EPS) * w


def _dot(a, b):
    return jnp.dot(a, b, preferred_element_type=F32)


def _dot_nt(a, b):
    return lax.dot_general(a, b, (((1,), (1,)), ((), ())), preferred_element_type=F32)


def _dot_tn(a, b):
    return lax.dot_general(a, b, (((0,), (0,)), ((), ())), preferred_element_type=F32)


PREP_COLS = 1024
PREP_CHUNK = 256


PREP_STEPS = N_REST // PREP_COLS
PREP_HEAD_COLS = N_HEAD // PREP_STEPS
PREP_SQ_ROWS = D_MODEL // PREP_STEPS


def _weight_prep_kernel(a_ref, c_ref, g_ref, pm_ref, pc_ref, po_ref, xs_ref, nw_ref,
                        wr_ref, wh_ref, wg_ref, wpm_ref, wpc_ref, wout_ref,
                        hs_ref, gs_ref, rs_ref):
    j = pl.program_id(0)
    xn = _rmsnorm(xs_ref[...], nw_ref[...]).astype(BF16)
    for r0 in range(0, PREP_COLS, PREP_CHUNK):
        w_t = a_ref[0, r0:r0 + PREP_CHUNK, :].T.astype(BF16)
        wr_ref[:, r0:r0 + PREP_CHUNK] = w_t
        rs_ref[:, r0:r0 + PREP_CHUNK] = _dot(xn, w_t)
    w_t = c_ref[0].T.astype(BF16)
    wh_ref[...] = w_t
    hs_ref[...] = _dot(xn, w_t)

    @pl.when(j == 0)
    def _():
        lane = lax.broadcasted_iota(jnp.int32, (D_MODEL, LANES), 1)
        w_g = jnp.where(lane < N_GATE_COLS, g_ref[0].T, 0.0).astype(BF16)
        wg_ref[...] = w_g
        gs_ref[...] = _dot(xn, w_g)

    wpm_ref[...] = pm_ref[0].astype(BF16)
    wpc_ref[...] = pc_ref[0].astype(BF16)
    wout_ref[...] = po_ref[0].astype(BF16)


def _weight_prep_call(w_in_t, w_proj_m, w_proj_c, w_out, xs, nw):
    nseq = xs.shape[0]
    rest_row0 = N_HEAD + N_GATE_COLS
    const = lambda shape: pl.BlockSpec(shape, lambda j: (0,) * len(shape))
    sq_in = pl.BlockSpec((1, PREP_SQ_ROWS, D_MODEL), lambda j: (0, j, 0))
    sq_out = pl.BlockSpec((PREP_SQ_ROWS, D_MODEL), lambda j: (j, 0))
    sq_shape = jax.ShapeDtypeStruct((D_MODEL, D_MODEL), BF16)
    return pl.pallas_call(
        _weight_prep_kernel,
        grid=(PREP_STEPS,),
        in_specs=[
            pl.BlockSpec((pl.Element(1), pl.Element(PREP_COLS), pl.Element(D_MODEL)),
                         lambda j: (0, pl.multiple_of(rest_row0 + j * PREP_COLS, SUBLANES), 0)),
            pl.BlockSpec((1, PREP_HEAD_COLS, D_MODEL), lambda j: (0, j, 0)),
            pl.BlockSpec((1, LANES, D_MODEL), lambda j: (0, N_HEAD // LANES, 0)),
            sq_in, sq_in, sq_in,
            const((nseq, D_MODEL)),
            const((1, D_MODEL)),
        ],
        out_specs=(
            pl.BlockSpec((D_MODEL, PREP_COLS), lambda j: (0, j)),
            pl.BlockSpec((D_MODEL, PREP_HEAD_COLS), lambda j: (0, j)),
            const((D_MODEL, LANES)),
            sq_out, sq_out, sq_out,
            pl.BlockSpec((nseq, PREP_HEAD_COLS), lambda j: (0, j)),
            const((nseq, LANES)),
            pl.BlockSpec((nseq, PREP_COLS), lambda j: (0, j)),
        ),
        out_shape=(
            jax.ShapeDtypeStruct((D_MODEL, N_REST), BF16),
            jax.ShapeDtypeStruct((D_MODEL, N_HEAD), BF16),
            jax.ShapeDtypeStruct((D_MODEL, LANES), BF16),
            sq_shape, sq_shape, sq_shape,
            jax.ShapeDtypeStruct((nseq, N_HEAD), F32),
            jax.ShapeDtypeStruct((nseq, LANES), F32),
            jax.ShapeDtypeStruct((nseq, N_REST), F32),
        ),
        compiler_params=pltpu.CompilerParams(dimension_semantics=("arbitrary",),
                                             vmem_limit_bytes=VMEM_LIMIT_SAMPLE),
        name="weight_prep",
    )(w_in_t, w_in_t, w_in_t, w_proj_m, w_proj_c, w_out, xs, nw)


def _prompt_kernel(x_ref, wh_ref, wg_ref, wr_ref, wpm_ref, wpc_ref, wout_ref,
                   nw_ref, gb_ref, hnw_ref, cw_ref, fnw_ref,
                   y_ref, c_ref, n_ref, m_ref, conv_ref,
                   hg_s, ubuf_s, ct_s):
    tl = x_ref.shape[1]
    l = pl.program_id(1)

    @pl.when(l == 0)
    def _():
        ct_s[...] = jnp.zeros_like(ct_s)
        n_ref[...] = jnp.zeros_like(n_ref)
        m_ref[...] = jnp.zeros_like(m_ref)
        ubuf_s[0:SUBLANES, :] = jnp.zeros((SUBLANES, D_C), F32)

    x = x_ref[0]
    xn = _rmsnorm(x, nw_ref[...]).astype(BF16)

    def rest(off, width):
        return _dot(xn, wr_ref[:, off:off + width])

    g = _dot(xn, wg_ref[...]) + gb_ref[...]
    gc = jnp.where(lax.broadcasted_iota(jnp.int32, (tl, LANES), 1) < NH, g, _log_sigmoid(g))
    qkv = _dot(xn, wh_ref[...])

    ch = MLSTM_CHUNK
    lane = lax.broadcasted_iota(jnp.int32, (ch, LANES), 1)
    row = lax.broadcasted_iota(jnp.int32, (ch, ch), 0)
    col = lax.broadcasted_iota(jnp.int32, (ch, ch), 1)
    causal = row >= col
    row_g = lax.broadcasted_iota(jnp.int32, (ch, LANES), 0)
    chunk_gates = []
    for r0 in range(0, tl, ch):
        gc_c = gc[r0:r0 + ch, :]
        bc = gc_c
        shift = 1
        while shift < ch:
            bc = bc + jnp.where(row_g >= shift, pltpu.roll(bc, shift, axis=0), 0.0)
            shift *= 2
        gt = jnp.where(lane < NH, gc_c, bc).T
        chunk_gates.append((gc_c, bc, gt))

    def mlstm_unit(ci, h):
        r0 = ci * ch
        gc_c, bc, gt = chunk_gates[ci]
        xn_c = xn[r0:r0 + ch, :]
        sig_o = _sigmoid(_dot(xn_c, wr_ref[:, R_O + h * DHV:R_O + (h + 1) * DHV]))
        silu_z = _silu(_dot(xn_c, wr_ref[:, R_ZM + h * DHV:R_ZM + (h + 1) * DHV]))
        q_f = qkv[r0:r0 + ch, OFF_Q + h * DQK:OFF_Q + (h + 1) * DQK]
        k_f = qkv[r0:r0 + ch, OFF_K + h * DQK:OFF_K + (h + 1) * DQK] * K_SCALE
        v_f = qkv[r0:r0 + ch, OFF_V + h * DHV:OFF_V + (h + 1) * DHV]
        q_b = q_f.astype(BF16)
        k_b = k_f.astype(BF16)
        v_b = v_f.astype(BF16)
        ct_old = ct_s[h]
        n_old = n_ref[0, 0, h:h + 1, :]
        m_old = m_ref[0, h:h + 1, 0:1]

        b_c = bc[:, NH + h:NH + h + 1]
        ig_c = gc_c[:, h:h + 1]
        ig_r = gt[h:h + 1, :]
        b_r = gt[NH + h:NH + h + 1, :]
        log_d = jnp.where(causal, b_c - b_r + ig_r, NEG_BIG)
        inter = b_c + m_old
        m_t = jnp.maximum(inter, jnp.max(log_d, axis=-1, keepdims=True))
        d_m = jnp.exp(log_d - m_t)
        w_int = jnp.exp(inter - m_t)
        s = _dot_nt(q_b, k_b) * d_m
        num = w_int * _dot(q_b, ct_old.astype(BF16)) + _dot(s.astype(BF16), v_b)
        den = w_int * jnp.sum(q_f * n_old, axis=-1, keepdims=True) + jnp.sum(s, axis=-1, keepdims=True)
        h_t = num / jnp.maximum(jnp.abs(den), jnp.exp(-m_t))

        b_end = b_c[ch - 1:ch, :]
        inter_end = b_end + m_old
        m_new = jnp.maximum(inter_end, jnp.max(b_end - b_r + ig_r, axis=-1, keepdims=True))
        w_end = jnp.exp(b_end - b_c + ig_c - m_new)
        f_end = jnp.exp(inter_end - m_new)
        ct_s[h] = f_end * ct_old + _dot_tn(k_b, (w_end * v_f).astype(BF16))
        n_ref[0, 0, h:h + 1, :] = f_end * n_old + jnp.sum(w_end * k_f, axis=0, keepdims=True)
        m_ref[0, h:h + 1, :] = jnp.broadcast_to(m_new, (1, LANES))

        hn = h_t * lax.rsqrt(jnp.mean(h_t * h_t, axis=-1, keepdims=True) + EPS)
        hn = hn * hnw_ref[:, h * DHV:(h + 1) * DHV]
        hg_s[r0:r0 + ch, h * DHV:(h + 1) * DHV] = (hn * sig_o * silu_z).astype(BF16)

    val = {}

    def conv_input():
        u = rest(R_CG, D_C) * rest(R_XC, D_C)
        ubuf_s[SUBLANES:SUBLANES + tl, :] = u
        cv = cw_ref[0:1, :] * ubuf_s[SUBLANES - 2:SUBLANES - 2 + tl, :]
        cv = cv + cw_ref[1:2, :] * ubuf_s[SUBLANES - 1:SUBLANES - 1 + tl, :]
        val["cv"] = cv + cw_ref[2:3, :] * u

    def conv_gate():
        val["yc_in"] = (_silu(rest(R_ZC, D_C)) * rest(R_BG, D_C) * val["cv"]).astype(BF16)

    def conv_proj():
        val["gy_c"] = _sigmoid(rest(R_GC, D_MODEL)) * _dot(val["yc_in"], wpc_ref[...])

    def merge_gate():
        val["sig_gm"] = _sigmoid(rest(R_GM, D_MODEL))

    fillers = [conv_input, conv_gate, conv_proj, merge_gate]
    units = [(ci, h) for ci in range(tl // ch) for h in range(NH)]
    per = len(units) // len(fillers)
    for i, (ci, h) in enumerate(units):
        mlstm_unit(ci, h)
        if (i + 1) % per == 0:
            fillers[(i + 1) // per - 1]()

    ubuf_s[0:SUBLANES, :] = ubuf_s[tl:tl + SUBLANES, :]

    conv_ref[0, 0] = ubuf_s[SUBLANES - (CONV_W - 1):SUBLANES, :]
    for h in range(NH):
        c_ref[0, 0, h] = ct_s[h].T

    y_m = _dot(hg_s[...], wpm_ref[...])
    mix = (val["sig_gm"] * y_m + val["gy_c"]).astype(BF16)
    out = x + _dot(mix, wout_ref[...])
    y_ref[0] = _rmsnorm(out, fnw_ref[...])


def _resident(shape):
    return pl.BlockSpec(shape, lambda *_: (0,) * len(shape), pipeline_mode=pl.Buffered(1))


def _prompt_call(x, wh, wg, wr, wpm, wpc, wout, nw, gb, hnw, cw, fnw):
    bsz, seq, _ = x.shape
    tl = PROMPT_BLOCK
    grid = (bsz, seq // tl)
    out_shape = (
        jax.ShapeDtypeStruct((bsz, seq, D_MODEL), F32),
        jax.ShapeDtypeStruct((1, bsz, NH, DHV, DQK), F32),
        jax.ShapeDtypeStruct((1, bsz, NH, DQK), F32),
        jax.ShapeDtypeStruct((bsz, SUBLANES, LANES), F32),
        jax.ShapeDtypeStruct((1, bsz, CONV_W - 1, D_C), F32),
    )
    in_specs = [
        pl.BlockSpec((1, tl, D_MODEL), lambda b, l: (b, l, 0)),
        _resident((D_MODEL, N_HEAD)),
        _resident((D_MODEL, LANES)),
        _resident((D_MODEL, N_REST)),
        _resident((D_M, D_MODEL)),
        _resident((D_C, D_MODEL)),
        _resident((D_MODEL, D_MODEL)),
        _resident((1, D_MODEL)),
        _resident((1, LANES)),
        _resident((1, D_M)),
        _resident((CONV_W, D_C)),
        _resident((1, D_MODEL)),
    ]
    out_specs = (
        pl.BlockSpec((1, tl, D_MODEL), lambda b, l: (b, l, 0)),
        pl.BlockSpec((1, 1, NH, DHV, DQK), lambda b, l: (0, b, 0, 0, 0)),
        pl.BlockSpec((1, 1, NH, DQK), lambda b, l: (0, b, 0, 0)),
        pl.BlockSpec((1, SUBLANES, LANES), lambda b, l: (b, 0, 0)),
        pl.BlockSpec((1, 1, CONV_W - 1, D_C), lambda b, l: (0, b, 0, 0)),
    )
    return pl.pallas_call(
        _prompt_kernel,
        grid=grid,
        in_specs=in_specs,
        out_specs=out_specs,
        out_shape=out_shape,
        scratch_shapes=[
            pltpu.VMEM((tl, D_M), BF16),
            pltpu.VMEM((tl + 2 * SUBLANES, D_C), F32),
            pltpu.VMEM((NH, DQK, DHV), F32),
        ],
        compiler_params=pltpu.CompilerParams(
            dimension_semantics=("arbitrary", "arbitrary"),
            vmem_limit_bytes=VMEM_LIMIT_PROMPT),
        name="prompt_layer",
    )(x, wh, wg, wr, wpm, wpc, wout, nw, gb, hnw, cw, fnw)


SC_LANES = 16
SC_CORES = 2
SC_SUBCORES = 16


def _sample_gate_scalars(g_ref, gb_ref, m_ref):
    g = g_ref[...] + gb_ref[...]
    ig = g[:, 0:NH]
    lf = _log_sigmoid(g[:, NH:2 * NH])
    inter = lf + m_ref[0]
    m_t = jnp.maximum(inter, ig)
    return m_t, jnp.exp(ig - m_t), jnp.exp(inter - m_t), jnp.exp(-m_t)


def _sample_gates_kernel(hs_ref, g_ref, gb_ref, n_ref, m_ref, kw_ref, fb_ref, no_ref, mo_ref):
    nseq = hs_ref.shape[0]
    m_t, w_end, f_end, _ = _sample_gate_scalars(g_ref, gb_ref, m_ref)
    mo_ref[0] = m_t
    for h in range(NH):
        k_h = hs_ref[:, OFF_K + h * DQK:OFF_K + (h + 1) * DQK] * K_SCALE
        n_h = n_ref[:, h * DQK:(h + 1) * DQK]
        w_h = w_end[:, h:h + 1]
        f_h = f_end[:, h:h + 1]
        kw_ref[:, h * DQK:(h + 1) * DQK] = w_h * k_h
        fb_ref[:, h * DQK:(h + 1) * DQK] = jnp.broadcast_to(f_h, (nseq, DQK))
        no_ref[:, h * DQK:(h + 1) * DQK] = f_h * n_h + w_h * k_h


def _sample_gates_call(head_s, g, gb, n0, m0):
    nseq = head_s.shape[0]
    args = (head_s, g, gb, n0, m0)
    full = lambda a: pl.BlockSpec(a.shape, lambda i: (0,) * a.ndim)
    row = jax.ShapeDtypeStruct((nseq, D_QK), F32)
    return pl.pallas_call(
        _sample_gates_kernel,
        grid=(1,),
        in_specs=[full(a) for a in args],
        out_specs=(full(row), full(row), full(row), full(m0)),
        out_shape=(row, row, row, jax.ShapeDtypeStruct(m0.shape, F32)),
        name="sample_gates",
    )(*args)


def _sample_state_sc_call(c0, head_s, kw_s, fb_s):
    nseq = c0.shape[1]
    nworkers = SC_CORES * SC_SUBCORES
    per_worker = nseq * NH // nworkers
    nvec = DQK // SC_LANES
    mesh = plsc.VectorSubcoreMesh(core_axis_name="c", subcore_axis_name="s")

    def body(c_hbm, hs_hbm, kw_hbm, fb_hbm, co_hbm, cq_hbm, c_v, q_v, kw_v, v_v, f_v, cq_v):
        wid = lax.axis_index("c") * SC_SUBCORES + lax.axis_index("s")
        lane = lax.iota(jnp.int32, SC_LANES)
        zero_i = jnp.zeros((SC_LANES,), jnp.int32)

        @pl.loop(0, per_worker)
        def _(t):
            pair = wid * per_worker + t
            b = pair // NH
            h = pair % NH
            pltpu.sync_copy(c_hbm.at[0, b, h], c_v)
            pltpu.sync_copy(hs_hbm.at[b, pl.ds(OFF_Q + h * DQK, DQK)], q_v)
            pltpu.sync_copy(hs_hbm.at[b, pl.ds(OFF_V + h * DHV, DHV)], v_v)
            pltpu.sync_copy(kw_hbm.at[b, pl.ds(h * DQK, DQK)], kw_v)
            pltpu.sync_copy(fb_hbm.at[b, pl.ds(h * DQK, DQK)], f_v)
            q = [q_v[pl.ds(SC_LANES * j, SC_LANES)] for j in range(nvec)]
            kw = [kw_v[pl.ds(SC_LANES * j, SC_LANES)] for j in range(nvec)]
            f_vec = f_v[pl.ds(0, SC_LANES)]

            @pl.loop(0, DHV // SC_LANES)
            def _(g):
                cq_vec = jnp.zeros((SC_LANES,), F32)
                for i in range(SC_LANES):
                    r = g * SC_LANES + i
                    v_r = plsc.load_gather(v_v, [zero_i + r])
                    acc = jnp.zeros((SC_LANES,), F32)
                    for j in range(nvec):
                        c = c_v[r, pl.ds(SC_LANES * j, SC_LANES)]
                        acc = acc + c * q[j]
                        c_v[r, pl.ds(SC_LANES * j, SC_LANES)] = f_vec * c + v_r * kw[j]
                    cq_vec = jnp.where(lane == i, jnp.sum(acc), cq_vec)
                cq_v[pl.ds(g * SC_LANES, SC_LANES)] = cq_vec

            pltpu.sync_copy(c_v, co_hbm.at[0, b, h])
            pltpu.sync_copy(cq_v, cq_hbm.at[b, pl.ds(h * DHV, DHV)])

    return pl.kernel(
        body,
        out_type=(jax.ShapeDtypeStruct(c0.shape, F32),
                  jax.ShapeDtypeStruct((nseq, D_M), F32)),
        mesh=mesh,
        scratch_types=[pltpu.VMEM((DHV, DQK), F32),
                       pltpu.VMEM((DQK,), F32),
                       pltpu.VMEM((DQK,), F32),
                       pltpu.VMEM((DHV,), F32),
                       pltpu.VMEM((DQK,), F32),
                       pltpu.VMEM((DHV,), F32)],
        compiler_params=pltpu.CompilerParams(use_tc_tiling_on_sc=True, needs_layout_passes=False),
        name="sample_state_sc",
    )(c0, head_s, kw_s, fb_s)


def _sample_tail_kernel(cq_ref, hs_ref, g_ref, gb_ref, n_ref, m_ref,
                        x_ref, r_ref, conv_ref, wpm_ref, wpc_ref, wout_ref, hnw_ref, cw_ref, fnw_ref,
                        y_ref, convo_ref):
    def piece(off, width):
        return r_ref[:, off:off + width]

    m_t, w_end, f_end, floor = _sample_gate_scalars(g_ref, gb_ref, m_ref)
    for h in range(NH):
        q_h = hs_ref[:, OFF_Q + h * DQK:OFF_Q + (h + 1) * DQK]
        k_h = hs_ref[:, OFF_K + h * DQK:OFF_K + (h + 1) * DQK] * K_SCALE
        v_h = hs_ref[:, OFF_V + h * DHV:OFF_V + (h + 1) * DHV]
        n_h = n_ref[:, h * DQK:(h + 1) * DQK]
        w_h = w_end[:, h:h + 1]
        f_h = f_end[:, h:h + 1]
        s = jnp.sum(q_h * k_h, axis=-1, keepdims=True) * w_h
        den = f_h * jnp.sum(n_h * q_h, axis=-1, keepdims=True) + s
        num = f_h * cq_ref[:, h * DHV:(h + 1) * DHV] + s * v_h
        h_t = num / jnp.maximum(jnp.abs(den), floor[:, h:h + 1])
        hn = h_t * lax.rsqrt(jnp.mean(h_t * h_t, axis=-1, keepdims=True) + EPS)
        hn = hn * hnw_ref[:, h * DHV:(h + 1) * DHV]
        o_h = piece(R_O + h * DHV, DHV)
        zm_h = piece(R_ZM + h * DHV, DHV)
        hg = (hn * _sigmoid(o_h) * _silu(zm_h)).astype(BF16)
        part = _dot(hg, wpm_ref[h * DHV:(h + 1) * DHV, :])
        y_m = part if h == 0 else y_m + part

    u = piece(R_CG, D_C) * piece(R_XC, D_C)
    cv = cw_ref[0:1, :] * conv_ref[:, 0:D_C]
    cv = cv + cw_ref[1:2, :] * conv_ref[:, D_C:2 * D_C]
    cv = cv + cw_ref[2:3, :] * u
    convo_ref[:, 0:D_C] = conv_ref[:, D_C:2 * D_C]
    convo_ref[:, D_C:2 * D_C] = u
    yc_in = (_silu(piece(R_ZC, D_C)) * piece(R_BG, D_C) * cv).astype(BF16)
    y_c = _dot(yc_in, wpc_ref[...])
    mix = (_sigmoid(piece(R_GM, D_MODEL)) * y_m + _sigmoid(piece(R_GC, D_MODEL)) * y_c).astype(BF16)
    out = x_ref[...] + _dot(mix, wout_ref[...])
    y_ref[...] = _rmsnorm(out, fnw_ref[...])


def _sample_tail_call(cq, head_s, g, gb, n0, m0, x, rest, conv0, wpm, wpc, wout, hnw, cw, fnw):
    nseq = x.shape[0]
    args = (cq, head_s, g, gb, n0, m0, x, rest, conv0, wpm, wpc, wout, hnw, cw, fnw)
    full = lambda a: pl.BlockSpec(a.shape, lambda i: (0,) * a.ndim)
    return pl.pallas_call(
        _sample_tail_kernel,
        grid=(1,),
        in_specs=[full(a) for a in args],
        out_specs=(pl.BlockSpec((nseq, D_MODEL), lambda i: (0, 0)),
                   pl.BlockSpec((nseq, (CONV_W - 1) * D_C), lambda i: (0, 0))),
        out_shape=(jax.ShapeDtypeStruct((nseq, D_MODEL), F32),
                   jax.ShapeDtypeStruct((nseq, (CONV_W - 1) * D_C), F32)),
        compiler_params=pltpu.CompilerParams(dimension_semantics=("arbitrary",),
                                             vmem_limit_bytes=VMEM_LIMIT_SAMPLE),
        name="sample_tail",
    )(*args)


def kernel(x_prompt, x_sample, state_mlstm_C, state_mlstm_n, state_mlstm_m, state_conv, norm_w, w_in, b_i, b_f,
           head_norm_w, conv_w, w_proj_m, w_proj_c, w_out, final_norm_w):
    depth = norm_w.shape[0]
    assert depth == 1, "single-layer trunk"
    bsz = x_prompt.shape[0]
    nseq = x_sample.shape[0]

    assert w_in.shape == (1, D_MODEL, N_HEAD + N_GATE_COLS + N_REST)
    gb = jnp.pad(jnp.concatenate([b_i[0], b_f[0]]), (0, LANES - N_GATE_COLS)).reshape(1, LANES)
    nw = norm_w[0].reshape(1, D_MODEL)
    hnw = head_norm_w[0].reshape(1, D_M)
    cw = conv_w[0]
    fnw = final_norm_w.reshape(1, D_MODEL)
    xs = x_sample.reshape(nseq, D_MODEL)
    (wr, wh, wg, wpm, wpc, wout, head_s, g_s, rest_s) = _weight_prep_call(
        jnp.swapaxes(w_in, 1, 2), w_proj_m, w_proj_c, w_out, xs, nw)

    n0 = state_mlstm_n[0].reshape(nseq, D_QK)
    kw_s, fb_s, n_s, m_s = _sample_gates_call(head_s, g_s, gb, n0, state_mlstm_m)
    c_s, cq = _sample_state_sc_call(state_mlstm_C, head_s, kw_s, fb_s)

    y_p, c_p, n_p, m_p, conv_p = _prompt_call(x_prompt, wh, wg, wr, wpm, wpc, wout, nw, gb, hnw, cw, fnw)
    m_p = m_p[:, :NH, 0].reshape(1, bsz, NH)

    conv0 = state_conv[0].reshape(nseq, (CONV_W - 1) * D_C)
    y_s, conv_s = _sample_tail_call(cq, head_s, g_s, gb, n0, state_mlstm_m,
                                    xs, rest_s, conv0, wpm, wpc, wout, hnw, cw, fnw)

    return (y_p, y_s.reshape(nseq, 1, D_MODEL), c_p, n_p, m_p, conv_p,
            c_s, n_s.reshape(1, nseq, NH, DQK), m_s,
            conv_s.reshape(1, nseq, CONV_W - 1, D_C))
```

```python
import jax
import jax.numpy as jnp
from jax import lax
from jax.experimental import pallas as pl
from jax.experimental.pallas import tpu as pltpu
from jax.experimental.pallas import tpu_sc as plsc

F32 = jnp.float32
BF16 = jnp.bfloat16

D_MODEL = 1024
NH = 4
DHV = 256
DQK = 128
D_QK = NH * DQK
D_M = NH * DHV
D_C = D_MODEL
CONV_W = 3
EPS = 1e-6
NEG_BIG = -1e30
K_SCALE = DQK ** -0.5

LANES = 128
SUBLANES = 8

OFF_Q = 0
OFF_K = OFF_Q + D_QK
OFF_V = OFF_K + D_QK
N_HEAD = OFF_V + D_M
N_GATE_COLS = 2 * NH
R_O = 0
R_ZM = R_O + D_M
R_BG = R_ZM + D_M
R_CG = R_BG + D_C
R_XC = R_CG + D_C
R_ZC = R_XC + D_C
R_GM = R_ZC + D_C
R_GC = R_GM + D_MODEL
N_REST = R_GC + D_MODEL

PROMPT_BLOCK = 512
MLSTM_CHUNK = 256
PROMPT_ORDER = "UUUUFUFUFUFU"
VMEM_LIMIT_PROMPT = 60 * 1024 * 1024
VMEM_LIMIT_SAMPLE = 48 * 1024 * 1024


def _sigmoid(x):
    return 0.5 * jnp.tanh(0.5 * x) + 0.5


def _silu(x):
    return x * _sigmoid(x)


def _log_sigmoid(x):
    return jnp.minimum(x, 0.0) - jnp.log1p(jnp.exp(-jnp.abs(x)))


def _rmsnorm(x, w):
    return x * lax.rsqrt(jnp.mean(x * x, axis=-1, keepdims=True) + EPS) * w


def _dot(a, b):
    return jnp.dot(a, b, preferred_element_type=F32)


def _dot_nt(a, b):
    return lax.dot_general(a, b, (((1,), (1,)), ((), ())), preferred_element_type=F32)


def _dot_tn(a, b):
    return lax.dot_general(a, b, (((0,), (0,)), ((), ())), preferred_element_type=F32)


PREP_COLS = 1024
PREP_CHUNK = 256


PREP_STEPS = N_REST // PREP_COLS
PREP_HEAD_COLS = N_HEAD // PREP_STEPS
PREP_SQ_ROWS = D_MODEL // PREP_STEPS


def _weight_prep_kernel(a_ref, c_ref, g_ref, pm_ref, pc_ref, po_ref, xs_ref, nw_ref,
                        wr_ref, wh_ref, wg_ref, wpm_ref, wpc_ref, wout_ref,
                        hs_ref, gs_ref, rs_ref):
    j = pl.program_id(0)
    xn = _rmsnorm(xs_ref[...], nw_ref[...]).astype(BF16)
    for r0 in range(0, PREP_COLS, PREP_CHUNK):
        w_t = a_ref[0, r0:r0 + PREP_CHUNK, :].T.astype(BF16)
        wr_ref[:, r0:r0 + PREP_CHUNK] = w_t
        rs_ref[:, r0:r0 + PREP_CHUNK] = _dot(xn, w_t)
    w_t = c_ref[0].T.astype(BF16)
    wh_ref[...] = w_t
    hs_ref[...] = _dot(xn, w_t)

    @pl.when(j == 0)
    def _():
        lane = lax.broadcasted_iota(jnp.int32, (D_MODEL, LANES), 1)
        w_g = jnp.where(lane < N_GATE_COLS, g_ref[0].T, 0.0).astype(BF16)
        wg_ref[...] = w_g
        gs_ref[...] = _dot(xn, w_g)

    wpm_ref[...] = pm_ref[0].astype(BF16)
    wpc_ref[...] = pc_ref[0].astype(BF16)
    wout_ref[...] = po_ref[0].astype(BF16)


def _weight_prep_call(w_in_t, w_proj_m, w_proj_c, w_out, xs, nw):
    nseq = xs.shape[0]
    rest_row0 = N_HEAD + N_GATE_COLS
    const = lambda shape: pl.BlockSpec(shape, lambda j: (0,) * len(shape))
    sq_in = pl.BlockSpec((1, PREP_SQ_ROWS, D_MODEL), lambda j: (0, j, 0))
    sq_out = pl.BlockSpec((PREP_SQ_ROWS, D_MODEL), lambda j: (j, 0))
    sq_shape = jax.ShapeDtypeStruct((D_MODEL, D_MODEL), BF16)
    return pl.pallas_call(
        _weight_prep_kernel,
        grid=(PREP_STEPS,),
        in_specs=[
            pl.BlockSpec((pl.Element(1), pl.Element(PREP_COLS), pl.Element(D_MODEL)),
                         lambda j: (0, pl.multiple_of(rest_row0 + j * PREP_COLS, SUBLANES), 0)),
            pl.BlockSpec((1, PREP_HEAD_COLS, D_MODEL), lambda j: (0, j, 0)),
            pl.BlockSpec((1, LANES, D_MODEL), lambda j: (0, N_HEAD // LANES, 0)),
            sq_in, sq_in, sq_in,
            const((nseq, D_MODEL)),
            const((1, D_MODEL)),
        ],
        out_specs=(
            pl.BlockSpec((D_MODEL, PREP_COLS), lambda j: (0, j)),
            pl.BlockSpec((D_MODEL, PREP_HEAD_COLS), lambda j: (0, j)),
            const((D_MODEL, LANES)),
            sq_out, sq_out, sq_out,
            pl.BlockSpec((nseq, PREP_HEAD_COLS), lambda j: (0, j)),
            const((nseq, LANES)),
            pl.BlockSpec((nseq, PREP_COLS), lambda j: (0, j)),
        ),
        out_shape=(
            jax.ShapeDtypeStruct((D_MODEL, N_REST), BF16),
            jax.ShapeDtypeStruct((D_MODEL, N_HEAD), BF16),
            jax.ShapeDtypeStruct((D_MODEL, LANES), BF16),
            sq_shape, sq_shape, sq_shape,
            jax.ShapeDtypeStruct((nseq, N_HEAD), F32),
            jax.ShapeDtypeStruct((nseq, LANES), F32),
            jax.ShapeDtypeStruct((nseq, N_REST), F32),
        ),
        compiler_params=pltpu.CompilerParams(dimension_semantics=("arbitrary",),
                                             vmem_limit_bytes=VMEM_LIMIT_SAMPLE),
        name="weight_prep",
    )(w_in_t, w_in_t, w_in_t, w_proj_m, w_proj_c, w_out, xs, nw)


def _prompt_kernel(x_ref, wh_ref, wg_ref, wr_ref, wpm_ref, wpc_ref, wout_ref,
                   nw_ref, gb_ref, hnw_ref, cw_ref, fnw_ref,
                   y_ref, c_ref, n_ref, m_ref, conv_ref,
                   hg_s, ubuf_s, ct_s):
    tl = x_ref.shape[1]
    l = pl.program_id(1)

    @pl.when(l == 0)
    def _():
        ct_s[...] = jnp.zeros_like(ct_s)
        n_ref[...] = jnp.zeros_like(n_ref)
        m_ref[...] = jnp.zeros_like(m_ref)
        ubuf_s[0:SUBLANES, :] = jnp.zeros((SUBLANES, D_C), F32)

    x = x_ref[0]
    xn = _rmsnorm(x, nw_ref[...]).astype(BF16)

    def rest(off, width):
        return _dot(xn, wr_ref[:, off:off + width])

    g = _dot(xn, wg_ref[...]) + gb_ref[...]
    gc = jnp.where(lax.broadcasted_iota(jnp.int32, (tl, LANES), 1) < NH, g, _log_sigmoid(g))
    qkv = _dot(xn, wh_ref[...])

    ch = MLSTM_CHUNK
    lane = lax.broadcasted_iota(jnp.int32, (ch, LANES), 1)
    row = lax.broadcasted_iota(jnp.int32, (ch, ch), 0)
    col = lax.broadcasted_iota(jnp.int32, (ch, ch), 1)
    causal = row >= col
    row_g = lax.broadcasted_iota(jnp.int32, (ch, LANES), 0)
    chunk_gates = []
    for r0 in range(0, tl, ch):
        gc_c = gc[r0:r0 + ch, :]
        bc = gc_c
        shift = 1
        while shift < ch:
            bc = bc + jnp.where(row_g >= shift, pltpu.roll(bc, shift, axis=0), 0.0)
            shift *= 2
        gt = jnp.where(lane < NH, gc_c, bc).T
        chunk_gates.append((gc_c, bc, gt))

    def mlstm_unit(ci, h):
        r0 = ci * ch
        gc_c, bc, gt = chunk_gates[ci]
        xn_c = xn[r0:r0 + ch, :]
        sig_o = _sigmoid(_dot(xn_c, wr_ref[:, R_O + h * DHV:R_O + (h + 1) * DHV]))
        silu_z = _silu(_dot(xn_c, wr_ref[:, R_ZM + h * DHV:R_ZM + (h + 1) * DHV]))
        q_f = qkv[r0:r0 + ch, OFF_Q + h * DQK:OFF_Q + (h + 1) * DQK]
        k_f = qkv[r0:r0 + ch, OFF_K + h * DQK:OFF_K + (h + 1) * DQK] * K_SCALE
        v_f = qkv[r0:r0 + ch, OFF_V + h * DHV:OFF_V + (h + 1) * DHV]
        q_b = q_f.astype(BF16)
        k_b = k_f.astype(BF16)
        v_b = v_f.astype(BF16)
        ct_old = ct_s[h]
        n_old = n_ref[0, 0, h:h + 1, :]
        m_old = m_ref[0, h:h + 1, 0:1]

        b_c = bc[:, NH + h:NH + h + 1]
        ig_c = gc_c[:, h:h + 1]
        ig_r = gt[h:h + 1, :]
        b_r = gt[NH + h:NH + h + 1, :]
        log_d = jnp.where(causal, b_c - b_r + ig_r, NEG_BIG)
        inter = b_c + m_old
        m_t = jnp.maximum(inter, jnp.max(log_d, axis=-1, keepdims=True))
        d_m = jnp.exp(log_d - m_t)
        w_int = jnp.exp(inter - m_t)
        s = _dot_nt(q_b, k_b) * d_m
        num = w_int * _dot(q_b, ct_old.astype(BF16)) + _dot(s.astype(BF16), v_b)
        den = w_int * jnp.sum(q_f * n_old, axis=-1, keepdims=True) + jnp.sum(s, axis=-1, keepdims=True)
        h_t = num / jnp.maximum(jnp.abs(den), jnp.exp(-m_t))

        b_end = b_c[ch - 1:ch, :]
        inter_end = b_end + m_old
        m_new = jnp.maximum(inter_end, jnp.max(b_end - b_r + ig_r, axis=-1, keepdims=True))
        w_end = jnp.exp(b_end - b_c + ig_c - m_new)
        f_end = jnp.exp(inter_end - m_new)
        ct_s[h] = f_end * ct_old + _dot_tn(k_b, (w_end * v_f).astype(BF16))
        n_ref[0, 0, h:h + 1, :] = f_end * n_old + jnp.sum(w_end * k_f, axis=0, keepdims=True)
        m_ref[0, h:h + 1, :] = jnp.broadcast_to(m_new, (1, LANES))

        hn = h_t * lax.rsqrt(jnp.mean(h_t * h_t, axis=-1, keepdims=True) + EPS)
        hn = hn * hnw_ref[:, h * DHV:(h + 1) * DHV]
        hg_s[r0:r0 + ch, h * DHV:(h + 1) * DHV] = (hn * sig_o * silu_z).astype(BF16)

    val = {}

    def conv_input():
        u = rest(R_CG, D_C) * rest(R_XC, D_C)
        ubuf_s[SUBLANES:SUBLANES + tl, :] = u
        cv = cw_ref[0:1, :] * ubuf_s[SUBLANES - 2:SUBLANES - 2 + tl, :]
        cv = cv + cw_ref[1:2, :] * ubuf_s[SUBLANES - 1:SUBLANES - 1 + tl, :]
        val["cv"] = cv + cw_ref[2:3, :] * u

    def conv_gate():
        val["yc_in"] = (_silu(rest(R_ZC, D_C)) * rest(R_BG, D_C) * val["cv"]).astype(BF16)

    def conv_proj():
        val["gy_c"] = _sigmoid(rest(R_GC, D_MODEL)) * _dot(val["yc_in"], wpc_ref[...])

    def merge_gate():
        val["sig_gm"] = _sigmoid(rest(R_GM, D_MODEL))

    fillers = iter([conv_input, conv_gate, conv_proj, merge_gate])
    units = iter([(ci, h) for ci in range(tl // ch) for h in range(NH)])
    for step in PROMPT_ORDER:
        if step == "U":
            mlstm_unit(*next(units))
        else:
            next(fillers)()

    ubuf_s[0:SUBLANES, :] = ubuf_s[tl:tl + SUBLANES, :]

    conv_ref[0, 0] = ubuf_s[SUBLANES - (CONV_W - 1):SUBLANES, :]
    for h in range(NH):
        c_ref[0, 0, h] = ct_s[h].T

    y_m = _dot(hg_s[...], wpm_ref[...])
    mix = (val["sig_gm"] * y_m + val["gy_c"]).astype(BF16)
    out = x + _dot(mix, wout_ref[...])
    y_ref[0] = _rmsnorm(out, fnw_ref[...])


def _resident(shape):
    return pl.BlockSpec(shape, lambda *_: (0,) * len(shape), pipeline_mode=pl.Buffered(1))


def _prompt_call(x, wh, wg, wr, wpm, wpc, wout, nw, gb, hnw, cw, fnw):
    bsz, seq, _ = x.shape
    tl = PROMPT_BLOCK
    grid = (bsz, seq // tl)
    out_shape = (
        jax.ShapeDtypeStruct((bsz, seq, D_MODEL), F32),
        jax.ShapeDtypeStruct((1, bsz, NH, DHV, DQK), F32),
        jax.ShapeDtypeStruct((1, bsz, NH, DQK), F32),
        jax.ShapeDtypeStruct((bsz, SUBLANES, LANES), F32),
        jax.ShapeDtypeStruct((1, bsz, CONV_W - 1, D_C), F32),
    )
    in_specs = [
        pl.BlockSpec((1, tl, D_MODEL), lambda b, l: (b, l, 0)),
        _resident((D_MODEL, N_HEAD)),
        _resident((D_MODEL, LANES)),
        _resident((D_MODEL, N_REST)),
        _resident((D_M, D_MODEL)),
        _resident((D_C, D_MODEL)),
        _resident((D_MODEL, D_MODEL)),
        _resident((1, D_MODEL)),
        _resident((1, LANES)),
        _resident((1, D_M)),
        _resident((CONV_W, D_C)),
        _resident((1, D_MODEL)),
    ]
    out_specs = (
        pl.BlockSpec((1, tl, D_MODEL), lambda b, l: (b, l, 0)),
        pl.BlockSpec((1, 1, NH, DHV, DQK), lambda b, l: (0, b, 0, 0, 0)),
        pl.BlockSpec((1, 1, NH, DQK), lambda b, l: (0, b, 0, 0)),
        pl.BlockSpec((1, SUBLANES, LANES), lambda b, l: (b, 0, 0)),
        pl.BlockSpec((1, 1, CONV_W - 1, D_C), lambda b, l: (0, b, 0, 0)),
    )
    return pl.pallas_call(
        _prompt_kernel,
        grid=grid,
        in_specs=in_specs,
        out_specs=out_specs,
        out_shape=out_shape,
        scratch_shapes=[
            pltpu.VMEM((tl, D_M), BF16),
            pltpu.VMEM((tl + 2 * SUBLANES, D_C), F32),
            pltpu.VMEM((NH, DQK, DHV), F32),
        ],
        compiler_params=pltpu.CompilerParams(
            dimension_semantics=("arbitrary", "arbitrary"),
            vmem_limit_bytes=VMEM_LIMIT_PROMPT),
        name="prompt_layer",
    )(x, wh, wg, wr, wpm, wpc, wout, nw, gb, hnw, cw, fnw)


SC_LANES = 16
SC_CORES = 2
SC_SUBCORES = 16


def _sample_gate_scalars(g_ref, gb_ref, m_ref):
    g = g_ref[...] + gb_ref[...]
    ig = g[:, 0:NH]
    lf = _log_sigmoid(g[:, NH:2 * NH])
    inter = lf + m_ref[0]
    m_t = jnp.maximum(inter, ig)
    return m_t, jnp.exp(ig - m_t), jnp.exp(inter - m_t), jnp.exp(-m_t)


def _sample_gates_kernel(hs_ref, g_ref, gb_ref, n_ref, m_ref, kw_ref, fb_ref, no_ref, mo_ref):
    nseq = hs_ref.shape[0]
    m_t, w_end, f_end, _ = _sample_gate_scalars(g_ref, gb_ref, m_ref)
    mo_ref[0] = m_t
    for h in range(NH):
        k_h = hs_ref[:, OFF_K + h * DQK:OFF_K + (h + 1) * DQK] * K_SCALE
        n_h = n_ref[:, h * DQK:(h + 1) * DQK]
        w_h = w_end[:, h:h + 1]
        f_h = f_end[:, h:h + 1]
        kw_ref[:, h * DQK:(h + 1) * DQK] = w_h * k_h
        fb_ref[:, h * DQK:(h + 1) * DQK] = jnp.broadcast_to(f_h, (nseq, DQK))
        no_ref[:, h * DQK:(h + 1) * DQK] = f_h * n_h + w_h * k_h


def _sample_gates_call(head_s, g, gb, n0, m0):
    nseq = head_s.shape[0]
    args = (head_s, g, gb, n0, m0)
    full = lambda a: pl.BlockSpec(a.shape, lambda i: (0,) * a.ndim)
    row = jax.ShapeDtypeStruct((nseq, D_QK), F32)
    return pl.pallas_call(
        _sample_gates_kernel,
        grid=(1,),
        in_specs=[full(a) for a in args],
        out_specs=(full(row), full(row), full(row), full(m0)),
        out_shape=(row, row, row, jax.ShapeDtypeStruct(m0.shape, F32)),
        name="sample_gates",
    )(*args)


def _sample_state_sc_call(c0, head_s, kw_s, fb_s):
    nseq = c0.shape[1]
    nworkers = SC_CORES * SC_SUBCORES
    per_worker = nseq * NH // nworkers
    nvec = DQK // SC_LANES
    mesh = plsc.VectorSubcoreMesh(core_axis_name="c", subcore_axis_name="s")

    def body(c_hbm, hs_hbm, kw_hbm, fb_hbm, co_hbm, cq_hbm, c_v, q_v, kw_v, v_v, f_v, cq_v):
        wid = lax.axis_index("c") * SC_SUBCORES + lax.axis_index("s")
        lane = lax.iota(jnp.int32, SC_LANES)
        zero_i = jnp.zeros((SC_LANES,), jnp.int32)

        @pl.loop(0, per_worker)
        def _(t):
            pair = wid * per_worker + t
            b = pair // NH
            h = pair % NH
            pltpu.sync_copy(c_hbm.at[0, b, h], c_v)
            pltpu.sync_copy(hs_hbm.at[b, pl.ds(OFF_Q + h * DQK, DQK)], q_v)
            pltpu.sync_copy(hs_hbm.at[b, pl.ds(OFF_V + h * DHV, DHV)], v_v)
            pltpu.sync_copy(kw_hbm.at[b, pl.ds(h * DQK, DQK)], kw_v)
            pltpu.sync_copy(fb_hbm.at[b, pl.ds(h * DQK, DQK)], f_v)
            q = [q_v[pl.ds(SC_LANES * j, SC_LANES)] for j in range(nvec)]
            kw = [kw_v[pl.ds(SC_LANES * j, SC_LANES)] for j in range(nvec)]
            f_vec = f_v[pl.ds(0, SC_LANES)]

            @pl.loop(0, DHV // SC_LANES)
            def _(g):
                cq_vec = jnp.zeros((SC_LANES,), F32)
                for i in range(SC_LANES):
                    r = g * SC_LANES + i
                    v_r = plsc.load_gather(v_v, [zero_i + r])
                    acc = jnp.zeros((SC_LANES,), F32)
                    for j in range(nvec):
                        c = c_v[r, pl.ds(SC_LANES * j, SC_LANES)]
                        acc = acc + c * q[j]
                        c_v[r, pl.ds(SC_LANES * j, SC_LANES)] = f_vec * c + v_r * kw[j]
                    cq_vec = jnp.where(lane == i, jnp.sum(acc), cq_vec)
                cq_v[pl.ds(g * SC_LANES, SC_LANES)] = cq_vec

            pltpu.sync_copy(c_v, co_hbm.at[0, b, h])
            pltpu.sync_copy(cq_v, cq_hbm.at[b, pl.ds(h * DHV, DHV)])

    return pl.kernel(
        body,
        out_type=(jax.ShapeDtypeStruct(c0.shape, F32),
                  jax.ShapeDtypeStruct((nseq, D_M), F32)),
        mesh=mesh,
        scratch_types=[pltpu.VMEM((DHV, DQK), F32),
                       pltpu.VMEM((DQK,), F32),
                       pltpu.VMEM((DQK,), F32),
                       pltpu.VMEM((DHV,), F32),
                       pltpu.VMEM((DQK,), F32),
                       pltpu.VMEM((DHV,), F32)],
        compiler_params=pltpu.CompilerParams(use_tc_tiling_on_sc=True, needs_layout_passes=False),
        name="sample_state_sc",
    )(c0, head_s, kw_s, fb_s)


def _sample_tail_kernel(cq_ref, hs_ref, g_ref, gb_ref, n_ref, m_ref,
                        x_ref, r_ref, conv_ref, wpm_ref, wpc_ref, wout_ref, hnw_ref, cw_ref, fnw_ref,
                        y_ref, convo_ref):
    def piece(off, width):
        return r_ref[:, off:off + width]

    m_t, w_end, f_end, floor = _sample_gate_scalars(g_ref, gb_ref, m_ref)
    for h in range(NH):
        q_h = hs_ref[:, OFF_Q + h * DQK:OFF_Q + (h + 1) * DQK]
        k_h = hs_ref[:, OFF_K + h * DQK:OFF_K + (h + 1) * DQK] * K_SCALE
        v_h = hs_ref[:, OFF_V + h * DHV:OFF_V + (h + 1) * DHV]
        n_h = n_ref[:, h * DQK:(h + 1) * DQK]
        w_h = w_end[:, h:h + 1]
        f_h = f_end[:, h:h + 1]
        s = jnp.sum(q_h * k_h, axis=-1, keepdims=True) * w_h
        den = f_h * jnp.sum(n_h * q_h, axis=-1, keepdims=True) + s
        num = f_h * cq_ref[:, h * DHV:(h + 1) * DHV] + s * v_h
        h_t = num / jnp.maximum(jnp.abs(den), floor[:, h:h + 1])
        hn = h_t * lax.rsqrt(jnp.mean(h_t * h_t, axis=-1, keepdims=True) + EPS)
        hn = hn * hnw_ref[:, h * DHV:(h + 1) * DHV]
        o_h = piece(R_O + h * DHV, DHV)
        zm_h = piece(R_ZM + h * DHV, DHV)
        hg = (hn * _sigmoid(o_h) * _silu(zm_h)).astype(BF16)
        part = _dot(hg, wpm_ref[h * DHV:(h + 1) * DHV, :])
        y_m = part if h == 0 else y_m + part

    u = piece(R_CG, D_C) * piece(R_XC, D_C)
    cv = cw_ref[0:1, :] * conv_ref[:, 0:D_C]
    cv = cv + cw_ref[1:2, :] * conv_ref[:, D_C:2 * D_C]
    cv = cv + cw_ref[2:3, :] * u
    convo_ref[:, 0:D_C] = conv_ref[:, D_C:2 * D_C]
    convo_ref[:, D_C:2 * D_C] = u
    yc_in = (_silu(piece(R_ZC, D_C)) * piece(R_BG, D_C) * cv).astype(BF16)
    y_c = _dot(yc_in, wpc_ref[...])
    mix = (_sigmoid(piece(R_GM, D_MODEL)) * y_m + _sigmoid(piece(R_GC, D_MODEL)) * y_c).astype(BF16)
    out = x_ref[...] + _dot(mix, wout_ref[...])
    y_ref[...] = _rmsnorm(out, fnw_ref[...])


def _sample_tail_call(cq, head_s, g, gb, n0, m0, x, rest, conv0, wpm, wpc, wout, hnw, cw, fnw):
    nseq = x.shape[0]
    args = (cq, head_s, g, gb, n0, m0, x, rest, conv0, wpm, wpc, wout, hnw, cw, fnw)
    full = lambda a: pl.BlockSpec(a.shape, lambda i: (0,) * a.ndim)
    return pl.pallas_call(
        _sample_tail_kernel,
        grid=(1,),
        in_specs=[full(a) for a in args],
        out_specs=(pl.BlockSpec((nseq, D_MODEL), lambda i: (0, 0)),
                   pl.BlockSpec((nseq, (CONV_W - 1) * D_C), lambda i: (0, 0))),
        out_shape=(jax.ShapeDtypeStruct((nseq, D_MODEL), F32),
                   jax.ShapeDtypeStruct((nseq, (CONV_W - 1) * D_C), F32)),
        compiler_params=pltpu.CompilerParams(dimension_semantics=("arbitrary",),
                                             vmem_limit_bytes=VMEM_LIMIT_SAMPLE),
        name="sample_tail",
    )(*args)


def kernel(x_prompt, x_sample, state_mlstm_C, state_mlstm_n, state_mlstm_m, state_conv, norm_w, w_in, b_i, b_f,
           head_norm_w, conv_w, w_proj_m, w_proj_c, w_out, final_norm_w):
    depth = norm_w.shape[0]
    assert depth == 1, "single-layer trunk"
    bsz = x_prompt.shape[0]
    nseq = x_sample.shape[0]

    assert w_in.shape == (1, D_MODEL, N_HEAD + N_GATE_COLS + N_REST)
    gb = jnp.pad(jnp.concatenate([b_i[0], b_f[0]]), (0, LANES - N_GATE_COLS)).reshape(1, LANES)
    nw = norm_w[0].reshape(1, D_MODEL)
    hnw = head_norm_w[0].reshape(1, D_M)
    cw = conv_w[0]
    fnw = final_norm_w.reshape(1, D_MODEL)
    xs = x_sample.reshape(nseq, D_MODEL)
    (wr, wh, wg, wpm, wpc, wout, head_s, g_s, rest_s) = _weight_prep_call(
        jnp.swapaxes(w_in, 1, 2), w_proj_m, w_proj_c, w_out, xs, nw)

    n0 = state_mlstm_n[0].reshape(nseq, D_QK)
    kw_s, fb_s, n_s, m_s = _sample_gates_call(head_s, g_s, gb, n0, state_mlstm_m)
    c_s, cq = _sample_state_sc_call(state_mlstm_C, head_s, kw_s, fb_s)

    y_p, c_p, n_p, m_p, conv_p = _prompt_call(x_prompt, wh, wg, wr, wpm, wpc, wout, nw, gb, hnw, cw, fnw)
    m_p = m_p[:, :NH, 0].reshape(1, bsz, NH)

    conv0 = state_conv[0].reshape(nseq, (CONV_W - 1) * D_C)
    y_s, conv_s = _sample_tail_call(cq, head_s, g_s, gb, n0, state_mlstm_m,
                                    xs, rest_s, conv0, wpm, wpc, wout, hnw, cw, fnw)

    return (y_p, y_s.reshape(nseq, 1, D_MODEL), c_p, n_p, m_p, conv_p,
            c_s, n_s.reshape(1, nseq, NH, DQK), m_s,
            conv_s.reshape(1, nseq, CONV_W - 1, D_C))
```

```python
import jax
import jax.numpy as jnp
from jax import lax
from jax.experimental import pallas as pl
from jax.experimental.pallas import tpu as pltpu
from jax.experimental.pallas import tpu_sc as plsc

F32 = jnp.float32
BF16 = jnp.bfloat16

D_MODEL = 1024
NH = 4
DHV = 256
DQK = 128
D_QK = NH * DQK
D_M = NH * DHV
D_C = D_MODEL
CONV_W = 3
EPS = 1e-6
NEG_BIG = -1e30
K_SCALE = DQK ** -0.5

LANES = 128
SUBLANES = 8

OFF_Q = 0
OFF_K = OFF_Q + D_QK
OFF_V = OFF_K + D_QK
N_HEAD = OFF_V + D_M
N_GATE_COLS = 2 * NH
R_O = 0
R_ZM = R_O + D_M
R_BG = R_ZM + D_M
R_CG = R_BG + D_C
R_XC = R_CG + D_C
R_ZC = R_XC + D_C
R_GM = R_ZC + D_C
R_GC = R_GM + D_MODEL
N_REST = R_GC + D_MODEL

PROMPT_BLOCK = 512
MLSTM_CHUNK = 256
PROMPT_ORDER = "UUUUFUFUFUFU"
VMEM_LIMIT_PROMPT = 60 * 1024 * 1024
VMEM_LIMIT_SAMPLE = 48 * 1024 * 1024


def _sigmoid(x):
    return 0.5 * jnp.tanh(0.5 * x) + 0.5


def _silu(x):
    return x * _sigmoid(x)


def _log_sigmoid(x):
    return jnp.minimum(x, 0.0) - jnp.log1p(jnp.exp(-jnp.abs(x)))


def _rmsnorm(x, w):
    return x * lax.rsqrt(jnp.mean(x * x, axis=-1, keepdims=True) + EPS) * w


def _dot(a, b):
    return jnp.dot(a, b, preferred_element_type=F32)


def _dot_nt(a, b):
    return lax.dot_general(a, b, (((1,), (1,)), ((), ())), preferred_element_type=F32)


def _dot_tn(a, b):
    return lax.dot_general(a, b, (((0,), (0,)), ((), ())), preferred_element_type=F32)


PREP_COLS = 1024
PREP_CHUNK = 256


PREP_STEPS = N_REST // PREP_COLS
PREP_HEAD_COLS = N_HEAD // PREP_STEPS
PREP_SQ_ROWS = D_MODEL // PREP_STEPS


def _weight_prep_kernel(a_ref, c_ref, g_ref, pm_ref, pc_ref, po_ref, xs_ref, nw_ref,
                        wr_ref, wh_ref, wg_ref, wpm_ref, wpc_ref, wout_ref,
                        hs_ref, gs_ref, rs_ref):
    j = pl.program_id(0)
    xn = _rmsnorm(xs_ref[...], nw_ref[...]).astype(BF16)
    for r0 in range(0, PREP_COLS, PREP_CHUNK):
        w_t = a_ref[0, r0:r0 + PREP_CHUNK, :].T.astype(BF16)
        wr_ref[:, r0:r0 + PREP_CHUNK] = w_t
        rs_ref[:, r0:r0 + PREP_CHUNK] = _dot(xn, w_t)
    w_t = c_ref[0].T.astype(BF16)
    wh_ref[...] = w_t
    hs_ref[...] = _dot(xn, w_t)

    @pl.when(j == 0)
    def _():
        lane = lax.broadcasted_iota(jnp.int32, (D_MODEL, LANES), 1)
        w_g = jnp.where(lane < N_GATE_COLS, g_ref[0].T, 0.0).astype(BF16)
        wg_ref[...] = w_g
        gs_ref[...] = _dot(xn, w_g)

    wpm_ref[...] = pm_ref[0].astype(BF16)
    wpc_ref[...] = pc_ref[0].astype(BF16)
    wout_ref[...] = po_ref[0].astype(BF16)


def _weight_prep_call(w_in_t, w_proj_m, w_proj_c, w_out, xs, nw):
    nseq = xs.shape[0]
    rest_row0 = N_HEAD + N_GATE_COLS
    const = lambda shape: pl.BlockSpec(shape, lambda j: (0,) * len(shape))
    sq_in = pl.BlockSpec((1, PREP_SQ_ROWS, D_MODEL), lambda j: (0, j, 0))
    sq_out = pl.BlockSpec((PREP_SQ_ROWS, D_MODEL), lambda j: (j, 0))
    sq_shape = jax.ShapeDtypeStruct((D_MODEL, D_MODEL), BF16)
    return pl.pallas_call(
        _weight_prep_kernel,
        grid=(PREP_STEPS,),
        in_specs=[
            pl.BlockSpec((pl.Element(1), pl.Element(PREP_COLS), pl.Element(D_MODEL)),
                         lambda j: (0, pl.multiple_of(rest_row0 + j * PREP_COLS, SUBLANES), 0)),
            pl.BlockSpec((1, PREP_HEAD_COLS, D_MODEL), lambda j: (0, j, 0)),
            pl.BlockSpec((1, LANES, D_MODEL), lambda j: (0, N_HEAD // LANES, 0)),
            sq_in, sq_in, sq_in,
            const((nseq, D_MODEL)),
            const((1, D_MODEL)),
        ],
        out_specs=(
            pl.BlockSpec((D_MODEL, PREP_COLS), lambda j: (0, j)),
            pl.BlockSpec((D_MODEL, PREP_HEAD_COLS), lambda j: (0, j)),
            const((D_MODEL, LANES)),
            sq_out, sq_out, sq_out,
            pl.BlockSpec((nseq, PREP_HEAD_COLS), lambda j: (0, j)),
            const((nseq, LANES)),
            pl.BlockSpec((nseq, PREP_COLS), lambda j: (0, j)),
        ),
        out_shape=(
            jax.ShapeDtypeStruct((D_MODEL, N_REST), BF16),
            jax.ShapeDtypeStruct((D_MODEL, N_HEAD), BF16),
            jax.ShapeDtypeStruct((D_MODEL, LANES), BF16),
            sq_shape, sq_shape, sq_shape,
            jax.ShapeDtypeStruct((nseq, N_HEAD), F32),
            jax.ShapeDtypeStruct((nseq, LANES), F32),
            jax.ShapeDtypeStruct((nseq, N_REST), F32),
        ),
        compiler_params=pltpu.CompilerParams(dimension_semantics=("arbitrary",),
                                             vmem_limit_bytes=VMEM_LIMIT_SAMPLE),
        name="weight_prep",
    )(w_in_t, w_in_t, w_in_t, w_proj_m, w_proj_c, w_out, xs, nw)


def _prompt_kernel(x_ref, wh_ref, wg_ref, wr_ref, wpm_ref, wpc_ref, wout_ref,
                   nw_ref, gb_ref, hnw_ref, cw_ref, fnw_ref,
                   y_ref, c_ref, n_ref, m_ref, conv_ref,
                   hg_s, ubuf_s, ct_s):
    tl = x_ref.shape[1]
    l = pl.program_id(1)

    @pl.when(l == 0)
    def _():
        ct_s[...] = jnp.zeros_like(ct_s)
        n_ref[...] = jnp.zeros_like(n_ref)
        m_ref[...] = jnp.zeros_like(m_ref)
        ubuf_s[0:SUBLANES, :] = jnp.zeros((SUBLANES, D_C), F32)

    x = x_ref[0]
    xn = _rmsnorm(x, nw_ref[...]).astype(BF16)

    def rest(off, width):
        return _dot(xn, wr_ref[:, off:off + width])

    g = _dot(xn, wg_ref[...]) + gb_ref[...]
    gc = jnp.where(lax.broadcasted_iota(jnp.int32, (tl, LANES), 1) < NH, g, _log_sigmoid(g))
    qkv = _dot(xn, wh_ref[...])

    ch = MLSTM_CHUNK
    lane = lax.broadcasted_iota(jnp.int32, (ch, LANES), 1)
    row = lax.broadcasted_iota(jnp.int32, (ch, ch), 0)
    col = lax.broadcasted_iota(jnp.int32, (ch, ch), 1)
    causal = row >= col
    row_g = lax.broadcasted_iota(jnp.int32, (ch, LANES), 0)
    chunk_gates = []
    for r0 in range(0, tl, ch):
        gc_c = gc[r0:r0 + ch, :]
        bc = gc_c
        shift = 1
        while shift < ch:
            bc = bc + jnp.where(row_g >= shift, pltpu.roll(bc, shift, axis=0), 0.0)
            shift *= 2
        gt = jnp.where(lane < NH, gc_c, bc).T
        chunk_gates.append((gc_c, bc, gt))

    head_gates = {}

    def mlstm_unit(ci, h):
        r0 = ci * ch
        gc_c, bc, gt = chunk_gates[ci]
        if ci == 0:
            head_gates[h] = (_sigmoid(rest(R_O + h * DHV, DHV)), _silu(rest(R_ZM + h * DHV, DHV)))
        sig_o = head_gates[h][0][r0:r0 + ch, :]
        silu_z = head_gates[h][1][r0:r0 + ch, :]
        q_f = qkv[r0:r0 + ch, OFF_Q + h * DQK:OFF_Q + (h + 1) * DQK]
        k_f = qkv[r0:r0 + ch, OFF_K + h * DQK:OFF_K + (h + 1) * DQK] * K_SCALE
        v_f = qkv[r0:r0 + ch, OFF_V + h * DHV:OFF_V + (h + 1) * DHV]
        q_b = q_f.astype(BF16)
        k_b = k_f.astype(BF16)
        v_b = v_f.astype(BF16)
        ct_old = ct_s[h]
        n_old = n_ref[0, 0, h:h + 1, :]
        m_old = m_ref[0, h:h + 1, 0:1]

        b_c = bc[:, NH + h:NH + h + 1]
        ig_c = gc_c[:, h:h + 1]
        ig_r = gt[h:h + 1, :]
        b_r = gt[NH + h:NH + h + 1, :]
        log_d = jnp.where(causal, b_c - b_r + ig_r, NEG_BIG)
        inter = b_c + m_old
        m_t = jnp.maximum(inter, jnp.max(log_d, axis=-1, keepdims=True))
        d_m = jnp.exp(log_d - m_t)
        w_int = jnp.exp(inter - m_t)
        s = _dot_nt(q_b, k_b) * d_m
        num = w_int * _dot(q_b, ct_old.astype(BF16)) + _dot(s.astype(BF16), v_b)
        den = w_int * jnp.sum(q_f * n_old, axis=-1, keepdims=True) + jnp.sum(s, axis=-1, keepdims=True)
        h_t = num / jnp.maximum(jnp.abs(den), jnp.exp(-m_t))

        b_end = b_c[ch - 1:ch, :]
        inter_end = b_end + m_old
        m_new = jnp.maximum(inter_end, jnp.max(b_end - b_r + ig_r, axis=-1, keepdims=True))
        w_end = jnp.exp(b_end - b_c + ig_c - m_new)
        f_end = jnp.exp(inter_end - m_new)
        ct_s[h] = f_end * ct_old + _dot_tn(k_b, (w_end * v_f).astype(BF16))
        n_ref[0, 0, h:h + 1, :] = f_end * n_old + jnp.sum(w_end * k_f, axis=0, keepdims=True)
        m_ref[0, h:h + 1, :] = jnp.broadcast_to(m_new, (1, LANES))

        hn = h_t * lax.rsqrt(jnp.mean(h_t * h_t, axis=-1, keepdims=True) + EPS)
        hn = hn * hnw_ref[:, h * DHV:(h + 1) * DHV]
        hg_s[r0:r0 + ch, h * DHV:(h + 1) * DHV] = (hn * sig_o * silu_z).astype(BF16)

    val = {}

    def conv_input():
        u = rest(R_CG, D_C) * rest(R_XC, D_C)
        ubuf_s[SUBLANES:SUBLANES + tl, :] = u
        cv = cw_ref[0:1, :] * ubuf_s[SUBLANES - 2:SUBLANES - 2 + tl, :]
        cv = cv + cw_ref[1:2, :] * ubuf_s[SUBLANES - 1:SUBLANES - 1 + tl, :]
        val["cv"] = cv + cw_ref[2:3, :] * u

    def conv_gate():
        val["yc_in"] = (_silu(rest(R_ZC, D_C)) * rest(R_BG, D_C) * val["cv"]).astype(BF16)

    def conv_proj():
        val["gy_c"] = _sigmoid(rest(R_GC, D_MODEL)) * _dot(val["yc_in"], wpc_ref[...])

    def merge_gate():
        val["sig_gm"] = _sigmoid(rest(R_GM, D_MODEL))

    fillers = iter([conv_input, conv_gate, conv_proj, merge_gate])
    units = iter([(ci, h) for ci in range(tl // ch) for h in range(NH)])
    for step in PROMPT_ORDER:
        if step == "U":
            mlstm_unit(*next(units))
        else:
            next(fillers)()

    ubuf_s[0:SUBLANES, :] = ubuf_s[tl:tl + SUBLANES, :]

    conv_ref[0, 0] = ubuf_s[SUBLANES - (CONV_W - 1):SUBLANES, :]
    for h in range(NH):
        c_ref[0, 0, h] = ct_s[h].T

    y_m = _dot(hg_s[...], wpm_ref[...])
    mix = (val["sig_gm"] * y_m + val["gy_c"]).astype(BF16)
    out = x + _dot(mix, wout_ref[...])
    y_ref[0] = _rmsnorm(out, fnw_ref[...])


def _resident(shape):
    return pl.BlockSpec(shape, lambda *_: (0,) * len(shape), pipeline_mode=pl.Buffered(1))


def _prompt_call(x, wh, wg, wr, wpm, wpc, wout, nw, gb, hnw, cw, fnw):
    bsz, seq, _ = x.shape
    tl = PROMPT_BLOCK
    grid = (bsz, seq // tl)
    out_shape = (
        jax.ShapeDtypeStruct((bsz, seq, D_MODEL), F32),
        jax.ShapeDtypeStruct((1, bsz, NH, DHV, DQK), F32),
        jax.ShapeDtypeStruct((1, bsz, NH, DQK), F32),
        jax.ShapeDtypeStruct((bsz, SUBLANES, LANES), F32),
        jax.ShapeDtypeStruct((1, bsz, CONV_W - 1, D_C), F32),
    )
    in_specs = [
        pl.BlockSpec((1, tl, D_MODEL), lambda b, l: (b, l, 0)),
        _resident((D_MODEL, N_HEAD)),
        _resident((D_MODEL, LANES)),
        _resident((D_MODEL, N_REST)),
        _resident((D_M, D_MODEL)),
        _resident((D_C, D_MODEL)),
        _resident((D_MODEL, D_MODEL)),
        _resident((1, D_MODEL)),
        _resident((1, LANES)),
        _resident((1, D_M)),
        _resident((CONV_W, D_C)),
        _resident((1, D_MODEL)),
    ]
    out_specs = (
        pl.BlockSpec((1, tl, D_MODEL), lambda b, l: (b, l, 0)),
        pl.BlockSpec((1, 1, NH, DHV, DQK), lambda b, l: (0, b, 0, 0, 0)),
        pl.BlockSpec((1, 1, NH, DQK), lambda b, l: (0, b, 0, 0)),
        pl.BlockSpec((1, SUBLANES, LANES), lambda b, l: (b, 0, 0)),
        pl.BlockSpec((1, 1, CONV_W - 1, D_C), lambda b, l: (0, b, 0, 0)),
    )
    return pl.pallas_call(
        _prompt_kernel,
        grid=grid,
        in_specs=in_specs,
        out_specs=out_specs,
        out_shape=out_shape,
        scratch_shapes=[
            pltpu.VMEM((tl, D_M), BF16),
            pltpu.VMEM((tl + 2 * SUBLANES, D_C), F32),
            pltpu.VMEM((NH, DQK, DHV), F32),
        ],
        compiler_params=pltpu.CompilerParams(
            dimension_semantics=("arbitrary", "arbitrary"),
            vmem_limit_bytes=VMEM_LIMIT_PROMPT),
        name="prompt_layer",
    )(x, wh, wg, wr, wpm, wpc, wout, nw, gb, hnw, cw, fnw)


SC_LANES = 16
SC_CORES = 2
SC_SUBCORES = 16


def _sample_gate_scalars(g_ref, gb_ref, m_ref):
    g = g_ref[...] + gb_ref[...]
    ig = g[:, 0:NH]
    lf = _log_sigmoid(g[:, NH:2 * NH])
    inter = lf + m_ref[0]
    m_t = jnp.maximum(inter, ig)
    return m_t, jnp.exp(ig - m_t), jnp.exp(inter - m_t), jnp.exp(-m_t)


def _sample_gates_kernel(hs_ref, g_ref, gb_ref, n_ref, m_ref, kw_ref, fb_ref, no_ref, mo_ref):
    nseq = hs_ref.shape[0]
    m_t, w_end, f_end, _ = _sample_gate_scalars(g_ref, gb_ref, m_ref)
    mo_ref[0] = m_t
    for h in range(NH):
        k_h = hs_ref[:, OFF_K + h * DQK:OFF_K + (h + 1) * DQK] * K_SCALE
        n_h = n_ref[:, h * DQK:(h + 1) * DQK]
        w_h = w_end[:, h:h + 1]
        f_h = f_end[:, h:h + 1]
        kw_ref[:, h * DQK:(h + 1) * DQK] = w_h * k_h
        fb_ref[:, h * DQK:(h + 1) * DQK] = jnp.broadcast_to(f_h, (nseq, DQK))
        no_ref[:, h * DQK:(h + 1) * DQK] = f_h * n_h + w_h * k_h


def _sample_gates_call(head_s, g, gb, n0, m0):
    nseq = head_s.shape[0]
    args = (head_s, g, gb, n0, m0)
    full = lambda a: pl.BlockSpec(a.shape, lambda i: (0,) * a.ndim)
    row = jax.ShapeDtypeStruct((nseq, D_QK), F32)
    return pl.pallas_call(
        _sample_gates_kernel,
        grid=(1,),
        in_specs=[full(a) for a in args],
        out_specs=(full(row), full(row), full(row), full(m0)),
        out_shape=(row, row, row, jax.ShapeDtypeStruct(m0.shape, F32)),
        name="sample_gates",
    )(*args)


def _sample_state_sc_call(c0, head_s, kw_s, fb_s):
    nseq = c0.shape[1]
    nworkers = SC_CORES * SC_SUBCORES
    per_worker = nseq * NH // nworkers
    nvec = DQK // SC_LANES
    mesh = plsc.VectorSubcoreMesh(core_axis_name="c", subcore_axis_name="s")

    def body(c_hbm, hs_hbm, kw_hbm, fb_hbm, co_hbm, cq_hbm, c_v, q_v, kw_v, v_v, f_v, cq_v):
        wid = lax.axis_index("c") * SC_SUBCORES + lax.axis_index("s")
        lane = lax.iota(jnp.int32, SC_LANES)
        zero_i = jnp.zeros((SC_LANES,), jnp.int32)

        @pl.loop(0, per_worker)
        def _(t):
            pair = wid * per_worker + t
            b = pair // NH
            h = pair % NH
            pltpu.sync_copy(c_hbm.at[0, b, h], c_v)
            pltpu.sync_copy(hs_hbm.at[b, pl.ds(OFF_Q + h * DQK, DQK)], q_v)
            pltpu.sync_copy(hs_hbm.at[b, pl.ds(OFF_V + h * DHV, DHV)], v_v)
            pltpu.sync_copy(kw_hbm.at[b, pl.ds(h * DQK, DQK)], kw_v)
            pltpu.sync_copy(fb_hbm.at[b, pl.ds(h * DQK, DQK)], f_v)
            q = [q_v[pl.ds(SC_LANES * j, SC_LANES)] for j in range(nvec)]
            kw = [kw_v[pl.ds(SC_LANES * j, SC_LANES)] for j in range(nvec)]
            f_vec = f_v[pl.ds(0, SC_LANES)]

            @pl.loop(0, DHV // SC_LANES)
            def _(g):
                cq_vec = jnp.zeros((SC_LANES,), F32)
                for i in range(SC_LANES):
                    r = g * SC_LANES + i
                    v_r = plsc.load_gather(v_v, [zero_i + r])
                    acc = jnp.zeros((SC_LANES,), F32)
                    for j in range(nvec):
                        c = c_v[r, pl.ds(SC_LANES * j, SC_LANES)]
                        acc = acc + c * q[j]
                        c_v[r, pl.ds(SC_LANES * j, SC_LANES)] = f_vec * c + v_r * kw[j]
                    cq_vec = jnp.where(lane == i, jnp.sum(acc), cq_vec)
                cq_v[pl.ds(g * SC_LANES, SC_LANES)] = cq_vec

            pltpu.sync_copy(c_v, co_hbm.at[0, b, h])
            pltpu.sync_copy(cq_v, cq_hbm.at[b, pl.ds(h * DHV, DHV)])

    return pl.kernel(
        body,
        out_type=(jax.ShapeDtypeStruct(c0.shape, F32),
                  jax.ShapeDtypeStruct((nseq, D_M), F32)),
        mesh=mesh,
        scratch_types=[pltpu.VMEM((DHV, DQK), F32),
                       pltpu.VMEM((DQK,), F32),
                       pltpu.VMEM((DQK,), F32),
                       pltpu.VMEM((DHV,), F32),
                       pltpu.VMEM((DQK,), F32),
                       pltpu.VMEM((DHV,), F32)],
        compiler_params=pltpu.CompilerParams(use_tc_tiling_on_sc=True, needs_layout_passes=False),
        name="sample_state_sc",
    )(c0, head_s, kw_s, fb_s)


def _sample_tail_kernel(cq_ref, hs_ref, g_ref, gb_ref, n_ref, m_ref,
                        x_ref, r_ref, conv_ref, wpm_ref, wpc_ref, wout_ref, hnw_ref, cw_ref, fnw_ref,
                        y_ref, convo_ref):
    def piece(off, width):
        return r_ref[:, off:off + width]

    m_t, w_end, f_end, floor = _sample_gate_scalars(g_ref, gb_ref, m_ref)
    for h in range(NH):
        q_h = hs_ref[:, OFF_Q + h * DQK:OFF_Q + (h + 1) * DQK]
        k_h = hs_ref[:, OFF_K + h * DQK:OFF_K + (h + 1) * DQK] * K_SCALE
        v_h = hs_ref[:, OFF_V + h * DHV:OFF_V + (h + 1) * DHV]
        n_h = n_ref[:, h * DQK:(h + 1) * DQK]
        w_h = w_end[:, h:h + 1]
        f_h = f_end[:, h:h + 1]
        s = jnp.sum(q_h * k_h, axis=-1, keepdims=True) * w_h
        den = f_h * jnp.sum(n_h * q_h, axis=-1, keepdims=True) + s
        num = f_h * cq_ref[:, h * DHV:(h + 1) * DHV] + s * v_h
        h_t = num / jnp.maximum(jnp.abs(den), floor[:, h:h + 1])
        hn = h_t * lax.rsqrt(jnp.mean(h_t * h_t, axis=-1, keepdims=True) + EPS)
        hn = hn * hnw_ref[:, h * DHV:(h + 1) * DHV]
        o_h = piece(R_O + h * DHV, DHV)
        zm_h = piece(R_ZM + h * DHV, DHV)
        hg = (hn * _sigmoid(o_h) * _silu(zm_h)).astype(BF16)
        part = _dot(hg, wpm_ref[h * DHV:(h + 1) * DHV, :])
        y_m = part if h == 0 else y_m + part

    u = piece(R_CG, D_C) * piece(R_XC, D_C)
    cv = cw_ref[0:1, :] * conv_ref[:, 0:D_C]
    cv = cv + cw_ref[1:2, :] * conv_ref[:, D_C:2 * D_C]
    cv = cv + cw_ref[2:3, :] * u
    convo_ref[:, 0:D_C] = conv_ref[:, D_C:2 * D_C]
    convo_ref[:, D_C:2 * D_C] = u
    yc_in = (_silu(piece(R_ZC, D_C)) * piece(R_BG, D_C) * cv).astype(BF16)
    y_c = _dot(yc_in, wpc_ref[...])
    mix = (_sigmoid(piece(R_GM, D_MODEL)) * y_m + _sigmoid(piece(R_GC, D_MODEL)) * y_c).astype(BF16)
    out = x_ref[...] + _dot(mix, wout_ref[...])
    y_ref[...] = _rmsnorm(out, fnw_ref[...])


def _sample_tail_call(cq, head_s, g, gb, n0, m0, x, rest, conv0, wpm, wpc, wout, hnw, cw, fnw):
    nseq = x.shape[0]
    args = (cq, head_s, g, gb, n0, m0, x, rest, conv0, wpm, wpc, wout, hnw, cw, fnw)
    full = lambda a: pl.BlockSpec(a.shape, lambda i: (0,) * a.ndim)
    return pl.pallas_call(
        _sample_tail_kernel,
        grid=(1,),
        in_specs=[full(a) for a in args],
        out_specs=(pl.BlockSpec((nseq, D_MODEL), lambda i: (0, 0)),
                   pl.BlockSpec((nseq, (CONV_W - 1) * D_C), lambda i: (0, 0))),
        out_shape=(jax.ShapeDtypeStruct((nseq, D_MODEL), F32),
                   jax.ShapeDtypeStruct((nseq, (CONV_W - 1) * D_C), F32)),
        compiler_params=pltpu.CompilerParams(dimension_semantics=("arbitrary",),
                                             vmem_limit_bytes=VMEM_LIMIT_SAMPLE),
        name="sample_tail",
    )(*args)


def kernel(x_prompt, x_sample, state_mlstm_C, state_mlstm_n, state_mlstm_m, state_conv, norm_w, w_in, b_i, b_f,
           head_norm_w, conv_w, w_proj_m, w_proj_c, w_out, final_norm_w):
    depth = norm_w.shape[0]
    assert depth == 1, "single-layer trunk"
    bsz = x_prompt.shape[0]
    nseq = x_sample.shape[0]

    assert w_in.shape == (1, D_MODEL, N_HEAD + N_GATE_COLS + N_REST)
    gb = jnp.pad(jnp.concatenate([b_i[0], b_f[0]]), (0, LANES - N_GATE_COLS)).reshape(1, LANES)
    nw = norm_w[0].reshape(1, D_MODEL)
    hnw = head_norm_w[0].reshape(1, D_M)
    cw = conv_w[0]
    fnw = final_norm_w.reshape(1, D_MODEL)
    xs = x_sample.reshape(nseq, D_MODEL)
    (wr, wh, wg, wpm, wpc, wout, head_s, g_s, rest_s) = _weight_prep_call(
        jnp.swapaxes(w_in, 1, 2), w_proj_m, w_proj_c, w_out, xs, nw)

    n0 = state_mlstm_n[0].reshape(nseq, D_QK)
    kw_s, fb_s, n_s, m_s = _sample_gates_call(head_s, g_s, gb, n0, state_mlstm_m)
    y_p, c_p, n_p, m_p, conv_p = _prompt_call(x_prompt, wh, wg, wr, wpm, wpc, wout, nw, gb, hnw, cw, fnw)
    m_p = m_p[:, :NH, 0].reshape(1, bsz, NH)
    c_s, cq = _sample_state_sc_call(state_mlstm_C, head_s, kw_s, fb_s)

    conv0 = state_conv[0].reshape(nseq, (CONV_W - 1) * D_C)
    y_s, conv_s = _sample_tail_call(cq, head_s, g_s, gb, n0, state_mlstm_m,
                                    xs, rest_s, conv0, wpm, wpc, wout, hnw, cw, fnw)

    return (y_p, y_s.reshape(nseq, 1, D_MODEL), c_p, n_p, m_p, conv_p,
            c_s, n_s.reshape(1, nseq, NH, DQK), m_s,
            conv_s.reshape(1, nseq, CONV_W - 1, D_C))
```

```python
import jax
import jax.numpy as jnp
from jax import lax
from jax.experimental import pallas as pl
from jax.experimental.pallas import tpu as pltpu
from jax.experimental.pallas import tpu_sc as plsc

F32 = jnp.float32
BF16 = jnp.bfloat16

D_MODEL = 1024
NH = 4
DHV = 256
DQK = 128
D_QK = NH * DQK
D_M = NH * DHV
D_C = D_MODEL
CONV_W = 3
EPS = 1e-6
NEG_BIG = -1e30
K_SCALE = DQK ** -0.5

LANES = 128
SUBLANES = 8

OFF_Q = 0
OFF_K = OFF_Q + D_QK
OFF_V = OFF_K + D_QK
N_HEAD = OFF_V + D_M
N_GATE_COLS = 2 * NH
R_O = 0
R_ZM = R_O + D_M
R_BG = R_ZM + D_M
R_CG = R_BG + D_C
R_XC = R_CG + D_C
R_ZC = R_XC + D_C
R_GM = R_ZC + D_C
R_GC = R_GM + D_MODEL
N_REST = R_GC + D_MODEL

PROMPT_BLOCK = 512
MLSTM_CHUNK = 256
PROMPT_ORDER = "UUUUFUFUFUFU"
VMEM_LIMIT_PROMPT = 60 * 1024 * 1024
VMEM_LIMIT_SAMPLE = 48 * 1024 * 1024


def _sigmoid(x):
    return 0.5 * jnp.tanh(0.5 * x) + 0.5


def _silu(x):
    return x * _sigmoid(x)


def _log_sigmoid(x):
    return jnp.minimum(x, 0.0) - jnp.log1p(jnp.exp(-jnp.abs(x)))


def _rmsnorm(x, w):
    return x * lax.rsqrt(jnp.mean(x * x, axis=-1, keepdims=True) + EPS) * w


def _dot(a, b):
    return jnp.dot(a, b, preferred_element_type=F32)


def _dot_nt(a, b):
    return lax.dot_general(a, b, (((1,), (1,)), ((), ())), preferred_element_type=F32)


def _dot_tn(a, b):
    return lax.dot_general(a, b, (((0,), (0,)), ((), ())), preferred_element_type=F32)


PREP_COLS = 1024
PREP_CHUNK = 256


PREP_STEPS = N_REST // PREP_COLS
PREP_HEAD_COLS = N_HEAD // PREP_STEPS
PREP_SQ_ROWS = D_MODEL // PREP_STEPS


def _weight_prep_kernel(a_ref, c_ref, g_ref, pm_ref, pc_ref, po_ref, xs_ref, nw_ref,
                        wr_ref, wh_ref, wg_ref, wpm_ref, wpc_ref, wout_ref,
                        hs_ref, gs_ref, rs_ref):
    j = pl.program_id(0)
    xn = _rmsnorm(xs_ref[:, 0, :], nw_ref[...]).astype(BF16)
    for r0 in range(0, PREP_COLS, PREP_CHUNK):
        w_t = a_ref[0, r0:r0 + PREP_CHUNK, :].T.astype(BF16)
        wr_ref[:, r0:r0 + PREP_CHUNK] = w_t
        rs_ref[:, r0:r0 + PREP_CHUNK] = _dot(xn, w_t)
    w_t = c_ref[0].T.astype(BF16)
    wh_ref[...] = w_t
    hs_ref[...] = _dot(xn, w_t)

    @pl.when(j == 0)
    def _():
        lane = lax.broadcasted_iota(jnp.int32, (D_MODEL, LANES), 1)
        w_g = jnp.where(lane < N_GATE_COLS, g_ref[0].T, 0.0).astype(BF16)
        wg_ref[...] = w_g
        gs_ref[...] = _dot(xn, w_g)

    wpm_ref[...] = pm_ref[0].astype(BF16)
    wpc_ref[...] = pc_ref[0].astype(BF16)
    wout_ref[...] = po_ref[0].astype(BF16)


def _weight_prep_call(w_in_t, w_proj_m, w_proj_c, w_out, xs, nw):
    nseq = xs.shape[0]
    rest_row0 = N_HEAD + N_GATE_COLS
    const = lambda shape: pl.BlockSpec(shape, lambda j: (0,) * len(shape))
    sq_in = pl.BlockSpec((1, PREP_SQ_ROWS, D_MODEL), lambda j: (0, j, 0))
    sq_out = pl.BlockSpec((PREP_SQ_ROWS, D_MODEL), lambda j: (j, 0))
    sq_shape = jax.ShapeDtypeStruct((D_MODEL, D_MODEL), BF16)
    return pl.pallas_call(
        _weight_prep_kernel,
        grid=(PREP_STEPS,),
        in_specs=[
            pl.BlockSpec((pl.Element(1), pl.Element(PREP_COLS), pl.Element(D_MODEL)),
                         lambda j: (0, pl.multiple_of(rest_row0 + j * PREP_COLS, SUBLANES), 0)),
            pl.BlockSpec((1, PREP_HEAD_COLS, D_MODEL), lambda j: (0, j, 0)),
            pl.BlockSpec((1, LANES, D_MODEL), lambda j: (0, N_HEAD // LANES, 0)),
            sq_in, sq_in, sq_in,
            const((nseq, 1, D_MODEL)),
            const((1, D_MODEL)),
        ],
        out_specs=(
            pl.BlockSpec((D_MODEL, PREP_COLS), lambda j: (0, j)),
            pl.BlockSpec((D_MODEL, PREP_HEAD_COLS), lambda j: (0, j)),
            const((D_MODEL, LANES)),
            sq_out, sq_out, sq_out,
            pl.BlockSpec((nseq, PREP_HEAD_COLS), lambda j: (0, j)),
            const((nseq, LANES)),
            pl.BlockSpec((nseq, PREP_COLS), lambda j: (0, j)),
        ),
        out_shape=(
            jax.ShapeDtypeStruct((D_MODEL, N_REST), BF16),
            jax.ShapeDtypeStruct((D_MODEL, N_HEAD), BF16),
            jax.ShapeDtypeStruct((D_MODEL, LANES), BF16),
            sq_shape, sq_shape, sq_shape,
            jax.ShapeDtypeStruct((nseq, N_HEAD), F32),
            jax.ShapeDtypeStruct((nseq, LANES), F32),
            jax.ShapeDtypeStruct((nseq, N_REST), F32),
        ),
        compiler_params=pltpu.CompilerParams(dimension_semantics=("arbitrary",),
                                             vmem_limit_bytes=VMEM_LIMIT_SAMPLE),
        name="weight_prep",
    )(w_in_t, w_in_t, w_in_t, w_proj_m, w_proj_c, w_out, xs, nw)


def _prompt_kernel(x_ref, wh_ref, wg_ref, wr_ref, wpm_ref, wpc_ref, wout_ref,
                   nw_ref, gb_ref, hnw_ref, cw_ref, fnw_ref,
                   y_ref, c_ref, n_ref, m_ref, conv_ref,
                   hg_s, ubuf_s, ct_s):
    tl = x_ref.shape[1]
    l = pl.program_id(1)

    @pl.when(l == 0)
    def _():
        ct_s[...] = jnp.zeros_like(ct_s)
        n_ref[...] = jnp.zeros_like(n_ref)
        m_ref[...] = jnp.zeros_like(m_ref)
        ubuf_s[0:SUBLANES, :] = jnp.zeros((SUBLANES, D_C), F32)

    x = x_ref[0]
    xn = _rmsnorm(x, nw_ref[...]).astype(BF16)

    def rest(off, width):
        return _dot(xn, wr_ref[:, off:off + width])

    g = _dot(xn, wg_ref[...]) + gb_ref[...]
    gc = jnp.where(lax.broadcasted_iota(jnp.int32, (tl, LANES), 1) < NH, g, _log_sigmoid(g))
    qkv = _dot(xn, wh_ref[...])

    ch = MLSTM_CHUNK
    lane = lax.broadcasted_iota(jnp.int32, (ch, LANES), 1)
    row = lax.broadcasted_iota(jnp.int32, (ch, ch), 0)
    col = lax.broadcasted_iota(jnp.int32, (ch, ch), 1)
    causal = row >= col
    row_g = lax.broadcasted_iota(jnp.int32, (ch, LANES), 0)
    chunk_gates = []
    for r0 in range(0, tl, ch):
        gc_c = gc[r0:r0 + ch, :]
        bc = gc_c
        shift = 1
        while shift < ch:
            bc = bc + jnp.where(row_g >= shift, pltpu.roll(bc, shift, axis=0), 0.0)
            shift *= 2
        gt = jnp.where(lane < NH, gc_c, bc).T
        chunk_gates.append((gc_c, bc, gt))

    head_gates = {}

    def mlstm_unit(ci, h):
        r0 = ci * ch
        gc_c, bc, gt = chunk_gates[ci]
        if ci == 0:
            head_gates[h] = (_sigmoid(rest(R_O + h * DHV, DHV)), _silu(rest(R_ZM + h * DHV, DHV)))
        sig_o = head_gates[h][0][r0:r0 + ch, :]
        silu_z = head_gates[h][1][r0:r0 + ch, :]
        q_f = qkv[r0:r0 + ch, OFF_Q + h * DQK:OFF_Q + (h + 1) * DQK]
        k_f = qkv[r0:r0 + ch, OFF_K + h * DQK:OFF_K + (h + 1) * DQK] * K_SCALE
        v_f = qkv[r0:r0 + ch, OFF_V + h * DHV:OFF_V + (h + 1) * DHV]
        q_b = q_f.astype(BF16)
        k_b = k_f.astype(BF16)
        v_b = v_f.astype(BF16)
        ct_old = ct_s[h]
        n_old = n_ref[0, 0, h:h + 1, :]
        m_old = m_ref[0, h:h + 1, 0:1]

        b_c = bc[:, NH + h:NH + h + 1]
        ig_c = gc_c[:, h:h + 1]
        ig_r = gt[h:h + 1, :]
        b_r = gt[NH + h:NH + h + 1, :]
        log_d = jnp.where(causal, b_c - b_r + ig_r, NEG_BIG)
        inter = b_c + m_old
        m_t = jnp.maximum(inter, jnp.max(log_d, axis=-1, keepdims=True))
        d_m = jnp.exp(log_d - m_t)
        w_int = jnp.exp(inter - m_t)
        s = _dot_nt(q_b, k_b) * d_m
        num = w_int * _dot(q_b, ct_old.astype(BF16)) + _dot(s.astype(BF16), v_b)
        den = w_int * jnp.sum(q_f * n_old, axis=-1, keepdims=True) + jnp.sum(s, axis=-1, keepdims=True)
        h_t = num / jnp.maximum(jnp.abs(den), jnp.exp(-m_t))

        b_end = b_c[ch - 1:ch, :]
        inter_end = b_end + m_old
        m_new = jnp.maximum(inter_end, jnp.max(b_end - b_r + ig_r, axis=-1, keepdims=True))
        w_end = jnp.exp(b_end - b_c + ig_c - m_new)
        f_end = jnp.exp(inter_end - m_new)
        ct_s[h] = f_end * ct_old + _dot_tn(k_b, (w_end * v_f).astype(BF16))
        n_ref[0, 0, h:h + 1, :] = f_end * n_old + jnp.sum(w_end * k_f, axis=0, keepdims=True)
        m_ref[0, h:h + 1, :] = jnp.broadcast_to(m_new, (1, LANES))

        hn = h_t * lax.rsqrt(jnp.mean(h_t * h_t, axis=-1, keepdims=True) + EPS)
        hn = hn * hnw_ref[:, h * DHV:(h + 1) * DHV]
        hg_s[r0:r0 + ch, h * DHV:(h + 1) * DHV] = (hn * sig_o * silu_z).astype(BF16)

    val = {}

    def conv_input():
        u = rest(R_CG, D_C) * rest(R_XC, D_C)
        ubuf_s[SUBLANES:SUBLANES + tl, :] = u
        cv = cw_ref[0:1, :] * ubuf_s[SUBLANES - 2:SUBLANES - 2 + tl, :]
        cv = cv + cw_ref[1:2, :] * ubuf_s[SUBLANES - 1:SUBLANES - 1 + tl, :]
        val["cv"] = cv + cw_ref[2:3, :] * u

    def conv_gate():
        val["yc_in"] = (_silu(rest(R_ZC, D_C)) * rest(R_BG, D_C) * val["cv"]).astype(BF16)

    def conv_proj():
        val["gy_c"] = _sigmoid(rest(R_GC, D_MODEL)) * _dot(val["yc_in"], wpc_ref[...])

    def merge_gate():
        val["sig_gm"] = _sigmoid(rest(R_GM, D_MODEL))

    fillers = iter([conv_input, conv_gate, conv_proj, merge_gate])
    units = iter([(ci, h) for ci in range(tl // ch) for h in range(NH)])
    for step in PROMPT_ORDER:
        if step == "U":
            mlstm_unit(*next(units))
        else:
            next(fillers)()

    ubuf_s[0:SUBLANES, :] = ubuf_s[tl:tl + SUBLANES, :]

    conv_ref[0, 0] = ubuf_s[SUBLANES - (CONV_W - 1):SUBLANES, :]
    for h in range(NH):
        c_ref[0, 0, h] = ct_s[h].T

    y_m = _dot(hg_s[...], wpm_ref[...])
    mix = (val["sig_gm"] * y_m + val["gy_c"]).astype(BF16)
    out = x + _dot(mix, wout_ref[...])
    y_ref[0] = _rmsnorm(out, fnw_ref[...])


def _resident(shape):
    return pl.BlockSpec(shape, lambda *_: (0,) * len(shape), pipeline_mode=pl.Buffered(1))


def _prompt_call(x, wh, wg, wr, wpm, wpc, wout, nw, gb, hnw, cw, fnw):
    bsz, seq, _ = x.shape
    tl = PROMPT_BLOCK
    grid = (bsz, seq // tl)
    out_shape = (
        jax.ShapeDtypeStruct((bsz, seq, D_MODEL), F32),
        jax.ShapeDtypeStruct((1, bsz, NH, DHV, DQK), F32),
        jax.ShapeDtypeStruct((1, bsz, NH, DQK), F32),
        jax.ShapeDtypeStruct((bsz, SUBLANES, LANES), F32),
        jax.ShapeDtypeStruct((1, bsz, CONV_W - 1, D_C), F32),
    )
    in_specs = [
        pl.BlockSpec((1, tl, D_MODEL), lambda b, l: (b, l, 0)),
        _resident((D_MODEL, N_HEAD)),
        _resident((D_MODEL, LANES)),
        _resident((D_MODEL, N_REST)),
        _resident((D_M, D_MODEL)),
        _resident((D_C, D_MODEL)),
        _resident((D_MODEL, D_MODEL)),
        _resident((1, D_MODEL)),
        _resident((1, LANES)),
        _resident((1, D_M)),
        _resident((CONV_W, D_C)),
        _resident((1, D_MODEL)),
    ]
    out_specs = (
        pl.BlockSpec((1, tl, D_MODEL), lambda b, l: (b, l, 0)),
        pl.BlockSpec((1, 1, NH, DHV, DQK), lambda b, l: (0, b, 0, 0, 0)),
        pl.BlockSpec((1, 1, NH, DQK), lambda b, l: (0, b, 0, 0)),
        pl.BlockSpec((1, SUBLANES, LANES), lambda b, l: (b, 0, 0)),
        pl.BlockSpec((1, 1, CONV_W - 1, D_C), lambda b, l: (0, b, 0, 0)),
    )
    return pl.pallas_call(
        _prompt_kernel,
        grid=grid,
        in_specs=in_specs,
        out_specs=out_specs,
        out_shape=out_shape,
        scratch_shapes=[
            pltpu.VMEM((tl, D_M), BF16),
            pltpu.VMEM((tl + 2 * SUBLANES, D_C), F32),
            pltpu.VMEM((NH, DQK, DHV), F32),
        ],
        compiler_params=pltpu.CompilerParams(
            dimension_semantics=("arbitrary", "arbitrary"),
            vmem_limit_bytes=VMEM_LIMIT_PROMPT),
        name="prompt_layer",
    )(x, wh, wg, wr, wpm, wpc, wout, nw, gb, hnw, cw, fnw)


SC_LANES = 16
SC_CORES = 2
SC_SUBCORES = 16


def _sample_gate_scalars(g_ref, gb_ref, m_ref):
    g = g_ref[...] + gb_ref[...]
    ig = g[:, 0:NH]
    lf = _log_sigmoid(g[:, NH:2 * NH])
    inter = lf + m_ref[0]
    m_t = jnp.maximum(inter, ig)
    return m_t, jnp.exp(ig - m_t), jnp.exp(inter - m_t), jnp.exp(-m_t)


def _sample_gates_kernel(hs_ref, g_ref, gb_ref, n_ref, m_ref, kw_ref, fb_ref, no_ref, mo_ref):
    nseq = hs_ref.shape[0]
    m_t, w_end, f_end, _ = _sample_gate_scalars(g_ref, gb_ref, m_ref)
    mo_ref[0] = m_t
    for h in range(NH):
        k_h = hs_ref[:, OFF_K + h * DQK:OFF_K + (h + 1) * DQK] * K_SCALE
        n_h = n_ref[0, :, h, :]
        w_h = w_end[:, h:h + 1]
        f_h = f_end[:, h:h + 1]
        kw_ref[:, h * DQK:(h + 1) * DQK] = w_h * k_h
        fb_ref[:, h * DQK:(h + 1) * DQK] = jnp.broadcast_to(f_h, (nseq, DQK))
        no_ref[0, :, h, :] = f_h * n_h + w_h * k_h


def _sample_gates_call(head_s, g, gb, n0, m0):
    nseq = head_s.shape[0]
    args = (head_s, g, gb, n0, m0)
    full = lambda a: pl.BlockSpec(a.shape, lambda i: (0,) * a.ndim)
    row = jax.ShapeDtypeStruct((nseq, D_QK), F32)
    return pl.pallas_call(
        _sample_gates_kernel,
        grid=(1,),
        in_specs=[full(a) for a in args],
        out_specs=(full(row), full(row), full(n0), full(m0)),
        out_shape=(row, row, jax.ShapeDtypeStruct(n0.shape, F32), jax.ShapeDtypeStruct(m0.shape, F32)),
        name="sample_gates",
    )(*args)


def _sample_state_sc_call(c0, head_s, kw_s, fb_s):
    nseq = c0.shape[1]
    nworkers = SC_CORES * SC_SUBCORES
    per_worker = nseq * NH // nworkers
    nvec = DQK // SC_LANES
    mesh = plsc.VectorSubcoreMesh(core_axis_name="c", subcore_axis_name="s")

    def body(c_hbm, hs_hbm, kw_hbm, fb_hbm, co_hbm, cq_hbm, c_v, q_v, kw_v, v_v, f_v, cq_v):
        wid = lax.axis_index("c") * SC_SUBCORES + lax.axis_index("s")
        lane = lax.iota(jnp.int32, SC_LANES)
        zero_i = jnp.zeros((SC_LANES,), jnp.int32)

        @pl.loop(0, per_worker)
        def _(t):
            pair = wid * per_worker + t
            b = pair // NH
            h = pair % NH
            pltpu.sync_copy(c_hbm.at[0, b, h], c_v)
            pltpu.sync_copy(hs_hbm.at[b, pl.ds(OFF_Q + h * DQK, DQK)], q_v)
            pltpu.sync_copy(hs_hbm.at[b, pl.ds(OFF_V + h * DHV, DHV)], v_v)
            pltpu.sync_copy(kw_hbm.at[b, pl.ds(h * DQK, DQK)], kw_v)
            pltpu.sync_copy(fb_hbm.at[b, pl.ds(h * DQK, DQK)], f_v)
            q = [q_v[pl.ds(SC_LANES * j, SC_LANES)] for j in range(nvec)]
            kw = [kw_v[pl.ds(SC_LANES * j, SC_LANES)] for j in range(nvec)]
            f_vec = f_v[pl.ds(0, SC_LANES)]

            @pl.loop(0, DHV // SC_LANES)
            def _(g):
                cq_vec = jnp.zeros((SC_LANES,), F32)
                for i in range(SC_LANES):
                    r = g * SC_LANES + i
                    v_r = plsc.load_gather(v_v, [zero_i + r])
                    acc = jnp.zeros((SC_LANES,), F32)
                    for j in range(nvec):
                        c = c_v[r, pl.ds(SC_LANES * j, SC_LANES)]
                        acc = acc + c * q[j]
                        c_v[r, pl.ds(SC_LANES * j, SC_LANES)] = f_vec * c + v_r * kw[j]
                    cq_vec = jnp.where(lane == i, jnp.sum(acc), cq_vec)
                cq_v[pl.ds(g * SC_LANES, SC_LANES)] = cq_vec

            pltpu.sync_copy(c_v, co_hbm.at[0, b, h])
            pltpu.sync_copy(cq_v, cq_hbm.at[b, pl.ds(h * DHV, DHV)])

    return pl.kernel(
        body,
        out_type=(jax.ShapeDtypeStruct(c0.shape, F32),
                  jax.ShapeDtypeStruct((nseq, D_M), F32)),
        mesh=mesh,
        scratch_types=[pltpu.VMEM((DHV, DQK), F32),
                       pltpu.VMEM((DQK,), F32),
                       pltpu.VMEM((DQK,), F32),
                       pltpu.VMEM((DHV,), F32),
                       pltpu.VMEM((DQK,), F32),
                       pltpu.VMEM((DHV,), F32)],
        compiler_params=pltpu.CompilerParams(use_tc_tiling_on_sc=True, needs_layout_passes=False),
        name="sample_state_sc",
    )(c0, head_s, kw_s, fb_s)


def _sample_tail_kernel(cq_ref, hs_ref, g_ref, gb_ref, n_ref, m_ref,
                        x_ref, r_ref, conv_ref, wpm_ref, wpc_ref, wout_ref, hnw_ref, cw_ref, fnw_ref,
                        y_ref, convo_ref):
    def piece(off, width):
        return r_ref[:, off:off + width]

    m_t, w_end, f_end, floor = _sample_gate_scalars(g_ref, gb_ref, m_ref)
    for h in range(NH):
        q_h = hs_ref[:, OFF_Q + h * DQK:OFF_Q + (h + 1) * DQK]
        k_h = hs_ref[:, OFF_K + h * DQK:OFF_K + (h + 1) * DQK] * K_SCALE
        v_h = hs_ref[:, OFF_V + h * DHV:OFF_V + (h + 1) * DHV]
        n_h = n_ref[0, :, h, :]
        w_h = w_end[:, h:h + 1]
        f_h = f_end[:, h:h + 1]
        s = jnp.sum(q_h * k_h, axis=-1, keepdims=True) * w_h
        den = f_h * jnp.sum(n_h * q_h, axis=-1, keepdims=True) + s
        num = f_h * cq_ref[:, h * DHV:(h + 1) * DHV] + s * v_h
        h_t = num / jnp.maximum(jnp.abs(den), floor[:, h:h + 1])
        hn = h_t * lax.rsqrt(jnp.mean(h_t * h_t, axis=-1, keepdims=True) + EPS)
        hn = hn * hnw_ref[:, h * DHV:(h + 1) * DHV]
        o_h = piece(R_O + h * DHV, DHV)
        zm_h = piece(R_ZM + h * DHV, DHV)
        hg = (hn * _sigmoid(o_h) * _silu(zm_h)).astype(BF16)
        part = _dot(hg, wpm_ref[h * DHV:(h + 1) * DHV, :])
        y_m = part if h == 0 else y_m + part

    u = piece(R_CG, D_C) * piece(R_XC, D_C)
    conv_old = conv_ref[0, :, 1, :]
    cv = cw_ref[0:1, :] * conv_ref[0, :, 0, :]
    cv = cv + cw_ref[1:2, :] * conv_old
    cv = cv + cw_ref[2:3, :] * u
    convo_ref[0, :, 0, :] = conv_old
    convo_ref[0, :, 1, :] = u
    yc_in = (_silu(piece(R_ZC, D_C)) * piece(R_BG, D_C) * cv).astype(BF16)
    y_c = _dot(yc_in, wpc_ref[...])
    mix = (_sigmoid(piece(R_GM, D_MODEL)) * y_m + _sigmoid(piece(R_GC, D_MODEL)) * y_c).astype(BF16)
    out = x_ref[:, 0, :] + _dot(mix, wout_ref[...])
    y_ref[:, 0, :] = _rmsnorm(out, fnw_ref[...])


def _sample_tail_call(cq, head_s, g, gb, n0, m0, x, rest, conv0, wpm, wpc, wout, hnw, cw, fnw):
    nseq = x.shape[0]
    args = (cq, head_s, g, gb, n0, m0, x, rest, conv0, wpm, wpc, wout, hnw, cw, fnw)
    full = lambda a: pl.BlockSpec(a.shape, lambda i: (0,) * a.ndim)
    return pl.pallas_call(
        _sample_tail_kernel,
        grid=(1,),
        in_specs=[full(a) for a in args],
        out_specs=(pl.BlockSpec((nseq, 1, D_MODEL), lambda i: (0, 0, 0)),
                   pl.BlockSpec((1, nseq, CONV_W - 1, D_C), lambda i: (0, 0, 0, 0))),
        out_shape=(jax.ShapeDtypeStruct((nseq, 1, D_MODEL), F32),
                   jax.ShapeDtypeStruct((1, nseq, CONV_W - 1, D_C), F32)),
        compiler_params=pltpu.CompilerParams(dimension_semantics=("arbitrary",),
                                             vmem_limit_bytes=VMEM_LIMIT_SAMPLE),
        name="sample_tail",
    )(*args)


def kernel(x_prompt, x_sample, state_mlstm_C, state_mlstm_n, state_mlstm_m, state_conv, norm_w, w_in, b_i, b_f,
           head_norm_w, conv_w, w_proj_m, w_proj_c, w_out, final_norm_w):
    depth = norm_w.shape[0]
    assert depth == 1, "single-layer trunk"
    bsz = x_prompt.shape[0]
    nseq = x_sample.shape[0]

    assert w_in.shape == (1, D_MODEL, N_HEAD + N_GATE_COLS + N_REST)
    gb = jnp.pad(jnp.concatenate([b_i[0], b_f[0]]), (0, LANES - N_GATE_COLS)).reshape(1, LANES)
    nw = norm_w[0].reshape(1, D_MODEL)
    hnw = head_norm_w[0].reshape(1, D_M)
    cw = conv_w[0]
    fnw = final_norm_w.reshape(1, D_MODEL)
    (wr, wh, wg, wpm, wpc, wout, head_s, g_s, rest_s) = _weight_prep_call(
        jnp.swapaxes(w_in, 1, 2), w_proj_m, w_proj_c, w_out, x_sample, nw)

    kw_s, fb_s, n_s, m_s = _sample_gates_call(head_s, g_s, gb, state_mlstm_n, state_mlstm_m)
    y_p, c_p, n_p, m_p, conv_p = _prompt_call(x_prompt, wh, wg, wr, wpm, wpc, wout, nw, gb, hnw, cw, fnw)
    m_p = m_p[:, :NH, 0].reshape(1, bsz, NH)
    c_s, cq = _sample_state_sc_call(state_mlstm_C, head_s, kw_s, fb_s)

    y_s, conv_s = _sample_tail_call(cq, head_s, g_s, gb, state_mlstm_n, state_mlstm_m,
                                    x_sample, rest_s, state_conv, wpm, wpc, wout, hnw, cw, fnw)

    return (y_p, y_s, c_p, n_p, m_p, conv_p, c_s, n_s, m_s, conv_s)
```

```python
import jax
import jax.numpy as jnp
from jax import lax
from jax.experimental import pallas as pl
from jax.experimental.pallas import tpu as pltpu
from jax.experimental.pallas import tpu_sc as plsc

F32 = jnp.float32
BF16 = jnp.bfloat16

D_MODEL = 1024
NH = 4
DHV = 256
DQK = 128
D_QK = NH * DQK
D_M = NH * DHV
D_C = D_MODEL
CONV_W = 3
EPS = 1e-6
NEG_BIG = -1e30
K_SCALE = DQK ** -0.5

LANES = 128
SUBLANES = 8

OFF_Q = 0
OFF_K = OFF_Q + D_QK
OFF_V = OFF_K + D_QK
N_HEAD = OFF_V + D_M
N_GATE_COLS = 2 * NH
R_O = 0
R_ZM = R_O + D_M
R_BG = R_ZM + D_M
R_CG = R_BG + D_C
R_XC = R_CG + D_C
R_ZC = R_XC + D_C
R_GM = R_ZC + D_C
R_GC = R_GM + D_MODEL
N_REST = R_GC + D_MODEL

PROMPT_BLOCK = 512
MLSTM_CHUNK = 256
PROMPT_ORDER = "UUUUFUFUFUFU"
VMEM_LIMIT_PROMPT = 60 * 1024 * 1024
VMEM_LIMIT_SAMPLE = 48 * 1024 * 1024


def _sigmoid(x):
    return 0.5 * jnp.tanh(0.5 * x) + 0.5


def _silu(x):
    return x * _sigmoid(x)


def _log_sigmoid(x):
    return jnp.minimum(x, 0.0) - jnp.log1p(jnp.exp(-jnp.abs(x)))


def _rmsnorm(x, w):
    return x * lax.rsqrt(jnp.mean(x * x, axis=-1, keepdims=True) + EPS) * w


def _dot(a, b):
    return jnp.dot(a, b, preferred_element_type=F32)


def _dot_nt(a, b):
    return lax.dot_general(a, b, (((1,), (1,)), ((), ())), preferred_element_type=F32)


def _dot_tn(a, b):
    return lax.dot_general(a, b, (((0,), (0,)), ((), ())), preferred_element_type=F32)


PREP_COLS = 1024
PREP_CHUNK = 256


PREP_STEPS = N_REST // PREP_COLS
PREP_HEAD_COLS = N_HEAD // PREP_STEPS
PREP_SQ_ROWS = D_MODEL // PREP_STEPS


def _weight_prep_kernel(a_ref, c_ref, g_ref, pm_ref, pc_ref, po_ref, xs_ref, nw_ref,
                        wr_ref, wh_ref, wg_ref, wpm_ref, wpc_ref, wout_ref,
                        hs_ref, gs_ref, rs_ref, xn_s):
    j = pl.program_id(0)

    @pl.when(j == 0)
    def _():
        xn_s[...] = _rmsnorm(xs_ref[:, 0, :], nw_ref[...]).astype(BF16)

    xn = xn_s[...]
    for r0 in range(0, PREP_COLS, PREP_CHUNK):
        w_t = a_ref[0, r0:r0 + PREP_CHUNK, :].T.astype(BF16)
        wr_ref[:, r0:r0 + PREP_CHUNK] = w_t
        rs_ref[:, r0:r0 + PREP_CHUNK] = _dot(xn, w_t)
    w_t = c_ref[0].T.astype(BF16)
    wh_ref[...] = w_t
    hs_ref[...] = _dot(xn, w_t)

    @pl.when(j == 0)
    def _():
        lane = lax.broadcasted_iota(jnp.int32, (D_MODEL, LANES), 1)
        w_g = jnp.where(lane < N_GATE_COLS, g_ref[0].T, 0.0).astype(BF16)
        wg_ref[...] = w_g
        gs_ref[...] = _dot(xn, w_g)

    wpm_ref[...] = pm_ref[0].astype(BF16)
    wpc_ref[...] = pc_ref[0].astype(BF16)
    wout_ref[...] = po_ref[0].astype(BF16)


def _weight_prep_call(w_in_t, w_proj_m, w_proj_c, w_out, xs, nw):
    nseq = xs.shape[0]
    rest_row0 = N_HEAD + N_GATE_COLS
    const = lambda shape: pl.BlockSpec(shape, lambda j: (0,) * len(shape))
    sq_in = pl.BlockSpec((1, PREP_SQ_ROWS, D_MODEL), lambda j: (0, j, 0))
    sq_out = pl.BlockSpec((PREP_SQ_ROWS, D_MODEL), lambda j: (j, 0))
    sq_shape = jax.ShapeDtypeStruct((D_MODEL, D_MODEL), BF16)
    return pl.pallas_call(
        _weight_prep_kernel,
        grid=(PREP_STEPS,),
        in_specs=[
            pl.BlockSpec((pl.Element(1), pl.Element(PREP_COLS), pl.Element(D_MODEL)),
                         lambda j: (0, pl.multiple_of(rest_row0 + j * PREP_COLS, SUBLANES), 0)),
            pl.BlockSpec((1, PREP_HEAD_COLS, D_MODEL), lambda j: (0, j, 0)),
            pl.BlockSpec((1, LANES, D_MODEL), lambda j: (0, N_HEAD // LANES, 0)),
            sq_in, sq_in, sq_in,
            const((nseq, 1, D_MODEL)),
            const((1, D_MODEL)),
        ],
        out_specs=(
            pl.BlockSpec((D_MODEL, PREP_COLS), lambda j: (0, j)),
            pl.BlockSpec((D_MODEL, PREP_HEAD_COLS), lambda j: (0, j)),
            const((D_MODEL, LANES)),
            sq_out, sq_out, sq_out,
            pl.BlockSpec((nseq, PREP_HEAD_COLS), lambda j: (0, j)),
            const((nseq, LANES)),
            pl.BlockSpec((nseq, PREP_COLS), lambda j: (0, j)),
        ),
        out_shape=(
            jax.ShapeDtypeStruct((D_MODEL, N_REST), BF16),
            jax.ShapeDtypeStruct((D_MODEL, N_HEAD), BF16),
            jax.ShapeDtypeStruct((D_MODEL, LANES), BF16),
            sq_shape, sq_shape, sq_shape,
            jax.ShapeDtypeStruct((nseq, N_HEAD), F32),
            jax.ShapeDtypeStruct((nseq, LANES), F32),
            jax.ShapeDtypeStruct((nseq, N_REST), F32),
        ),
        scratch_shapes=[pltpu.VMEM((nseq, D_MODEL), BF16)],
        compiler_params=pltpu.CompilerParams(dimension_semantics=("arbitrary",),
                                             vmem_limit_bytes=VMEM_LIMIT_SAMPLE),
        name="weight_prep",
    )(w_in_t, w_in_t, w_in_t, w_proj_m, w_proj_c, w_out, xs, nw)


def _prompt_kernel(x_ref, wh_ref, wg_ref, wr_ref, wpm_ref, wpc_ref, wout_ref,
                   nw_ref, gb_ref, hnw_ref, cw_ref, fnw_ref,
                   y_ref, c_ref, n_ref, m_ref, conv_ref,
                   hg_s, ubuf_s, ct_s):
    tl = x_ref.shape[1]
    l = pl.program_id(1)

    @pl.when(l == 0)
    def _():
        ct_s[...] = jnp.zeros_like(ct_s)
        n_ref[...] = jnp.zeros_like(n_ref)
        m_ref[...] = jnp.zeros_like(m_ref)
        ubuf_s[0:SUBLANES, :] = jnp.zeros((SUBLANES, D_C), F32)

    x = x_ref[0]
    xn = _rmsnorm(x, nw_ref[...]).astype(BF16)

    def rest(off, width):
        return _dot(xn, wr_ref[:, off:off + width])

    g = _dot(xn, wg_ref[...]) + gb_ref[...]
    gc = jnp.where(lax.broadcasted_iota(jnp.int32, (tl, LANES), 1) < NH, g, _log_sigmoid(g))
    qkv = _dot(xn, wh_ref[...])

    ch = MLSTM_CHUNK
    lane = lax.broadcasted_iota(jnp.int32, (ch, LANES), 1)
    row = lax.broadcasted_iota(jnp.int32, (ch, ch), 0)
    col = lax.broadcasted_iota(jnp.int32, (ch, ch), 1)
    causal = row >= col
    row_g = lax.broadcasted_iota(jnp.int32, (ch, LANES), 0)
    chunk_gates = []
    for r0 in range(0, tl, ch):
        gc_c = gc[r0:r0 + ch, :]
        bc = gc_c
        shift = 1
        while shift < ch:
            bc = bc + jnp.where(row_g >= shift, pltpu.roll(bc, shift, axis=0), 0.0)
            shift *= 2
        gt = jnp.where(lane < NH, gc_c, bc).T
        chunk_gates.append((gc_c, bc, gt))

    head_gates = {}

    def mlstm_unit(ci, h):
        r0 = ci * ch
        gc_c, bc, gt = chunk_gates[ci]
        if ci == 0:
            head_gates[h] = (_sigmoid(rest(R_O + h * DHV, DHV)), _silu(rest(R_ZM + h * DHV, DHV)))
        sig_o = head_gates[h][0][r0:r0 + ch, :]
        silu_z = head_gates[h][1][r0:r0 + ch, :]
        q_f = qkv[r0:r0 + ch, OFF_Q + h * DQK:OFF_Q + (h + 1) * DQK]
        k_f = qkv[r0:r0 + ch, OFF_K + h * DQK:OFF_K + (h + 1) * DQK] * K_SCALE
        v_f = qkv[r0:r0 + ch, OFF_V + h * DHV:OFF_V + (h + 1) * DHV]
        q_b = q_f.astype(BF16)
        k_b = k_f.astype(BF16)
        v_b = v_f.astype(BF16)
        ct_old = ct_s[h]
        n_old = n_ref[0, 0, h:h + 1, :]
        m_old = m_ref[0, h:h + 1, 0:1]

        b_c = bc[:, NH + h:NH + h + 1]
        ig_c = gc_c[:, h:h + 1]
        ig_r = gt[h:h + 1, :]
        b_r = gt[NH + h:NH + h + 1, :]
        log_d = jnp.where(causal, b_c - b_r + ig_r, NEG_BIG)
        inter = b_c + m_old
        m_t = jnp.maximum(inter, jnp.max(log_d, axis=-1, keepdims=True))
        d_m = jnp.exp(log_d - m_t)
        w_int = jnp.exp(inter - m_t)
        s = _dot_nt(q_b, k_b) * d_m
        num = w_int * _dot(q_b, ct_old.astype(BF16)) + _dot(s.astype(BF16), v_b)
        den = w_int * jnp.sum(q_f * n_old, axis=-1, keepdims=True) + jnp.sum(s, axis=-1, keepdims=True)
        h_t = num / jnp.maximum(jnp.abs(den), jnp.exp(-m_t))

        b_end = b_c[ch - 1:ch, :]
        inter_end = b_end + m_old
        m_new = jnp.maximum(inter_end, jnp.max(b_end - b_r + ig_r, axis=-1, keepdims=True))
        w_end = jnp.exp(b_end - b_c + ig_c - m_new)
        f_end = jnp.exp(inter_end - m_new)
        ct_s[h] = f_end * ct_old + _dot_tn(k_b, (w_end * v_f).astype(BF16))
        n_ref[0, 0, h:h + 1, :] = f_end * n_old + jnp.sum(w_end * k_f, axis=0, keepdims=True)
        m_ref[0, h:h + 1, :] = jnp.broadcast_to(m_new, (1, LANES))

        hn = h_t * lax.rsqrt(jnp.mean(h_t * h_t, axis=-1, keepdims=True) + EPS)
        hn = hn * hnw_ref[:, h * DHV:(h + 1) * DHV]
        hg_s[r0:r0 + ch, h * DHV:(h + 1) * DHV] = (hn * sig_o * silu_z).astype(BF16)

    val = {}

    def conv_input():
        u = rest(R_CG, D_C) * rest(R_XC, D_C)
        ubuf_s[SUBLANES:SUBLANES + tl, :] = u
        cv = cw_ref[0:1, :] * ubuf_s[SUBLANES - 2:SUBLANES - 2 + tl, :]
        cv = cv + cw_ref[1:2, :] * ubuf_s[SUBLANES - 1:SUBLANES - 1 + tl, :]
        val["cv"] = cv + cw_ref[2:3, :] * u

    def conv_gate():
        val["yc_in"] = (_silu(rest(R_ZC, D_C)) * rest(R_BG, D_C) * val["cv"]).astype(BF16)

    def conv_proj():
        val["gy_c"] = _sigmoid(rest(R_GC, D_MODEL)) * _dot(val["yc_in"], wpc_ref[...])

    def merge_gate():
        val["sig_gm"] = _sigmoid(rest(R_GM, D_MODEL))

    fillers = iter([conv_input, conv_gate, conv_proj, merge_gate])
    units = iter([(ci, h) for ci in range(tl // ch) for h in range(NH)])
    for step in PROMPT_ORDER:
        if step == "U":
            mlstm_unit(*next(units))
        else:
            next(fillers)()

    ubuf_s[0:SUBLANES, :] = ubuf_s[tl:tl + SUBLANES, :]

    conv_ref[0, 0] = ubuf_s[SUBLANES - (CONV_W - 1):SUBLANES, :]
    for h in range(NH):
        c_ref[0, 0, h] = ct_s[h].T

    y_m = _dot(hg_s[...], wpm_ref[...])
    mix = (val["sig_gm"] * y_m + val["gy_c"]).astype(BF16)
    out = x + _dot(mix, wout_ref[...])
    y_ref[0] = _rmsnorm(out, fnw_ref[...])


def _resident(shape):
    return pl.BlockSpec(shape, lambda *_: (0,) * len(shape), pipeline_mode=pl.Buffered(1))


def _prompt_call(x, wh, wg, wr, wpm, wpc, wout, nw, gb, hnw, cw, fnw):
    bsz, seq, _ = x.shape
    tl = PROMPT_BLOCK
    grid = (bsz, seq // tl)
    out_shape = (
        jax.ShapeDtypeStruct((bsz, seq, D_MODEL), F32),
        jax.ShapeDtypeStruct((1, bsz, NH, DHV, DQK), F32),
        jax.ShapeDtypeStruct((1, bsz, NH, DQK), F32),
        jax.ShapeDtypeStruct((bsz, SUBLANES, LANES), F32),
        jax.ShapeDtypeStruct((1, bsz, CONV_W - 1, D_C), F32),
    )
    in_specs = [
        pl.BlockSpec((1, tl, D_MODEL), lambda b, l: (b, l, 0)),
        _resident((D_MODEL, N_HEAD)),
        _resident((D_MODEL, LANES)),
        _resident((D_MODEL, N_REST)),
        _resident((D_M, D_MODEL)),
        _resident((D_C, D_MODEL)),
        _resident((D_MODEL, D_MODEL)),
        _resident((1, D_MODEL)),
        _resident((1, LANES)),
        _resident((1, D_M)),
        _resident((CONV_W, D_C)),
        _resident((1, D_MODEL)),
    ]
    out_specs = (
        pl.BlockSpec((1, tl, D_MODEL), lambda b, l: (b, l, 0)),
        pl.BlockSpec((1, 1, NH, DHV, DQK), lambda b, l: (0, b, 0, 0, 0)),
        pl.BlockSpec((1, 1, NH, DQK), lambda b, l: (0, b, 0, 0)),
        pl.BlockSpec((1, SUBLANES, LANES), lambda b, l: (b, 0, 0)),
        pl.BlockSpec((1, 1, CONV_W - 1, D_C), lambda b, l: (0, b, 0, 0)),
    )
    return pl.pallas_call(
        _prompt_kernel,
        grid=grid,
        in_specs=in_specs,
        out_specs=out_specs,
        out_shape=out_shape,
        scratch_shapes=[
            pltpu.VMEM((tl, D_M), BF16),
            pltpu.VMEM((tl + 2 * SUBLANES, D_C), F32),
            pltpu.VMEM((NH, DQK, DHV), F32),
        ],
        compiler_params=pltpu.CompilerParams(
            dimension_semantics=("arbitrary", "arbitrary"),
            vmem_limit_bytes=VMEM_LIMIT_PROMPT),
        name="prompt_layer",
    )(x, wh, wg, wr, wpm, wpc, wout, nw, gb, hnw, cw, fnw)


SC_LANES = 16
SC_CORES = 2
SC_SUBCORES = 16


def _sample_gate_scalars(g_ref, gb_ref, m_ref):
    g = g_ref[...] + gb_ref[...]
    ig = g[:, 0:NH]
    lf = _log_sigmoid(g[:, NH:2 * NH])
    inter = lf + m_ref[0]
    m_t = jnp.maximum(inter, ig)
    return m_t, jnp.exp(ig - m_t), jnp.exp(inter - m_t), jnp.exp(-m_t)


def _sample_gates_kernel(hs_ref, g_ref, gb_ref, n_ref, m_ref, kw_ref, fb_ref, no_ref, mo_ref):
    nseq = hs_ref.shape[0]
    m_t, w_end, f_end, _ = _sample_gate_scalars(g_ref, gb_ref, m_ref)
    mo_ref[0] = m_t
    for h in range(NH):
        k_h = hs_ref[:, OFF_K + h * DQK:OFF_K + (h + 1) * DQK] * K_SCALE
        n_h = n_ref[0, :, h, :]
        w_h = w_end[:, h:h + 1]
        f_h = f_end[:, h:h + 1]
        kw_ref[:, h * DQK:(h + 1) * DQK] = w_h * k_h
        fb_ref[:, h * DQK:(h + 1) * DQK] = jnp.broadcast_to(f_h, (nseq, DQK))
        no_ref[0, :, h, :] = f_h * n_h + w_h * k_h


def _sample_gates_call(head_s, g, gb, n0, m0):
    nseq = head_s.shape[0]
    args = (head_s, g, gb, n0, m0)
    full = lambda a: pl.BlockSpec(a.shape, lambda i: (0,) * a.ndim)
    row = jax.ShapeDtypeStruct((nseq, D_QK), F32)
    return pl.pallas_call(
        _sample_gates_kernel,
        grid=(1,),
        in_specs=[full(a) for a in args],
        out_specs=(full(row), full(row), full(n0), full(m0)),
        out_shape=(row, row, jax.ShapeDtypeStruct(n0.shape, F32), jax.ShapeDtypeStruct(m0.shape, F32)),
        name="sample_gates",
    )(*args)


def _sample_state_sc_call(c0, head_s, kw_s, fb_s):
    nseq = c0.shape[1]
    nworkers = SC_CORES * SC_SUBCORES
    per_worker = nseq * NH // nworkers
    nvec = DQK // SC_LANES
    mesh = plsc.VectorSubcoreMesh(core_axis_name="c", subcore_axis_name="s")

    def body(c_hbm, hs_hbm, kw_hbm, fb_hbm, co_hbm, cq_hbm, c_v, q_v, kw_v, v_v, f_v, cq_v):
        wid = lax.axis_index("c") * SC_SUBCORES + lax.axis_index("s")
        lane = lax.iota(jnp.int32, SC_LANES)
        zero_i = jnp.zeros((SC_LANES,), jnp.int32)

        @pl.loop(0, per_worker)
        def _(t):
            pair = wid * per_worker + t
            b = pair // NH
            h = pair % NH
            pltpu.sync_copy(c_hbm.at[0, b, h], c_v)
            pltpu.sync_copy(hs_hbm.at[b, pl.ds(OFF_Q + h * DQK, DQK)], q_v)
            pltpu.sync_copy(hs_hbm.at[b, pl.ds(OFF_V + h * DHV, DHV)], v_v)
            pltpu.sync_copy(kw_hbm.at[b, pl.ds(h * DQK, DQK)], kw_v)
            pltpu.sync_copy(fb_hbm.at[b, pl.ds(h * DQK, DQK)], f_v)
            q = [q_v[pl.ds(SC_LANES * j, SC_LANES)] for j in range(nvec)]
            kw = [kw_v[pl.ds(SC_LANES * j, SC_LANES)] for j in range(nvec)]
            f_vec = f_v[pl.ds(0, SC_LANES)]

            @pl.loop(0, DHV // SC_LANES)
            def _(g):
                cq_vec = jnp.zeros((SC_LANES,), F32)
                for i in range(SC_LANES):
                    r = g * SC_LANES + i
                    v_r = plsc.load_gather(v_v, [zero_i + r])
                    acc = jnp.zeros((SC_LANES,), F32)
                    for j in range(nvec):
                        c = c_v[r, pl.ds(SC_LANES * j, SC_LANES)]
                        acc = acc + c * q[j]
                        c_v[r, pl.ds(SC_LANES * j, SC_LANES)] = f_vec * c + v_r * kw[j]
                    cq_vec = jnp.where(lane == i, jnp.sum(acc), cq_vec)
                cq_v[pl.ds(g * SC_LANES, SC_LANES)] = cq_vec

            pltpu.sync_copy(c_v, co_hbm.at[0, b, h])
            pltpu.sync_copy(cq_v, cq_hbm.at[b, pl.ds(h * DHV, DHV)])

    return pl.kernel(
        body,
        out_type=(jax.ShapeDtypeStruct(c0.shape, F32),
                  jax.ShapeDtypeStruct((nseq, D_M), F32)),
        mesh=mesh,
        scratch_types=[pltpu.VMEM((DHV, DQK), F32),
                       pltpu.VMEM((DQK,), F32),
                       pltpu.VMEM((DQK,), F32),
                       pltpu.VMEM((DHV,), F32),
                       pltpu.VMEM((DQK,), F32),
                       pltpu.VMEM((DHV,), F32)],
        compiler_params=pltpu.CompilerParams(use_tc_tiling_on_sc=True, needs_layout_passes=False),
        name="sample_state_sc",
    )(c0, head_s, kw_s, fb_s)


def _sample_tail_kernel(cq_ref, hs_ref, g_ref, gb_ref, n_ref, m_ref,
                        x_ref, r_ref, conv_ref, wpm_ref, wpc_ref, wout_ref, hnw_ref, cw_ref, fnw_ref,
                        y_ref, convo_ref):
    def piece(off, width):
        return r_ref[:, off:off + width]

    m_t, w_end, f_end, floor = _sample_gate_scalars(g_ref, gb_ref, m_ref)
    for h in range(NH):
        q_h = hs_ref[:, OFF_Q + h * DQK:OFF_Q + (h + 1) * DQK]
        k_h = hs_ref[:, OFF_K + h * DQK:OFF_K + (h + 1) * DQK] * K_SCALE
        v_h = hs_ref[:, OFF_V + h * DHV:OFF_V + (h + 1) * DHV]
        n_h = n_ref[0, :, h, :]
        w_h = w_end[:, h:h + 1]
        f_h = f_end[:, h:h + 1]
        s = jnp.sum(q_h * k_h, axis=-1, keepdims=True) * w_h
        den = f_h * jnp.sum(n_h * q_h, axis=-1, keepdims=True) + s
        num = f_h * cq_ref[:, h * DHV:(h + 1) * DHV] + s * v_h
        h_t = num / jnp.maximum(jnp.abs(den), floor[:, h:h + 1])
        hn = h_t * lax.rsqrt(jnp.mean(h_t * h_t, axis=-1, keepdims=True) + EPS)
        hn = hn * hnw_ref[:, h * DHV:(h + 1) * DHV]
        o_h = piece(R_O + h * DHV, DHV)
        zm_h = piece(R_ZM + h * DHV, DHV)
        hg = (hn * _sigmoid(o_h) * _silu(zm_h)).astype(BF16)
        part = _dot(hg, wpm_ref[h * DHV:(h + 1) * DHV, :])
        y_m = part if h == 0 else y_m + part

    u = piece(R_CG, D_C) * piece(R_XC, D_C)
    conv_old = conv_ref[0, :, 1, :]
    cv = cw_ref[0:1, :] * conv_ref[0, :, 0, :]
    cv = cv + cw_ref[1:2, :] * conv_old
    cv = cv + cw_ref[2:3, :] * u
    convo_ref[0, :, 0, :] = conv_old
    convo_ref[0, :, 1, :] = u
    yc_in = (_silu(piece(R_ZC, D_C)) * piece(R_BG, D_C) * cv).astype(BF16)
    y_c = _dot(yc_in, wpc_ref[...])
    mix = (_sigmoid(piece(R_GM, D_MODEL)) * y_m + _sigmoid(piece(R_GC, D_MODEL)) * y_c).astype(BF16)
    out = x_ref[:, 0, :] + _dot(mix, wout_ref[...])
    y_ref[:, 0, :] = _rmsnorm(out, fnw_ref[...])


def _sample_tail_call(cq, head_s, g, gb, n0, m0, x, rest, conv0, wpm, wpc, wout, hnw, cw, fnw):
    nseq = x.shape[0]
    args = (cq, head_s, g, gb, n0, m0, x, rest, conv0, wpm, wpc, wout, hnw, cw, fnw)
    full = lambda a: pl.BlockSpec(a.shape, lambda i: (0,) * a.ndim)
    return pl.pallas_call(
        _sample_tail_kernel,
        grid=(1,),
        in_specs=[full(a) for a in args],
        out_specs=(pl.BlockSpec((nseq, 1, D_MODEL), lambda i: (0, 0, 0)),
                   pl.BlockSpec((1, nseq, CONV_W - 1, D_C), lambda i: (0, 0, 0, 0))),
        out_shape=(jax.ShapeDtypeStruct((nseq, 1, D_MODEL), F32),
                   jax.ShapeDtypeStruct((1, nseq, CONV_W - 1, D_C), F32)),
        compiler_params=pltpu.CompilerParams(dimension_semantics=("arbitrary",),
                                             vmem_limit_bytes=VMEM_LIMIT_SAMPLE),
        name="sample_tail",
    )(*args)


def kernel(x_prompt, x_sample, state_mlstm_C, state_mlstm_n, state_mlstm_m, state_conv, norm_w, w_in, b_i, b_f,
           head_norm_w, conv_w, w_proj_m, w_proj_c, w_out, final_norm_w):
    depth = norm_w.shape[0]
    assert depth == 1, "single-layer trunk"
    bsz = x_prompt.shape[0]
    nseq = x_sample.shape[0]

    assert w_in.shape == (1, D_MODEL, N_HEAD + N_GATE_COLS + N_REST)
    gb = jnp.pad(jnp.concatenate([b_i[0], b_f[0]]), (0, LANES - N_GATE_COLS)).reshape(1, LANES)
    nw = norm_w[0].reshape(1, D_MODEL)
    hnw = head_norm_w[0].reshape(1, D_M)
    cw = conv_w[0]
    fnw = final_norm_w.reshape(1, D_MODEL)
    (wr, wh, wg, wpm, wpc, wout, head_s, g_s, rest_s) = _weight_prep_call(
        jnp.swapaxes(w_in, 1, 2), w_proj_m, w_proj_c, w_out, x_sample, nw)

    kw_s, fb_s, n_s, m_s = _sample_gates_call(head_s, g_s, gb, state_mlstm_n, state_mlstm_m)
    y_p, c_p, n_p, m_p, conv_p = _prompt_call(x_prompt, wh, wg, wr, wpm, wpc, wout, nw, gb, hnw, cw, fnw)
    m_p = m_p[:, :NH, 0].reshape(1, bsz, NH)
    c_s, cq = _sample_state_sc_call(state_mlstm_C, head_s, kw_s, fb_s)

    y_s, conv_s = _sample_tail_call(cq, head_s, g_s, gb, state_mlstm_n, state_mlstm_m,
                                    x_sample, rest_s, state_conv, wpm, wpc, wout, hnw, cw, fnw)

    return (y_p, y_s, c_p, n_p, m_p, conv_p, c_s, n_s, m_s, conv_s)
```

```python
import jax
import jax.numpy as jnp
from jax import lax
from jax.experimental import pallas as pl
from jax.experimental.pallas import tpu as pltpu
from jax.experimental.pallas import tpu_sc as plsc

F32 = jnp.float32
BF16 = jnp.bfloat16

D_MODEL = 1024
NH = 4
DHV = 256
DQK = 128
D_QK = NH * DQK
D_M = NH * DHV
D_C = D_MODEL
CONV_W = 3
EPS = 1e-6
NEG_BIG = -1e30
K_SCALE = DQK ** -0.5

LANES = 128
SUBLANES = 8

OFF_Q = 0
OFF_K = OFF_Q + D_QK
OFF_V = OFF_K + D_QK
N_HEAD = OFF_V + D_M
N_GATE_COLS = 2 * NH
R_O = 0
R_ZM = R_O + D_M
R_BG = R_ZM + D_M
R_CG = R_BG + D_C
R_XC = R_CG + D_C
R_ZC = R_XC + D_C
R_GM = R_ZC + D_C
R_GC = R_GM + D_MODEL
N_REST = R_GC + D_MODEL

PROMPT_BLOCK = 512
MLSTM_CHUNK = 256
PROMPT_ORDER = "UUUUFUFUFUFU"
VMEM_LIMIT_PROMPT = 60 * 1024 * 1024
VMEM_LIMIT_SAMPLE = 60 * 1024 * 1024


def _sigmoid(x):
    return 0.5 * jnp.tanh(0.5 * x) + 0.5


def _silu(x):
    return x * _sigmoid(x)


def _log_sigmoid(x):
    return jnp.minimum(x, 0.0) - jnp.log1p(jnp.exp(-jnp.abs(x)))


def _rmsnorm(x, w):
    return x * lax.rsqrt(jnp.mean(x * x, axis=-1, keepdims=True) + EPS) * w


def _dot(a, b):
    return jnp.dot(a, b, preferred_element_type=F32)


def _dot_nt(a, b):
    return lax.dot_general(a, b, (((1,), (1,)), ((), ())), preferred_element_type=F32)


def _dot_tn(a, b):
    return lax.dot_general(a, b, (((0,), (0,)), ((), ())), preferred_element_type=F32)


PREP_COLS = 1024
PREP_CHUNK = 256


PREP_STEPS = N_REST // PREP_COLS
PREP_HEAD_COLS = N_HEAD // PREP_STEPS
PREP_SQ_ROWS = D_MODEL // PREP_STEPS
N_SQUARE = 3


def _weight_prep_kernel(a_ref, c_ref, g_ref, pm_ref, pc_ref, po_ref, xs_ref, nw_ref,
                        wr_ref, wh_ref, wg_ref, wsq_ref,
                        hs_ref, gs_ref, rs_ref, xn_s):
    j = pl.program_id(0)

    @pl.when(j == 0)
    def _():
        xn_s[...] = _rmsnorm(xs_ref[:, 0, :], nw_ref[...]).astype(BF16)

    xn = xn_s[...]
    for r0 in range(0, PREP_COLS, PREP_CHUNK):
        w_t = a_ref[0, r0:r0 + PREP_CHUNK, :].T.astype(BF16)
        wr_ref[:, r0:r0 + PREP_CHUNK] = w_t
        rs_ref[:, r0:r0 + PREP_CHUNK] = _dot(xn, w_t)
    w_t = c_ref[0].T.astype(BF16)
    wh_ref[...] = w_t
    hs_ref[...] = _dot(xn, w_t)

    @pl.when(j == 0)
    def _():
        lane = lax.broadcasted_iota(jnp.int32, (D_MODEL, LANES), 1)
        w_g = jnp.where(lane < N_GATE_COLS, g_ref[0].T, 0.0).astype(BF16)
        wg_ref[...] = w_g
        gs_ref[...] = _dot(xn, w_g)

    wsq_ref[:, 0:D_MODEL] = pm_ref[0].astype(BF16)
    wsq_ref[:, D_MODEL:2 * D_MODEL] = pc_ref[0].astype(BF16)
    wsq_ref[:, 2 * D_MODEL:3 * D_MODEL] = po_ref[0].astype(BF16)


def _weight_prep_call(w_in_t, w_proj_m, w_proj_c, w_out, xs, nw):
    nseq = xs.shape[0]
    rest_row0 = N_HEAD + N_GATE_COLS
    const = lambda shape: pl.BlockSpec(shape, lambda j: (0,) * len(shape))
    sq_in = pl.BlockSpec((1, PREP_SQ_ROWS, D_MODEL), lambda j: (0, j, 0))
    sq_out = pl.BlockSpec((PREP_SQ_ROWS, N_SQUARE * D_MODEL), lambda j: (j, 0))
    sq_shape = jax.ShapeDtypeStruct((D_MODEL, N_SQUARE * D_MODEL), BF16)
    return pl.pallas_call(
        _weight_prep_kernel,
        grid=(PREP_STEPS,),
        in_specs=[
            pl.BlockSpec((pl.Element(1), pl.Element(PREP_COLS), pl.Element(D_MODEL)),
                         lambda j: (0, pl.multiple_of(rest_row0 + j * PREP_COLS, SUBLANES), 0)),
            pl.BlockSpec((1, PREP_HEAD_COLS, D_MODEL), lambda j: (0, j, 0)),
            pl.BlockSpec((1, LANES, D_MODEL), lambda j: (0, N_HEAD // LANES, 0)),
            sq_in, sq_in, sq_in,
            const((nseq, 1, D_MODEL)),
            const((1, D_MODEL)),
        ],
        out_specs=(
            pl.BlockSpec((D_MODEL, PREP_COLS), lambda j: (0, j)),
            pl.BlockSpec((D_MODEL, PREP_HEAD_COLS), lambda j: (0, j)),
            const((D_MODEL, LANES)),
            sq_out,
            pl.BlockSpec((nseq, PREP_HEAD_COLS), lambda j: (0, j)),
            const((nseq, LANES)),
            pl.BlockSpec((nseq, PREP_COLS), lambda j: (0, j)),
        ),
        out_shape=(
            jax.ShapeDtypeStruct((D_MODEL, N_REST), BF16),
            jax.ShapeDtypeStruct((D_MODEL, N_HEAD), BF16),
            jax.ShapeDtypeStruct((D_MODEL, LANES), BF16),
            sq_shape,
            jax.ShapeDtypeStruct((nseq, N_HEAD), F32),
            jax.ShapeDtypeStruct((nseq, LANES), F32),
            jax.ShapeDtypeStruct((nseq, N_REST), F32),
        ),
        scratch_shapes=[pltpu.VMEM((nseq, D_MODEL), BF16)],
        compiler_params=pltpu.CompilerParams(dimension_semantics=("arbitrary",),
                                             vmem_limit_bytes=VMEM_LIMIT_SAMPLE),
        name="weight_prep",
    )(w_in_t, w_in_t, w_in_t, w_proj_m, w_proj_c, w_out, xs, nw)


def _prompt_kernel(x_ref, wh_ref, wg_ref, wr_ref, wpm_ref, wpc_ref, wout_ref,
                   nw_ref, gb_ref, hnw_ref, cw_ref, fnw_ref,
                   y_ref, c_ref, n_ref, m_ref, conv_ref,
                   hg_s, ubuf_s, ct_s):
    tl = x_ref.shape[1]
    l = pl.program_id(1)

    @pl.when(l == 0)
    def _():
        ct_s[...] = jnp.zeros_like(ct_s)
        n_ref[...] = jnp.zeros_like(n_ref)
        m_ref[...] = jnp.zeros_like(m_ref)
        ubuf_s[0:SUBLANES, :] = jnp.zeros((SUBLANES, D_C), F32)

    x = x_ref[0]
    xn = _rmsnorm(x, nw_ref[...]).astype(BF16)

    def rest(off, width):
        return _dot(xn, wr_ref[:, off:off + width])

    g = _dot(xn, wg_ref[...]) + gb_ref[...]
    gc = jnp.where(lax.broadcasted_iota(jnp.int32, (tl, LANES), 1) < NH, g, _log_sigmoid(g))
    qkv = _dot(xn, wh_ref[...])

    ch = MLSTM_CHUNK
    lane = lax.broadcasted_iota(jnp.int32, (ch, LANES), 1)
    row = lax.broadcasted_iota(jnp.int32, (ch, ch), 0)
    col = lax.broadcasted_iota(jnp.int32, (ch, ch), 1)
    causal = row >= col
    row_g = lax.broadcasted_iota(jnp.int32, (ch, LANES), 0)
    chunk_gates = []
    for r0 in range(0, tl, ch):
        gc_c = gc[r0:r0 + ch, :]
        bc = gc_c
        shift = 1
        while shift < ch:
            bc = bc + jnp.where(row_g >= shift, pltpu.roll(bc, shift, axis=0), 0.0)
            shift *= 2
        gt = jnp.where(lane < NH, gc_c, bc).T
        chunk_gates.append((gc_c, bc, gt))

    head_gates = {}

    def mlstm_unit(ci, h):
        r0 = ci * ch
        gc_c, bc, gt = chunk_gates[ci]
        if ci == 0:
            head_gates[h] = (_sigmoid(rest(R_O + h * DHV, DHV)), _silu(rest(R_ZM + h * DHV, DHV)))
        sig_o = head_gates[h][0][r0:r0 + ch, :]
        silu_z = head_gates[h][1][r0:r0 + ch, :]
        q_f = qkv[r0:r0 + ch, OFF_Q + h * DQK:OFF_Q + (h + 1) * DQK]
        k_f = qkv[r0:r0 + ch, OFF_K + h * DQK:OFF_K + (h + 1) * DQK] * K_SCALE
        v_f = qkv[r0:r0 + ch, OFF_V + h * DHV:OFF_V + (h + 1) * DHV]
        q_b = q_f.astype(BF16)
        k_b = k_f.astype(BF16)
        v_b = v_f.astype(BF16)
        ct_old = ct_s[h]
        n_old = n_ref[0, 0, h:h + 1, :]
        m_old = m_ref[0, h:h + 1, 0:1]

        b_c = bc[:, NH + h:NH + h + 1]
        ig_c = gc_c[:, h:h + 1]
        ig_r = gt[h:h + 1, :]
        b_r = gt[NH + h:NH + h + 1, :]
        log_d = jnp.where(causal, b_c - b_r + ig_r, NEG_BIG)
        inter = b_c + m_old
        m_t = jnp.maximum(inter, jnp.max(log_d, axis=-1, keepdims=True))
        d_m = jnp.exp(log_d - m_t)
        w_int = jnp.exp(inter - m_t)
        s = _dot_nt(q_b, k_b) * d_m
        num = w_int * _dot(q_b, ct_old.astype(BF16)) + _dot(s.astype(BF16), v_b)
        den = w_int * jnp.sum(q_f * n_old, axis=-1, keepdims=True) + jnp.sum(s, axis=-1, keepdims=True)
        h_t = num / jnp.maximum(jnp.abs(den), jnp.exp(-m_t))

        b_end = b_c[ch - 1:ch, :]
        inter_end = b_end + m_old
        m_new = jnp.maximum(inter_end, jnp.max(b_end - b_r + ig_r, axis=-1, keepdims=True))
        w_end = jnp.exp(b_end - b_c + ig_c - m_new)
        f_end = jnp.exp(inter_end - m_new)
        ct_s[h] = f_end * ct_old + _dot_tn(k_b, (w_end * v_f).astype(BF16))
        n_ref[0, 0, h:h + 1, :] = f_end * n_old + jnp.sum(w_end * k_f, axis=0, keepdims=True)
        m_ref[0, h:h + 1, :] = jnp.broadcast_to(m_new, (1, LANES))

        hn = h_t * lax.rsqrt(jnp.mean(h_t * h_t, axis=-1, keepdims=True) + EPS)
        hn = hn * hnw_ref[:, h * DHV:(h + 1) * DHV]
        hg_s[r0:r0 + ch, h * DHV:(h + 1) * DHV] = (hn * sig_o * silu_z).astype(BF16)

    val = {}

    def conv_input():
        u = rest(R_CG, D_C) * rest(R_XC, D_C)
        ubuf_s[SUBLANES:SUBLANES + tl, :] = u
        cv = cw_ref[0:1, :] * ubuf_s[SUBLANES - 2:SUBLANES - 2 + tl, :]
        cv = cv + cw_ref[1:2, :] * ubuf_s[SUBLANES - 1:SUBLANES - 1 + tl, :]
        val["cv"] = cv + cw_ref[2:3, :] * u

    def conv_gate():
        val["yc_in"] = (_silu(rest(R_ZC, D_C)) * rest(R_BG, D_C) * val["cv"]).astype(BF16)

    def conv_proj():
        val["gy_c"] = _sigmoid(rest(R_GC, D_MODEL)) * _dot(val["yc_in"], wpc_ref[...])

    def merge_gate():
        val["sig_gm"] = _sigmoid(rest(R_GM, D_MODEL))

    fillers = iter([conv_input, conv_gate, conv_proj, merge_gate])
    units = iter([(ci, h) for ci in range(tl // ch) for h in range(NH)])
    for step in PROMPT_ORDER:
        if step == "U":
            mlstm_unit(*next(units))
        else:
            next(fillers)()

    ubuf_s[0:SUBLANES, :] = ubuf_s[tl:tl + SUBLANES, :]

    conv_ref[0, 0] = ubuf_s[SUBLANES - (CONV_W - 1):SUBLANES, :]
    for h in range(NH):
        c_ref[0, 0, h] = ct_s[h].T

    y_m = _dot(hg_s[...], wpm_ref[...])
    mix = (val["sig_gm"] * y_m + val["gy_c"]).astype(BF16)
    out = x + _dot(mix, wout_ref[...])
    y_ref[0] = _rmsnorm(out, fnw_ref[...])


def _resident(shape, block_index=None):
    index = (0,) * len(shape) if block_index is None else block_index
    return pl.BlockSpec(shape, lambda *_: index, pipeline_mode=pl.Buffered(1))


def _prompt_call(x, wh, wg, wr, wsq, nw, gb, hnw, cw, fnw):
    bsz, seq, _ = x.shape
    tl = PROMPT_BLOCK
    grid = (bsz, seq // tl)
    out_shape = (
        jax.ShapeDtypeStruct((bsz, seq, D_MODEL), F32),
        jax.ShapeDtypeStruct((1, bsz, NH, DHV, DQK), F32),
        jax.ShapeDtypeStruct((1, bsz, NH, DQK), F32),
        jax.ShapeDtypeStruct((bsz, SUBLANES, LANES), F32),
        jax.ShapeDtypeStruct((1, bsz, CONV_W - 1, D_C), F32),
    )
    in_specs = [
        pl.BlockSpec((1, tl, D_MODEL), lambda b, l: (b, l, 0)),
        _resident((D_MODEL, N_HEAD)),
        _resident((D_MODEL, LANES)),
        _resident((D_MODEL, N_REST)),
        _resident((D_M, D_MODEL), (0, 0)),
        _resident((D_C, D_MODEL), (0, 1)),
        _resident((D_MODEL, D_MODEL), (0, 2)),
        _resident((1, D_MODEL)),
        _resident((1, LANES)),
        _resident((1, D_M)),
        _resident((CONV_W, D_C)),
        _resident((1, D_MODEL)),
    ]
    out_specs = (
        pl.BlockSpec((1, tl, D_MODEL), lambda b, l: (b, l, 0)),
        pl.BlockSpec((1, 1, NH, DHV, DQK), lambda b, l: (0, b, 0, 0, 0)),
        pl.BlockSpec((1, 1, NH, DQK), lambda b, l: (0, b, 0, 0)),
        pl.BlockSpec((1, SUBLANES, LANES), lambda b, l: (b, 0, 0)),
        pl.BlockSpec((1, 1, CONV_W - 1, D_C), lambda b, l: (0, b, 0, 0)),
    )
    return pl.pallas_call(
        _prompt_kernel,
        grid=grid,
        in_specs=in_specs,
        out_specs=out_specs,
        out_shape=out_shape,
        scratch_shapes=[
            pltpu.VMEM((tl, D_M), BF16),
            pltpu.VMEM((tl + 2 * SUBLANES, D_C), F32),
            pltpu.VMEM((NH, DQK, DHV), F32),
        ],
        compiler_params=pltpu.CompilerParams(
            dimension_semantics=("arbitrary", "arbitrary"),
            vmem_limit_bytes=VMEM_LIMIT_PROMPT),
        name="prompt_layer",
    )(x, wh, wg, wr, wsq, wsq, wsq, nw, gb, hnw, cw, fnw)


SC_LANES = 16
SC_CORES = 2
SC_SUBCORES = 16


def _sample_gate_scalars(g_ref, gb_ref, m_ref):
    g = g_ref[...] + gb_ref[...]
    ig = g[:, 0:NH]
    lf = _log_sigmoid(g[:, NH:2 * NH])
    inter = lf + m_ref[0]
    m_t = jnp.maximum(inter, ig)
    return m_t, jnp.exp(ig - m_t), jnp.exp(inter - m_t), jnp.exp(-m_t)


def _sample_gates_kernel(hs_ref, g_ref, gb_ref, n_ref, m_ref, kw_ref, fb_ref, no_ref, mo_ref):
    nseq = hs_ref.shape[0]
    m_t, w_end, f_end, _ = _sample_gate_scalars(g_ref, gb_ref, m_ref)
    mo_ref[0] = m_t
    for h in range(NH):
        k_h = hs_ref[:, OFF_K + h * DQK:OFF_K + (h + 1) * DQK] * K_SCALE
        n_h = n_ref[0, :, h, :]
        w_h = w_end[:, h:h + 1]
        f_h = f_end[:, h:h + 1]
        kw_ref[:, h * DQK:(h + 1) * DQK] = w_h * k_h
        fb_ref[:, h * DQK:(h + 1) * DQK] = jnp.broadcast_to(f_h, (nseq, DQK))
        no_ref[0, :, h, :] = f_h * n_h + w_h * k_h


def _sample_gates_call(head_s, g, gb, n0, m0):
    nseq = head_s.shape[0]
    args = (head_s, g, gb, n0, m0)
    full = lambda a: pl.BlockSpec(a.shape, lambda i: (0,) * a.ndim)
    row = jax.ShapeDtypeStruct((nseq, D_QK), F32)
    return pl.pallas_call(
        _sample_gates_kernel,
        grid=(1,),
        in_specs=[full(a) for a in args],
        out_specs=(full(row), full(row), full(n0), full(m0)),
        out_shape=(row, row, jax.ShapeDtypeStruct(n0.shape, F32), jax.ShapeDtypeStruct(m0.shape, F32)),
        name="sample_gates",
    )(*args)


def _sample_state_sc_call(c0, head_s, kw_s, fb_s):
    nseq = c0.shape[1]
    nworkers = SC_CORES * SC_SUBCORES
    per_worker = nseq * NH // nworkers
    nvec = DQK // SC_LANES
    mesh = plsc.VectorSubcoreMesh(core_axis_name="c", subcore_axis_name="s")

    def body(c_hbm, hs_hbm, kw_hbm, fb_hbm, co_hbm, cq_hbm, c_v, q_v, kw_v, v_v, f_v, cq_v):
        wid = lax.axis_index("c") * SC_SUBCORES + lax.axis_index("s")
        lane = lax.iota(jnp.int32, SC_LANES)
        zero_i = jnp.zeros((SC_LANES,), jnp.int32)

        @pl.loop(0, per_worker)
        def _(t):
            pair = wid * per_worker + t
            b = pair // NH
            h = pair % NH
            pltpu.sync_copy(c_hbm.at[0, b, h], c_v)
            pltpu.sync_copy(hs_hbm.at[b, pl.ds(OFF_Q + h * DQK, DQK)], q_v)
            pltpu.sync_copy(hs_hbm.at[b, pl.ds(OFF_V + h * DHV, DHV)], v_v)
            pltpu.sync_copy(kw_hbm.at[b, pl.ds(h * DQK, DQK)], kw_v)
            pltpu.sync_copy(fb_hbm.at[b, pl.ds(h * DQK, DQK)], f_v)
            q = [q_v[pl.ds(SC_LANES * j, SC_LANES)] for j in range(nvec)]
            kw = [kw_v[pl.ds(SC_LANES * j, SC_LANES)] for j in range(nvec)]
            f_vec = f_v[pl.ds(0, SC_LANES)]

            @pl.loop(0, DHV // SC_LANES)
            def _(g):
                cq_vec = jnp.zeros((SC_LANES,), F32)
                for i in range(SC_LANES):
                    r = g * SC_LANES + i
                    v_r = plsc.load_gather(v_v, [zero_i + r])
                    acc = jnp.zeros((SC_LANES,), F32)
                    for j in range(nvec):
                        c = c_v[r, pl.ds(SC_LANES * j, SC_LANES)]
                        acc = acc + c * q[j]
                        c_v[r, pl.ds(SC_LANES * j, SC_LANES)] = f_vec * c + v_r * kw[j]
                    cq_vec = jnp.where(lane == i, jnp.sum(acc), cq_vec)
                cq_v[pl.ds(g * SC_LANES, SC_LANES)] = cq_vec

            pltpu.sync_copy(c_v, co_hbm.at[0, b, h])
            pltpu.sync_copy(cq_v, cq_hbm.at[b, pl.ds(h * DHV, DHV)])

    return pl.kernel(
        body,
        out_type=(jax.ShapeDtypeStruct(c0.shape, F32),
                  jax.ShapeDtypeStruct((nseq, D_M), F32)),
        mesh=mesh,
        scratch_types=[pltpu.VMEM((DHV, DQK), F32),
                       pltpu.VMEM((DQK,), F32),
                       pltpu.VMEM((DQK,), F32),
                       pltpu.VMEM((DHV,), F32),
                       pltpu.VMEM((DQK,), F32),
                       pltpu.VMEM((DHV,), F32)],
        compiler_params=pltpu.CompilerParams(use_tc_tiling_on_sc=True, needs_layout_passes=False),
        name="sample_state_sc",
    )(c0, head_s, kw_s, fb_s)


def _sample_tail_kernel(cq_ref, hs_ref, g_ref, gb_ref, n_ref, m_ref,
                        x_ref, r_ref, conv_ref, wpm_ref, wpc_ref, wout_ref, hnw_ref, cw_ref, fnw_ref,
                        y_ref, convo_ref):
    def piece(off, width):
        return r_ref[:, off:off + width]

    m_t, w_end, f_end, floor = _sample_gate_scalars(g_ref, gb_ref, m_ref)
    for h in range(NH):
        q_h = hs_ref[:, OFF_Q + h * DQK:OFF_Q + (h + 1) * DQK]
        k_h = hs_ref[:, OFF_K + h * DQK:OFF_K + (h + 1) * DQK] * K_SCALE
        v_h = hs_ref[:, OFF_V + h * DHV:OFF_V + (h + 1) * DHV]
        n_h = n_ref[0, :, h, :]
        w_h = w_end[:, h:h + 1]
        f_h = f_end[:, h:h + 1]
        s = jnp.sum(q_h * k_h, axis=-1, keepdims=True) * w_h
        den = f_h * jnp.sum(n_h * q_h, axis=-1, keepdims=True) + s
        num = f_h * cq_ref[:, h * DHV:(h + 1) * DHV] + s * v_h
        h_t = num / jnp.maximum(jnp.abs(den), floor[:, h:h + 1])
        hn = h_t * lax.rsqrt(jnp.mean(h_t * h_t, axis=-1, keepdims=True) + EPS)
        hn = hn * hnw_ref[:, h * DHV:(h + 1) * DHV]
        o_h = piece(R_O + h * DHV, DHV)
        zm_h = piece(R_ZM + h * DHV, DHV)
        hg = (hn * _sigmoid(o_h) * _silu(zm_h)).astype(BF16)
        part = _dot(hg, wpm_ref[h * DHV:(h + 1) * DHV, :])
        y_m = part if h == 0 else y_m + part

    u = piece(R_CG, D_C) * piece(R_XC, D_C)
    conv_old = conv_ref[0, :, 1, :]
    cv = cw_ref[0:1, :] * conv_ref[0, :, 0, :]
    cv = cv + cw_ref[1:2, :] * conv_old
    cv = cv + cw_ref[2:3, :] * u
    convo_ref[0, :, 0, :] = conv_old
    convo_ref[0, :, 1, :] = u
    yc_in = (_silu(piece(R_ZC, D_C)) * piece(R_BG, D_C) * cv).astype(BF16)
    y_c = _dot(yc_in, wpc_ref[...])
    mix = (_sigmoid(piece(R_GM, D_MODEL)) * y_m + _sigmoid(piece(R_GC, D_MODEL)) * y_c).astype(BF16)
    out = x_ref[:, 0, :] + _dot(mix, wout_ref[...])
    y_ref[:, 0, :] = _rmsnorm(out, fnw_ref[...])


def _sample_tail_call(cq, head_s, g, gb, n0, m0, x, rest, conv0, wsq, hnw, cw, fnw):
    nseq = x.shape[0]
    args = (cq, head_s, g, gb, n0, m0, x, rest, conv0, wsq, wsq, wsq, hnw, cw, fnw)
    full = lambda a: pl.BlockSpec(a.shape, lambda i: (0,) * a.ndim)
    square = lambda k: pl.BlockSpec((D_MODEL, D_MODEL), lambda i: (0, k))
    specs = [full(a) for a in args]
    specs[9:12] = [square(0), square(1), square(2)]
    return pl.pallas_call(
        _sample_tail_kernel,
        grid=(1,),
        in_specs=specs,
        out_specs=(pl.BlockSpec((nseq, 1, D_MODEL), lambda i: (0, 0, 0)),
                   pl.BlockSpec((1, nseq, CONV_W - 1, D_C), lambda i: (0, 0, 0, 0))),
        out_shape=(jax.ShapeDtypeStruct((nseq, 1, D_MODEL), F32),
                   jax.ShapeDtypeStruct((1, nseq, CONV_W - 1, D_C), F32)),
        compiler_params=pltpu.CompilerParams(dimension_semantics=("arbitrary",),
                                             vmem_limit_bytes=VMEM_LIMIT_SAMPLE),
        name="sample_tail",
    )(*args)


def kernel(x_prompt, x_sample, state_mlstm_C, state_mlstm_n, state_mlstm_m, state_conv, norm_w, w_in, b_i, b_f,
           head_norm_w, conv_w, w_proj_m, w_proj_c, w_out, final_norm_w):
    depth = norm_w.shape[0]
    assert depth == 1, "single-layer trunk"
    bsz = x_prompt.shape[0]
    nseq = x_sample.shape[0]

    assert w_in.shape == (1, D_MODEL, N_HEAD + N_GATE_COLS + N_REST)
    gb = jnp.pad(jnp.concatenate([b_i[0], b_f[0]]), (0, LANES - N_GATE_COLS)).reshape(1, LANES)
    nw = norm_w[0].reshape(1, D_MODEL)
    hnw = head_norm_w[0].reshape(1, D_M)
    cw = conv_w[0]
    fnw = final_norm_w.reshape(1, D_MODEL)
    (wr, wh, wg, wsq, head_s, g_s, rest_s) = _weight_prep_call(
        jnp.swapaxes(w_in, 1, 2), w_proj_m, w_proj_c, w_out, x_sample, nw)

    kw_s, fb_s, n_s, m_s = _sample_gates_call(head_s, g_s, gb, state_mlstm_n, state_mlstm_m)
    y_p, c_p, n_p, m_p, conv_p = _prompt_call(x_prompt, wh, wg, wr, wsq, nw, gb, hnw, cw, fnw)
    m_p = m_p[:, :NH, 0].reshape(1, bsz, NH)
    c_s, cq = _sample_state_sc_call(state_mlstm_C, head_s, kw_s, fb_s)

    y_s, conv_s = _sample_tail_call(cq, head_s, g_s, gb, state_mlstm_n, state_mlstm_m,
                                    x_sample, rest_s, state_conv, wsq, hnw, cw, fnw)

    return (y_p, y_s, c_p, n_p, m_p, conv_p, c_s, n_s, m_s, conv_s)
```

```python
import jax
import jax.numpy as jnp
from jax import lax
from jax.experimental import pallas as pl
from jax.experimental.pallas import tpu as pltpu
from jax.experimental.pallas import tpu_sc as plsc

F32 = jnp.float32
BF16 = jnp.bfloat16

D_MODEL = 1024
NH = 4
DHV = 256
DQK = 128
D_QK = NH * DQK
D_M = NH * DHV
D_C = D_MODEL
CONV_W = 3
EPS = 1e-6
NEG_BIG = -1e30
K_SCALE = DQK ** -0.5

LANES = 128
SUBLANES = 8

OFF_Q = 0
OFF_K = OFF_Q + D_QK
OFF_V = OFF_K + D_QK
N_HEAD = OFF_V + D_M
N_GATE_COLS = 2 * NH
R_O = 0
R_ZM = R_O + D_M
R_BG = R_ZM + D_M
R_CG = R_BG + D_C
R_XC = R_CG + D_C
R_ZC = R_XC + D_C
R_GM = R_ZC + D_C
R_GC = R_GM + D_MODEL
N_REST = R_GC + D_MODEL

PROMPT_BLOCK = 512
MLSTM_CHUNK = 256
PROMPT_ORDER = "UUUUFUFUFUFU"
VMEM_LIMIT_PROMPT = 60 * 1024 * 1024
VMEM_LIMIT_SAMPLE = 60 * 1024 * 1024


def _sigmoid(x):
    return 0.5 * jnp.tanh(0.5 * x) + 0.5


def _silu(x):
    return x * _sigmoid(x)


def _log_sigmoid(x):
    return jnp.minimum(x, 0.0) - jnp.log1p(jnp.exp(-jnp.abs(x)))


def _rmsnorm(x, w):
    return x * lax.rsqrt(jnp.mean(x * x, axis=-1, keepdims=True) + EPS) * w


def _conv_tap(cw_ref, j):
    return cw_ref[:, j * D_C:(j + 1) * D_C]


def _dot(a, b):
    return jnp.dot(a, b, preferred_element_type=F32)


def _dot_nt(a, b):
    return lax.dot_general(a, b, (((1,), (1,)), ((), ())), preferred_element_type=F32)


def _dot_tn(a, b):
    return lax.dot_general(a, b, (((0,), (0,)), ((), ())), preferred_element_type=F32)


PREP_COLS = 1024
PREP_CHUNK = 256


PREP_STEPS = N_REST // PREP_COLS
PREP_HEAD_COLS = N_HEAD // PREP_STEPS
PREP_SQ_ROWS = D_MODEL // PREP_STEPS
N_SQUARE = 3


def _weight_prep_kernel(a_ref, c_ref, g_ref, pm_ref, pc_ref, po_ref, xs_ref, nw_ref,
                        wr_ref, wh_ref, wg_ref, wsq_ref,
                        hs_ref, gs_ref, rs_ref, xn_s):
    j = pl.program_id(0)

    @pl.when(j == 0)
    def _():
        xn_s[...] = _rmsnorm(xs_ref[:, 0, :], nw_ref[...]).astype(BF16)

    xn = xn_s[...]
    for r0 in range(0, PREP_COLS, PREP_CHUNK):
        w_t = a_ref[0, r0:r0 + PREP_CHUNK, :].T.astype(BF16)
        wr_ref[:, r0:r0 + PREP_CHUNK] = w_t
        rs_ref[:, r0:r0 + PREP_CHUNK] = _dot(xn, w_t)
    w_t = c_ref[0].T.astype(BF16)
    wh_ref[...] = w_t
    hs_ref[...] = _dot(xn, w_t)

    @pl.when(j == 0)
    def _():
        lane = lax.broadcasted_iota(jnp.int32, (D_MODEL, LANES), 1)
        w_g = jnp.where(lane < N_GATE_COLS, g_ref[0].T, 0.0).astype(BF16)
        wg_ref[...] = w_g
        gs_ref[...] = _dot(xn, w_g)

    wsq_ref[:, 0:D_MODEL] = pm_ref[0].astype(BF16)
    wsq_ref[:, D_MODEL:2 * D_MODEL] = pc_ref[0].astype(BF16)
    wsq_ref[:, 2 * D_MODEL:3 * D_MODEL] = po_ref[0].astype(BF16)


def _weight_prep_call(w_in_t, w_proj_m, w_proj_c, w_out, xs, nw):
    nseq = xs.shape[0]
    rest_row0 = N_HEAD + N_GATE_COLS
    const = lambda shape: pl.BlockSpec(shape, lambda j: (0,) * len(shape))
    sq_in = pl.BlockSpec((1, PREP_SQ_ROWS, D_MODEL), lambda j: (0, j, 0))
    sq_out = pl.BlockSpec((PREP_SQ_ROWS, N_SQUARE * D_MODEL), lambda j: (j, 0))
    sq_shape = jax.ShapeDtypeStruct((D_MODEL, N_SQUARE * D_MODEL), BF16)
    return pl.pallas_call(
        _weight_prep_kernel,
        grid=(PREP_STEPS,),
        in_specs=[
            pl.BlockSpec((pl.Element(1), pl.Element(PREP_COLS), pl.Element(D_MODEL)),
                         lambda j: (0, pl.multiple_of(rest_row0 + j * PREP_COLS, SUBLANES), 0)),
            pl.BlockSpec((1, PREP_HEAD_COLS, D_MODEL), lambda j: (0, j, 0)),
            pl.BlockSpec((1, LANES, D_MODEL), lambda j: (0, N_HEAD // LANES, 0)),
            sq_in, sq_in, sq_in,
            const((nseq, 1, D_MODEL)),
            const((1, D_MODEL)),
        ],
        out_specs=(
            pl.BlockSpec((D_MODEL, PREP_COLS), lambda j: (0, j)),
            pl.BlockSpec((D_MODEL, PREP_HEAD_COLS), lambda j: (0, j)),
            const((D_MODEL, LANES)),
            sq_out,
            pl.BlockSpec((nseq, PREP_HEAD_COLS), lambda j: (0, j)),
            const((nseq, LANES)),
            pl.BlockSpec((nseq, PREP_COLS), lambda j: (0, j)),
        ),
        out_shape=(
            jax.ShapeDtypeStruct((D_MODEL, N_REST), BF16),
            jax.ShapeDtypeStruct((D_MODEL, N_HEAD), BF16),
            jax.ShapeDtypeStruct((D_MODEL, LANES), BF16),
            sq_shape,
            jax.ShapeDtypeStruct((nseq, N_HEAD), F32),
            jax.ShapeDtypeStruct((nseq, LANES), F32),
            jax.ShapeDtypeStruct((nseq, N_REST), F32),
        ),
        scratch_shapes=[pltpu.VMEM((nseq, D_MODEL), BF16)],
        compiler_params=pltpu.CompilerParams(dimension_semantics=("arbitrary",),
                                             vmem_limit_bytes=VMEM_LIMIT_SAMPLE),
        name="weight_prep",
    )(w_in_t, w_in_t, w_in_t, w_proj_m, w_proj_c, w_out, xs, nw)


def _prompt_kernel(x_ref, wh_ref, wg_ref, wr_ref, wpm_ref, wpc_ref, wout_ref,
                   nw_ref, gb_ref, hnw_ref, cw_ref, fnw_ref,
                   y_ref, c_ref, n_ref, mo_ref, conv_ref,
                   hg_s, ubuf_s, ct_s, m_ref):
    tl = x_ref.shape[1]
    seq_id = pl.program_id(0)
    l = pl.program_id(1)
    seq_lane = lax.broadcasted_iota(jnp.int32, (1, mo_ref.shape[2]), 1)

    @pl.when((seq_id == 0) & (l == 0))
    def _():
        mo_ref[...] = jnp.zeros_like(mo_ref)

    @pl.when(l == 0)
    def _():
        ct_s[...] = jnp.zeros_like(ct_s)
        n_ref[...] = jnp.zeros_like(n_ref)
        m_ref[...] = jnp.zeros_like(m_ref)
        ubuf_s[0:SUBLANES, :] = jnp.zeros((SUBLANES, D_C), F32)

    x = x_ref[0]
    xn = _rmsnorm(x, nw_ref[...]).astype(BF16)

    def rest(off, width):
        return _dot(xn, wr_ref[:, off:off + width])

    g = _dot(xn, wg_ref[...]) + gb_ref[...]
    gc = jnp.where(lax.broadcasted_iota(jnp.int32, (tl, LANES), 1) < NH, g, _log_sigmoid(g))
    qkv = _dot(xn, wh_ref[...])

    ch = MLSTM_CHUNK
    lane = lax.broadcasted_iota(jnp.int32, (ch, LANES), 1)
    row = lax.broadcasted_iota(jnp.int32, (ch, ch), 0)
    col = lax.broadcasted_iota(jnp.int32, (ch, ch), 1)
    causal = row >= col
    row_g = lax.broadcasted_iota(jnp.int32, (ch, LANES), 0)
    chunk_gates = []
    for r0 in range(0, tl, ch):
        gc_c = gc[r0:r0 + ch, :]
        bc = gc_c
        shift = 1
        while shift < ch:
            bc = bc + jnp.where(row_g >= shift, pltpu.roll(bc, shift, axis=0), 0.0)
            shift *= 2
        gt = jnp.where(lane < NH, gc_c, bc).T
        chunk_gates.append((gc_c, bc, gt))

    head_gates = {}

    def mlstm_unit(ci, h):
        r0 = ci * ch
        gc_c, bc, gt = chunk_gates[ci]
        if ci == 0:
            head_gates[h] = (_sigmoid(rest(R_O + h * DHV, DHV)), _silu(rest(R_ZM + h * DHV, DHV)))
        sig_o = head_gates[h][0][r0:r0 + ch, :]
        silu_z = head_gates[h][1][r0:r0 + ch, :]
        q_f = qkv[r0:r0 + ch, OFF_Q + h * DQK:OFF_Q + (h + 1) * DQK]
        k_f = qkv[r0:r0 + ch, OFF_K + h * DQK:OFF_K + (h + 1) * DQK] * K_SCALE
        v_f = qkv[r0:r0 + ch, OFF_V + h * DHV:OFF_V + (h + 1) * DHV]
        q_b = q_f.astype(BF16)
        k_b = k_f.astype(BF16)
        v_b = v_f.astype(BF16)
        ct_old = ct_s[h]
        n_old = n_ref[0, 0, h:h + 1, :]
        m_old = m_ref[h:h + 1, 0:1]

        b_c = bc[:, NH + h:NH + h + 1]
        ig_c = gc_c[:, h:h + 1]
        ig_r = gt[h:h + 1, :]
        b_r = gt[NH + h:NH + h + 1, :]
        log_d = jnp.where(causal, b_c - b_r + ig_r, NEG_BIG)
        inter = b_c + m_old
        m_t = jnp.maximum(inter, jnp.max(log_d, axis=-1, keepdims=True))
        d_m = jnp.exp(log_d - m_t)
        w_int = jnp.exp(inter - m_t)
        s = _dot_nt(q_b, k_b) * d_m
        num = w_int * _dot(q_b, ct_old.astype(BF16)) + _dot(s.astype(BF16), v_b)
        den = w_int * jnp.sum(q_f * n_old, axis=-1, keepdims=True) + jnp.sum(s, axis=-1, keepdims=True)
        h_t = num / jnp.maximum(jnp.abs(den), jnp.exp(-m_t))

        b_end = b_c[ch - 1:ch, :]
        inter_end = b_end + m_old
        m_new = jnp.maximum(inter_end, jnp.max(b_end - b_r + ig_r, axis=-1, keepdims=True))
        w_end = jnp.exp(b_end - b_c + ig_c - m_new)
        f_end = jnp.exp(inter_end - m_new)
        ct_s[h] = f_end * ct_old + _dot_tn(k_b, (w_end * v_f).astype(BF16))
        n_ref[0, 0, h:h + 1, :] = f_end * n_old + jnp.sum(w_end * k_f, axis=0, keepdims=True)
        m_ref[h:h + 1, :] = jnp.broadcast_to(m_new, (1, LANES))
        mo_ref[0, h:h + 1, :] = jnp.where(seq_lane == seq_id, m_new, mo_ref[0, h:h + 1, :])

        hn = h_t * lax.rsqrt(jnp.mean(h_t * h_t, axis=-1, keepdims=True) + EPS)
        hn = hn * hnw_ref[:, h * DHV:(h + 1) * DHV]
        hg_s[r0:r0 + ch, h * DHV:(h + 1) * DHV] = (hn * sig_o * silu_z).astype(BF16)

    val = {}

    def conv_input():
        u = rest(R_CG, D_C) * rest(R_XC, D_C)
        ubuf_s[SUBLANES:SUBLANES + tl, :] = u
        cv = _conv_tap(cw_ref, 0) * ubuf_s[SUBLANES - 2:SUBLANES - 2 + tl, :]
        cv = cv + _conv_tap(cw_ref, 1) * ubuf_s[SUBLANES - 1:SUBLANES - 1 + tl, :]
        val["cv"] = cv + _conv_tap(cw_ref, 2) * u

    def conv_gate():
        val["yc_in"] = (_silu(rest(R_ZC, D_C)) * rest(R_BG, D_C) * val["cv"]).astype(BF16)

    def conv_proj():
        val["gy_c"] = _sigmoid(rest(R_GC, D_MODEL)) * _dot(val["yc_in"], wpc_ref[...])

    def merge_gate():
        val["sig_gm"] = _sigmoid(rest(R_GM, D_MODEL))

    fillers = iter([conv_input, conv_gate, conv_proj, merge_gate])
    units = iter([(ci, h) for ci in range(tl // ch) for h in range(NH)])
    for step in PROMPT_ORDER:
        if step == "U":
            mlstm_unit(*next(units))
        else:
            next(fillers)()

    ubuf_s[0:SUBLANES, :] = ubuf_s[tl:tl + SUBLANES, :]

    conv_ref[0, 0] = ubuf_s[SUBLANES - (CONV_W - 1):SUBLANES, :]
    for h in range(NH):
        c_ref[0, 0, h] = ct_s[h].T

    y_m = _dot(hg_s[...], wpm_ref[...])
    mix = (val["sig_gm"] * y_m + val["gy_c"]).astype(BF16)
    out = x + _dot(mix, wout_ref[...])
    y_ref[0] = _rmsnorm(out, fnw_ref[...])


def _resident(shape, block_index=None):
    index = (0,) * len(shape) if block_index is None else block_index
    return pl.BlockSpec(shape, lambda *_: index, pipeline_mode=pl.Buffered(1))


def _prompt_call(x, wh, wg, wr, wsq, nw, gb, hnw, cw, fnw):
    bsz, seq, _ = x.shape
    tl = PROMPT_BLOCK
    grid = (bsz, seq // tl)
    out_shape = (
        jax.ShapeDtypeStruct((bsz, seq, D_MODEL), F32),
        jax.ShapeDtypeStruct((1, bsz, NH, DHV, DQK), F32),
        jax.ShapeDtypeStruct((1, bsz, NH, DQK), F32),
        jax.ShapeDtypeStruct((1, NH, bsz), F32),
        jax.ShapeDtypeStruct((1, bsz, CONV_W - 1, D_C), F32),
    )
    in_specs = [
        pl.BlockSpec((1, tl, D_MODEL), lambda b, l: (b, l, 0)),
        _resident((D_MODEL, N_HEAD)),
        _resident((D_MODEL, LANES)),
        _resident((D_MODEL, N_REST)),
        _resident((D_M, D_MODEL), (0, 0)),
        _resident((D_C, D_MODEL), (0, 1)),
        _resident((D_MODEL, D_MODEL), (0, 2)),
        _resident((1, D_MODEL)),
        _resident((1, LANES)),
        _resident((1, D_M)),
        _resident((1, CONV_W * D_C)),
        _resident((1, D_MODEL)),
    ]
    out_specs = (
        pl.BlockSpec((1, tl, D_MODEL), lambda b, l: (b, l, 0)),
        pl.BlockSpec((1, 1, NH, DHV, DQK), lambda b, l: (0, b, 0, 0, 0)),
        pl.BlockSpec((1, 1, NH, DQK), lambda b, l: (0, b, 0, 0)),
        pl.BlockSpec((1, NH, bsz), lambda b, l: (0, 0, 0)),
        pl.BlockSpec((1, 1, CONV_W - 1, D_C), lambda b, l: (0, b, 0, 0)),
    )
    return pl.pallas_call(
        _prompt_kernel,
        grid=grid,
        in_specs=in_specs,
        out_specs=out_specs,
        out_shape=out_shape,
        scratch_shapes=[
            pltpu.VMEM((tl, D_M), BF16),
            pltpu.VMEM((tl + 2 * SUBLANES, D_C), F32),
            pltpu.VMEM((NH, DQK, DHV), F32),
            pltpu.VMEM((SUBLANES, LANES), F32),
        ],
        compiler_params=pltpu.CompilerParams(
            dimension_semantics=("arbitrary", "arbitrary"),
            vmem_limit_bytes=VMEM_LIMIT_PROMPT),
        name="prompt_layer",
    )(x, wh, wg, wr, wsq, wsq, wsq, nw, gb, hnw, cw, fnw)


SC_LANES = 16
SC_CORES = 2
SC_SUBCORES = 16


def _sample_gate_scalars(g_ref, gb_ref, m_ref):
    g = g_ref[...] + gb_ref[...]
    ig = g[:, 0:NH]
    lf = _log_sigmoid(g[:, NH:2 * NH])
    inter = lf + m_ref[0]
    m_t = jnp.maximum(inter, ig)
    return m_t, jnp.exp(ig - m_t), jnp.exp(inter - m_t), jnp.exp(-m_t)


def _sample_gates_kernel(hs_ref, g_ref, gb_ref, n_ref, m_ref, kw_ref, fb_ref, no_ref, mo_ref):
    nseq = hs_ref.shape[0]
    m_t, w_end, f_end, _ = _sample_gate_scalars(g_ref, gb_ref, m_ref)
    mo_ref[0] = m_t
    for h in range(NH):
        k_h = hs_ref[:, OFF_K + h * DQK:OFF_K + (h + 1) * DQK] * K_SCALE
        n_h = n_ref[0, :, h, :]
        w_h = w_end[:, h:h + 1]
        f_h = f_end[:, h:h + 1]
        kw_ref[:, h * DQK:(h + 1) * DQK] = w_h * k_h
        fb_ref[:, h * DQK:(h + 1) * DQK] = jnp.broadcast_to(f_h, (nseq, DQK))
        no_ref[0, :, h, :] = f_h * n_h + w_h * k_h


def _sample_gates_call(head_s, g, gb, n0, m0):
    nseq = head_s.shape[0]
    args = (head_s, g, gb, n0, m0)
    full = lambda a: pl.BlockSpec(a.shape, lambda i: (0,) * a.ndim)
    row = jax.ShapeDtypeStruct((nseq, D_QK), F32)
    return pl.pallas_call(
        _sample_gates_kernel,
        grid=(1,),
        in_specs=[full(a) for a in args],
        out_specs=(full(row), full(row), full(n0), full(m0)),
        out_shape=(row, row, jax.ShapeDtypeStruct(n0.shape, F32), jax.ShapeDtypeStruct(m0.shape, F32)),
        name="sample_gates",
    )(*args)


def _sample_state_sc_call(c0, head_s, kw_s, fb_s):
    nseq = c0.shape[1]
    nworkers = SC_CORES * SC_SUBCORES
    per_worker = nseq * NH // nworkers
    nvec = DQK // SC_LANES
    mesh = plsc.VectorSubcoreMesh(core_axis_name="c", subcore_axis_name="s")

    def body(c_hbm, hs_hbm, kw_hbm, fb_hbm, co_hbm, cq_hbm, c_v, q_v, kw_v, v_v, f_v, cq_v):
        wid = lax.axis_index("c") * SC_SUBCORES + lax.axis_index("s")
        lane = lax.iota(jnp.int32, SC_LANES)
        zero_i = jnp.zeros((SC_LANES,), jnp.int32)

        @pl.loop(0, per_worker)
        def _(t):
            pair = wid * per_worker + t
            b = pair // NH
            h = pair % NH
            pltpu.sync_copy(c_hbm.at[0, b, h], c_v)
            pltpu.sync_copy(hs_hbm.at[b, pl.ds(OFF_Q + h * DQK, DQK)], q_v)
            pltpu.sync_copy(hs_hbm.at[b, pl.ds(OFF_V + h * DHV, DHV)], v_v)
            pltpu.sync_copy(kw_hbm.at[b, pl.ds(h * DQK, DQK)], kw_v)
            pltpu.sync_copy(fb_hbm.at[b, pl.ds(h * DQK, DQK)], f_v)
            q = [q_v[pl.ds(SC_LANES * j, SC_LANES)] for j in range(nvec)]
            kw = [kw_v[pl.ds(SC_LANES * j, SC_LANES)] for j in range(nvec)]
            f_vec = f_v[pl.ds(0, SC_LANES)]

            @pl.loop(0, DHV // SC_LANES)
            def _(g):
                cq_vec = jnp.zeros((SC_LANES,), F32)
                for i in range(SC_LANES):
                    r = g * SC_LANES + i
                    v_r = plsc.load_gather(v_v, [zero_i + r])
                    acc = jnp.zeros((SC_LANES,), F32)
                    for j in range(nvec):
                        c = c_v[r, pl.ds(SC_LANES * j, SC_LANES)]
                        acc = acc + c * q[j]
                        c_v[r, pl.ds(SC_LANES * j, SC_LANES)] = f_vec * c + v_r * kw[j]
                    cq_vec = jnp.where(lane == i, jnp.sum(acc), cq_vec)
                cq_v[pl.ds(g * SC_LANES, SC_LANES)] = cq_vec

            pltpu.sync_copy(c_v, co_hbm.at[0, b, h])
            pltpu.sync_copy(cq_v, cq_hbm.at[b, pl.ds(h * DHV, DHV)])

    return pl.kernel(
        body,
        out_type=(jax.ShapeDtypeStruct(c0.shape, F32),
                  jax.ShapeDtypeStruct((nseq, D_M), F32)),
        mesh=mesh,
        scratch_types=[pltpu.VMEM((DHV, DQK), F32),
                       pltpu.VMEM((DQK,), F32),
                       pltpu.VMEM((DQK,), F32),
                       pltpu.VMEM((DHV,), F32),
                       pltpu.VMEM((DQK,), F32),
                       pltpu.VMEM((DHV,), F32)],
        compiler_params=pltpu.CompilerParams(use_tc_tiling_on_sc=True, needs_layout_passes=False),
        name="sample_state_sc",
    )(c0, head_s, kw_s, fb_s)


def _sample_tail_kernel(cq_ref, hs_ref, g_ref, gb_ref, n_ref, m_ref,
                        x_ref, r_ref, conv_ref, wpm_ref, wpc_ref, wout_ref, hnw_ref, cw_ref, fnw_ref,
                        y_ref, convo_ref):
    def piece(off, width):
        return r_ref[:, off:off + width]

    m_t, w_end, f_end, floor = _sample_gate_scalars(g_ref, gb_ref, m_ref)
    for h in range(NH):
        q_h = hs_ref[:, OFF_Q + h * DQK:OFF_Q + (h + 1) * DQK]
        k_h = hs_ref[:, OFF_K + h * DQK:OFF_K + (h + 1) * DQK] * K_SCALE
        v_h = hs_ref[:, OFF_V + h * DHV:OFF_V + (h + 1) * DHV]
        n_h = n_ref[0, :, h, :]
        w_h = w_end[:, h:h + 1]
        f_h = f_end[:, h:h + 1]
        s = jnp.sum(q_h * k_h, axis=-1, keepdims=True) * w_h
        den = f_h * jnp.sum(n_h * q_h, axis=-1, keepdims=True) + s
        num = f_h * cq_ref[:, h * DHV:(h + 1) * DHV] + s * v_h
        h_t = num / jnp.maximum(jnp.abs(den), floor[:, h:h + 1])
        hn = h_t * lax.rsqrt(jnp.mean(h_t * h_t, axis=-1, keepdims=True) + EPS)
        hn = hn * hnw_ref[:, h * DHV:(h + 1) * DHV]
        o_h = piece(R_O + h * DHV, DHV)
        zm_h = piece(R_ZM + h * DHV, DHV)
        hg = (hn * _sigmoid(o_h) * _silu(zm_h)).astype(BF16)
        part = _dot(hg, wpm_ref[h * DHV:(h + 1) * DHV, :])
        y_m = part if h == 0 else y_m + part

    u = piece(R_CG, D_C) * piece(R_XC, D_C)
    conv_old = conv_ref[0, :, 1, :]
    cv = _conv_tap(cw_ref, 0) * conv_ref[0, :, 0, :]
    cv = cv + _conv_tap(cw_ref, 1) * conv_old
    cv = cv + _conv_tap(cw_ref, 2) * u
    convo_ref[0, :, 0, :] = conv_old
    convo_ref[0, :, 1, :] = u
    yc_in = (_silu(piece(R_ZC, D_C)) * piece(R_BG, D_C) * cv).astype(BF16)
    y_c = _dot(yc_in, wpc_ref[...])
    mix = (_sigmoid(piece(R_GM, D_MODEL)) * y_m + _sigmoid(piece(R_GC, D_MODEL)) * y_c).astype(BF16)
    out = x_ref[:, 0, :] + _dot(mix, wout_ref[...])
    y_ref[:, 0, :] = _rmsnorm(out, fnw_ref[...])


def _sample_tail_call(cq, head_s, g, gb, n0, m0, x, rest, conv0, wsq, hnw, cw, fnw):
    nseq = x.shape[0]
    args = (cq, head_s, g, gb, n0, m0, x, rest, conv0, wsq, wsq, wsq, hnw, cw, fnw)
    full = lambda a: pl.BlockSpec(a.shape, lambda i: (0,) * a.ndim)
    square = lambda k: pl.BlockSpec((D_MODEL, D_MODEL), lambda i: (0, k))
    specs = [full(a) for a in args]
    specs[9:12] = [square(0), square(1), square(2)]
    return pl.pallas_call(
        _sample_tail_kernel,
        grid=(1,),
        in_specs=specs,
        out_specs=(pl.BlockSpec((nseq, 1, D_MODEL), lambda i: (0, 0, 0)),
                   pl.BlockSpec((1, nseq, CONV_W - 1, D_C), lambda i: (0, 0, 0, 0))),
        out_shape=(jax.ShapeDtypeStruct((nseq, 1, D_MODEL), F32),
                   jax.ShapeDtypeStruct((1, nseq, CONV_W - 1, D_C), F32)),
        compiler_params=pltpu.CompilerParams(dimension_semantics=("arbitrary",),
                                             vmem_limit_bytes=VMEM_LIMIT_SAMPLE),
        name="sample_tail",
    )(*args)


def kernel(x_prompt, x_sample, state_mlstm_C, state_mlstm_n, state_mlstm_m, state_conv, norm_w, w_in, b_i, b_f,
           head_norm_w, conv_w, w_proj_m, w_proj_c, w_out, final_norm_w):
    depth = norm_w.shape[0]
    assert depth == 1, "single-layer trunk"
    bsz = x_prompt.shape[0]
    nseq = x_sample.shape[0]

    assert w_in.shape == (1, D_MODEL, N_HEAD + N_GATE_COLS + N_REST)
    gb = jnp.pad(jnp.concatenate([b_i[0], b_f[0]]), (0, LANES - N_GATE_COLS)).reshape(1, LANES)
    nw = norm_w[0].reshape(1, D_MODEL)
    hnw = head_norm_w[0].reshape(1, D_M)
    cw = conv_w.reshape(1, CONV_W * D_C)
    fnw = final_norm_w.reshape(1, D_MODEL)
    (wr, wh, wg, wsq, head_s, g_s, rest_s) = _weight_prep_call(
        jnp.swapaxes(w_in, 1, 2), w_proj_m, w_proj_c, w_out, x_sample, nw)

    kw_s, fb_s, n_s, m_s = _sample_gates_call(head_s, g_s, gb, state_mlstm_n, state_mlstm_m)
    y_p, c_p, n_p, m_p, conv_p = _prompt_call(x_prompt, wh, wg, wr, wsq, nw, gb, hnw, cw, fnw)
    m_p = jnp.swapaxes(m_p, 1, 2)
    c_s, cq = _sample_state_sc_call(state_mlstm_C, head_s, kw_s, fb_s)

    y_s, conv_s = _sample_tail_call(cq, head_s, g_s, gb, state_mlstm_n, state_mlstm_m,
                                    x_sample, rest_s, state_conv, wsq, hnw, cw, fnw)

    return (y_p, y_s, c_p, n_p, m_p, conv_p, c_s, n_s, m_s, conv_s)
```

```python
import jax
import jax.numpy as jnp
from jax import lax
from jax.experimental import pallas as pl
from jax.experimental.pallas import tpu as pltpu
from jax.experimental.pallas import tpu_sc as plsc

F32 = jnp.float32
BF16 = jnp.bfloat16

D_MODEL = 1024
NH = 4
DHV = 256
DQK = 128
D_QK = NH * DQK
D_M = NH * DHV
D_C = D_MODEL
CONV_W = 3
EPS = 1e-6
NEG_BIG = -1e30
K_SCALE = DQK ** -0.5

LANES = 128
SUBLANES = 8

OFF_Q = 0
OFF_K = OFF_Q + D_QK
OFF_V = OFF_K + D_QK
N_HEAD = OFF_V + D_M
N_GATE_COLS = 2 * NH
R_O = 0
R_ZM = R_O + D_M
R_BG = R_ZM + D_M
R_CG = R_BG + D_C
R_XC = R_CG + D_C
R_ZC = R_XC + D_C
R_GM = R_ZC + D_C
R_GC = R_GM + D_MODEL
N_REST = R_GC + D_MODEL

PROMPT_BLOCK = 512
MLSTM_CHUNK = 256
PROMPT_ORDER = "UUUUFUFUFUFU"
VMEM_LIMIT_PROMPT = 60 * 1024 * 1024
VMEM_LIMIT_SAMPLE = 60 * 1024 * 1024


def _sigmoid(x):
    return 0.5 * jnp.tanh(0.5 * x) + 0.5


def _silu(x):
    return x * _sigmoid(x)


def _log_sigmoid(x):
    return jnp.minimum(x, 0.0) - jnp.log1p(jnp.exp(-jnp.abs(x)))


def _rmsnorm(x, w):
    return x * lax.rsqrt(jnp.mean(x * x, axis=-1, keepdims=True) + EPS) * w


def _conv_tap(cw_ref, j):
    return cw_ref[:, j * D_C:(j + 1) * D_C]


def _dot(a, b):
    return jnp.dot(a, b, preferred_element_type=F32)


def _dot_nt(a, b):
    return lax.dot_general(a, b, (((1,), (1,)), ((), ())), preferred_element_type=F32)


def _dot_tn(a, b):
    return lax.dot_general(a, b, (((0,), (0,)), ((), ())), preferred_element_type=F32)


PREP_COLS = 1024
PREP_CHUNK = 256


PREP_STEPS = N_REST // PREP_COLS
PREP_HEAD_COLS = N_HEAD // PREP_STEPS
PREP_SQ_ROWS = D_MODEL // PREP_STEPS
N_SQUARE = 3


def _weight_prep_kernel(a_ref, c_ref, g_ref, pm_ref, pc_ref, po_ref, xs_ref, nw_ref, gb_ref, n_ref, m_ref,
                        wr_ref, wh_ref, wg_ref, wsq_ref,
                        hs_ref, gs_ref, rs_ref, kw_ref, fb_ref, no_ref, mo_ref,
                        xn_s, w_s, f_s):
    j = pl.program_id(0)
    nseq = xs_ref.shape[0]

    @pl.when(j == 0)
    def _():
        xn_s[...] = _rmsnorm(xs_ref[:, 0, :], nw_ref[...]).astype(BF16)

    xn = xn_s[...]
    for r0 in range(0, PREP_COLS, PREP_CHUNK):
        w_t = a_ref[0, r0:r0 + PREP_CHUNK, :].T.astype(BF16)
        wr_ref[:, r0:r0 + PREP_CHUNK] = w_t
        rs_ref[:, r0:r0 + PREP_CHUNK] = _dot(xn, w_t)
    w_t = c_ref[0].T.astype(BF16)
    wh_ref[...] = w_t
    head = _dot(xn, w_t)
    hs_ref[...] = head

    @pl.when(j == 0)
    def _():
        lane = lax.broadcasted_iota(jnp.int32, (D_MODEL, LANES), 1)
        w_g = jnp.where(lane < N_GATE_COLS, g_ref[0].T, 0.0).astype(BF16)
        wg_ref[...] = w_g
        gs_ref[...] = _dot(xn, w_g)
        m_t, w_end, f_end, _ = _sample_gate_scalars(gs_ref, gb_ref, m_ref)
        mo_ref[0] = m_t
        w_s[...] = w_end
        f_s[...] = f_end
        for h in range(NH):
            fb_ref[:, h * DQK:(h + 1) * DQK] = jnp.broadcast_to(f_end[:, h:h + 1], (nseq, DQK))

    heads_per_step = PREP_HEAD_COLS // DQK
    for step in range(D_QK // PREP_HEAD_COLS, 2 * D_QK // PREP_HEAD_COLS):
        @pl.when(j == step)
        def _(step=step):
            for i in range(heads_per_step):
                h = (step - D_QK // PREP_HEAD_COLS) * heads_per_step + i
                kw = w_s[:, h:h + 1] * (head[:, i * DQK:(i + 1) * DQK] * K_SCALE)
                kw_ref[:, h * DQK:(h + 1) * DQK] = kw
                no_ref[0, :, h, :] = f_s[:, h:h + 1] * n_ref[0, :, h, :] + kw

    wsq_ref[:, 0:D_MODEL] = pm_ref[0].astype(BF16)
    wsq_ref[:, D_MODEL:2 * D_MODEL] = pc_ref[0].astype(BF16)
    wsq_ref[:, 2 * D_MODEL:3 * D_MODEL] = po_ref[0].astype(BF16)


def _weight_prep_call(w_in_t, w_proj_m, w_proj_c, w_out, xs, nw, gb, n0, m0):
    nseq = xs.shape[0]
    row = jax.ShapeDtypeStruct((nseq, D_QK), F32)
    rest_row0 = N_HEAD + N_GATE_COLS
    const = lambda shape: pl.BlockSpec(shape, lambda j: (0,) * len(shape))
    sq_in = pl.BlockSpec((1, PREP_SQ_ROWS, D_MODEL), lambda j: (0, j, 0))
    sq_out = pl.BlockSpec((PREP_SQ_ROWS, N_SQUARE * D_MODEL), lambda j: (j, 0))
    sq_shape = jax.ShapeDtypeStruct((D_MODEL, N_SQUARE * D_MODEL), BF16)
    return pl.pallas_call(
        _weight_prep_kernel,
        grid=(PREP_STEPS,),
        in_specs=[
            pl.BlockSpec((pl.Element(1), pl.Element(PREP_COLS), pl.Element(D_MODEL)),
                         lambda j: (0, pl.multiple_of(rest_row0 + j * PREP_COLS, SUBLANES), 0)),
            pl.BlockSpec((1, PREP_HEAD_COLS, D_MODEL), lambda j: (0, j, 0)),
            pl.BlockSpec((1, LANES, D_MODEL), lambda j: (0, N_HEAD // LANES, 0)),
            sq_in, sq_in, sq_in,
            const((nseq, 1, D_MODEL)),
            const((1, D_MODEL)),
            const((1, LANES)),
            const(n0.shape),
            const(m0.shape),
        ],
        out_specs=(
            pl.BlockSpec((D_MODEL, PREP_COLS), lambda j: (0, j)),
            pl.BlockSpec((D_MODEL, PREP_HEAD_COLS), lambda j: (0, j)),
            const((D_MODEL, LANES)),
            sq_out,
            pl.BlockSpec((nseq, PREP_HEAD_COLS), lambda j: (0, j)),
            const((nseq, LANES)),
            pl.BlockSpec((nseq, PREP_COLS), lambda j: (0, j)),
            const(row.shape), const(row.shape), const(n0.shape), const(m0.shape),
        ),
        out_shape=(
            jax.ShapeDtypeStruct((D_MODEL, N_REST), BF16),
            jax.ShapeDtypeStruct((D_MODEL, N_HEAD), BF16),
            jax.ShapeDtypeStruct((D_MODEL, LANES), BF16),
            sq_shape,
            jax.ShapeDtypeStruct((nseq, N_HEAD), F32),
            jax.ShapeDtypeStruct((nseq, LANES), F32),
            jax.ShapeDtypeStruct((nseq, N_REST), F32),
            row,
            row,
            jax.ShapeDtypeStruct(n0.shape, F32),
            jax.ShapeDtypeStruct(m0.shape, F32),
        ),
        scratch_shapes=[pltpu.VMEM((nseq, D_MODEL), BF16),
                        pltpu.VMEM((nseq, NH), F32),
                        pltpu.VMEM((nseq, NH), F32)],
        compiler_params=pltpu.CompilerParams(dimension_semantics=("arbitrary",),
                                             vmem_limit_bytes=VMEM_LIMIT_SAMPLE),
        name="weight_prep",
    )(w_in_t, w_in_t, w_in_t, w_proj_m, w_proj_c, w_out, xs, nw, gb, n0, m0)


def _prompt_kernel(x_ref, wh_ref, wg_ref, wr_ref, wpm_ref, wpc_ref, wout_ref,
                   nw_ref, gb_ref, hnw_ref, cw_ref, fnw_ref,
                   y_ref, c_ref, n_ref, mo_ref, conv_ref,
                   hg_s, ubuf_s, ct_s, m_ref):
    tl = x_ref.shape[1]
    seq_id = pl.program_id(0)
    l = pl.program_id(1)
    seq_lane = lax.broadcasted_iota(jnp.int32, (1, mo_ref.shape[2]), 1)

    @pl.when((seq_id == 0) & (l == 0))
    def _():
        mo_ref[...] = jnp.zeros_like(mo_ref)

    @pl.when(l == 0)
    def _():
        ct_s[...] = jnp.zeros_like(ct_s)
        n_ref[...] = jnp.zeros_like(n_ref)
        m_ref[...] = jnp.zeros_like(m_ref)
        ubuf_s[0:SUBLANES, :] = jnp.zeros((SUBLANES, D_C), F32)

    x = x_ref[0]
    xn = _rmsnorm(x, nw_ref[...]).astype(BF16)

    def rest(off, width):
        return _dot(xn, wr_ref[:, off:off + width])

    g = _dot(xn, wg_ref[...]) + gb_ref[...]
    gc = jnp.where(lax.broadcasted_iota(jnp.int32, (tl, LANES), 1) < NH, g, _log_sigmoid(g))
    qkv = _dot(xn, wh_ref[...])

    ch = MLSTM_CHUNK
    lane = lax.broadcasted_iota(jnp.int32, (ch, LANES), 1)
    row = lax.broadcasted_iota(jnp.int32, (ch, ch), 0)
    col = lax.broadcasted_iota(jnp.int32, (ch, ch), 1)
    causal = row >= col
    row_g = lax.broadcasted_iota(jnp.int32, (ch, LANES), 0)
    chunk_gates = []
    for r0 in range(0, tl, ch):
        gc_c = gc[r0:r0 + ch, :]
        bc = gc_c
        shift = 1
        while shift < ch:
            bc = bc + jnp.where(row_g >= shift, pltpu.roll(bc, shift, axis=0), 0.0)
            shift *= 2
        gt = jnp.where(lane < NH, gc_c, bc).T
        chunk_gates.append((gc_c, bc, gt))

    head_gates = {}

    def mlstm_unit(ci, h):
        r0 = ci * ch
        gc_c, bc, gt = chunk_gates[ci]
        if ci == 0:
            head_gates[h] = (_sigmoid(rest(R_O + h * DHV, DHV)), _silu(rest(R_ZM + h * DHV, DHV)))
        sig_o = head_gates[h][0][r0:r0 + ch, :]
        silu_z = head_gates[h][1][r0:r0 + ch, :]
        q_f = qkv[r0:r0 + ch, OFF_Q + h * DQK:OFF_Q + (h + 1) * DQK]
        k_f = qkv[r0:r0 + ch, OFF_K + h * DQK:OFF_K + (h + 1) * DQK] * K_SCALE
        v_f = qkv[r0:r0 + ch, OFF_V + h * DHV:OFF_V + (h + 1) * DHV]
        q_b = q_f.astype(BF16)
        k_b = k_f.astype(BF16)
        v_b = v_f.astype(BF16)
        ct_old = ct_s[h]
        n_old = n_ref[0, 0, h:h + 1, :]
        m_old = m_ref[h:h + 1, 0:1]

        b_c = bc[:, NH + h:NH + h + 1]
        ig_c = gc_c[:, h:h + 1]
        ig_r = gt[h:h + 1, :]
        b_r = gt[NH + h:NH + h + 1, :]
        log_d = jnp.where(causal, b_c - b_r + ig_r, NEG_BIG)
        inter = b_c + m_old
        m_t = jnp.maximum(inter, jnp.max(log_d, axis=-1, keepdims=True))
        d_m = jnp.exp(log_d - m_t)
        w_int = jnp.exp(inter - m_t)
        s = _dot_nt(q_b, k_b) * d_m
        num = w_int * _dot(q_b, ct_old.astype(BF16)) + _dot(s.astype(BF16), v_b)
        den = w_int * jnp.sum(q_f * n_old, axis=-1, keepdims=True) + jnp.sum(s, axis=-1, keepdims=True)
        h_t = num / jnp.maximum(jnp.abs(den), jnp.exp(-m_t))

        b_end = b_c[ch - 1:ch, :]
        inter_end = b_end + m_old
        m_new = jnp.maximum(inter_end, jnp.max(b_end - b_r + ig_r, axis=-1, keepdims=True))
        w_end = jnp.exp(b_end - b_c + ig_c - m_new)
        f_end = jnp.exp(inter_end - m_new)
        ct_s[h] = f_end * ct_old + _dot_tn(k_b, (w_end * v_f).astype(BF16))
        n_ref[0, 0, h:h + 1, :] = f_end * n_old + jnp.sum(w_end * k_f, axis=0, keepdims=True)
        m_ref[h:h + 1, :] = jnp.broadcast_to(m_new, (1, LANES))
        mo_ref[0, h:h + 1, :] = jnp.where(seq_lane == seq_id, m_new, mo_ref[0, h:h + 1, :])

        hn = h_t * lax.rsqrt(jnp.mean(h_t * h_t, axis=-1, keepdims=True) + EPS)
        hn = hn * hnw_ref[:, h * DHV:(h + 1) * DHV]
        hg_s[r0:r0 + ch, h * DHV:(h + 1) * DHV] = (hn * sig_o * silu_z).astype(BF16)

    val = {}

    def conv_input():
        u = rest(R_CG, D_C) * rest(R_XC, D_C)
        ubuf_s[SUBLANES:SUBLANES + tl, :] = u
        cv = _conv_tap(cw_ref, 0) * ubuf_s[SUBLANES - 2:SUBLANES - 2 + tl, :]
        cv = cv + _conv_tap(cw_ref, 1) * ubuf_s[SUBLANES - 1:SUBLANES - 1 + tl, :]
        val["cv"] = cv + _conv_tap(cw_ref, 2) * u

    def conv_gate():
        val["yc_in"] = (_silu(rest(R_ZC, D_C)) * rest(R_BG, D_C) * val["cv"]).astype(BF16)

    def conv_proj():
        val["gy_c"] = _sigmoid(rest(R_GC, D_MODEL)) * _dot(val["yc_in"], wpc_ref[...])

    def merge_gate():
        val["sig_gm"] = _sigmoid(rest(R_GM, D_MODEL))

    fillers = iter([conv_input, conv_gate, conv_proj, merge_gate])
    units = iter([(ci, h) for ci in range(tl // ch) for h in range(NH)])
    for step in PROMPT_ORDER:
        if step == "U":
            mlstm_unit(*next(units))
        else:
            next(fillers)()

    ubuf_s[0:SUBLANES, :] = ubuf_s[tl:tl + SUBLANES, :]

    conv_ref[0, 0] = ubuf_s[SUBLANES - (CONV_W - 1):SUBLANES, :]
    for h in range(NH):
        c_ref[0, 0, h] = ct_s[h].T

    y_m = _dot(hg_s[...], wpm_ref[...])
    mix = (val["sig_gm"] * y_m + val["gy_c"]).astype(BF16)
    out = x + _dot(mix, wout_ref[...])
    y_ref[0] = _rmsnorm(out, fnw_ref[...])


def _resident(shape, block_index=None):
    index = (0,) * len(shape) if block_index is None else block_index
    return pl.BlockSpec(shape, lambda *_: index, pipeline_mode=pl.Buffered(1))


def _prompt_call(x, wh, wg, wr, wsq, nw, gb, hnw, cw, fnw):
    bsz, seq, _ = x.shape
    tl = PROMPT_BLOCK
    grid = (bsz, seq // tl)
    out_shape = (
        jax.ShapeDtypeStruct((bsz, seq, D_MODEL), F32),
        jax.ShapeDtypeStruct((1, bsz, NH, DHV, DQK), F32),
        jax.ShapeDtypeStruct((1, bsz, NH, DQK), F32),
        jax.ShapeDtypeStruct((1, NH, bsz), F32),
        jax.ShapeDtypeStruct((1, bsz, CONV_W - 1, D_C), F32),
    )
    in_specs = [
        pl.BlockSpec((1, tl, D_MODEL), lambda b, l: (b, l, 0)),
        _resident((D_MODEL, N_HEAD)),
        _resident((D_MODEL, LANES)),
        _resident((D_MODEL, N_REST)),
        _resident((D_M, D_MODEL), (0, 0)),
        _resident((D_C, D_MODEL), (0, 1)),
        _resident((D_MODEL, D_MODEL), (0, 2)),
        _resident((1, D_MODEL)),
        _resident((1, LANES)),
        _resident((1, D_M)),
        _resident((1, CONV_W * D_C)),
        _resident((1, D_MODEL)),
    ]
    out_specs = (
        pl.BlockSpec((1, tl, D_MODEL), lambda b, l: (b, l, 0)),
        pl.BlockSpec((1, 1, NH, DHV, DQK), lambda b, l: (0, b, 0, 0, 0)),
        pl.BlockSpec((1, 1, NH, DQK), lambda b, l: (0, b, 0, 0)),
        pl.BlockSpec((1, NH, bsz), lambda b, l: (0, 0, 0)),
        pl.BlockSpec((1, 1, CONV_W - 1, D_C), lambda b, l: (0, b, 0, 0)),
    )
    return pl.pallas_call(
        _prompt_kernel,
        grid=grid,
        in_specs=in_specs,
        out_specs=out_specs,
        out_shape=out_shape,
        scratch_shapes=[
            pltpu.VMEM((tl, D_M), BF16),
            pltpu.VMEM((tl + 2 * SUBLANES, D_C), F32),
            pltpu.VMEM((NH, DQK, DHV), F32),
            pltpu.VMEM((SUBLANES, LANES), F32),
        ],
        compiler_params=pltpu.CompilerParams(
            dimension_semantics=("arbitrary", "arbitrary"),
            vmem_limit_bytes=VMEM_LIMIT_PROMPT),
        name="prompt_layer",
    )(x, wh, wg, wr, wsq, wsq, wsq, nw, gb, hnw, cw, fnw)


SC_LANES = 16
SC_CORES = 2
SC_SUBCORES = 16


def _sample_gate_scalars(g_ref, gb_ref, m_ref):
    g = g_ref[...] + gb_ref[...]
    ig = g[:, 0:NH]
    lf = _log_sigmoid(g[:, NH:2 * NH])
    inter = lf + m_ref[0]
    m_t = jnp.maximum(inter, ig)
    return m_t, jnp.exp(ig - m_t), jnp.exp(inter - m_t), jnp.exp(-m_t)


def _sample_state_sc_call(c0, head_s, kw_s, fb_s):
    nseq = c0.shape[1]
    nworkers = SC_CORES * SC_SUBCORES
    per_worker = nseq * NH // nworkers
    nvec = DQK // SC_LANES
    mesh = plsc.VectorSubcoreMesh(core_axis_name="c", subcore_axis_name="s")

    def body(c_hbm, hs_hbm, kw_hbm, fb_hbm, co_hbm, cq_hbm, c_v, q_v, kw_v, v_v, f_v, cq_v):
        wid = lax.axis_index("c") * SC_SUBCORES + lax.axis_index("s")
        lane = lax.iota(jnp.int32, SC_LANES)
        zero_i = jnp.zeros((SC_LANES,), jnp.int32)

        @pl.loop(0, per_worker)
        def _(t):
            pair = wid * per_worker + t
            b = pair // NH
            h = pair % NH
            pltpu.sync_copy(c_hbm.at[0, b, h], c_v)
            pltpu.sync_copy(hs_hbm.at[b, pl.ds(OFF_Q + h * DQK, DQK)], q_v)
            pltpu.sync_copy(hs_hbm.at[b, pl.ds(OFF_V + h * DHV, DHV)], v_v)
            pltpu.sync_copy(kw_hbm.at[b, pl.ds(h * DQK, DQK)], kw_v)
            pltpu.sync_copy(fb_hbm.at[b, pl.ds(h * DQK, DQK)], f_v)
            q = [q_v[pl.ds(SC_LANES * j, SC_LANES)] for j in range(nvec)]
            kw = [kw_v[pl.ds(SC_LANES * j, SC_LANES)] for j in range(nvec)]
            f_vec = f_v[pl.ds(0, SC_LANES)]

            @pl.loop(0, DHV // SC_LANES)
            def _(g):
                cq_vec = jnp.zeros((SC_LANES,), F32)
                for i in range(SC_LANES):
                    r = g * SC_LANES + i
                    v_r = plsc.load_gather(v_v, [zero_i + r])
                    acc = jnp.zeros((SC_LANES,), F32)
                    for j in range(nvec):
                        c = c_v[r, pl.ds(SC_LANES * j, SC_LANES)]
                        acc = acc + c * q[j]
                        c_v[r, pl.ds(SC_LANES * j, SC_LANES)] = f_vec * c + v_r * kw[j]
                    cq_vec = jnp.where(lane == i, jnp.sum(acc), cq_vec)
                cq_v[pl.ds(g * SC_LANES, SC_LANES)] = cq_vec

            pltpu.sync_copy(c_v, co_hbm.at[0, b, h])
            pltpu.sync_copy(cq_v, cq_hbm.at[b, pl.ds(h * DHV, DHV)])

    return pl.kernel(
        body,
        out_type=(jax.ShapeDtypeStruct(c0.shape, F32),
                  jax.ShapeDtypeStruct((nseq, D_M), F32)),
        mesh=mesh,
        scratch_types=[pltpu.VMEM((DHV, DQK), F32),
                       pltpu.VMEM((DQK,), F32),
                       pltpu.VMEM((DQK,), F32),
                       pltpu.VMEM((DHV,), F32),
                       pltpu.VMEM((DQK,), F32),
                       pltpu.VMEM((DHV,), F32)],
        compiler_params=pltpu.CompilerParams(use_tc_tiling_on_sc=True, needs_layout_passes=False),
        name="sample_state_sc",
    )(c0, head_s, kw_s, fb_s)


def _sample_tail_kernel(cq_ref, hs_ref, g_ref, gb_ref, n_ref, m_ref,
                        x_ref, r_ref, conv_ref, wpm_ref, wpc_ref, wout_ref, hnw_ref, cw_ref, fnw_ref,
                        y_ref, convo_ref):
    def piece(off, width):
        return r_ref[:, off:off + width]

    m_t, w_end, f_end, floor = _sample_gate_scalars(g_ref, gb_ref, m_ref)
    for h in range(NH):
        q_h = hs_ref[:, OFF_Q + h * DQK:OFF_Q + (h + 1) * DQK]
        k_h = hs_ref[:, OFF_K + h * DQK:OFF_K + (h + 1) * DQK] * K_SCALE
        v_h = hs_ref[:, OFF_V + h * DHV:OFF_V + (h + 1) * DHV]
        n_h = n_ref[0, :, h, :]
        w_h = w_end[:, h:h + 1]
        f_h = f_end[:, h:h + 1]
        s = jnp.sum(q_h * k_h, axis=-1, keepdims=True) * w_h
        den = f_h * jnp.sum(n_h * q_h, axis=-1, keepdims=True) + s
        num = f_h * cq_ref[:, h * DHV:(h + 1) * DHV] + s * v_h
        h_t = num / jnp.maximum(jnp.abs(den), floor[:, h:h + 1])
        hn = h_t * lax.rsqrt(jnp.mean(h_t * h_t, axis=-1, keepdims=True) + EPS)
        hn = hn * hnw_ref[:, h * DHV:(h + 1) * DHV]
        o_h = piece(R_O + h * DHV, DHV)
        zm_h = piece(R_ZM + h * DHV, DHV)
        hg = (hn * _sigmoid(o_h) * _silu(zm_h)).astype(BF16)
        part = _dot(hg, wpm_ref[h * DHV:(h + 1) * DHV, :])
        y_m = part if h == 0 else y_m + part

    u = piece(R_CG, D_C) * piece(R_XC, D_C)
    conv_old = conv_ref[0, :, 1, :]
    cv = _conv_tap(cw_ref, 0) * conv_ref[0, :, 0, :]
    cv = cv + _conv_tap(cw_ref, 1) * conv_old
    cv = cv + _conv_tap(cw_ref, 2) * u
    convo_ref[0, :, 0, :] = conv_old
    convo_ref[0, :, 1, :] = u
    yc_in = (_silu(piece(R_ZC, D_C)) * piece(R_BG, D_C) * cv).astype(BF16)
    y_c = _dot(yc_in, wpc_ref[...])
    mix = (_sigmoid(piece(R_GM, D_MODEL)) * y_m + _sigmoid(piece(R_GC, D_MODEL)) * y_c).astype(BF16)
    out = x_ref[:, 0, :] + _dot(mix, wout_ref[...])
    y_ref[:, 0, :] = _rmsnorm(out, fnw_ref[...])


def _sample_tail_call(cq, head_s, g, gb, n0, m0, x, rest, conv0, wsq, hnw, cw, fnw):
    nseq = x.shape[0]
    args = (cq, head_s, g, gb, n0, m0, x, rest, conv0, wsq, wsq, wsq, hnw, cw, fnw)
    full = lambda a: pl.BlockSpec(a.shape, lambda i: (0,) * a.ndim)
    square = lambda k: pl.BlockSpec((D_MODEL, D_MODEL), lambda i: (0, k))
    specs = [full(a) for a in args]
    specs[9:12] = [square(0), square(1), square(2)]
    return pl.pallas_call(
        _sample_tail_kernel,
        grid=(1,),
        in_specs=specs,
        out_specs=(pl.BlockSpec((nseq, 1, D_MODEL), lambda i: (0, 0, 0)),
                   pl.BlockSpec((1, nseq, CONV_W - 1, D_C), lambda i: (0, 0, 0, 0))),
        out_shape=(jax.ShapeDtypeStruct((nseq, 1, D_MODEL), F32),
                   jax.ShapeDtypeStruct((1, nseq, CONV_W - 1, D_C), F32)),
        compiler_params=pltpu.CompilerParams(dimension_semantics=("arbitrary",),
                                             vmem_limit_bytes=VMEM_LIMIT_SAMPLE),
        name="sample_tail",
    )(*args)


def kernel(x_prompt, x_sample, state_mlstm_C, state_mlstm_n, state_mlstm_m, state_conv, norm_w, w_in, b_i, b_f,
           head_norm_w, conv_w, w_proj_m, w_proj_c, w_out, final_norm_w):
    depth = norm_w.shape[0]
    assert depth == 1, "single-layer trunk"
    bsz = x_prompt.shape[0]
    nseq = x_sample.shape[0]

    assert w_in.shape == (1, D_MODEL, N_HEAD + N_GATE_COLS + N_REST)
    gb = jnp.pad(jnp.concatenate([b_i[0], b_f[0]]), (0, LANES - N_GATE_COLS)).reshape(1, LANES)
    nw = norm_w[0].reshape(1, D_MODEL)
    hnw = head_norm_w[0].reshape(1, D_M)
    cw = conv_w.reshape(1, CONV_W * D_C)
    fnw = final_norm_w.reshape(1, D_MODEL)
    (wr, wh, wg, wsq, head_s, g_s, rest_s, kw_s, fb_s, n_s, m_s) = _weight_prep_call(
        jnp.swapaxes(w_in, 1, 2), w_proj_m, w_proj_c, w_out, x_sample, nw, gb, state_mlstm_n, state_mlstm_m)

    y_p, c_p, n_p, m_p, conv_p = _prompt_call(x_prompt, wh, wg, wr, wsq, nw, gb, hnw, cw, fnw)
    m_p = jnp.swapaxes(m_p, 1, 2)
    c_s, cq = _sample_state_sc_call(state_mlstm_C, head_s, kw_s, fb_s)

    y_s, conv_s = _sample_tail_call(cq, head_s, g_s, gb, state_mlstm_n, state_mlstm_m,
                                    x_sample, rest_s, state_conv, wsq, hnw, cw, fnw)

    return (y_p, y_s, c_p, n_p, m_p, conv_p, c_s, n_s, m_s, conv_s)
```

```python
import jax
import jax.numpy as jnp
from jax import lax
from jax.experimental import pallas as pl
from jax.experimental.pallas import tpu as pltpu
from jax.experimental.pallas import tpu_sc as plsc

F32 = jnp.float32
BF16 = jnp.bfloat16

D_MODEL = 1024
NH = 4
DHV = 256
DQK = 128
D_QK = NH * DQK
D_M = NH * DHV
D_C = D_MODEL
CONV_W = 3
EPS = 1e-6
NEG_BIG = -1e30
K_SCALE = DQK ** -0.5

LANES = 128
SUBLANES = 8

OFF_Q = 0
OFF_K = OFF_Q + D_QK
OFF_V = OFF_K + D_QK
N_HEAD = OFF_V + D_M
N_GATE_COLS = 2 * NH
R_O = 0
R_ZM = R_O + D_M
R_BG = R_ZM + D_M
R_CG = R_BG + D_C
R_XC = R_CG + D_C
R_ZC = R_XC + D_C
R_GM = R_ZC + D_C
R_GC = R_GM + D_MODEL
N_REST = R_GC + D_MODEL

PROMPT_BLOCK = 512
MLSTM_CHUNK = 256
PROMPT_ORDER = "UUUUFUFUFUFU"
VMEM_LIMIT_PROMPT = 60 * 1024 * 1024
VMEM_LIMIT_SAMPLE = 60 * 1024 * 1024


def _sigmoid(x):
    return 0.5 * jnp.tanh(0.5 * x) + 0.5


def _silu(x):
    return x * _sigmoid(x)


def _log_sigmoid(x):
    return jnp.minimum(x, 0.0) - jnp.log1p(jnp.exp(-jnp.abs(x)))


def _rmsnorm(x, w):
    return x * lax.rsqrt(jnp.mean(x * x, axis=-1, keepdims=True) + EPS) * w


def _conv_tap(cw_ref, j):
    return cw_ref[:, j * D_C:(j + 1) * D_C]


def _dot(a, b):
    return jnp.dot(a, b, preferred_element_type=F32)


def _dot_nt(a, b):
    return lax.dot_general(a, b, (((1,), (1,)), ((), ())), preferred_element_type=F32)


def _dot_tn(a, b):
    return lax.dot_general(a, b, (((0,), (0,)), ((), ())), preferred_element_type=F32)


PREP_COLS = 1024
PREP_CHUNK = 256


PREP_STEPS = N_REST // PREP_COLS
PREP_HEAD_COLS = N_HEAD // PREP_STEPS
PREP_SQ_ROWS = D_MODEL // PREP_STEPS
N_SQUARE = 3


def _weight_prep_kernel(a_ref, c_ref, g_ref, pm_ref, pc_ref, po_ref, xs_ref, nw_ref, gb_ref, n_ref, m_ref,
                        wr_ref, wh_ref, wg_ref, wsq_ref,
                        hs_ref, sc_ref, rs_ref, kw_ref, fb_ref, no_ref, mo_ref,
                        xn_s, m8_s):
    j = pl.program_id(0)
    nseq = xs_ref.shape[0]

    @pl.when(j == 0)
    def _():
        xn_s[...] = _rmsnorm(xs_ref[:, 0, :], nw_ref[...]).astype(BF16)

    xn = xn_s[...]
    for r0 in range(0, PREP_COLS, PREP_CHUNK):
        w_t = a_ref[0, r0:r0 + PREP_CHUNK, :].T.astype(BF16)
        wr_ref[:, r0:r0 + PREP_CHUNK] = w_t
        rs_ref[:, r0:r0 + PREP_CHUNK] = _dot(xn, w_t)
    w_t = c_ref[0].T.astype(BF16)
    wh_ref[...] = w_t
    head = _dot(xn, w_t)
    hs_ref[...] = head

    @pl.when(j == 0)
    def _():
        lane = lax.broadcasted_iota(jnp.int32, (D_MODEL, LANES), 1)
        w_g = jnp.where(lane < N_GATE_COLS, g_ref[0].T, 0.0).astype(BF16)
        wg_ref[...] = w_g
        g8 = (_dot(xn, w_g) + gb_ref[...]).T[0:SUBLANES, :]
        lf = pltpu.roll(_log_sigmoid(g8), NH, axis=0)
        m8_s[0:NH, :] = m_ref[0]
        m8_s[NH:2 * NH, :] = m_ref[0]
        inter = lf + m8_s[...]
        m_t = jnp.maximum(inter, g8)
        mo_ref[0] = m_t[0:NH, :]
        row8 = lax.broadcasted_iota(jnp.int32, (SUBLANES, nseq), 0)
        w_f = jnp.where(row8 < NH, jnp.exp(g8 - m_t), pltpu.roll(jnp.exp(inter - m_t), NH, axis=0))
        tile = jnp.concatenate([w_f, jnp.exp(-m_t), jnp.zeros((LANES - 2 * SUBLANES, nseq), F32)], axis=0)
        sc_ref[...] = tile.T
        for h in range(NH):
            fb_ref[:, h * DQK:(h + 1) * DQK] = jnp.broadcast_to(sc_ref[:, NH + h:NH + h + 1], (nseq, DQK))

    heads_per_step = PREP_HEAD_COLS // DQK
    for step in range(D_QK // PREP_HEAD_COLS, 2 * D_QK // PREP_HEAD_COLS):
        @pl.when(j == step)
        def _(step=step):
            for i in range(heads_per_step):
                h = (step - D_QK // PREP_HEAD_COLS) * heads_per_step + i
                kw = sc_ref[:, h:h + 1] * (head[:, i * DQK:(i + 1) * DQK] * K_SCALE)
                kw_ref[:, h * DQK:(h + 1) * DQK] = kw
                no_ref[0, :, h, :] = sc_ref[:, NH + h:NH + h + 1] * n_ref[0, :, h, :] + kw

    wsq_ref[:, 0:D_MODEL] = pm_ref[0].astype(BF16)
    wsq_ref[:, D_MODEL:2 * D_MODEL] = pc_ref[0].astype(BF16)
    wsq_ref[:, 2 * D_MODEL:3 * D_MODEL] = po_ref[0].astype(BF16)


def _weight_prep_call(w_in_t, w_proj_m, w_proj_c, w_out, xs, nw, gb, n0, m0):
    nseq = xs.shape[0]
    row = jax.ShapeDtypeStruct((nseq, D_QK), F32)
    rest_row0 = N_HEAD + N_GATE_COLS
    const = lambda shape: pl.BlockSpec(shape, lambda j: (0,) * len(shape))
    sq_in = pl.BlockSpec((1, PREP_SQ_ROWS, D_MODEL), lambda j: (0, j, 0))
    sq_out = pl.BlockSpec((PREP_SQ_ROWS, N_SQUARE * D_MODEL), lambda j: (j, 0))
    sq_shape = jax.ShapeDtypeStruct((D_MODEL, N_SQUARE * D_MODEL), BF16)
    return pl.pallas_call(
        _weight_prep_kernel,
        grid=(PREP_STEPS,),
        in_specs=[
            pl.BlockSpec((pl.Element(1), pl.Element(PREP_COLS), pl.Element(D_MODEL)),
                         lambda j: (0, pl.multiple_of(rest_row0 + j * PREP_COLS, SUBLANES), 0)),
            pl.BlockSpec((1, PREP_HEAD_COLS, D_MODEL), lambda j: (0, j, 0)),
            pl.BlockSpec((1, LANES, D_MODEL), lambda j: (0, N_HEAD // LANES, 0)),
            sq_in, sq_in, sq_in,
            const((nseq, 1, D_MODEL)),
            const((1, D_MODEL)),
            const((1, LANES)),
            const(n0.shape),
            const(m0.shape),
        ],
        out_specs=(
            pl.BlockSpec((D_MODEL, PREP_COLS), lambda j: (0, j)),
            pl.BlockSpec((D_MODEL, PREP_HEAD_COLS), lambda j: (0, j)),
            const((D_MODEL, LANES)),
            sq_out,
            pl.BlockSpec((nseq, PREP_HEAD_COLS), lambda j: (0, j)),
            const((nseq, LANES)),
            pl.BlockSpec((nseq, PREP_COLS), lambda j: (0, j)),
            const(row.shape), const(row.shape), const(n0.shape), const(m0.shape),
        ),
        out_shape=(
            jax.ShapeDtypeStruct((D_MODEL, N_REST), BF16),
            jax.ShapeDtypeStruct((D_MODEL, N_HEAD), BF16),
            jax.ShapeDtypeStruct((D_MODEL, LANES), BF16),
            sq_shape,
            jax.ShapeDtypeStruct((nseq, N_HEAD), F32),
            jax.ShapeDtypeStruct((nseq, LANES), F32),
            jax.ShapeDtypeStruct((nseq, N_REST), F32),
            row,
            row,
            jax.ShapeDtypeStruct(n0.shape, F32),
            jax.ShapeDtypeStruct(m0.shape, F32),
        ),
        scratch_shapes=[pltpu.VMEM((nseq, D_MODEL), BF16),
                        pltpu.VMEM((SUBLANES, nseq), F32)],
        compiler_params=pltpu.CompilerParams(dimension_semantics=("arbitrary",),
                                             vmem_limit_bytes=VMEM_LIMIT_SAMPLE),
        name="weight_prep",
    )(w_in_t, w_in_t, w_in_t, w_proj_m, w_proj_c, w_out, xs, nw, gb, n0, m0)


def _prompt_kernel(x_ref, wh_ref, wg_ref, wr_ref, wpm_ref, wpc_ref, wout_ref,
                   nw_ref, gb_ref, hnw_ref, cw_ref, fnw_ref,
                   y_ref, c_ref, n_ref, mo_ref, conv_ref,
                   hg_s, ubuf_s, ct_s, m_ref):
    tl = x_ref.shape[1]
    seq_id = pl.program_id(0)
    l = pl.program_id(1)
    seq_lane = lax.broadcasted_iota(jnp.int32, (1, mo_ref.shape[2]), 1)

    @pl.when((seq_id == 0) & (l == 0))
    def _():
        mo_ref[...] = jnp.zeros_like(mo_ref)

    @pl.when(l == 0)
    def _():
        ct_s[...] = jnp.zeros_like(ct_s)
        n_ref[...] = jnp.zeros_like(n_ref)
        m_ref[...] = jnp.zeros_like(m_ref)
        ubuf_s[0:SUBLANES, :] = jnp.zeros((SUBLANES, D_C), F32)

    x = x_ref[0]
    xn = _rmsnorm(x, nw_ref[...]).astype(BF16)

    def rest(off, width):
        return _dot(xn, wr_ref[:, off:off + width])

    g = _dot(xn, wg_ref[...]) + gb_ref[...]
    gc = jnp.where(lax.broadcasted_iota(jnp.int32, (tl, LANES), 1) < NH, g, _log_sigmoid(g))
    qkv = _dot(xn, wh_ref[...])

    ch = MLSTM_CHUNK
    lane = lax.broadcasted_iota(jnp.int32, (ch, LANES), 1)
    row = lax.broadcasted_iota(jnp.int32, (ch, ch), 0)
    col = lax.broadcasted_iota(jnp.int32, (ch, ch), 1)
    causal = row >= col
    row_g = lax.broadcasted_iota(jnp.int32, (ch, LANES), 0)
    chunk_gates = []
    for r0 in range(0, tl, ch):
        gc_c = gc[r0:r0 + ch, :]
        bc = gc_c
        shift = 1
        while shift < ch:
            bc = bc + jnp.where(row_g >= shift, pltpu.roll(bc, shift, axis=0), 0.0)
            shift *= 2
        gt = jnp.where(lane < NH, gc_c, bc).T
        chunk_gates.append((gc_c, bc, gt))

    head_gates = {}

    def mlstm_unit(ci, h):
        r0 = ci * ch
        gc_c, bc, gt = chunk_gates[ci]
        if ci == 0:
            head_gates[h] = (_sigmoid(rest(R_O + h * DHV, DHV)), _silu(rest(R_ZM + h * DHV, DHV)))
        sig_o = head_gates[h][0][r0:r0 + ch, :]
        silu_z = head_gates[h][1][r0:r0 + ch, :]
        q_f = qkv[r0:r0 + ch, OFF_Q + h * DQK:OFF_Q + (h + 1) * DQK]
        k_f = qkv[r0:r0 + ch, OFF_K + h * DQK:OFF_K + (h + 1) * DQK] * K_SCALE
        v_f = qkv[r0:r0 + ch, OFF_V + h * DHV:OFF_V + (h + 1) * DHV]
        q_b = q_f.astype(BF16)
        k_b = k_f.astype(BF16)
        v_b = v_f.astype(BF16)
        ct_old = ct_s[h]
        n_old = n_ref[0, 0, h:h + 1, :]
        m_old = m_ref[h:h + 1, 0:1]

        b_c = bc[:, NH + h:NH + h + 1]
        ig_c = gc_c[:, h:h + 1]
        ig_r = gt[h:h + 1, :]
        b_r = gt[NH + h:NH + h + 1, :]
        log_d = jnp.where(causal, b_c - b_r + ig_r, NEG_BIG)
        inter = b_c + m_old
        m_t = jnp.maximum(inter, jnp.max(log_d, axis=-1, keepdims=True))
        d_m = jnp.exp(log_d - m_t)
        w_int = jnp.exp(inter - m_t)
        s = _dot_nt(q_b, k_b) * d_m
        num = w_int * _dot(q_b, ct_old.astype(BF16)) + _dot(s.astype(BF16), v_b)
        den = w_int * jnp.sum(q_f * n_old, axis=-1, keepdims=True) + jnp.sum(s, axis=-1, keepdims=True)
        h_t = num / jnp.maximum(jnp.abs(den), jnp.exp(-m_t))

        b_end = b_c[ch - 1:ch, :]
        inter_end = b_end + m_old
        m_new = jnp.maximum(inter_end, jnp.max(b_end - b_r + ig_r, axis=-1, keepdims=True))
        w_end = jnp.exp(b_end - b_c + ig_c - m_new)
        f_end = jnp.exp(inter_end - m_new)
        ct_s[h] = f_end * ct_old + _dot_tn(k_b, (w_end * v_f).astype(BF16))
        n_ref[0, 0, h:h + 1, :] = f_end * n_old + jnp.sum(w_end * k_f, axis=0, keepdims=True)
        m_ref[h:h + 1, :] = jnp.broadcast_to(m_new, (1, LANES))
        mo_ref[0, h:h + 1, :] = jnp.where(seq_lane == seq_id, m_new, mo_ref[0, h:h + 1, :])

        hn = h_t * lax.rsqrt(jnp.mean(h_t * h_t, axis=-1, keepdims=True) + EPS)
        hn = hn * hnw_ref[:, h * DHV:(h + 1) * DHV]
        hg_s[r0:r0 + ch, h * DHV:(h + 1) * DHV] = (hn * sig_o * silu_z).astype(BF16)

    val = {}

    def conv_input():
        u = rest(R_CG, D_C) * rest(R_XC, D_C)
        ubuf_s[SUBLANES:SUBLANES + tl, :] = u
        cv = _conv_tap(cw_ref, 0) * ubuf_s[SUBLANES - 2:SUBLANES - 2 + tl, :]
        cv = cv + _conv_tap(cw_ref, 1) * ubuf_s[SUBLANES - 1:SUBLANES - 1 + tl, :]
        val["cv"] = cv + _conv_tap(cw_ref, 2) * u

    def conv_gate():
        val["yc_in"] = (_silu(rest(R_ZC, D_C)) * rest(R_BG, D_C) * val["cv"]).astype(BF16)

    def conv_proj():
        val["gy_c"] = _sigmoid(rest(R_GC, D_MODEL)) * _dot(val["yc_in"], wpc_ref[...])

    def merge_gate():
        val["sig_gm"] = _sigmoid(rest(R_GM, D_MODEL))

    fillers = iter([conv_input, conv_gate, conv_proj, merge_gate])
    units = iter([(ci, h) for ci in range(tl // ch) for h in range(NH)])
    for step in PROMPT_ORDER:
        if step == "U":
            mlstm_unit(*next(units))
        else:
            next(fillers)()

    ubuf_s[0:SUBLANES, :] = ubuf_s[tl:tl + SUBLANES, :]

    conv_ref[0, 0] = ubuf_s[SUBLANES - (CONV_W - 1):SUBLANES, :]
    for h in range(NH):
        c_ref[0, 0, h] = ct_s[h].T

    y_m = _dot(hg_s[...], wpm_ref[...])
    mix = (val["sig_gm"] * y_m + val["gy_c"]).astype(BF16)
    out = x + _dot(mix, wout_ref[...])
    y_ref[0] = _rmsnorm(out, fnw_ref[...])


def _resident(shape, block_index=None):
    index = (0,) * len(shape) if block_index is None else block_index
    return pl.BlockSpec(shape, lambda *_: index, pipeline_mode=pl.Buffered(1))


def _prompt_call(x, wh, wg, wr, wsq, nw, gb, hnw, cw, fnw):
    bsz, seq, _ = x.shape
    tl = PROMPT_BLOCK
    grid = (bsz, seq // tl)
    out_shape = (
        jax.ShapeDtypeStruct((bsz, seq, D_MODEL), F32),
        jax.ShapeDtypeStruct((1, bsz, NH, DHV, DQK), F32),
        jax.ShapeDtypeStruct((1, bsz, NH, DQK), F32),
        jax.ShapeDtypeStruct((1, NH, bsz), F32),
        jax.ShapeDtypeStruct((1, bsz, CONV_W - 1, D_C), F32),
    )
    in_specs = [
        pl.BlockSpec((1, tl, D_MODEL), lambda b, l: (b, l, 0)),
        _resident((D_MODEL, N_HEAD)),
        _resident((D_MODEL, LANES)),
        _resident((D_MODEL, N_REST)),
        _resident((D_M, D_MODEL), (0, 0)),
        _resident((D_C, D_MODEL), (0, 1)),
        _resident((D_MODEL, D_MODEL), (0, 2)),
        _resident((1, D_MODEL)),
        _resident((1, LANES)),
        _resident((1, D_M)),
        _resident((1, CONV_W * D_C)),
        _resident((1, D_MODEL)),
    ]
    out_specs = (
        pl.BlockSpec((1, tl, D_MODEL), lambda b, l: (b, l, 0)),
        pl.BlockSpec((1, 1, NH, DHV, DQK), lambda b, l: (0, b, 0, 0, 0)),
        pl.BlockSpec((1, 1, NH, DQK), lambda b, l: (0, b, 0, 0)),
        pl.BlockSpec((1, NH, bsz), lambda b, l: (0, 0, 0)),
        pl.BlockSpec((1, 1, CONV_W - 1, D_C), lambda b, l: (0, b, 0, 0)),
    )
    return pl.pallas_call(
        _prompt_kernel,
        grid=grid,
        in_specs=in_specs,
        out_specs=out_specs,
        out_shape=out_shape,
        scratch_shapes=[
            pltpu.VMEM((tl, D_M), BF16),
            pltpu.VMEM((tl + 2 * SUBLANES, D_C), F32),
            pltpu.VMEM((NH, DQK, DHV), F32),
            pltpu.VMEM((SUBLANES, LANES), F32),
        ],
        compiler_params=pltpu.CompilerParams(
            dimension_semantics=("arbitrary", "arbitrary"),
            vmem_limit_bytes=VMEM_LIMIT_PROMPT),
        name="prompt_layer",
    )(x, wh, wg, wr, wsq, wsq, wsq, nw, gb, hnw, cw, fnw)


SC_LANES = 16
SC_CORES = 2
SC_SUBCORES = 16


def _sample_state_sc_call(c0, head_s, kw_s, fb_s):
    nseq = c0.shape[1]
    nworkers = SC_CORES * SC_SUBCORES
    per_worker = nseq * NH // nworkers
    nvec = DQK // SC_LANES
    mesh = plsc.VectorSubcoreMesh(core_axis_name="c", subcore_axis_name="s")

    def body(c_hbm, hs_hbm, kw_hbm, fb_hbm, co_hbm, cq_hbm, c_v, q_v, kw_v, v_v, f_v, cq_v):
        wid = lax.axis_index("c") * SC_SUBCORES + lax.axis_index("s")
        lane = lax.iota(jnp.int32, SC_LANES)
        zero_i = jnp.zeros((SC_LANES,), jnp.int32)

        @pl.loop(0, per_worker)
        def _(t):
            pair = wid * per_worker + t
            b = pair // NH
            h = pair % NH
            pltpu.sync_copy(c_hbm.at[0, b, h], c_v)
            pltpu.sync_copy(hs_hbm.at[b, pl.ds(OFF_Q + h * DQK, DQK)], q_v)
            pltpu.sync_copy(hs_hbm.at[b, pl.ds(OFF_V + h * DHV, DHV)], v_v)
            pltpu.sync_copy(kw_hbm.at[b, pl.ds(h * DQK, DQK)], kw_v)
            pltpu.sync_copy(fb_hbm.at[b, pl.ds(h * DQK, DQK)], f_v)
            q = [q_v[pl.ds(SC_LANES * j, SC_LANES)] for j in range(nvec)]
            kw = [kw_v[pl.ds(SC_LANES * j, SC_LANES)] for j in range(nvec)]
            f_vec = f_v[pl.ds(0, SC_LANES)]

            @pl.loop(0, DHV // SC_LANES)
            def _(g):
                cq_vec = jnp.zeros((SC_LANES,), F32)
                for i in range(SC_LANES):
                    r = g * SC_LANES + i
                    v_r = plsc.load_gather(v_v, [zero_i + r])
                    acc = jnp.zeros((SC_LANES,), F32)
                    for j in range(nvec):
                        c = c_v[r, pl.ds(SC_LANES * j, SC_LANES)]
                        acc = acc + c * q[j]
                        c_v[r, pl.ds(SC_LANES * j, SC_LANES)] = f_vec * c + v_r * kw[j]
                    cq_vec = jnp.where(lane == i, jnp.sum(acc), cq_vec)
                cq_v[pl.ds(g * SC_LANES, SC_LANES)] = cq_vec

            pltpu.sync_copy(c_v, co_hbm.at[0, b, h])
            pltpu.sync_copy(cq_v, cq_hbm.at[b, pl.ds(h * DHV, DHV)])

    return pl.kernel(
        body,
        out_type=(jax.ShapeDtypeStruct(c0.shape, F32),
                  jax.ShapeDtypeStruct((nseq, D_M), F32)),
        mesh=mesh,
        scratch_types=[pltpu.VMEM((DHV, DQK), F32),
                       pltpu.VMEM((DQK,), F32),
                       pltpu.VMEM((DQK,), F32),
                       pltpu.VMEM((DHV,), F32),
                       pltpu.VMEM((DQK,), F32),
                       pltpu.VMEM((DHV,), F32)],
        compiler_params=pltpu.CompilerParams(use_tc_tiling_on_sc=True, needs_layout_passes=False),
        name="sample_state_sc",
    )(c0, head_s, kw_s, fb_s)


def _sample_tail_kernel(cq_ref, hs_ref, sc_ref, n_ref,
                        x_ref, r_ref, conv_ref, wpm_ref, wpc_ref, wout_ref, hnw_ref, cw_ref, fnw_ref,
                        y_ref, convo_ref):
    def piece(off, width):
        return r_ref[:, off:off + width]

    w_end = sc_ref[:, 0:NH]
    f_end = sc_ref[:, NH:2 * NH]
    floor = sc_ref[:, 2 * NH:3 * NH]
    for h in range(NH):
        q_h = hs_ref[:, OFF_Q + h * DQK:OFF_Q + (h + 1) * DQK]
        k_h = hs_ref[:, OFF_K + h * DQK:OFF_K + (h + 1) * DQK] * K_SCALE
        v_h = hs_ref[:, OFF_V + h * DHV:OFF_V + (h + 1) * DHV]
        n_h = n_ref[0, :, h, :]
        w_h = w_end[:, h:h + 1]
        f_h = f_end[:, h:h + 1]
        s = jnp.sum(q_h * k_h, axis=-1, keepdims=True) * w_h
        den = f_h * jnp.sum(n_h * q_h, axis=-1, keepdims=True) + s
        num = f_h * cq_ref[:, h * DHV:(h + 1) * DHV] + s * v_h
        h_t = num / jnp.maximum(jnp.abs(den), floor[:, h:h + 1])
        hn = h_t * lax.rsqrt(jnp.mean(h_t * h_t, axis=-1, keepdims=True) + EPS)
        hn = hn * hnw_ref[:, h * DHV:(h + 1) * DHV]
        o_h = piece(R_O + h * DHV, DHV)
        zm_h = piece(R_ZM + h * DHV, DHV)
        hg = (hn * _sigmoid(o_h) * _silu(zm_h)).astype(BF16)
        part = _dot(hg, wpm_ref[h * DHV:(h + 1) * DHV, :])
        y_m = part if h == 0 else y_m + part

    u = piece(R_CG, D_C) * piece(R_XC, D_C)
    conv_old = conv_ref[0, :, 1, :]
    cv = _conv_tap(cw_ref, 0) * conv_ref[0, :, 0, :]
    cv = cv + _conv_tap(cw_ref, 1) * conv_old
    cv = cv + _conv_tap(cw_ref, 2) * u
    convo_ref[0, :, 0, :] = conv_old
    convo_ref[0, :, 1, :] = u
    yc_in = (_silu(piece(R_ZC, D_C)) * piece(R_BG, D_C) * cv).astype(BF16)
    y_c = _dot(yc_in, wpc_ref[...])
    mix = (_sigmoid(piece(R_GM, D_MODEL)) * y_m + _sigmoid(piece(R_GC, D_MODEL)) * y_c).astype(BF16)
    out = x_ref[:, 0, :] + _dot(mix, wout_ref[...])
    y_ref[:, 0, :] = _rmsnorm(out, fnw_ref[...])


def _sample_tail_call(cq, head_s, sc, n0, x, rest, conv0, wsq, hnw, cw, fnw):
    nseq = x.shape[0]
    args = (cq, head_s, sc, n0, x, rest, conv0, wsq, wsq, wsq, hnw, cw, fnw)
    full = lambda a: pl.BlockSpec(a.shape, lambda i: (0,) * a.ndim)
    square = lambda k: pl.BlockSpec((D_MODEL, D_MODEL), lambda i: (0, k))
    specs = [full(a) for a in args]
    specs[7:10] = [square(0), square(1), square(2)]
    return pl.pallas_call(
        _sample_tail_kernel,
        grid=(1,),
        in_specs=specs,
        out_specs=(pl.BlockSpec((nseq, 1, D_MODEL), lambda i: (0, 0, 0)),
                   pl.BlockSpec((1, nseq, CONV_W - 1, D_C), lambda i: (0, 0, 0, 0))),
        out_shape=(jax.ShapeDtypeStruct((nseq, 1, D_MODEL), F32),
                   jax.ShapeDtypeStruct((1, nseq, CONV_W - 1, D_C), F32)),
        compiler_params=pltpu.CompilerParams(dimension_semantics=("arbitrary",),
                                             vmem_limit_bytes=VMEM_LIMIT_SAMPLE),
        name="sample_tail",
    )(*args)


def kernel(x_prompt, x_sample, state_mlstm_C, state_mlstm_n, state_mlstm_m, state_conv, norm_w, w_in, b_i, b_f,
           head_norm_w, conv_w, w_proj_m, w_proj_c, w_out, final_norm_w):
    depth = norm_w.shape[0]
    assert depth == 1, "single-layer trunk"
    bsz = x_prompt.shape[0]
    nseq = x_sample.shape[0]

    assert w_in.shape == (1, D_MODEL, N_HEAD + N_GATE_COLS + N_REST)
    gb = jnp.pad(jnp.concatenate([b_i[0], b_f[0]]), (0, LANES - N_GATE_COLS)).reshape(1, LANES)
    nw = norm_w[0].reshape(1, D_MODEL)
    hnw = head_norm_w[0].reshape(1, D_M)
    cw = conv_w.reshape(1, CONV_W * D_C)
    fnw = final_norm_w.reshape(1, D_MODEL)
    m0_t = jnp.swapaxes(state_mlstm_m, 1, 2)
    (wr, wh, wg, wsq, head_s, sc_s, rest_s, kw_s, fb_s, n_s, m_s_t) = _weight_prep_call(
        jnp.swapaxes(w_in, 1, 2), w_proj_m, w_proj_c, w_out, x_sample, nw, gb, state_mlstm_n, m0_t)
    m_s = jnp.swapaxes(m_s_t, 1, 2)

    y_p, c_p, n_p, m_p, conv_p = _prompt_call(x_prompt, wh, wg, wr, wsq, nw, gb, hnw, cw, fnw)
    m_p = jnp.swapaxes(m_p, 1, 2)
    c_s, cq = _sample_state_sc_call(state_mlstm_C, head_s, kw_s, fb_s)

    y_s, conv_s = _sample_tail_call(cq, head_s, sc_s, state_mlstm_n,
                                    x_sample, rest_s, state_conv, wsq, hnw, cw, fnw)

    return (y_p, y_s, c_p, n_p, m_p, conv_p, c_s, n_s, m_s, conv_s)
```

```python
import jax
import jax.numpy as jnp
from jax import lax
from jax.experimental import pallas as pl
from jax.experimental.pallas import tpu as pltpu
from jax.experimental.pallas import tpu_sc as plsc

F32 = jnp.float32
BF16 = jnp.bfloat16

D_MODEL = 1024
NH = 4
DHV = 256
DQK = 128
D_QK = NH * DQK
D_M = NH * DHV
D_C = D_MODEL
CONV_W = 3
EPS = 1e-6
NEG_BIG = -1e30
K_SCALE = DQK ** -0.5

LANES = 128
SUBLANES = 8

OFF_Q = 0
OFF_K = OFF_Q + D_QK
OFF_V = OFF_K + D_QK
N_HEAD = OFF_V + D_M
N_GATE_COLS = 2 * NH
R_O = 0
R_ZM = R_O + D_M
R_BG = R_ZM + D_M
R_CG = R_BG + D_C
R_XC = R_CG + D_C
R_ZC = R_XC + D_C
R_GM = R_ZC + D_C
R_GC = R_GM + D_MODEL
N_REST = R_GC + D_MODEL

PROMPT_BLOCK = 512
MLSTM_CHUNK = 256
PROMPT_ORDER = "UUUUFUFUFUFU"
VMEM_LIMIT_PROMPT = 60 * 1024 * 1024
VMEM_LIMIT_SAMPLE = 60 * 1024 * 1024


def _sigmoid(x):
    return 0.5 * jnp.tanh(0.5 * x) + 0.5


def _silu(x):
    return x * _sigmoid(x)


def _log_sigmoid(x):
    return jnp.minimum(x, 0.0) - jnp.log1p(jnp.exp(-jnp.abs(x)))


def _rmsnorm(x, w):
    return x * lax.rsqrt(jnp.mean(x * x, axis=-1, keepdims=True) + EPS) * w


def _conv_tap(cw_ref, j):
    return cw_ref[:, j * D_C:(j + 1) * D_C]


def _dot(a, b):
    return jnp.dot(a, b, preferred_element_type=F32)


def _dot_nt(a, b):
    return lax.dot_general(a, b, (((1,), (1,)), ((), ())), preferred_element_type=F32)


def _dot_tn(a, b):
    return lax.dot_general(a, b, (((0,), (0,)), ((), ())), preferred_element_type=F32)


PREP_COLS = 1024
PREP_CHUNK = 256


PREP_STEPS = N_REST // PREP_COLS
PREP_HEAD_COLS = N_HEAD // PREP_STEPS
PREP_SQ_ROWS = D_MODEL // PREP_STEPS
N_SQUARE = 3


def _weight_prep_kernel(a_ref, c_ref, g_ref, pm_ref, pc_ref, po_ref, xs_ref, nw_ref, bi_ref, bf_ref, n_ref, m_ref,
                        wr_ref, wh_ref, wg_ref, wsq_ref,
                        hs_ref, sc_ref, rs_ref, kw_ref, fb_ref, no_ref, mo_ref, gb_ref,
                        xn_s, m8_s):
    j = pl.program_id(0)
    nseq = xs_ref.shape[0]

    @pl.when(j == 0)
    def _():
        xn_s[...] = _rmsnorm(xs_ref[:, 0, :], nw_ref[...]).astype(BF16)

    xn = xn_s[...]
    for r0 in range(0, PREP_COLS, PREP_CHUNK):
        w_t = a_ref[0, r0:r0 + PREP_CHUNK, :].T.astype(BF16)
        wr_ref[:, r0:r0 + PREP_CHUNK] = w_t
        rs_ref[:, r0:r0 + PREP_CHUNK] = _dot(xn, w_t)
    w_t = c_ref[0].T.astype(BF16)
    wh_ref[...] = w_t
    head = _dot(xn, w_t)
    hs_ref[...] = head

    @pl.when(j == 0)
    def _():
        lane = lax.broadcasted_iota(jnp.int32, (D_MODEL, LANES), 1)
        w_g = jnp.where(lane < N_GATE_COLS, g_ref[0].T, 0.0).astype(BF16)
        wg_ref[...] = w_g
        gb_ref[...] = jnp.concatenate(
            [bi_ref[...], bf_ref[...], jnp.zeros((1, LANES - N_GATE_COLS), F32)], axis=1)
        g8 = (_dot(xn, w_g) + gb_ref[...]).T[0:SUBLANES, :]
        lf = pltpu.roll(_log_sigmoid(g8), NH, axis=0)
        m8_s[0:NH, :] = m_ref[0]
        m8_s[NH:2 * NH, :] = m_ref[0]
        inter = lf + m8_s[...]
        m_t = jnp.maximum(inter, g8)
        mo_ref[0] = m_t[0:NH, :]
        row8 = lax.broadcasted_iota(jnp.int32, (SUBLANES, nseq), 0)
        w_f = jnp.where(row8 < NH, jnp.exp(g8 - m_t), pltpu.roll(jnp.exp(inter - m_t), NH, axis=0))
        tile = jnp.concatenate([w_f, jnp.exp(-m_t), jnp.zeros((LANES - 2 * SUBLANES, nseq), F32)], axis=0)
        sc_ref[...] = tile.T
        for h in range(NH):
            fb_ref[:, h * DQK:(h + 1) * DQK] = jnp.broadcast_to(sc_ref[:, NH + h:NH + h + 1], (nseq, DQK))

    heads_per_step = PREP_HEAD_COLS // DQK
    for step in range(D_QK // PREP_HEAD_COLS, 2 * D_QK // PREP_HEAD_COLS):
        @pl.when(j == step)
        def _(step=step):
            for i in range(heads_per_step):
                h = (step - D_QK // PREP_HEAD_COLS) * heads_per_step + i
                kw = sc_ref[:, h:h + 1] * (head[:, i * DQK:(i + 1) * DQK] * K_SCALE)
                kw_ref[:, h * DQK:(h + 1) * DQK] = kw
                no_ref[0, :, h, :] = sc_ref[:, NH + h:NH + h + 1] * n_ref[0, :, h, :] + kw

    wsq_ref[:, 0:D_MODEL] = pm_ref[0].astype(BF16)
    wsq_ref[:, D_MODEL:2 * D_MODEL] = pc_ref[0].astype(BF16)
    wsq_ref[:, 2 * D_MODEL:3 * D_MODEL] = po_ref[0].astype(BF16)


def _weight_prep_call(w_in_t, w_proj_m, w_proj_c, w_out, xs, nw, b_i, b_f, n0, m0):
    nseq = xs.shape[0]
    row = jax.ShapeDtypeStruct((nseq, D_QK), F32)
    rest_row0 = N_HEAD + N_GATE_COLS
    const = lambda shape: pl.BlockSpec(shape, lambda j: (0,) * len(shape))
    sq_in = pl.BlockSpec((1, PREP_SQ_ROWS, D_MODEL), lambda j: (0, j, 0))
    sq_out = pl.BlockSpec((PREP_SQ_ROWS, N_SQUARE * D_MODEL), lambda j: (j, 0))
    sq_shape = jax.ShapeDtypeStruct((D_MODEL, N_SQUARE * D_MODEL), BF16)
    return pl.pallas_call(
        _weight_prep_kernel,
        grid=(PREP_STEPS,),
        in_specs=[
            pl.BlockSpec((pl.Element(1), pl.Element(PREP_COLS), pl.Element(D_MODEL)),
                         lambda j: (0, pl.multiple_of(rest_row0 + j * PREP_COLS, SUBLANES), 0)),
            pl.BlockSpec((1, PREP_HEAD_COLS, D_MODEL), lambda j: (0, j, 0)),
            pl.BlockSpec((1, LANES, D_MODEL), lambda j: (0, N_HEAD // LANES, 0)),
            sq_in, sq_in, sq_in,
            const((nseq, 1, D_MODEL)),
            const((1, D_MODEL)),
            const((1, NH)),
            const((1, NH)),
            const(n0.shape),
            const(m0.shape),
        ],
        out_specs=(
            pl.BlockSpec((D_MODEL, PREP_COLS), lambda j: (0, j)),
            pl.BlockSpec((D_MODEL, PREP_HEAD_COLS), lambda j: (0, j)),
            const((D_MODEL, LANES)),
            sq_out,
            pl.BlockSpec((nseq, PREP_HEAD_COLS), lambda j: (0, j)),
            const((nseq, LANES)),
            pl.BlockSpec((nseq, PREP_COLS), lambda j: (0, j)),
            const(row.shape), const(row.shape), const(n0.shape), const(m0.shape), const((1, LANES)),
        ),
        out_shape=(
            jax.ShapeDtypeStruct((D_MODEL, N_REST), BF16),
            jax.ShapeDtypeStruct((D_MODEL, N_HEAD), BF16),
            jax.ShapeDtypeStruct((D_MODEL, LANES), BF16),
            sq_shape,
            jax.ShapeDtypeStruct((nseq, N_HEAD), F32),
            jax.ShapeDtypeStruct((nseq, LANES), F32),
            jax.ShapeDtypeStruct((nseq, N_REST), F32),
            row,
            row,
            jax.ShapeDtypeStruct(n0.shape, F32),
            jax.ShapeDtypeStruct(m0.shape, F32),
            jax.ShapeDtypeStruct((1, LANES), F32),
        ),
        scratch_shapes=[pltpu.VMEM((nseq, D_MODEL), BF16),
                        pltpu.VMEM((SUBLANES, nseq), F32)],
        compiler_params=pltpu.CompilerParams(dimension_semantics=("arbitrary",),
                                             vmem_limit_bytes=VMEM_LIMIT_SAMPLE),
        name="weight_prep",
    )(w_in_t, w_in_t, w_in_t, w_proj_m, w_proj_c, w_out, xs, nw, b_i, b_f, n0, m0)


def _prompt_kernel(x_ref, wh_ref, wg_ref, wr_ref, wpm_ref, wpc_ref, wout_ref,
                   nw_ref, gb_ref, hnw_ref, cw_ref, fnw_ref,
                   y_ref, c_ref, n_ref, mo_ref, conv_ref,
                   hg_s, ubuf_s, ct_s, m_ref):
    tl = x_ref.shape[1]
    seq_id = pl.program_id(0)
    l = pl.program_id(1)
    seq_lane = lax.broadcasted_iota(jnp.int32, (1, mo_ref.shape[2]), 1)

    @pl.when((seq_id == 0) & (l == 0))
    def _():
        mo_ref[...] = jnp.zeros_like(mo_ref)

    @pl.when(l == 0)
    def _():
        ct_s[...] = jnp.zeros_like(ct_s)
        n_ref[...] = jnp.zeros_like(n_ref)
        m_ref[...] = jnp.zeros_like(m_ref)
        ubuf_s[0:SUBLANES, :] = jnp.zeros((SUBLANES, D_C), F32)

    x = x_ref[0]
    xn = _rmsnorm(x, nw_ref[...]).astype(BF16)

    def rest(off, width):
        return _dot(xn, wr_ref[:, off:off + width])

    g = _dot(xn, wg_ref[...]) + gb_ref[...]
    gc = jnp.where(lax.broadcasted_iota(jnp.int32, (tl, LANES), 1) < NH, g, _log_sigmoid(g))
    qkv = _dot(xn, wh_ref[...])

    ch = MLSTM_CHUNK
    lane = lax.broadcasted_iota(jnp.int32, (ch, LANES), 1)
    row = lax.broadcasted_iota(jnp.int32, (ch, ch), 0)
    col = lax.broadcasted_iota(jnp.int32, (ch, ch), 1)
    causal = row >= col
    row_g = lax.broadcasted_iota(jnp.int32, (ch, LANES), 0)
    chunk_gates = []
    for r0 in range(0, tl, ch):
        gc_c = gc[r0:r0 + ch, :]
        bc = gc_c
        shift = 1
        while shift < ch:
            bc = bc + jnp.where(row_g >= shift, pltpu.roll(bc, shift, axis=0), 0.0)
            shift *= 2
        gt = jnp.where(lane < NH, gc_c, bc).T
        chunk_gates.append((gc_c, bc, gt))

    head_gates = {}

    def mlstm_unit(ci, h):
        r0 = ci * ch
        gc_c, bc, gt = chunk_gates[ci]
        if ci == 0:
            head_gates[h] = (_sigmoid(rest(R_O + h * DHV, DHV)), _silu(rest(R_ZM + h * DHV, DHV)))
        sig_o = head_gates[h][0][r0:r0 + ch, :]
        silu_z = head_gates[h][1][r0:r0 + ch, :]
        q_f = qkv[r0:r0 + ch, OFF_Q + h * DQK:OFF_Q + (h + 1) * DQK]
        k_f = qkv[r0:r0 + ch, OFF_K + h * DQK:OFF_K + (h + 1) * DQK] * K_SCALE
        v_f = qkv[r0:r0 + ch, OFF_V + h * DHV:OFF_V + (h + 1) * DHV]
        q_b = q_f.astype(BF16)
        k_b = k_f.astype(BF16)
        v_b = v_f.astype(BF16)
        ct_old = ct_s[h]
        n_old = n_ref[0, 0, h:h + 1, :]
        m_old = m_ref[h:h + 1, 0:1]

        b_c = bc[:, NH + h:NH + h + 1]
        ig_c = gc_c[:, h:h + 1]
        ig_r = gt[h:h + 1, :]
        b_r = gt[NH + h:NH + h + 1, :]
        log_d = jnp.where(causal, b_c - b_r + ig_r, NEG_BIG)
        inter = b_c + m_old
        m_t = jnp.maximum(inter, jnp.max(log_d, axis=-1, keepdims=True))
        d_m = jnp.exp(log_d - m_t)
        w_int = jnp.exp(inter - m_t)
        s = _dot_nt(q_b, k_b) * d_m
        num = w_int * _dot(q_b, ct_old.astype(BF16)) + _dot(s.astype(BF16), v_b)
        den = w_int * jnp.sum(q_f * n_old, axis=-1, keepdims=True) + jnp.sum(s, axis=-1, keepdims=True)
        h_t = num / jnp.maximum(jnp.abs(den), jnp.exp(-m_t))

        b_end = b_c[ch - 1:ch, :]
        inter_end = b_end + m_old
        m_new = jnp.maximum(inter_end, jnp.max(b_end - b_r + ig_r, axis=-1, keepdims=True))
        w_end = jnp.exp(b_end - b_c + ig_c - m_new)
        f_end = jnp.exp(inter_end - m_new)
        ct_s[h] = f_end * ct_old + _dot_tn(k_b, (w_end * v_f).astype(BF16))
        n_ref[0, 0, h:h + 1, :] = f_end * n_old + jnp.sum(w_end * k_f, axis=0, keepdims=True)
        m_ref[h:h + 1, :] = jnp.broadcast_to(m_new, (1, LANES))
        mo_ref[0, h:h + 1, :] = jnp.where(seq_lane == seq_id, m_new, mo_ref[0, h:h + 1, :])

        hn = h_t * lax.rsqrt(jnp.mean(h_t * h_t, axis=-1, keepdims=True) + EPS)
        hn = hn * hnw_ref[:, h * DHV:(h + 1) * DHV]
        hg_s[r0:r0 + ch, h * DHV:(h + 1) * DHV] = (hn * sig_o * silu_z).astype(BF16)

    val = {}

    def conv_input():
        u = rest(R_CG, D_C) * rest(R_XC, D_C)
        ubuf_s[SUBLANES:SUBLANES + tl, :] = u
        cv = _conv_tap(cw_ref, 0) * ubuf_s[SUBLANES - 2:SUBLANES - 2 + tl, :]
        cv = cv + _conv_tap(cw_ref, 1) * ubuf_s[SUBLANES - 1:SUBLANES - 1 + tl, :]
        val["cv"] = cv + _conv_tap(cw_ref, 2) * u

    def conv_gate():
        val["yc_in"] = (_silu(rest(R_ZC, D_C)) * rest(R_BG, D_C) * val["cv"]).astype(BF16)

    def conv_proj():
        val["gy_c"] = _sigmoid(rest(R_GC, D_MODEL)) * _dot(val["yc_in"], wpc_ref[...])

    def merge_gate():
        val["sig_gm"] = _sigmoid(rest(R_GM, D_MODEL))

    fillers = iter([conv_input, conv_gate, conv_proj, merge_gate])
    units = iter([(ci, h) for ci in range(tl // ch) for h in range(NH)])
    for step in PROMPT_ORDER:
        if step == "U":
            mlstm_unit(*next(units))
        else:
            next(fillers)()

    ubuf_s[0:SUBLANES, :] = ubuf_s[tl:tl + SUBLANES, :]

    conv_ref[0, 0] = ubuf_s[SUBLANES - (CONV_W - 1):SUBLANES, :]
    for h in range(NH):
        c_ref[0, 0, h] = ct_s[h].T

    y_m = _dot(hg_s[...], wpm_ref[...])
    mix = (val["sig_gm"] * y_m + val["gy_c"]).astype(BF16)
    out = x + _dot(mix, wout_ref[...])
    y_ref[0] = _rmsnorm(out, fnw_ref[...])


def _resident(shape, block_index=None):
    index = (0,) * len(shape) if block_index is None else block_index
    return pl.BlockSpec(shape, lambda *_: index, pipeline_mode=pl.Buffered(1))


def _prompt_call(x, wh, wg, wr, wsq, nw, gb, hnw, cw, fnw):
    bsz, seq, _ = x.shape
    tl = PROMPT_BLOCK
    grid = (bsz, seq // tl)
    out_shape = (
        jax.ShapeDtypeStruct((bsz, seq, D_MODEL), F32),
        jax.ShapeDtypeStruct((1, bsz, NH, DHV, DQK), F32),
        jax.ShapeDtypeStruct((1, bsz, NH, DQK), F32),
        jax.ShapeDtypeStruct((1, NH, bsz), F32),
        jax.ShapeDtypeStruct((1, bsz, CONV_W - 1, D_C), F32),
    )
    in_specs = [
        pl.BlockSpec((1, tl, D_MODEL), lambda b, l: (b, l, 0)),
        _resident((D_MODEL, N_HEAD)),
        _resident((D_MODEL, LANES)),
        _resident((D_MODEL, N_REST)),
        _resident((D_M, D_MODEL), (0, 0)),
        _resident((D_C, D_MODEL), (0, 1)),
        _resident((D_MODEL, D_MODEL), (0, 2)),
        _resident((1, D_MODEL)),
        _resident((1, LANES)),
        _resident((1, D_M)),
        _resident((1, CONV_W * D_C)),
        _resident((1, D_MODEL)),
    ]
    out_specs = (
        pl.BlockSpec((1, tl, D_MODEL), lambda b, l: (b, l, 0)),
        pl.BlockSpec((1, 1, NH, DHV, DQK), lambda b, l: (0, b, 0, 0, 0)),
        pl.BlockSpec((1, 1, NH, DQK), lambda b, l: (0, b, 0, 0)),
        pl.BlockSpec((1, NH, bsz), lambda b, l: (0, 0, 0)),
        pl.BlockSpec((1, 1, CONV_W - 1, D_C), lambda b, l: (0, b, 0, 0)),
    )
    return pl.pallas_call(
        _prompt_kernel,
        grid=grid,
        in_specs=in_specs,
        out_specs=out_specs,
        out_shape=out_shape,
        scratch_shapes=[
            pltpu.VMEM((tl, D_M), BF16),
            pltpu.VMEM((tl + 2 * SUBLANES, D_C), F32),
            pltpu.VMEM((NH, DQK, DHV), F32),
            pltpu.VMEM((SUBLANES, LANES), F32),
        ],
        compiler_params=pltpu.CompilerParams(
            dimension_semantics=("arbitrary", "arbitrary"),
            vmem_limit_bytes=VMEM_LIMIT_PROMPT),
        name="prompt_layer",
    )(x, wh, wg, wr, wsq, wsq, wsq, nw, gb, hnw, cw, fnw)


SC_LANES = 16
SC_CORES = 2
SC_SUBCORES = 16


def _sample_state_sc_call(c0, head_s, kw_s, fb_s):
    nseq = c0.shape[1]
    nworkers = SC_CORES * SC_SUBCORES
    per_worker = nseq * NH // nworkers
    nvec = DQK // SC_LANES
    mesh = plsc.VectorSubcoreMesh(core_axis_name="c", subcore_axis_name="s")

    def body(c_hbm, hs_hbm, kw_hbm, fb_hbm, co_hbm, cq_hbm, c_v, q_v, kw_v, v_v, f_v, cq_v):
        wid = lax.axis_index("c") * SC_SUBCORES + lax.axis_index("s")
        lane = lax.iota(jnp.int32, SC_LANES)
        zero_i = jnp.zeros((SC_LANES,), jnp.int32)

        @pl.loop(0, per_worker)
        def _(t):
            pair = wid * per_worker + t
            b = pair // NH
            h = pair % NH
            pltpu.sync_copy(c_hbm.at[0, b, h], c_v)
            pltpu.sync_copy(hs_hbm.at[b, pl.ds(OFF_Q + h * DQK, DQK)], q_v)
            pltpu.sync_copy(hs_hbm.at[b, pl.ds(OFF_V + h * DHV, DHV)], v_v)
            pltpu.sync_copy(kw_hbm.at[b, pl.ds(h * DQK, DQK)], kw_v)
            pltpu.sync_copy(fb_hbm.at[b, pl.ds(h * DQK, DQK)], f_v)
            q = [q_v[pl.ds(SC_LANES * j, SC_LANES)] for j in range(nvec)]
            kw = [kw_v[pl.ds(SC_LANES * j, SC_LANES)] for j in range(nvec)]
            f_vec = f_v[pl.ds(0, SC_LANES)]

            @pl.loop(0, DHV // SC_LANES)
            def _(g):
                cq_vec = jnp.zeros((SC_LANES,), F32)
                for i in range(SC_LANES):
                    r = g * SC_LANES + i
                    v_r = plsc.load_gather(v_v, [zero_i + r])
                    acc = jnp.zeros((SC_LANES,), F32)
                    for j in range(nvec):
                        c = c_v[r, pl.ds(SC_LANES * j, SC_LANES)]
                        acc = acc + c * q[j]
                        c_v[r, pl.ds(SC_LANES * j, SC_LANES)] = f_vec * c + v_r * kw[j]
                    cq_vec = jnp.where(lane == i, jnp.sum(acc), cq_vec)
                cq_v[pl.ds(g * SC_LANES, SC_LANES)] = cq_vec

            pltpu.sync_copy(c_v, co_hbm.at[0, b, h])
            pltpu.sync_copy(cq_v, cq_hbm.at[b, pl.ds(h * DHV, DHV)])

    return pl.kernel(
        body,
        out_type=(jax.ShapeDtypeStruct(c0.shape, F32),
                  jax.ShapeDtypeStruct((nseq, D_M), F32)),
        mesh=mesh,
        scratch_types=[pltpu.VMEM((DHV, DQK), F32),
                       pltpu.VMEM((DQK,), F32),
                       pltpu.VMEM((DQK,), F32),
                       pltpu.VMEM((DHV,), F32),
                       pltpu.VMEM((DQK,), F32),
                       pltpu.VMEM((DHV,), F32)],
        compiler_params=pltpu.CompilerParams(use_tc_tiling_on_sc=True, needs_layout_passes=False),
        name="sample_state_sc",
    )(c0, head_s, kw_s, fb_s)


def _sample_tail_kernel(cq_ref, hs_ref, sc_ref, n_ref,
                        x_ref, r_ref, conv_ref, wpm_ref, wpc_ref, wout_ref, hnw_ref, cw_ref, fnw_ref,
                        y_ref, convo_ref):
    def piece(off, width):
        return r_ref[:, off:off + width]

    w_end = sc_ref[:, 0:NH]
    f_end = sc_ref[:, NH:2 * NH]
    floor = sc_ref[:, 2 * NH:3 * NH]
    for h in range(NH):
        q_h = hs_ref[:, OFF_Q + h * DQK:OFF_Q + (h + 1) * DQK]
        k_h = hs_ref[:, OFF_K + h * DQK:OFF_K + (h + 1) * DQK] * K_SCALE
        v_h = hs_ref[:, OFF_V + h * DHV:OFF_V + (h + 1) * DHV]
        n_h = n_ref[0, :, h, :]
        w_h = w_end[:, h:h + 1]
        f_h = f_end[:, h:h + 1]
        s = jnp.sum(q_h * k_h, axis=-1, keepdims=True) * w_h
        den = f_h * jnp.sum(n_h * q_h, axis=-1, keepdims=True) + s
        num = f_h * cq_ref[:, h * DHV:(h + 1) * DHV] + s * v_h
        h_t = num / jnp.maximum(jnp.abs(den), floor[:, h:h + 1])
        hn = h_t * lax.rsqrt(jnp.mean(h_t * h_t, axis=-1, keepdims=True) + EPS)
        hn = hn * hnw_ref[:, h * DHV:(h + 1) * DHV]
        o_h = piece(R_O + h * DHV, DHV)
        zm_h = piece(R_ZM + h * DHV, DHV)
        hg = (hn * _sigmoid(o_h) * _silu(zm_h)).astype(BF16)
        part = _dot(hg, wpm_ref[h * DHV:(h + 1) * DHV, :])
        y_m = part if h == 0 else y_m + part

    u = piece(R_CG, D_C) * piece(R_XC, D_C)
    conv_old = conv_ref[0, :, 1, :]
    cv = _conv_tap(cw_ref, 0) * conv_ref[0, :, 0, :]
    cv = cv + _conv_tap(cw_ref, 1) * conv_old
    cv = cv + _conv_tap(cw_ref, 2) * u
    convo_ref[0, :, 0, :] = conv_old
    convo_ref[0, :, 1, :] = u
    yc_in = (_silu(piece(R_ZC, D_C)) * piece(R_BG, D_C) * cv).astype(BF16)
    y_c = _dot(yc_in, wpc_ref[...])
    mix = (_sigmoid(piece(R_GM, D_MODEL)) * y_m + _sigmoid(piece(R_GC, D_MODEL)) * y_c).astype(BF16)
    out = x_ref[:, 0, :] + _dot(mix, wout_ref[...])
    y_ref[:, 0, :] = _rmsnorm(out, fnw_ref[...])


def _sample_tail_call(cq, head_s, sc, n0, x, rest, conv0, wsq, hnw, cw, fnw):
    nseq = x.shape[0]
    args = (cq, head_s, sc, n0, x, rest, conv0, wsq, wsq, wsq, hnw, cw, fnw)
    full = lambda a: pl.BlockSpec(a.shape, lambda i: (0,) * a.ndim)
    square = lambda k: pl.BlockSpec((D_MODEL, D_MODEL), lambda i: (0, k))
    specs = [full(a) for a in args]
    specs[7:10] = [square(0), square(1), square(2)]
    return pl.pallas_call(
        _sample_tail_kernel,
        grid=(1,),
        in_specs=specs,
        out_specs=(pl.BlockSpec((nseq, 1, D_MODEL), lambda i: (0, 0, 0)),
                   pl.BlockSpec((1, nseq, CONV_W - 1, D_C), lambda i: (0, 0, 0, 0))),
        out_shape=(jax.ShapeDtypeStruct((nseq, 1, D_MODEL), F32),
                   jax.ShapeDtypeStruct((1, nseq, CONV_W - 1, D_C), F32)),
        compiler_params=pltpu.CompilerParams(dimension_semantics=("arbitrary",),
                                             vmem_limit_bytes=VMEM_LIMIT_SAMPLE),
        name="sample_tail",
    )(*args)


def kernel(x_prompt, x_sample, state_mlstm_C, state_mlstm_n, state_mlstm_m, state_conv, norm_w, w_in, b_i, b_f,
           head_norm_w, conv_w, w_proj_m, w_proj_c, w_out, final_norm_w):
    depth = norm_w.shape[0]
    assert depth == 1, "single-layer trunk"
    bsz = x_prompt.shape[0]
    nseq = x_sample.shape[0]

    assert w_in.shape == (1, D_MODEL, N_HEAD + N_GATE_COLS + N_REST)
    nw = norm_w[0].reshape(1, D_MODEL)
    hnw = head_norm_w[0].reshape(1, D_M)
    cw = conv_w.reshape(1, CONV_W * D_C)
    fnw = final_norm_w.reshape(1, D_MODEL)
    m0_t = jnp.swapaxes(state_mlstm_m, 1, 2)
    (wr, wh, wg, wsq, head_s, sc_s, rest_s, kw_s, fb_s, n_s, m_s_t, gb) = _weight_prep_call(
        jnp.swapaxes(w_in, 1, 2), w_proj_m, w_proj_c, w_out, x_sample, nw, b_i, b_f, state_mlstm_n, m0_t)
    m_s = jnp.swapaxes(m_s_t, 1, 2)

    y_p, c_p, n_p, m_p, conv_p = _prompt_call(x_prompt, wh, wg, wr, wsq, nw, gb, hnw, cw, fnw)
    m_p = jnp.swapaxes(m_p, 1, 2)
    c_s, cq = _sample_state_sc_call(state_mlstm_C, head_s, kw_s, fb_s)

    y_s, conv_s = _sample_tail_call(cq, head_s, sc_s, state_mlstm_n,
                                    x_sample, rest_s, state_conv, wsq, hnw, cw, fnw)

    return (y_p, y_s, c_p, n_p, m_p, conv_p, c_s, n_s, m_s, conv_s)
```

```python
import jax
import jax.numpy as jnp
from jax import lax
from jax.experimental import pallas as pl
from jax.experimental.pallas import tpu as pltpu
from jax.experimental.pallas import tpu_sc as plsc

F32 = jnp.float32
BF16 = jnp.bfloat16

D_MODEL = 1024
NH = 4
DHV = 256
DQK = 128
D_QK = NH * DQK
D_M = NH * DHV
D_C = D_MODEL
CONV_W = 3
EPS = 1e-6
NEG_BIG = -1e30
K_SCALE = DQK ** -0.5

LANES = 128
SUBLANES = 8

OFF_Q = 0
OFF_K = OFF_Q + D_QK
OFF_V = OFF_K + D_QK
N_HEAD = OFF_V + D_M
N_GATE_COLS = 2 * NH
R_O = 0
R_ZM = R_O + D_M
R_BG = R_ZM + D_M
R_CG = R_BG + D_C
R_XC = R_CG + D_C
R_ZC = R_XC + D_C
R_GM = R_ZC + D_C
R_GC = R_GM + D_MODEL
N_REST = R_GC + D_MODEL

PROMPT_BLOCK = 512
MLSTM_CHUNK = 256
PROMPT_ORDER = "UUUUFUFUFUFU"
VMEM_LIMIT_PROMPT = 60 * 1024 * 1024
VMEM_LIMIT_SAMPLE = 60 * 1024 * 1024


def _sigmoid(x):
    return 0.5 * jnp.tanh(0.5 * x) + 0.5


def _silu(x):
    return x * _sigmoid(x)


def _log_sigmoid(x):
    return jnp.minimum(x, 0.0) - jnp.log1p(jnp.exp(-jnp.abs(x)))


def _rmsnorm(x, w):
    return x * lax.rsqrt(jnp.mean(x * x, axis=-1, keepdims=True) + EPS) * w


def _conv_tap(cw_ref, j):
    return cw_ref[:, j * D_C:(j + 1) * D_C]


def _dot(a, b):
    return jnp.dot(a, b, preferred_element_type=F32)


def _dot_nt(a, b):
    return lax.dot_general(a, b, (((1,), (1,)), ((), ())), preferred_element_type=F32)


def _dot_tn(a, b):
    return lax.dot_general(a, b, (((0,), (0,)), ((), ())), preferred_element_type=F32)


PREP_COLS = 1024
PREP_CHUNK = 256


PREP_STEPS = N_REST // PREP_COLS
PREP_HEAD_COLS = N_HEAD // PREP_STEPS
PREP_SQ_ROWS = D_MODEL // PREP_STEPS
N_SQUARE = 3


def _weight_prep_kernel(a_ref, c_ref, g_ref, pm_ref, pc_ref, po_ref, xs_ref, nw_ref, bi_ref, bf_ref, n_ref, m_ref,
                        wr_ref, wh_ref, wg_ref, wsq_ref,
                        hs_ref, sc_ref, rs_ref, kw_ref, fb_ref, no_ref, mo_ref, gb_ref,
                        xn_s, m8_s):
    j = pl.program_id(0)
    nseq = xs_ref.shape[0]

    @pl.when(j == 0)
    def _():
        xn_s[...] = _rmsnorm(xs_ref[:, 0, :], nw_ref[...]).astype(BF16)

    xn = xn_s[...]
    for r0 in range(0, PREP_COLS, PREP_CHUNK):
        w_t = a_ref[0, r0:r0 + PREP_CHUNK, :].T.astype(BF16)
        wr_ref[:, r0:r0 + PREP_CHUNK] = w_t
        rs_ref[:, r0:r0 + PREP_CHUNK] = _dot(xn, w_t)
    w_t = c_ref[0].T.astype(BF16)
    wh_ref[...] = w_t
    head = _dot(xn, w_t)
    hs_ref[...] = head

    @pl.when(j == 0)
    def _():
        lane = lax.broadcasted_iota(jnp.int32, (D_MODEL, LANES), 1)
        w_g = jnp.where(lane < N_GATE_COLS, g_ref[0].T, 0.0).astype(BF16)
        wg_ref[...] = w_g
        gb_ref[...] = jnp.concatenate(
            [bi_ref[...], bf_ref[...], jnp.zeros((1, LANES - N_GATE_COLS), F32)], axis=1)
        g8 = (_dot(xn, w_g) + gb_ref[...]).T[0:SUBLANES, :]
        lf = pltpu.roll(_log_sigmoid(g8), NH, axis=0)
        m8_s[0:NH, :] = m_ref[0]
        m8_s[NH:2 * NH, :] = m_ref[0]
        inter = lf + m8_s[...]
        m_t = jnp.maximum(inter, g8)
        mo_ref[0] = m_t[0:NH, :]
        row8 = lax.broadcasted_iota(jnp.int32, (SUBLANES, nseq), 0)
        w_f = jnp.where(row8 < NH, jnp.exp(g8 - m_t), pltpu.roll(jnp.exp(inter - m_t), NH, axis=0))
        tile = jnp.concatenate([w_f, jnp.exp(-m_t), jnp.zeros((LANES - 2 * SUBLANES, nseq), F32)], axis=0)
        sc_ref[...] = tile.T
        for h in range(NH):
            fb_ref[:, h * DQK:(h + 1) * DQK] = jnp.broadcast_to(sc_ref[:, NH + h:NH + h + 1], (nseq, DQK))

    heads_per_step = PREP_HEAD_COLS // DQK
    for step in range(D_QK // PREP_HEAD_COLS, 2 * D_QK // PREP_HEAD_COLS):
        @pl.when(j == step)
        def _(step=step):
            for i in range(heads_per_step):
                h = (step - D_QK // PREP_HEAD_COLS) * heads_per_step + i
                kw = sc_ref[:, h:h + 1] * (head[:, i * DQK:(i + 1) * DQK] * K_SCALE)
                kw_ref[:, h * DQK:(h + 1) * DQK] = kw
                no_ref[0, :, h, :] = sc_ref[:, NH + h:NH + h + 1] * n_ref[0, :, h, :] + kw

    wsq_ref[:, 0:D_MODEL] = pm_ref[0].astype(BF16)
    wsq_ref[:, D_MODEL:2 * D_MODEL] = pc_ref[0].astype(BF16)
    wsq_ref[:, 2 * D_MODEL:3 * D_MODEL] = po_ref[0].astype(BF16)


def _weight_prep_call(w_in_t, w_proj_m, w_proj_c, w_out, xs, nw, b_i, b_f, n0, m0):
    nseq = xs.shape[0]
    row = jax.ShapeDtypeStruct((nseq, D_QK), F32)
    rest_row0 = N_HEAD + N_GATE_COLS
    const = lambda shape: pl.BlockSpec(shape, lambda j: (0,) * len(shape))
    sq_in = pl.BlockSpec((1, PREP_SQ_ROWS, D_MODEL), lambda j: (0, j, 0))
    sq_out = pl.BlockSpec((PREP_SQ_ROWS, N_SQUARE * D_MODEL), lambda j: (j, 0))
    sq_shape = jax.ShapeDtypeStruct((D_MODEL, N_SQUARE * D_MODEL), BF16)
    return pl.pallas_call(
        _weight_prep_kernel,
        grid=(PREP_STEPS,),
        in_specs=[
            pl.BlockSpec((pl.Element(1), pl.Element(PREP_COLS), pl.Element(D_MODEL)),
                         lambda j: (0, pl.multiple_of(rest_row0 + j * PREP_COLS, SUBLANES), 0)),
            pl.BlockSpec((1, PREP_HEAD_COLS, D_MODEL), lambda j: (0, j, 0)),
            pl.BlockSpec((1, LANES, D_MODEL), lambda j: (0, N_HEAD // LANES, 0)),
            sq_in, sq_in, sq_in,
            const((nseq, 1, D_MODEL)),
            const((1, D_MODEL)),
            const((1, NH)),
            const((1, NH)),
            const(n0.shape),
            const(m0.shape),
        ],
        out_specs=(
            pl.BlockSpec((D_MODEL, PREP_COLS), lambda j: (0, j)),
            pl.BlockSpec((D_MODEL, PREP_HEAD_COLS), lambda j: (0, j)),
            const((D_MODEL, LANES)),
            sq_out,
            pl.BlockSpec((nseq, PREP_HEAD_COLS), lambda j: (0, j)),
            const((nseq, LANES)),
            pl.BlockSpec((nseq, PREP_COLS), lambda j: (0, j)),
            const(row.shape), const(row.shape), const(n0.shape), const(m0.shape), const((1, LANES)),
        ),
        out_shape=(
            jax.ShapeDtypeStruct((D_MODEL, N_REST), BF16),
            jax.ShapeDtypeStruct((D_MODEL, N_HEAD), BF16),
            jax.ShapeDtypeStruct((D_MODEL, LANES), BF16),
            sq_shape,
            jax.ShapeDtypeStruct((nseq, N_HEAD), F32),
            jax.ShapeDtypeStruct((nseq, LANES), F32),
            jax.ShapeDtypeStruct((nseq, N_REST), F32),
            row,
            row,
            jax.ShapeDtypeStruct(n0.shape, F32),
            jax.ShapeDtypeStruct(m0.shape, F32),
            jax.ShapeDtypeStruct((1, LANES), F32),
        ),
        scratch_shapes=[pltpu.VMEM((nseq, D_MODEL), BF16),
                        pltpu.VMEM((SUBLANES, nseq), F32)],
        compiler_params=pltpu.CompilerParams(dimension_semantics=("arbitrary",),
                                             vmem_limit_bytes=VMEM_LIMIT_SAMPLE),
        name="weight_prep",
    )(w_in_t, w_in_t, w_in_t, w_proj_m, w_proj_c, w_out, xs, nw, b_i, b_f, n0, m0)


def _prompt_kernel(x_ref, wh_ref, wg_ref, wr_ref, wpm_ref, wpc_ref, wout_ref,
                   nw_ref, gb_ref, hnw_ref, cw_ref, fnw_ref,
                   y_ref, c_ref, n_ref, mo_ref, conv_ref,
                   hg_s, ubuf_s, ct_s, m_ref):
    tl = x_ref.shape[1]
    seq_id = pl.program_id(0)
    l = pl.program_id(1)
    seq_lane = lax.broadcasted_iota(jnp.int32, (1, mo_ref.shape[2]), 1)

    @pl.when((seq_id == 0) & (l == 0))
    def _():
        mo_ref[...] = jnp.zeros_like(mo_ref)

    @pl.when(l == 0)
    def _():
        ct_s[...] = jnp.zeros_like(ct_s)
        n_ref[...] = jnp.zeros_like(n_ref)
        m_ref[...] = jnp.zeros_like(m_ref)
        ubuf_s[0:SUBLANES, :] = jnp.zeros((SUBLANES, D_C), F32)

    x = x_ref[0]
    xn = _rmsnorm(x, nw_ref[...]).astype(BF16)

    def rest(off, width):
        return _dot(xn, wr_ref[:, off:off + width])

    g = _dot(xn, wg_ref[...]) + gb_ref[...]
    gc = jnp.where(lax.broadcasted_iota(jnp.int32, (tl, LANES), 1) < NH, g, _log_sigmoid(g))
    qkv = _dot(xn, wh_ref[...])

    ch = MLSTM_CHUNK
    lane = lax.broadcasted_iota(jnp.int32, (ch, LANES), 1)
    row = lax.broadcasted_iota(jnp.int32, (ch, ch), 0)
    col = lax.broadcasted_iota(jnp.int32, (ch, ch), 1)
    causal = row >= col
    row_g = lax.broadcasted_iota(jnp.int32, (ch, LANES), 0)
    chunk_gates = []
    for r0 in range(0, tl, ch):
        gc_c = gc[r0:r0 + ch, :]
        bc = gc_c
        shift = 1
        while shift < ch:
            bc = bc + jnp.where(row_g >= shift, pltpu.roll(bc, shift, axis=0), 0.0)
            shift *= 2
        gt = jnp.where(lane < NH, gc_c, bc).T
        chunk_gates.append((gc_c, bc, gt))

    head_gates = {}

    def mlstm_unit(ci, h):
        r0 = ci * ch
        gc_c, bc, gt = chunk_gates[ci]
        if ci == 0:
            head_gates[h] = (_sigmoid(rest(R_O + h * DHV, DHV)), _silu(rest(R_ZM + h * DHV, DHV)))
        sig_o = head_gates[h][0][r0:r0 + ch, :]
        silu_z = head_gates[h][1][r0:r0 + ch, :]
        q_f = qkv[r0:r0 + ch, OFF_Q + h * DQK:OFF_Q + (h + 1) * DQK]
        k_f = qkv[r0:r0 + ch, OFF_K + h * DQK:OFF_K + (h + 1) * DQK] * K_SCALE
        v_f = qkv[r0:r0 + ch, OFF_V + h * DHV:OFF_V + (h + 1) * DHV]
        q_b = q_f.astype(BF16)
        k_b = k_f.astype(BF16)
        v_b = v_f.astype(BF16)
        ct_old = ct_s[h]
        n_old = n_ref[0, 0, h:h + 1, :]
        m_old = m_ref[h:h + 1, 0:1]

        b_c = bc[:, NH + h:NH + h + 1]
        ig_c = gc_c[:, h:h + 1]
        ig_r = gt[h:h + 1, :]
        b_r = gt[NH + h:NH + h + 1, :]
        log_d = jnp.where(causal, b_c - b_r + ig_r, NEG_BIG)
        m_loc = jnp.max(log_d, axis=-1, keepdims=True)
        s_loc = _dot_nt(q_b, k_b) * jnp.exp(log_d - m_loc)
        sv_loc = _dot(s_loc.astype(BF16), v_b)
        ssum_loc = jnp.sum(s_loc, axis=-1, keepdims=True)
        inter = b_c + m_old
        m_t = jnp.maximum(inter, m_loc)
        a_loc = jnp.exp(m_loc - m_t)
        w_int = jnp.exp(inter - m_t)
        num = w_int * _dot(q_b, ct_old.astype(BF16)) + a_loc * sv_loc
        den = w_int * jnp.sum(q_f * n_old, axis=-1, keepdims=True) + a_loc * ssum_loc
        h_t = num / jnp.maximum(jnp.abs(den), jnp.exp(-m_t))

        b_end = b_c[ch - 1:ch, :]
        m_e = jnp.max(b_end - b_r + ig_r, axis=-1, keepdims=True)
        w_loc = jnp.exp(b_end - b_c + ig_c - m_e)
        kv_loc = _dot_tn(k_b, (w_loc * v_f).astype(BF16))
        ksum_loc = jnp.sum(w_loc * k_f, axis=0, keepdims=True)
        inter_end = b_end + m_old
        m_new = jnp.maximum(inter_end, m_e)
        a_end = jnp.exp(m_e - m_new)
        f_end = jnp.exp(inter_end - m_new)
        ct_s[h] = f_end * ct_old + a_end * kv_loc
        n_ref[0, 0, h:h + 1, :] = f_end * n_old + a_end * ksum_loc
        m_ref[h:h + 1, :] = jnp.broadcast_to(m_new, (1, LANES))
        mo_ref[0, h:h + 1, :] = jnp.where(seq_lane == seq_id, m_new, mo_ref[0, h:h + 1, :])

        hn = h_t * lax.rsqrt(jnp.mean(h_t * h_t, axis=-1, keepdims=True) + EPS)
        hn = hn * hnw_ref[:, h * DHV:(h + 1) * DHV]
        hg_s[r0:r0 + ch, h * DHV:(h + 1) * DHV] = (hn * sig_o * silu_z).astype(BF16)

    val = {}

    def conv_input():
        u = rest(R_CG, D_C) * rest(R_XC, D_C)
        ubuf_s[SUBLANES:SUBLANES + tl, :] = u
        cv = _conv_tap(cw_ref, 0) * ubuf_s[SUBLANES - 2:SUBLANES - 2 + tl, :]
        cv = cv + _conv_tap(cw_ref, 1) * ubuf_s[SUBLANES - 1:SUBLANES - 1 + tl, :]
        val["cv"] = cv + _conv_tap(cw_ref, 2) * u

    def conv_gate():
        val["yc_in"] = (_silu(rest(R_ZC, D_C)) * rest(R_BG, D_C) * val["cv"]).astype(BF16)

    def conv_proj():
        val["gy_c"] = _sigmoid(rest(R_GC, D_MODEL)) * _dot(val["yc_in"], wpc_ref[...])

    def merge_gate():
        val["sig_gm"] = _sigmoid(rest(R_GM, D_MODEL))

    fillers = iter([conv_input, conv_gate, conv_proj, merge_gate])
    units = iter([(ci, h) for ci in range(tl // ch) for h in range(NH)])
    for step in PROMPT_ORDER:
        if step == "U":
            mlstm_unit(*next(units))
        else:
            next(fillers)()

    ubuf_s[0:SUBLANES, :] = ubuf_s[tl:tl + SUBLANES, :]

    conv_ref[0, 0] = ubuf_s[SUBLANES - (CONV_W - 1):SUBLANES, :]
    for h in range(NH):
        c_ref[0, 0, h] = ct_s[h].T

    y_m = _dot(hg_s[...], wpm_ref[...])
    mix = (val["sig_gm"] * y_m + val["gy_c"]).astype(BF16)
    out = x + _dot(mix, wout_ref[...])
    y_ref[0] = _rmsnorm(out, fnw_ref[...])


def _resident(shape, block_index=None):
    index = (0,) * len(shape) if block_index is None else block_index
    return pl.BlockSpec(shape, lambda *_: index, pipeline_mode=pl.Buffered(1))


def _prompt_call(x, wh, wg, wr, wsq, nw, gb, hnw, cw, fnw):
    bsz, seq, _ = x.shape
    tl = PROMPT_BLOCK
    grid = (bsz, seq // tl)
    out_shape = (
        jax.ShapeDtypeStruct((bsz, seq, D_MODEL), F32),
        jax.ShapeDtypeStruct((1, bsz, NH, DHV, DQK), F32),
        jax.ShapeDtypeStruct((1, bsz, NH, DQK), F32),
        jax.ShapeDtypeStruct((1, NH, bsz), F32),
        jax.ShapeDtypeStruct((1, bsz, CONV_W - 1, D_C), F32),
    )
    in_specs = [
        pl.BlockSpec((1, tl, D_MODEL), lambda b, l: (b, l, 0)),
        _resident((D_MODEL, N_HEAD)),
        _resident((D_MODEL, LANES)),
        _resident((D_MODEL, N_REST)),
        _resident((D_M, D_MODEL), (0, 0)),
        _resident((D_C, D_MODEL), (0, 1)),
        _resident((D_MODEL, D_MODEL), (0, 2)),
        _resident((1, D_MODEL)),
        _resident((1, LANES)),
        _resident((1, D_M)),
        _resident((1, CONV_W * D_C)),
        _resident((1, D_MODEL)),
    ]
    out_specs = (
        pl.BlockSpec((1, tl, D_MODEL), lambda b, l: (b, l, 0)),
        pl.BlockSpec((1, 1, NH, DHV, DQK), lambda b, l: (0, b, 0, 0, 0)),
        pl.BlockSpec((1, 1, NH, DQK), lambda b, l: (0, b, 0, 0)),
        pl.BlockSpec((1, NH, bsz), lambda b, l: (0, 0, 0)),
        pl.BlockSpec((1, 1, CONV_W - 1, D_C), lambda b, l: (0, b, 0, 0)),
    )
    return pl.pallas_call(
        _prompt_kernel,
        grid=grid,
        in_specs=in_specs,
        out_specs=out_specs,
        out_shape=out_shape,
        scratch_shapes=[
            pltpu.VMEM((tl, D_M), BF16),
            pltpu.VMEM((tl + 2 * SUBLANES, D_C), F32),
            pltpu.VMEM((NH, DQK, DHV), F32),
            pltpu.VMEM((SUBLANES, LANES), F32),
        ],
        compiler_params=pltpu.CompilerParams(
            dimension_semantics=("arbitrary", "arbitrary"),
            vmem_limit_bytes=VMEM_LIMIT_PROMPT),
        name="prompt_layer",
    )(x, wh, wg, wr, wsq, wsq, wsq, nw, gb, hnw, cw, fnw)


SC_LANES = 16
SC_CORES = 2
SC_SUBCORES = 16


def _sample_state_sc_call(c0, head_s, kw_s, fb_s):
    nseq = c0.shape[1]
    nworkers = SC_CORES * SC_SUBCORES
    per_worker = nseq * NH // nworkers
    nvec = DQK // SC_LANES
    mesh = plsc.VectorSubcoreMesh(core_axis_name="c", subcore_axis_name="s")

    def body(c_hbm, hs_hbm, kw_hbm, fb_hbm, co_hbm, cq_hbm, c_v, q_v, kw_v, v_v, f_v, cq_v):
        wid = lax.axis_index("c") * SC_SUBCORES + lax.axis_index("s")
        lane = lax.iota(jnp.int32, SC_LANES)
        zero_i = jnp.zeros((SC_LANES,), jnp.int32)

        @pl.loop(0, per_worker)
        def _(t):
            pair = wid * per_worker + t
            b = pair // NH
            h = pair % NH
            pltpu.sync_copy(c_hbm.at[0, b, h], c_v)
            pltpu.sync_copy(hs_hbm.at[b, pl.ds(OFF_Q + h * DQK, DQK)], q_v)
            pltpu.sync_copy(hs_hbm.at[b, pl.ds(OFF_V + h * DHV, DHV)], v_v)
            pltpu.sync_copy(kw_hbm.at[b, pl.ds(h * DQK, DQK)], kw_v)
            pltpu.sync_copy(fb_hbm.at[b, pl.ds(h * DQK, DQK)], f_v)
            q = [q_v[pl.ds(SC_LANES * j, SC_LANES)] for j in range(nvec)]
            kw = [kw_v[pl.ds(SC_LANES * j, SC_LANES)] for j in range(nvec)]
            f_vec = f_v[pl.ds(0, SC_LANES)]

            @pl.loop(0, DHV // SC_LANES)
            def _(g):
                cq_vec = jnp.zeros((SC_LANES,), F32)
                for i in range(SC_LANES):
                    r = g * SC_LANES + i
                    v_r = plsc.load_gather(v_v, [zero_i + r])
                    acc = jnp.zeros((SC_LANES,), F32)
                    for j in range(nvec):
                        c = c_v[r, pl.ds(SC_LANES * j, SC_LANES)]
                        acc = acc + c * q[j]
                        c_v[r, pl.ds(SC_LANES * j, SC_LANES)] = f_vec * c + v_r * kw[j]
                    cq_vec = jnp.where(lane == i, jnp.sum(acc), cq_vec)
                cq_v[pl.ds(g * SC_LANES, SC_LANES)] = cq_vec

            pltpu.sync_copy(c_v, co_hbm.at[0, b, h])
            pltpu.sync_copy(cq_v, cq_hbm.at[b, pl.ds(h * DHV, DHV)])

    return pl.kernel(
        body,
        out_type=(jax.ShapeDtypeStruct(c0.shape, F32),
                  jax.ShapeDtypeStruct((nseq, D_M), F32)),
        mesh=mesh,
        scratch_types=[pltpu.VMEM((DHV, DQK), F32),
                       pltpu.VMEM((DQK,), F32),
                       pltpu.VMEM((DQK,), F32),
                       pltpu.VMEM((DHV,), F32),
                       pltpu.VMEM((DQK,), F32),
                       pltpu.VMEM((DHV,), F32)],
        compiler_params=pltpu.CompilerParams(use_tc_tiling_on_sc=True, needs_layout_passes=False),
        name="sample_state_sc",
    )(c0, head_s, kw_s, fb_s)


def _sample_tail_kernel(cq_ref, hs_ref, sc_ref, n_ref,
                        x_ref, r_ref, conv_ref, wpm_ref, wpc_ref, wout_ref, hnw_ref, cw_ref, fnw_ref,
                        y_ref, convo_ref):
    def piece(off, width):
        return r_ref[:, off:off + width]

    w_end = sc_ref[:, 0:NH]
    f_end = sc_ref[:, NH:2 * NH]
    floor = sc_ref[:, 2 * NH:3 * NH]
    for h in range(NH):
        q_h = hs_ref[:, OFF_Q + h * DQK:OFF_Q + (h + 1) * DQK]
        k_h = hs_ref[:, OFF_K + h * DQK:OFF_K + (h + 1) * DQK] * K_SCALE
        v_h = hs_ref[:, OFF_V + h * DHV:OFF_V + (h + 1) * DHV]
        n_h = n_ref[0, :, h, :]
        w_h = w_end[:, h:h + 1]
        f_h = f_end[:, h:h + 1]
        s = jnp.sum(q_h * k_h, axis=-1, keepdims=True) * w_h
        den = f_h * jnp.sum(n_h * q_h, axis=-1, keepdims=True) + s
        num = f_h * cq_ref[:, h * DHV:(h + 1) * DHV] + s * v_h
        h_t = num / jnp.maximum(jnp.abs(den), floor[:, h:h + 1])
        hn = h_t * lax.rsqrt(jnp.mean(h_t * h_t, axis=-1, keepdims=True) + EPS)
        hn = hn * hnw_ref[:, h * DHV:(h + 1) * DHV]
        o_h = piece(R_O + h * DHV, DHV)
        zm_h = piece(R_ZM + h * DHV, DHV)
        hg = (hn * _sigmoid(o_h) * _silu(zm_h)).astype(BF16)
        part = _dot(hg, wpm_ref[h * DHV:(h + 1) * DHV, :])
        y_m = part if h == 0 else y_m + part

    u = piece(R_CG, D_C) * piece(R_XC, D_C)
    conv_old = conv_ref[0, :, 1, :]
    cv = _conv_tap(cw_ref, 0) * conv_ref[0, :, 0, :]
    cv = cv + _conv_tap(cw_ref, 1) * conv_old
    cv = cv + _conv_tap(cw_ref, 2) * u
    convo_ref[0, :, 0, :] = conv_old
    convo_ref[0, :, 1, :] = u
    yc_in = (_silu(piece(R_ZC, D_C)) * piece(R_BG, D_C) * cv).astype(BF16)
    y_c = _dot(yc_in, wpc_ref[...])
    mix = (_sigmoid(piece(R_GM, D_MODEL)) * y_m + _sigmoid(piece(R_GC, D_MODEL)) * y_c).astype(BF16)
    out = x_ref[:, 0, :] + _dot(mix, wout_ref[...])
    y_ref[:, 0, :] = _rmsnorm(out, fnw_ref[...])


def _sample_tail_call(cq, head_s, sc, n0, x, rest, conv0, wsq, hnw, cw, fnw):
    nseq = x.shape[0]
    args = (cq, head_s, sc, n0, x, rest, conv0, wsq, wsq, wsq, hnw, cw, fnw)
    full = lambda a: pl.BlockSpec(a.shape, lambda i: (0,) * a.ndim)
    square = lambda k: pl.BlockSpec((D_MODEL, D_MODEL), lambda i: (0, k))
    specs = [full(a) for a in args]
    specs[7:10] = [square(0), square(1), square(2)]
    return pl.pallas_call(
        _sample_tail_kernel,
        grid=(1,),
        in_specs=specs,
        out_specs=(pl.BlockSpec((nseq, 1, D_MODEL), lambda i: (0, 0, 0)),
                   pl.BlockSpec((1, nseq, CONV_W - 1, D_C), lambda i: (0, 0, 0, 0))),
        out_shape=(jax.ShapeDtypeStruct((nseq, 1, D_MODEL), F32),
                   jax.ShapeDtypeStruct((1, nseq, CONV_W - 1, D_C), F32)),
        compiler_params=pltpu.CompilerParams(dimension_semantics=("arbitrary",),
                                             vmem_limit_bytes=VMEM_LIMIT_SAMPLE),
        name="sample_tail",
    )(*args)


def kernel(x_prompt, x_sample, state_mlstm_C, state_mlstm_n, state_mlstm_m, state_conv, norm_w, w_in, b_i, b_f,
           head_norm_w, conv_w, w_proj_m, w_proj_c, w_out, final_norm_w):
    depth = norm_w.shape[0]
    assert depth == 1, "single-layer trunk"
    bsz = x_prompt.shape[0]
    nseq = x_sample.shape[0]

    assert w_in.shape == (1, D_MODEL, N_HEAD + N_GATE_COLS + N_REST)
    nw = norm_w[0].reshape(1, D_MODEL)
    hnw = head_norm_w[0].reshape(1, D_M)
    cw = conv_w.reshape(1, CONV_W * D_C)
    fnw = final_norm_w.reshape(1, D_MODEL)
    m0_t = jnp.swapaxes(state_mlstm_m, 1, 2)
    (wr, wh, wg, wsq, head_s, sc_s, rest_s, kw_s, fb_s, n_s, m_s_t, gb) = _weight_prep_call(
        jnp.swapaxes(w_in, 1, 2), w_proj_m, w_proj_c, w_out, x_sample, nw, b_i, b_f, state_mlstm_n, m0_t)
    m_s = jnp.swapaxes(m_s_t, 1, 2)

    y_p, c_p, n_p, m_p, conv_p = _prompt_call(x_prompt, wh, wg, wr, wsq, nw, gb, hnw, cw, fnw)
    m_p = jnp.swapaxes(m_p, 1, 2)
    c_s, cq = _sample_state_sc_call(state_mlstm_C, head_s, kw_s, fb_s)

    y_s, conv_s = _sample_tail_call(cq, head_s, sc_s, state_mlstm_n,
                                    x_sample, rest_s, state_conv, wsq, hnw, cw, fnw)

    return (y_p, y_s, c_p, n_p, m_p, conv_p, c_s, n_s, m_s, conv_s)
```

```python
import jax
import jax.numpy as jnp
from jax import lax
from jax.experimental import pallas as pl
from jax.experimental.pallas import tpu as pltpu
from jax.experimental.pallas import tpu_sc as plsc

F32 = jnp.float32
BF16 = jnp.bfloat16

D_MODEL = 1024
NH = 4
DHV = 256
DQK = 128
D_QK = NH * DQK
D_M = NH * DHV
D_C = D_MODEL
CONV_W = 3
EPS = 1e-6
NEG_BIG = -1e30
K_SCALE = DQK ** -0.5

LANES = 128
SUBLANES = 8

OFF_Q = 0
OFF_K = OFF_Q + D_QK
OFF_V = OFF_K + D_QK
N_HEAD = OFF_V + D_M
N_GATE_COLS = 2 * NH
R_O = 0
R_ZM = R_O + D_M
R_BG = R_ZM + D_M
R_CG = R_BG + D_C
R_XC = R_CG + D_C
R_ZC = R_XC + D_C
R_GM = R_ZC + D_C
R_GC = R_GM + D_MODEL
N_REST = R_GC + D_MODEL

PROMPT_BLOCK = 512
MLSTM_CHUNK = 256
PROMPT_ORDER = "UUUUFUFUFUFU"
VMEM_LIMIT_PROMPT = 60 * 1024 * 1024
VMEM_LIMIT_SAMPLE = 60 * 1024 * 1024


def _sigmoid(x):
    return 0.5 * jnp.tanh(0.5 * x) + 0.5


def _silu(x):
    return x * _sigmoid(x)


def _log_sigmoid(x):
    return jnp.minimum(x, 0.0) - jnp.log1p(jnp.exp(-jnp.abs(x)))


def _rmsnorm(x, w):
    return x * lax.rsqrt(jnp.mean(x * x, axis=-1, keepdims=True) + EPS) * w


def _conv_tap(cw_ref, j):
    return cw_ref[:, j * D_C:(j + 1) * D_C]


def _dot(a, b):
    return jnp.dot(a, b, preferred_element_type=F32)


def _dot_nt(a, b):
    return lax.dot_general(a, b, (((1,), (1,)), ((), ())), preferred_element_type=F32)


def _dot_tn(a, b):
    return lax.dot_general(a, b, (((0,), (0,)), ((), ())), preferred_element_type=F32)


PREP_COLS = 1024
PREP_CHUNK = 256


PREP_STEPS = N_REST // PREP_COLS
PREP_HEAD_COLS = N_HEAD // PREP_STEPS
PREP_SQ_ROWS = D_MODEL // PREP_STEPS
N_SQUARE = 3


def _weight_prep_kernel(a_ref, c_ref, g_ref, pm_ref, pc_ref, po_ref, xs_ref, nw_ref, bi_ref, bf_ref, n_ref, m_ref,
                        wr_ref, wh_ref, wg_ref, wsq_ref,
                        hs_ref, sc_ref, rs_ref, kw_ref, fb_ref, no_ref, mo_ref, gb_ref,
                        xn_s, m8_s):
    j = pl.program_id(0)
    nseq = xs_ref.shape[0]

    @pl.when(j == 0)
    def _():
        xn_s[...] = _rmsnorm(xs_ref[:, 0, :], nw_ref[...]).astype(BF16)

    xn = xn_s[...]
    for r0 in range(0, PREP_COLS, PREP_CHUNK):
        w_t = a_ref[0, r0:r0 + PREP_CHUNK, :].T.astype(BF16)
        wr_ref[:, r0:r0 + PREP_CHUNK] = w_t
        rs_ref[:, r0:r0 + PREP_CHUNK] = _dot(xn, w_t)
    w_t = c_ref[0].T.astype(BF16)
    wh_ref[...] = w_t
    head = _dot(xn, w_t)
    hs_ref[...] = head

    @pl.when(j == 0)
    def _():
        lane = lax.broadcasted_iota(jnp.int32, (D_MODEL, LANES), 1)
        w_g = jnp.where(lane < N_GATE_COLS, g_ref[0].T, 0.0).astype(BF16)
        wg_ref[...] = w_g
        gb_ref[...] = jnp.concatenate(
            [bi_ref[...], bf_ref[...], jnp.zeros((1, LANES - N_GATE_COLS), F32)], axis=1)
        g8 = (_dot(xn, w_g) + gb_ref[...]).T[0:SUBLANES, :]
        lf = pltpu.roll(_log_sigmoid(g8), NH, axis=0)
        m8_s[0:NH, :] = m_ref[0]
        m8_s[NH:2 * NH, :] = m_ref[0]
        inter = lf + m8_s[...]
        m_t = jnp.maximum(inter, g8)
        mo_ref[0] = m_t[0:NH, :]
        row8 = lax.broadcasted_iota(jnp.int32, (SUBLANES, nseq), 0)
        w_f = jnp.where(row8 < NH, jnp.exp(g8 - m_t), pltpu.roll(jnp.exp(inter - m_t), NH, axis=0))
        tile = jnp.concatenate([w_f, jnp.exp(-m_t), jnp.zeros((LANES - 2 * SUBLANES, nseq), F32)], axis=0)
        sc_ref[...] = tile.T
        for h in range(NH):
            fb_ref[:, h * DQK:(h + 1) * DQK] = jnp.broadcast_to(sc_ref[:, NH + h:NH + h + 1], (nseq, DQK))

    heads_per_step = PREP_HEAD_COLS // DQK
    for step in range(D_QK // PREP_HEAD_COLS, 2 * D_QK // PREP_HEAD_COLS):
        @pl.when(j == step)
        def _(step=step):
            for i in range(heads_per_step):
                h = (step - D_QK // PREP_HEAD_COLS) * heads_per_step + i
                kw = sc_ref[:, h:h + 1] * (head[:, i * DQK:(i + 1) * DQK] * K_SCALE)
                kw_ref[:, h * DQK:(h + 1) * DQK] = kw
                no_ref[0, :, h, :] = sc_ref[:, NH + h:NH + h + 1] * n_ref[0, :, h, :] + kw

    wsq_ref[:, 0:D_MODEL] = pm_ref[0].astype(BF16)
    wsq_ref[:, D_MODEL:2 * D_MODEL] = pc_ref[0].astype(BF16)
    wsq_ref[:, 2 * D_MODEL:3 * D_MODEL] = po_ref[0].astype(BF16)


def _weight_prep_call(w_in_t, w_proj_m, w_proj_c, w_out, xs, nw, b_i, b_f, n0, m0):
    nseq = xs.shape[0]
    row = jax.ShapeDtypeStruct((nseq, D_QK), F32)
    rest_row0 = N_HEAD + N_GATE_COLS
    const = lambda shape: pl.BlockSpec(shape, lambda j: (0,) * len(shape))
    sq_in = pl.BlockSpec((1, PREP_SQ_ROWS, D_MODEL), lambda j: (0, j, 0))
    sq_out = pl.BlockSpec((PREP_SQ_ROWS, N_SQUARE * D_MODEL), lambda j: (j, 0))
    sq_shape = jax.ShapeDtypeStruct((D_MODEL, N_SQUARE * D_MODEL), BF16)
    return pl.pallas_call(
        _weight_prep_kernel,
        grid=(PREP_STEPS,),
        in_specs=[
            pl.BlockSpec((pl.Element(1), pl.Element(PREP_COLS), pl.Element(D_MODEL)),
                         lambda j: (0, pl.multiple_of(rest_row0 + j * PREP_COLS, SUBLANES), 0)),
            pl.BlockSpec((1, PREP_HEAD_COLS, D_MODEL), lambda j: (0, j, 0)),
            pl.BlockSpec((1, LANES, D_MODEL), lambda j: (0, N_HEAD // LANES, 0)),
            sq_in, sq_in, sq_in,
            const((nseq, 1, D_MODEL)),
            const((1, D_MODEL)),
            const((1, NH)),
            const((1, NH)),
            const(n0.shape),
            const(m0.shape),
        ],
        out_specs=(
            pl.BlockSpec((D_MODEL, PREP_COLS), lambda j: (0, j)),
            pl.BlockSpec((D_MODEL, PREP_HEAD_COLS), lambda j: (0, j)),
            const((D_MODEL, LANES)),
            sq_out,
            pl.BlockSpec((nseq, PREP_HEAD_COLS), lambda j: (0, j)),
            const((nseq, LANES)),
            pl.BlockSpec((nseq, PREP_COLS), lambda j: (0, j)),
            const(row.shape), const(row.shape), const(n0.shape), const(m0.shape), const((1, LANES)),
        ),
        out_shape=(
            jax.ShapeDtypeStruct((D_MODEL, N_REST), BF16),
            jax.ShapeDtypeStruct((D_MODEL, N_HEAD), BF16),
            jax.ShapeDtypeStruct((D_MODEL, LANES), BF16),
            sq_shape,
            jax.ShapeDtypeStruct((nseq, N_HEAD), F32),
            jax.ShapeDtypeStruct((nseq, LANES), F32),
            jax.ShapeDtypeStruct((nseq, N_REST), F32),
            row,
            row,
            jax.ShapeDtypeStruct(n0.shape, F32),
            jax.ShapeDtypeStruct(m0.shape, F32),
            jax.ShapeDtypeStruct((1, LANES), F32),
        ),
        scratch_shapes=[pltpu.VMEM((nseq, D_MODEL), BF16),
                        pltpu.VMEM((SUBLANES, nseq), F32)],
        compiler_params=pltpu.CompilerParams(dimension_semantics=("arbitrary",),
                                             vmem_limit_bytes=VMEM_LIMIT_SAMPLE),
        name="weight_prep",
    )(w_in_t, w_in_t, w_in_t, w_proj_m, w_proj_c, w_out, xs, nw, b_i, b_f, n0, m0)


def _prompt_kernel(x_ref, wh_ref, wg_ref, wr_ref, wpm_ref, wpc_ref, wout_ref,
                   nw_ref, gb_ref, hnw_ref, cw_ref, fnw_ref,
                   y_ref, c_ref, n_ref, mo_ref, conv_ref,
                   hg_s, ubuf_s, ct_s, m_ref):
    tl = x_ref.shape[1]
    seq_id = pl.program_id(0)
    l = pl.program_id(1)
    seq_lane = lax.broadcasted_iota(jnp.int32, (1, mo_ref.shape[2]), 1)

    @pl.when((seq_id == 0) & (l == 0))
    def _():
        mo_ref[...] = jnp.zeros_like(mo_ref)

    @pl.when(l == 0)
    def _():
        ct_s[...] = jnp.zeros_like(ct_s)
        n_ref[...] = jnp.zeros_like(n_ref)
        m_ref[...] = jnp.zeros_like(m_ref)
        ubuf_s[0:SUBLANES, :] = jnp.zeros((SUBLANES, D_C), F32)

    x = x_ref[0]
    xn = _rmsnorm(x, nw_ref[...]).astype(BF16)

    def rest(off, width):
        return _dot(xn, wr_ref[:, off:off + width])

    g = _dot(xn, wg_ref[...]) + gb_ref[...]
    gc = jnp.where(lax.broadcasted_iota(jnp.int32, (tl, LANES), 1) < NH, g, _log_sigmoid(g))
    qkv = _dot(xn, wh_ref[...])

    ch = MLSTM_CHUNK
    lane = lax.broadcasted_iota(jnp.int32, (ch, LANES), 1)
    row = lax.broadcasted_iota(jnp.int32, (ch, ch), 0)
    col = lax.broadcasted_iota(jnp.int32, (ch, ch), 1)
    causal = row >= col
    row_g = lax.broadcasted_iota(jnp.int32, (ch, LANES), 0)
    chunk_gates = []
    for r0 in range(0, tl, ch):
        gc_c = gc[r0:r0 + ch, :]
        bc = gc_c
        shift = 1
        while shift < ch:
            bc = bc + jnp.where(row_g >= shift, pltpu.roll(bc, shift, axis=0), 0.0)
            shift *= 2
        gt = jnp.where(lane < NH, gc_c, bc).T
        chunk_gates.append((gc_c, bc, gt))

    head_gates = {}

    def mlstm_unit(ci, h):
        r0 = ci * ch
        gc_c, bc, gt = chunk_gates[ci]
        if ci == 0:
            head_gates[h] = (_sigmoid(rest(R_O + h * DHV, DHV)), _silu(rest(R_ZM + h * DHV, DHV)))
        sig_o = head_gates[h][0][r0:r0 + ch, :]
        silu_z = head_gates[h][1][r0:r0 + ch, :]
        q_f = qkv[r0:r0 + ch, OFF_Q + h * DQK:OFF_Q + (h + 1) * DQK]
        k_f = qkv[r0:r0 + ch, OFF_K + h * DQK:OFF_K + (h + 1) * DQK] * K_SCALE
        v_f = qkv[r0:r0 + ch, OFF_V + h * DHV:OFF_V + (h + 1) * DHV]
        q_b = q_f.astype(BF16)
        k_b = k_f.astype(BF16)
        v_b = v_f.astype(BF16)
        ct_old = ct_s[h]
        n_old = n_ref[0, 0, h:h + 1, :]
        m_old = m_ref[h:h + 1, 0:1]

        b_c = bc[:, NH + h:NH + h + 1]
        ig_c = gc_c[:, h:h + 1]
        ig_r = gt[h:h + 1, :]
        b_r = gt[NH + h:NH + h + 1, :]
        log_d = jnp.where(causal, b_c - b_r + ig_r, NEG_BIG)
        inter = b_c + m_old
        m_t = jnp.maximum(inter, jnp.max(log_d, axis=-1, keepdims=True))
        d_m = jnp.exp(log_d - m_t)
        w_int = jnp.exp(inter - m_t)
        s = _dot_nt(q_b, k_b) * d_m
        num = w_int * _dot(q_b, ct_old.astype(BF16)) + _dot(s.astype(BF16), v_b)
        den = w_int * jnp.sum(q_f * n_old, axis=-1, keepdims=True) + jnp.sum(s, axis=-1, keepdims=True)
        h_t = num / jnp.maximum(jnp.abs(den), jnp.exp(-m_t))

        b_end = b_c[ch - 1:ch, :]
        inter_end = b_end + m_old
        m_new = jnp.maximum(inter_end, jnp.max(b_end - b_r + ig_r, axis=-1, keepdims=True))
        w_end = jnp.exp(b_end - b_c + ig_c - m_new)
        f_end = jnp.exp(inter_end - m_new)
        ct_s[h] = f_end * ct_old + _dot_tn(k_b, (w_end * v_f).astype(BF16))
        n_ref[0, 0, h:h + 1, :] = f_end * n_old + jnp.sum(w_end * k_f, axis=0, keepdims=True)
        m_ref[h:h + 1, :] = jnp.broadcast_to(m_new, (1, LANES))
        mo_ref[0, h:h + 1, :] = jnp.where(seq_lane == seq_id, m_new, mo_ref[0, h:h + 1, :])

        hn = h_t * lax.rsqrt(jnp.mean(h_t * h_t, axis=-1, keepdims=True) + EPS)
        hn = hn * hnw_ref[:, h * DHV:(h + 1) * DHV]
        hg_s[r0:r0 + ch, h * DHV:(h + 1) * DHV] = (hn * sig_o * silu_z).astype(BF16)

    val = {}

    def conv_input():
        u = rest(R_CG, D_C) * rest(R_XC, D_C)
        ubuf_s[SUBLANES:SUBLANES + tl, :] = u
        cv = _conv_tap(cw_ref, 0) * ubuf_s[SUBLANES - 2:SUBLANES - 2 + tl, :]
        cv = cv + _conv_tap(cw_ref, 1) * ubuf_s[SUBLANES - 1:SUBLANES - 1 + tl, :]
        val["cv"] = cv + _conv_tap(cw_ref, 2) * u

    def conv_gate():
        val["yc_in"] = (_silu(rest(R_ZC, D_C)) * rest(R_BG, D_C) * val["cv"]).astype(BF16)

    def conv_proj():
        val["gy_c"] = _sigmoid(rest(R_GC, D_MODEL)) * _dot(val["yc_in"], wpc_ref[...])

    def merge_gate():
        val["sig_gm"] = _sigmoid(rest(R_GM, D_MODEL))

    fillers = iter([conv_input, conv_gate, conv_proj, merge_gate])
    units = iter([(ci, h) for ci in range(tl // ch) for h in range(NH)])
    for step in PROMPT_ORDER:
        if step == "U":
            mlstm_unit(*next(units))
        else:
            next(fillers)()

    ubuf_s[0:SUBLANES, :] = ubuf_s[tl:tl + SUBLANES, :]

    conv_ref[0, 0] = ubuf_s[SUBLANES - (CONV_W - 1):SUBLANES, :]
    for h in range(NH):
        c_ref[0, 0, h] = ct_s[h].T

    y_m = _dot(hg_s[...], wpm_ref[...])
    mix = (val["sig_gm"] * y_m + val["gy_c"]).astype(BF16)
    out = x_ref[0] + _dot(mix, wout_ref[...])
    y_ref[0] = _rmsnorm(out, fnw_ref[...])


def _resident(shape, block_index=None):
    index = (0,) * len(shape) if block_index is None else block_index
    return pl.BlockSpec(shape, lambda *_: index, pipeline_mode=pl.Buffered(1))


def _prompt_call(x, wh, wg, wr, wsq, nw, gb, hnw, cw, fnw):
    bsz, seq, _ = x.shape
    tl = PROMPT_BLOCK
    grid = (bsz, seq // tl)
    out_shape = (
        jax.ShapeDtypeStruct((bsz, seq, D_MODEL), F32),
        jax.ShapeDtypeStruct((1, bsz, NH, DHV, DQK), F32),
        jax.ShapeDtypeStruct((1, bsz, NH, DQK), F32),
        jax.ShapeDtypeStruct((1, NH, bsz), F32),
        jax.ShapeDtypeStruct((1, bsz, CONV_W - 1, D_C), F32),
    )
    in_specs = [
        pl.BlockSpec((1, tl, D_MODEL), lambda b, l: (b, l, 0)),
        _resident((D_MODEL, N_HEAD)),
        _resident((D_MODEL, LANES)),
        _resident((D_MODEL, N_REST)),
        _resident((D_M, D_MODEL), (0, 0)),
        _resident((D_C, D_MODEL), (0, 1)),
        _resident((D_MODEL, D_MODEL), (0, 2)),
        _resident((1, D_MODEL)),
        _resident((1, LANES)),
        _resident((1, D_M)),
        _resident((1, CONV_W * D_C)),
        _resident((1, D_MODEL)),
    ]
    out_specs = (
        pl.BlockSpec((1, tl, D_MODEL), lambda b, l: (b, l, 0)),
        pl.BlockSpec((1, 1, NH, DHV, DQK), lambda b, l: (0, b, 0, 0, 0)),
        pl.BlockSpec((1, 1, NH, DQK), lambda b, l: (0, b, 0, 0)),
        pl.BlockSpec((1, NH, bsz), lambda b, l: (0, 0, 0)),
        pl.BlockSpec((1, 1, CONV_W - 1, D_C), lambda b, l: (0, b, 0, 0)),
    )
    return pl.pallas_call(
        _prompt_kernel,
        grid=grid,
        in_specs=in_specs,
        out_specs=out_specs,
        out_shape=out_shape,
        scratch_shapes=[
            pltpu.VMEM((tl, D_M), BF16),
            pltpu.VMEM((tl + 2 * SUBLANES, D_C), F32),
            pltpu.VMEM((NH, DQK, DHV), F32),
            pltpu.VMEM((SUBLANES, LANES), F32),
        ],
        compiler_params=pltpu.CompilerParams(
            dimension_semantics=("arbitrary", "arbitrary"),
            vmem_limit_bytes=VMEM_LIMIT_PROMPT),
        name="prompt_layer",
    )(x, wh, wg, wr, wsq, wsq, wsq, nw, gb, hnw, cw, fnw)


SC_LANES = 16
SC_CORES = 2
SC_SUBCORES = 16


def _sample_state_sc_call(c0, head_s, kw_s, fb_s):
    nseq = c0.shape[1]
    nworkers = SC_CORES * SC_SUBCORES
    per_worker = nseq * NH // nworkers
    nvec = DQK // SC_LANES
    mesh = plsc.VectorSubcoreMesh(core_axis_name="c", subcore_axis_name="s")

    def body(c_hbm, hs_hbm, kw_hbm, fb_hbm, co_hbm, cq_hbm, c_v, q_v, kw_v, v_v, f_v, cq_v):
        wid = lax.axis_index("c") * SC_SUBCORES + lax.axis_index("s")
        lane = lax.iota(jnp.int32, SC_LANES)
        zero_i = jnp.zeros((SC_LANES,), jnp.int32)

        @pl.loop(0, per_worker)
        def _(t):
            pair = wid * per_worker + t
            b = pair // NH
            h = pair % NH
            pltpu.sync_copy(c_hbm.at[0, b, h], c_v)
            pltpu.sync_copy(hs_hbm.at[b, pl.ds(OFF_Q + h * DQK, DQK)], q_v)
            pltpu.sync_copy(hs_hbm.at[b, pl.ds(OFF_V + h * DHV, DHV)], v_v)
            pltpu.sync_copy(kw_hbm.at[b, pl.ds(h * DQK, DQK)], kw_v)
            pltpu.sync_copy(fb_hbm.at[b, pl.ds(h * DQK, DQK)], f_v)
            q = [q_v[pl.ds(SC_LANES * j, SC_LANES)] for j in range(nvec)]
            kw = [kw_v[pl.ds(SC_LANES * j, SC_LANES)] for j in range(nvec)]
            f_vec = f_v[pl.ds(0, SC_LANES)]

            @pl.loop(0, DHV // SC_LANES)
            def _(g):
                cq_vec = jnp.zeros((SC_LANES,), F32)
                for i in range(SC_LANES):
                    r = g * SC_LANES + i
                    v_r = plsc.load_gather(v_v, [zero_i + r])
                    acc = jnp.zeros((SC_LANES,), F32)
                    for j in range(nvec):
                        c = c_v[r, pl.ds(SC_LANES * j, SC_LANES)]
                        acc = acc + c * q[j]
                        c_v[r, pl.ds(SC_LANES * j, SC_LANES)] = f_vec * c + v_r * kw[j]
                    cq_vec = jnp.where(lane == i, jnp.sum(acc), cq_vec)
                cq_v[pl.ds(g * SC_LANES, SC_LANES)] = cq_vec

            pltpu.sync_copy(c_v, co_hbm.at[0, b, h])
            pltpu.sync_copy(cq_v, cq_hbm.at[b, pl.ds(h * DHV, DHV)])

    return pl.kernel(
        body,
        out_type=(jax.ShapeDtypeStruct(c0.shape, F32),
                  jax.ShapeDtypeStruct((nseq, D_M), F32)),
        mesh=mesh,
        scratch_types=[pltpu.VMEM((DHV, DQK), F32),
                       pltpu.VMEM((DQK,), F32),
                       pltpu.VMEM((DQK,), F32),
                       pltpu.VMEM((DHV,), F32),
                       pltpu.VMEM((DQK,), F32),
                       pltpu.VMEM((DHV,), F32)],
        compiler_params=pltpu.CompilerParams(use_tc_tiling_on_sc=True, needs_layout_passes=False),
        name="sample_state_sc",
    )(c0, head_s, kw_s, fb_s)


def _sample_tail_kernel(cq_ref, hs_ref, sc_ref, n_ref,
                        x_ref, r_ref, conv_ref, wpm_ref, wpc_ref, wout_ref, hnw_ref, cw_ref, fnw_ref,
                        y_ref, convo_ref):
    def piece(off, width):
        return r_ref[:, off:off + width]

    w_end = sc_ref[:, 0:NH]
    f_end = sc_ref[:, NH:2 * NH]
    floor = sc_ref[:, 2 * NH:3 * NH]
    for h in range(NH):
        q_h = hs_ref[:, OFF_Q + h * DQK:OFF_Q + (h + 1) * DQK]
        k_h = hs_ref[:, OFF_K + h * DQK:OFF_K + (h + 1) * DQK] * K_SCALE
        v_h = hs_ref[:, OFF_V + h * DHV:OFF_V + (h + 1) * DHV]
        n_h = n_ref[0, :, h, :]
        w_h = w_end[:, h:h + 1]
        f_h = f_end[:, h:h + 1]
        s = jnp.sum(q_h * k_h, axis=-1, keepdims=True) * w_h
        den = f_h * jnp.sum(n_h * q_h, axis=-1, keepdims=True) + s
        num = f_h * cq_ref[:, h * DHV:(h + 1) * DHV] + s * v_h
        h_t = num / jnp.maximum(jnp.abs(den), floor[:, h:h + 1])
        hn = h_t * lax.rsqrt(jnp.mean(h_t * h_t, axis=-1, keepdims=True) + EPS)
        hn = hn * hnw_ref[:, h * DHV:(h + 1) * DHV]
        o_h = piece(R_O + h * DHV, DHV)
        zm_h = piece(R_ZM + h * DHV, DHV)
        hg = (hn * _sigmoid(o_h) * _silu(zm_h)).astype(BF16)
        part = _dot(hg, wpm_ref[h * DHV:(h + 1) * DHV, :])
        y_m = part if h == 0 else y_m + part

    u = piece(R_CG, D_C) * piece(R_XC, D_C)
    conv_old = conv_ref[0, :, 1, :]
    cv = _conv_tap(cw_ref, 0) * conv_ref[0, :, 0, :]
    cv = cv + _conv_tap(cw_ref, 1) * conv_old
    cv = cv + _conv_tap(cw_ref, 2) * u
    convo_ref[0, :, 0, :] = conv_old
    convo_ref[0, :, 1, :] = u
    yc_in = (_silu(piece(R_ZC, D_C)) * piece(R_BG, D_C) * cv).astype(BF16)
    y_c = _dot(yc_in, wpc_ref[...])
    mix = (_sigmoid(piece(R_GM, D_MODEL)) * y_m + _sigmoid(piece(R_GC, D_MODEL)) * y_c).astype(BF16)
    out = x_ref[:, 0, :] + _dot(mix, wout_ref[...])
    y_ref[:, 0, :] = _rmsnorm(out, fnw_ref[...])


def _sample_tail_call(cq, head_s, sc, n0, x, rest, conv0, wsq, hnw, cw, fnw):
    nseq = x.shape[0]
    args = (cq, head_s, sc, n0, x, rest, conv0, wsq, wsq, wsq, hnw, cw, fnw)
    full = lambda a: pl.BlockSpec(a.shape, lambda i: (0,) * a.ndim)
    square = lambda k: pl.BlockSpec((D_MODEL, D_MODEL), lambda i: (0, k))
    specs = [full(a) for a in args]
    specs[7:10] = [square(0), square(1), square(2)]
    return pl.pallas_call(
        _sample_tail_kernel,
        grid=(1,),
        in_specs=specs,
        out_specs=(pl.BlockSpec((nseq, 1, D_MODEL), lambda i: (0, 0, 0)),
                   pl.BlockSpec((1, nseq, CONV_W - 1, D_C), lambda i: (0, 0, 0, 0))),
        out_shape=(jax.ShapeDtypeStruct((nseq, 1, D_MODEL), F32),
                   jax.ShapeDtypeStruct((1, nseq, CONV_W - 1, D_C), F32)),
        compiler_params=pltpu.CompilerParams(dimension_semantics=("arbitrary",),
                                             vmem_limit_bytes=VMEM_LIMIT_SAMPLE),
        name="sample_tail",
    )(*args)


def kernel(x_prompt, x_sample, state_mlstm_C, state_mlstm_n, state_mlstm_m, state_conv, norm_w, w_in, b_i, b_f,
           head_norm_w, conv_w, w_proj_m, w_proj_c, w_out, final_norm_w):
    depth = norm_w.shape[0]
    assert depth == 1, "single-layer trunk"
    bsz = x_prompt.shape[0]
    nseq = x_sample.shape[0]

    assert w_in.shape == (1, D_MODEL, N_HEAD + N_GATE_COLS + N_REST)
    nw = norm_w[0].reshape(1, D_MODEL)
    hnw = head_norm_w[0].reshape(1, D_M)
    cw = conv_w.reshape(1, CONV_W * D_C)
    fnw = final_norm_w.reshape(1, D_MODEL)
    m0_t = jnp.swapaxes(state_mlstm_m, 1, 2)
    (wr, wh, wg, wsq, head_s, sc_s, rest_s, kw_s, fb_s, n_s, m_s_t, gb) = _weight_prep_call(
        jnp.swapaxes(w_in, 1, 2), w_proj_m, w_proj_c, w_out, x_sample, nw, b_i, b_f, state_mlstm_n, m0_t)
    m_s = jnp.swapaxes(m_s_t, 1, 2)

    y_p, c_p, n_p, m_p, conv_p = _prompt_call(x_prompt, wh, wg, wr, wsq, nw, gb, hnw, cw, fnw)
    m_p = jnp.swapaxes(m_p, 1, 2)
    c_s, cq = _sample_state_sc_call(state_mlstm_C, head_s, kw_s, fb_s)

    y_s, conv_s = _sample_tail_call(cq, head_s, sc_s, state_mlstm_n,
                                    x_sample, rest_s, state_conv, wsq, hnw, cw, fnw)

    return (y_p, y_s, c_p, n_p, m_p, conv_p, c_s, n_s, m_s, conv_s)
```
